```python
import math
import jax, jax.numpy as jnp
from jax import lax
import numpy as np

D_MODEL = 1024
BATCH = 32
SEQ = 2048
DEPTH = 2

CTX_LEN = 256
GRID_W = 64
F32 = jnp.float32
EPS = 1e-6

N_MOD = 9
D_FF = 256 * ((8 * D_MODEL // 3 + 255) // 256)
A_INNER = D_MODEL // 2
A_HEADS = 8
A_HEAD_DIM = A_INNER // A_HEADS
A_GROUPS = 2
A_STATE = 64
A_CONV = 5
A_CONV_DIM = A_INNER + 2 * A_GROUPS * A_STATE
A_COLS = A_INNER + A_CONV_DIM + 2 * A_HEADS
SSD_CHUNK = 64
B_WIDTH = D_MODEL // 4
B_GROUP = 16
B_NGROUPS = B_WIDTH // B_GROUP
B_STATE = 64
B_COLS = B_WIDTH
C_WIDTH = D_MODEL // 4
C_HEADS = 4
C_KEY = C_WIDTH // C_HEADS
C_VAL = C_WIDTH // C_HEADS
C_COLS = 5 * C_WIDTH
HG_CHUNK = 64

MIX_WIDTH = A_INNER + B_WIDTH + C_WIDTH
IN_COLS = A_COLS + B_COLS + C_COLS

kernel_name = "hybrid_ssd_s5_hgrn2_dit_block"


def rms_norm(x):
    xf = x.astype(F32)
    return (xf * lax.rsqrt(jnp.mean(xf * xf, axis=-1, keepdims=True) + EPS)).astype(x.dtype)


def modulate(h, shift, scale):
    return rms_norm(h) * (1 + scale) + shift


def swiglu(u, w_in, w_out):
    gate, up = jnp.split(u @ w_in, 2, axis=-1)
    return (jax.nn.silu(gate) * up) @ w_out


def masked_exp(diff, mask):
    return jnp.where(mask, jnp.exp(jnp.where(mask, diff, 0.0)), 0.0)


def _flip(t):
    return jnp.flip(t, axis=1)


def raster_to_column(t, rows):
    b, s, d = t.shape
    return t.reshape(b, rows, GRID_W, d).transpose(0, 2, 1, 3).reshape(b, s, d)


def column_to_raster(t, rows):
    b, s, d = t.shape
    return t.reshape(b, GRID_W, rows, d).transpose(0, 2, 1, 3).reshape(b, s, d)


def depthwise_conv(x, w, b):
    pad = A_CONV // 2
    y = lax.conv_general_dilated(x, w[:, None, :].astype(x.dtype), window_strides=(1,),
                                 padding=[(pad, pad)], dimension_numbers=('NWC', 'WIO', 'NWC'),
                                 feature_group_count=x.shape[-1])
    return y + b.astype(x.dtype)


def ssd_chunked(x, dt, a, bm, cm, s0):
    bsz, L, H, P = x.shape
    N = bm.shape[-1]
    Q = SSD_CHUNK
    nc = L // Q
    x = x.reshape(bsz, nc, Q, H, P)
    dt = dt.reshape(bsz, nc, Q, H)
    bm = bm.reshape(bsz, nc, Q, H, N)
    cm = cm.reshape(bsz, nc, Q, H, N)
    cum = jnp.cumsum(dt * a, axis=2)
    tri = jnp.tril(jnp.ones((Q, Q), dtype=bool))[None, None, :, :, None]
    decay = masked_exp(cum[:, :, :, None, :] - cum[:, :, None, :, :], tri)
    scores = jnp.einsum('bcihn,bcjhn->bcijh', cm, bm) * decay * dt[:, :, None, :, :]
    y_intra = jnp.einsum('bcijh,bcjhp->bcihp', scores, x)
    w_end = jnp.exp(cum[:, :, -1:, :] - cum) * dt
    chunk_states = jnp.einsum('bcjh,bcjhn,bcjhp->bchpn', w_end, bm, x)
    chunk_decay = jnp.exp(cum[:, :, -1, :])

    def step(s, inp):
        dec, cs = inp
        return s * dec[:, :, None, None] + cs, s

    s_fin, s_in = lax.scan(step, s0, (jnp.moveaxis(chunk_decay, 1, 0), jnp.moveaxis(chunk_states, 1, 0)))
    s_in = jnp.moveaxis(s_in, 0, 1)
    y_inter = jnp.einsum('bcihn,bchpn->bcihp', cm, s_in) * jnp.exp(cum)[..., None]
    return (y_intra + y_inter).reshape(bsz, L, H, P), s_fin


def mamba2_mixer(p, conv_w, conv_b, dt_bias, a_log, d_skip, norm_w, init):
    bsz, L, _ = p.shape
    z, xbc, dt_raw = jnp.split(p, [A_INNER, A_INNER + A_CONV_DIM], axis=-1)
    xbc = jax.nn.silu(depthwise_conv(xbc, conv_w, conv_b))
    xs, bm, cm = jnp.split(xbc, [A_INNER, A_INNER + A_GROUPS * A_STATE], axis=-1)
    rep = A_HEADS // A_GROUPS
    xs = xs.astype(F32).reshape(bsz, L, A_HEADS, A_HEAD_DIM)
    bm = jnp.repeat(bm.astype(F32).reshape(bsz, L, A_GROUPS, A_STATE), rep, axis=2)
    cm = jnp.repeat(cm.astype(F32).reshape(bsz, L, A_GROUPS, A_STATE), rep, axis=2)
    dt = jax.nn.softplus(dt_raw.astype(F32).reshape(bsz, L, 2, A_HEADS) + dt_bias.astype(F32))
    a = -jnp.exp(a_log.astype(F32))
    y_f, s_f = ssd_chunked(xs, dt[:, :, 0], a[0], bm, cm, init[0])
    y_b, s_b = ssd_chunked(_flip(xs), _flip(dt[:, :, 1]), a[1], _flip(bm), _flip(cm), init[1])
    y = y_f + _flip(y_b) + d_skip.astype(F32)[:, None] * xs
    y = y.reshape(bsz, L, A_INNER) * jax.nn.silu(z.astype(F32))
    y = rms_norm(y) * norm_w.astype(F32)
    return y.astype(p.dtype), jnp.stack([s_f, s_b])


def _complex_affine_combine(e1, e2):
    a1r, a1i, b1r, b1i = e1
    a2r, a2i, b2r, b2i = e2
    return (a1r * a2r - a1i * a2i, a1r * a2i + a1i * a2r,
            a2r * b1r - a2i * b1i + b2r, a2r * b1i + a2i * b1r + b2i)


def s5_scan(u, lam_re, lam_im, log_step, b_re, b_im, s0_re, s0_im):
    L = u.shape[1]
    lam_re = lam_re.astype(F32)
    lam_im = lam_im.astype(F32)
    step = jnp.exp(log_step.astype(F32))[:, None]
    mag = jnp.exp(lam_re * step)
    ar = mag * jnp.cos(lam_im * step)
    ai = mag * jnp.sin(lam_im * step)
    den = lam_re * lam_re + lam_im * lam_im
    nr = ar - 1.0
    kr = (nr * lam_re + ai * lam_im) / den
    ki = (ai * lam_re - nr * lam_im) / den
    b_re = b_re.astype(F32)
    b_im = b_im.astype(F32)
    br = kr[..., None] * b_re - ki[..., None] * b_im
    bi = kr[..., None] * b_im + ki[..., None] * b_re
    vr = jnp.einsum('gnc,blgc->blgn', br, u)
    vi = jnp.einsum('gnc,blgc->blgn', bi, u)
    vr = vr.at[:, 0].add(ar * s0_re - ai * s0_im)
    vi = vi.at[:, 0].add(ar * s0_im + ai * s0_re)
    shape_a = (1, L) + ar.shape
    elems = (jnp.broadcast_to(ar, shape_a), jnp.broadcast_to(ai, shape_a), vr, vi)
    _, _, xr, xi = lax.associative_scan(_complex_affine_combine, elems, axis=1)
    return xr, xi


def s5_mixer(p, lam_re, lam_im, log_step, b_re, b_im, c_re, c_im, d_skip, glu_w, glu_b, init_re, init_im):
    bsz, L, _ = p.shape
    pf = p.astype(F32)
    u = pf.reshape(bsz, L, B_NGROUPS, B_GROUP)
    c_re = c_re.astype(F32)
    c_im = c_im.astype(F32)
    xr_f, xi_f = s5_scan(u, lam_re[0], lam_im[0], log_step[0], b_re[0], b_im[0], init_re[0], init_im[0])
    xr_b, xi_b = s5_scan(_flip(u), lam_re[1], lam_im[1], log_step[1], b_re[1], b_im[1], init_re[1], init_im[1])
    y_f = jnp.einsum('gcn,blgn->blgc', c_re[0], xr_f) - jnp.einsum('gcn,blgn->blgc', c_im[0], xi_f)
    y_b = jnp.einsum('gcn,blgn->blgc', c_re[1], xr_b) - jnp.einsum('gcn,blgn->blgc', c_im[1], xi_b)
    y = (y_f + _flip(y_b)).reshape(bsz, L, B_WIDTH) + d_skip.astype(F32) * pf
    y = jax.nn.gelu(y)
    y = y * jax.nn.sigmoid(y @ glu_w.astype(F32) + glu_b.astype(F32))
    fin_re = jnp.stack([xr_f[:, -1], xr_b[:, -1]])
    fin_im = jnp.stack([xi_f[:, -1], xi_b[:, -1]])
    return y.astype(p.dtype), fin_re, fin_im


def hgrn2_chunked(q, log_f, k, v, s0):
    bsz, L, H, K = q.shape
    V = v.shape[-1]
    nc = L // HG_CHUNK

    def chunks(t):
        return jnp.moveaxis(t.reshape((bsz, nc, HG_CHUNK) + t.shape[2:]), 1, 0)

    tri = jnp.tril(jnp.ones((HG_CHUNK, HG_CHUNK), dtype=bool))[None, :, :, None, None]

    def step(s, inp):
        qc, lfc, kc, vc = inp
        cum = jnp.cumsum(lfc, axis=1)
        decay = masked_exp(cum[:, :, None] - cum[:, None, :], tri)
        scores = jnp.einsum('bihk,bjhk,bijhk->bijh', qc, kc, decay)
        o = jnp.einsum('bijh,bjhv->bihv', scores, vc) + jnp.einsum('bihk,bhkv->bihv', qc * jnp.exp(cum), s)
        w = kc * jnp.exp(cum[:, -1:] - cum)
        s_new = s * jnp.exp(cum[:, -1])[..., None] + jnp.einsum('bjhk,bjhv->bhkv', w, vc)
        return s_new, o

    s_fin, o = lax.scan(step, s0, (chunks(q), chunks(log_f), chunks(k), chunks(v)))
    return jnp.moveaxis(o, 0, 1).reshape(bsz, L, H, V), s_fin


def hgrn2_mixer(p, lower, norm_w, init):
    bsz, L, _ = p.shape
    q, f_raw, i, g = jnp.split(p, [C_WIDTH, 3 * C_WIDTH, 4 * C_WIDTH], axis=-1)
    q = jax.nn.silu(q.astype(F32)).reshape(bsz, L, C_HEADS, C_KEY)
    v = i.astype(F32).reshape(bsz, L, C_HEADS, C_VAL)
    f_raw = f_raw.astype(F32).reshape(bsz, L, 2, C_HEADS, C_KEY)
    lower = lower.astype(F32).reshape(2, C_HEADS, C_KEY)
    outs, finals = [], []
    for d in range(2):
        zf = f_raw[:, :, d]
        lb = lower[d]
        f = lb + (1.0 - lb) * jax.nn.sigmoid(zf)
        log_f = jnp.log(f)
        k = 1.0 - f
        if d == 0:
            o, s = hgrn2_chunked(q, log_f, k, v, init[0])
        else:
            o, s = hgrn2_chunked(_flip(q), _flip(log_f), _flip(k), _flip(v), init[1])
            o = _flip(o)
        outs.append(o)
        finals.append(s)
    o = rms_norm(outs[0] + outs[1]) * norm_w.astype(F32).reshape(C_HEADS, C_VAL)
    o = o.reshape(bsz, L, C_WIDTH) * jax.nn.silu(g.astype(F32))
    return o.astype(p.dtype), jnp.stack(finals)


def token_mixers(p_ctx, p_lat, conv_w, conv_b, dt_bias, a_log, a_d, a_norm_w,
                 lam_re, lam_im, log_step, b_re, b_im, c_re, c_im, s5_d, glu_w, glu_b,
                 lower, hg_norm_w):
    bsz = p_ctx.shape[0]
    cuts = [A_COLS, A_COLS + B_COLS]
    pa_c, pb_c, pc_c = jnp.split(p_ctx, cuts, axis=-1)
    pa_l, pb_l, pc_l = jnp.split(p_lat, cuts, axis=-1)
    za = jnp.zeros((2, bsz, A_HEADS, A_HEAD_DIM, A_STATE), F32)
    ya_c, sa = mamba2_mixer(pa_c, conv_w, conv_b, dt_bias, a_log, a_d, a_norm_w, za)
    ya_l, _ = mamba2_mixer(pa_l, conv_w, conv_b, dt_bias, a_log, a_d, a_norm_w, sa)
    zb = jnp.zeros((2, bsz, B_NGROUPS, B_STATE), F32)
    yb_c, sb_re, sb_im = s5_mixer(pb_c, lam_re, lam_im, log_step, b_re, b_im, c_re, c_im, s5_d, glu_w, glu_b, zb, zb)
    yb_l, _, _ = s5_mixer(pb_l, lam_re, lam_im, log_step, b_re, b_im, c_re, c_im, s5_d, glu_w, glu_b, sb_re, sb_im)
    zc = jnp.zeros((2, bsz, C_HEADS, C_KEY, C_VAL), F32)
    yc_c, sc = hgrn2_mixer(pc_c, lower, hg_norm_w, zc)
    yc_l, _ = hgrn2_mixer(pc_l, lower, hg_norm_w, sc)
    return (jnp.concatenate([ya_c, yb_c, yc_c], axis=-1), jnp.concatenate([ya_l, yb_l, yc_l], axis=-1))


def _fwd_setup_inputs(seed: int = 0) -> dict:
    key = jax.random.key(seed)
    ks = jax.random.split(key, 32)
    dm = D_MODEL

    def nrm(k, shape, scale):
        return scale * jax.random.normal(k, shape, F32)

    x = nrm(ks[0], (BATCH, SEQ, dm), 1.0)
    c = nrm(ks[1], (BATCH, dm), 1.0)
    ctx = nrm(ks[2], (BATCH, CTX_LEN, dm), 1.0)
    c_ctx = nrm(ks[3], (dm,), 1.0)
    mod_w = nrm(ks[4], (DEPTH, dm, N_MOD * dm), dm ** -0.5)
    mod_b = nrm(ks[5], (DEPTH, N_MOD * dm), 0.01)
    ffn_w_in = nrm(ks[6], (DEPTH, 2, dm, 2 * D_FF), dm ** -0.5)
    ffn_w_out = nrm(ks[7], (DEPTH, 2, D_FF, dm), D_FF ** -0.5)
    w_in = nrm(ks[8], (DEPTH, dm, IN_COLS), dm ** -0.5)
    w_out = nrm(ks[9], (DEPTH, MIX_WIDTH, dm), MIX_WIDTH ** -0.5)
    a_conv_w = nrm(ks[10], (DEPTH, A_CONV, A_CONV_DIM), A_CONV ** -0.5)
    a_conv_b = nrm(ks[11], (DEPTH, A_CONV_DIM), 0.01)
    dt0 = jnp.exp(jax.random.uniform(ks[12], (DEPTH, 2, A_HEADS), F32, math.log(1e-3), math.log(1e-1)))
    a_dt_bias = dt0 + jnp.log(-jnp.expm1(-dt0))
    a_log = jnp.log(jax.random.uniform(ks[13], (DEPTH, 2, A_HEADS), F32, 1.0, 16.0))
    a_d = 1.0 + nrm(ks[14], (DEPTH, A_HEADS), 0.01)
    a_norm_w = 1.0 + nrm(ks[15], (DEPTH, A_INNER), 0.01)
    s5_lam_re = -0.5 + nrm(ks[16], (DEPTH, 2, B_NGROUPS, B_STATE), 0.01)
    s5_lam_im = math.pi * jnp.arange(B_STATE, dtype=F32) + nrm(ks[17], (DEPTH, 2, B_NGROUPS, B_STATE), 0.01)
    s5_log_step = jax.random.uniform(ks[18], (DEPTH, 2, B_NGROUPS), F32, math.log(1e-3), math.log(1e-1))
    s5_b_re = nrm(ks[19], (DEPTH, 2, B_NGROUPS, B_STATE, B_GROUP), (2 * B_GROUP) ** -0.5)
    s5_b_im = nrm(ks[20], (DEPTH, 2, B_NGROUPS, B_STATE, B_GROUP), (2 * B_GROUP) ** -0.5)
    s5_c_re = nrm(ks[21], (DEPTH, 2, B_NGROUPS, B_GROUP, B_STATE), (2 * B_STATE) ** -0.5)
    s5_c_im = nrm(ks[22], (DEPTH, 2, B_NGROUPS, B_GROUP, B_STATE), (2 * B_STATE) ** -0.5)
    s5_d = nrm(ks[23], (DEPTH, B_WIDTH), 1.0)
    s5_glu_w = nrm(ks[24], (DEPTH, B_WIDTH, B_WIDTH), B_WIDTH ** -0.5)
    s5_glu_b = nrm(ks[25], (DEPTH, B_WIDTH), 0.01)
    hg_lb_logits = nrm(ks[26], (DEPTH, 2, C_WIDTH), 0.5)
    hg_norm_w = 1.0 + nrm(ks[27], (DEPTH, C_WIDTH), 0.01)
    final_norm_w = 1.0 + nrm(ks[28], (dm,), 0.01)
    return {"x": x, "c": c, "ctx": ctx, "c_ctx": c_ctx, "mod_w": mod_w, "mod_b": mod_b,
            "ffn_w_in": ffn_w_in, "ffn_w_out": ffn_w_out, "w_in": w_in, "w_out": w_out,
            "a_conv_w": a_conv_w, "a_conv_b": a_conv_b, "a_dt_bias": a_dt_bias, "a_log": a_log,
            "a_d": a_d, "a_norm_w": a_norm_w, "s5_lam_re": s5_lam_re, "s5_lam_im": s5_lam_im,
            "s5_log_step": s5_log_step, "s5_b_re": s5_b_re, "s5_b_im": s5_b_im, "s5_c_re": s5_c_re,
            "s5_c_im": s5_c_im, "s5_d": s5_d, "s5_glu_w": s5_glu_w, "s5_glu_b": s5_glu_b,
            "hg_lb_logits": hg_lb_logits, "hg_norm_w": hg_norm_w, "final_norm_w": final_norm_w}


def _fwd_reference(x, c, ctx, c_ctx, mod_w, mod_b, ffn_w_in, ffn_w_out, w_in, w_out,
              a_conv_w, a_conv_b, a_dt_bias, a_log, a_d, a_norm_w,
              s5_lam_re, s5_lam_im, s5_log_step, s5_b_re, s5_b_im, s5_c_re, s5_c_im,
              s5_d, s5_glu_w, s5_glu_b, hg_lb_logits, hg_norm_w, final_norm_w):
    bsz, seq, dm = x.shape
    rows = seq // GRID_W
    p_lb = jax.nn.softmax(hg_lb_logits.astype(F32), axis=0)
    lower_bounds = jnp.cumsum(p_lb, axis=0) - p_lb[:1]
    h_lat, h_ctx = x, ctx
    for l in range(DEPTH):
        last = l == DEPTH - 1
        col_major = l % 2 == 1
        m_lat = (jax.nn.silu(c) @ mod_w[l] + mod_b[l]).reshape(bsz, N_MOD, 1, dm)
        m_ctx = (jax.nn.silu(c_ctx) @ mod_w[l] + mod_b[l]).reshape(N_MOD, dm)
        h_lat = h_lat + 0.5 * m_lat[:, 2] * swiglu(modulate(h_lat, m_lat[:, 0], m_lat[:, 1]), ffn_w_in[l, 0], ffn_w_out[l, 0])
        h_ctx = h_ctx + 0.5 * m_ctx[2] * swiglu(modulate(h_ctx, m_ctx[0], m_ctx[1]), ffn_w_in[l, 0], ffn_w_out[l, 0])
        u_lat = modulate(h_lat, m_lat[:, 3], m_lat[:, 4])
        u_ctx = modulate(h_ctx, m_ctx[3], m_ctx[4])
        if col_major:
            u_lat = raster_to_column(u_lat, rows)
        mix_ctx, mix_lat = token_mixers(u_ctx @ w_in[l], u_lat @ w_in[l],
                                        a_conv_w[l], a_conv_b[l], a_dt_bias[l], a_log[l], a_d[l], a_norm_w[l],
                                        s5_lam_re[l], s5_lam_im[l], s5_log_step[l], s5_b_re[l], s5_b_im[l],
                                        s5_c_re[l], s5_c_im[l], s5_d[l], s5_glu_w[l], s5_glu_b[l],
                                        lower_bounds[l], hg_norm_w[l])
        y_lat = mix_lat @ w_out[l]
        if col_major:
            y_lat = column_to_raster(y_lat, rows)
        h_lat = h_lat + m_lat[:, 5] * y_lat
        h_lat = h_lat + 0.5 * m_lat[:, 8] * swiglu(modulate(h_lat, m_lat[:, 6], m_lat[:, 7]), ffn_w_in[l, 1], ffn_w_out[l, 1])
        if not last:
            h_ctx = h_ctx + m_ctx[5] * (mix_ctx @ w_out[l])
            h_ctx = h_ctx + 0.5 * m_ctx[8] * swiglu(modulate(h_ctx, m_ctx[6], m_ctx[7]), ffn_w_in[l, 1], ffn_w_out[l, 1])
    return rms_norm(h_lat) * final_norm_w


import jax as _jax
import jax.numpy as _jnp

TWIN_FORMAT = 'train_step'
FWD_PARAMS = ['x', 'c', 'ctx', 'c_ctx', 'mod_w', 'mod_b', 'ffn_w_in', 'ffn_w_out', 'w_in', 'w_out', 'a_conv_w', 'a_conv_b', 'a_dt_bias', 'a_log', 'a_d', 'a_norm_w', 's5_lam_re', 's5_lam_im', 's5_log_step', 's5_b_re', 's5_b_im', 's5_c_re', 's5_c_im', 's5_d', 's5_glu_w', 's5_glu_b', 'hg_lb_logits', 'hg_norm_w', 'final_norm_w']
TWIN_WEIGHTS = ['c_ctx', 'mod_w', 'mod_b', 'ffn_w_in', 'ffn_w_out', 'w_in', 'w_out', 'a_conv_w', 'a_conv_b', 'a_dt_bias', 'a_log', 'a_d', 'a_norm_w', 's5_lam_re', 's5_lam_im', 's5_log_step', 's5_b_re', 's5_b_im', 's5_c_re', 's5_c_im', 's5_d', 's5_glu_w', 's5_glu_b', 'hg_lb_logits', 'hg_norm_w', 'final_norm_w']
TWIN_DIFF_INPUT = 'x'
TWIN_INPUTS = ['x', 'c', 'ctx', 'c_ctx', 'mod_w', 'mod_b', 'ffn_w_in', 'ffn_w_out', 'w_in', 'w_out', 'a_conv_w', 'a_conv_b', 'a_dt_bias', 'a_log', 'a_d', 'a_norm_w', 's5_lam_re', 's5_lam_im', 's5_log_step', 's5_b_re', 's5_b_im', 's5_c_re', 's5_c_im', 's5_d', 's5_glu_w', 's5_glu_b', 'hg_lb_logits', 'hg_norm_w', 'final_norm_w', 'loss_target', 'm_c_ctx', 'm_mod_w', 'm_mod_b', 'm_ffn_w_in', 'm_ffn_w_out', 'm_w_in', 'm_w_out', 'm_a_conv_w', 'm_a_conv_b', 'm_a_dt_bias', 'm_a_log', 'm_a_d', 'm_a_norm_w', 'm_s5_lam_re', 'm_s5_lam_im', 'm_s5_log_step', 'm_s5_b_re', 'm_s5_b_im', 'm_s5_c_re', 'm_s5_c_im', 'm_s5_d', 'm_s5_glu_w', 'm_s5_glu_b', 'm_hg_lb_logits', 'm_hg_norm_w', 'm_final_norm_w', 'v_c_ctx', 'v_mod_w', 'v_mod_b', 'v_ffn_w_in', 'v_ffn_w_out', 'v_w_in', 'v_w_out', 'v_a_conv_w', 'v_a_conv_b', 'v_a_dt_bias', 'v_a_log', 'v_a_d', 'v_a_norm_w', 'v_s5_lam_re', 'v_s5_lam_im', 'v_s5_log_step', 'v_s5_b_re', 'v_s5_b_im', 'v_s5_c_re', 'v_s5_c_im', 'v_s5_d', 'v_s5_glu_w', 'v_s5_glu_b', 'v_hg_lb_logits', 'v_hg_norm_w', 'v_final_norm_w']
TWIN_OUTPUTS = ['loss', 'grad_x', 'grad_c_ctx', 'grad_mod_w', 'grad_mod_b', 'grad_ffn_w_in', 'grad_ffn_w_out', 'grad_w_in', 'grad_w_out', 'grad_a_conv_w', 'grad_a_conv_b', 'grad_a_dt_bias', 'grad_a_log', 'grad_a_d', 'grad_a_norm_w', 'grad_s5_lam_re', 'grad_s5_lam_im', 'grad_s5_log_step', 'grad_s5_b_re', 'grad_s5_b_im', 'grad_s5_c_re', 'grad_s5_c_im', 'grad_s5_d', 'grad_s5_glu_w', 'grad_s5_glu_b', 'grad_hg_lb_logits', 'grad_hg_norm_w', 'grad_final_norm_w', 'delta_c_ctx', 'delta_mod_w', 'delta_mod_b', 'delta_ffn_w_in', 'delta_ffn_w_out', 'delta_w_in', 'delta_w_out', 'delta_a_conv_w', 'delta_a_conv_b', 'delta_a_dt_bias', 'delta_a_log', 'delta_a_d', 'delta_a_norm_w', 'delta_s5_lam_re', 'delta_s5_lam_im', 'delta_s5_log_step', 'delta_s5_b_re', 'delta_s5_b_im', 'delta_s5_c_re', 'delta_s5_c_im', 'delta_s5_d', 'delta_s5_glu_w', 'delta_s5_glu_b', 'delta_hg_lb_logits', 'delta_hg_norm_w', 'delta_final_norm_w', 'new_m_c_ctx', 'new_m_mod_w', 'new_m_mod_b', 'new_m_ffn_w_in', 'new_m_ffn_w_out', 'new_m_w_in', 'new_m_w_out', 'new_m_a_conv_w', 'new_m_a_conv_b', 'new_m_a_dt_bias', 'new_m_a_log', 'new_m_a_d', 'new_m_a_norm_w', 'new_m_s5_lam_re', 'new_m_s5_lam_im', 'new_m_s5_log_step', 'new_m_s5_b_re', 'new_m_s5_b_im', 'new_m_s5_c_re', 'new_m_s5_c_im', 'new_m_s5_d', 'new_m_s5_glu_w', 'new_m_s5_glu_b', 'new_m_hg_lb_logits', 'new_m_hg_norm_w', 'new_m_final_norm_w', 'new_v_c_ctx', 'new_v_mod_w', 'new_v_mod_b', 'new_v_ffn_w_in', 'new_v_ffn_w_out', 'new_v_w_in', 'new_v_w_out', 'new_v_a_conv_w', 'new_v_a_conv_b', 'new_v_a_dt_bias', 'new_v_a_log', 'new_v_a_d', 'new_v_a_norm_w', 'new_v_s5_lam_re', 'new_v_s5_lam_im', 'new_v_s5_log_step', 'new_v_s5_b_re', 'new_v_s5_b_im', 'new_v_s5_c_re', 'new_v_s5_c_im', 'new_v_s5_d', 'new_v_s5_glu_w', 'new_v_s5_glu_b', 'new_v_hg_lb_logits', 'new_v_hg_norm_w', 'new_v_final_norm_w']
TWIN_LEAF_KINDS = {'loss': 'loss', 'grad_x': 'grad_x', 'grad_c_ctx': 'grad_w', 'grad_mod_w': 'grad_w', 'grad_mod_b': 'grad_w', 'grad_ffn_w_in': 'grad_w', 'grad_ffn_w_out': 'grad_w', 'grad_w_in': 'grad_w', 'grad_w_out': 'grad_w', 'grad_a_conv_w': 'grad_w', 'grad_a_conv_b': 'grad_w', 'grad_a_dt_bias': 'grad_w', 'grad_a_log': 'grad_w', 'grad_a_d': 'grad_w', 'grad_a_norm_w': 'grad_w', 'grad_s5_lam_re': 'grad_w', 'grad_s5_lam_im': 'grad_w', 'grad_s5_log_step': 'grad_w', 'grad_s5_b_re': 'grad_w', 'grad_s5_b_im': 'grad_w', 'grad_s5_c_re': 'grad_w', 'grad_s5_c_im': 'grad_w', 'grad_s5_d': 'grad_w', 'grad_s5_glu_w': 'grad_w', 'grad_s5_glu_b': 'grad_w', 'grad_hg_lb_logits': 'grad_w', 'grad_hg_norm_w': 'grad_w', 'grad_final_norm_w': 'grad_w', 'delta_c_ctx': 'delta_w', 'delta_mod_w': 'delta_w', 'delta_mod_b': 'delta_w', 'delta_ffn_w_in': 'delta_w', 'delta_ffn_w_out': 'delta_w', 'delta_w_in': 'delta_w', 'delta_w_out': 'delta_w', 'delta_a_conv_w': 'delta_w', 'delta_a_conv_b': 'delta_w', 'delta_a_dt_bias': 'delta_w', 'delta_a_log': 'delta_w', 'delta_a_d': 'delta_w', 'delta_a_norm_w': 'delta_w', 'delta_s5_lam_re': 'delta_w', 'delta_s5_lam_im': 'delta_w', 'delta_s5_log_step': 'delta_w', 'delta_s5_b_re': 'delta_w', 'delta_s5_b_im': 'delta_w', 'delta_s5_c_re': 'delta_w', 'delta_s5_c_im': 'delta_w', 'delta_s5_d': 'delta_w', 'delta_s5_glu_w': 'delta_w', 'delta_s5_glu_b': 'delta_w', 'delta_hg_lb_logits': 'delta_w', 'delta_hg_norm_w': 'delta_w', 'delta_final_norm_w': 'delta_w', 'new_m_c_ctx': 'new_m', 'new_m_mod_w': 'new_m', 'new_m_mod_b': 'new_m', 'new_m_ffn_w_in': 'new_m', 'new_m_ffn_w_out': 'new_m', 'new_m_w_in': 'new_m', 'new_m_w_out': 'new_m', 'new_m_a_conv_w': 'new_m', 'new_m_a_conv_b': 'new_m', 'new_m_a_dt_bias': 'new_m', 'new_m_a_log': 'new_m', 'new_m_a_d': 'new_m', 'new_m_a_norm_w': 'new_m', 'new_m_s5_lam_re': 'new_m', 'new_m_s5_lam_im': 'new_m', 'new_m_s5_log_step': 'new_m', 'new_m_s5_b_re': 'new_m', 'new_m_s5_b_im': 'new_m', 'new_m_s5_c_re': 'new_m', 'new_m_s5_c_im': 'new_m', 'new_m_s5_d': 'new_m', 'new_m_s5_glu_w': 'new_m', 'new_m_s5_glu_b': 'new_m', 'new_m_hg_lb_logits': 'new_m', 'new_m_hg_norm_w': 'new_m', 'new_m_final_norm_w': 'new_m', 'new_v_c_ctx': 'new_v', 'new_v_mod_w': 'new_v', 'new_v_mod_b': 'new_v', 'new_v_ffn_w_in': 'new_v', 'new_v_ffn_w_out': 'new_v', 'new_v_w_in': 'new_v', 'new_v_w_out': 'new_v', 'new_v_a_conv_w': 'new_v', 'new_v_a_conv_b': 'new_v', 'new_v_a_dt_bias': 'new_v', 'new_v_a_log': 'new_v', 'new_v_a_d': 'new_v', 'new_v_a_norm_w': 'new_v', 'new_v_s5_lam_re': 'new_v', 'new_v_s5_lam_im': 'new_v', 'new_v_s5_log_step': 'new_v', 'new_v_s5_b_re': 'new_v', 'new_v_s5_b_im': 'new_v', 'new_v_s5_c_re': 'new_v', 'new_v_s5_c_im': 'new_v', 'new_v_s5_d': 'new_v', 'new_v_s5_glu_w': 'new_v', 'new_v_s5_glu_b': 'new_v', 'new_v_hg_lb_logits': 'new_v', 'new_v_hg_norm_w': 'new_v', 'new_v_final_norm_w': 'new_v'}


def _forward(args):
    return _fwd_reference(*[args[k] for k in FWD_PARAMS])


def _output_shape():
    out = _jax.eval_shape(lambda: _forward(_fwd_setup_inputs(0)))
    return out.shape, out.dtype

N_MICROBATCH = 1
ADAM_LR = 0.001
ADAM_B1 = 0.9
ADAM_B2 = 0.999
ADAM_EPS = 1e-08
ADAM_WD = 0.01
ADAM_STEP = 10
PER_EXAMPLE_BATCH_AXIS = {'x': 0, 'c': 0, 'ctx': 0, 'loss_target': 0}
SHARED_INPUTS = []
_WEIGHT_DTYPES = {'c_ctx': _jnp.float32, 'mod_w': _jnp.float32, 'mod_b': _jnp.float32, 'ffn_w_in': _jnp.float32, 'ffn_w_out': _jnp.float32, 'w_in': _jnp.float32, 'w_out': _jnp.float32, 'a_conv_w': _jnp.float32, 'a_conv_b': _jnp.float32, 'a_dt_bias': _jnp.float32, 'a_log': _jnp.float32, 'a_d': _jnp.float32, 'a_norm_w': _jnp.float32, 's5_lam_re': _jnp.float32, 's5_lam_im': _jnp.float32, 's5_log_step': _jnp.float32, 's5_b_re': _jnp.float32, 's5_b_im': _jnp.float32, 's5_c_re': _jnp.float32, 's5_c_im': _jnp.float32, 's5_d': _jnp.float32, 's5_glu_w': _jnp.float32, 's5_glu_b': _jnp.float32, 'hg_lb_logits': _jnp.float32, 'hg_norm_w': _jnp.float32, 'final_norm_w': _jnp.float32}
MOMENT_SCALE = {'c_ctx': 1.630211e-02, 'mod_w': 7.802807e-02, 'mod_b': 1.306243e-01, 'ffn_w_in': 4.634071e-02, 'ffn_w_out': 7.593358e-02, 'w_in': 1.077990e-01, 'w_out': 1.136084e-01, 'a_conv_w': 1.281480e-01, 'a_conv_b': 1.546843e-01, 'a_dt_bias': 1.419068e-01, 'a_log': 9.541766e-01, 'a_d': 6.068323e-01, 'a_norm_w': 1.639538e-01, 's5_lam_re': 7.917223e-03, 's5_lam_im': 9.852266e-03, 's5_log_step': 2.233444e+00, 's5_b_re': 3.707198e-03, 's5_b_im': 4.536951e-03, 's5_c_re': 8.454905e-03, 's5_c_im': 8.686113e-03, 's5_d': 7.099687e-02, 's5_glu_w': 2.778775e-02, 's5_glu_b': 2.703917e-02, 'hg_lb_logits': 3.811888e-03, 'hg_norm_w': 1.121288e-01, 'final_norm_w': 6.511725e+01}


def _to_microbatches(a, axis):
    t = _jnp.moveaxis(a, axis, 0)
    t = t.reshape((N_MICROBATCH, t.shape[0] // N_MICROBATCH) + t.shape[1:])
    return _jnp.moveaxis(t, 1, axis + 1)


def setup_inputs(seed: int = 0) -> dict:
    inp = _fwd_setup_inputs(seed)
    key = _jax.random.fold_in(_jax.random.key(seed), 7919)
    shape, _ = _output_shape()
    out = dict(inp)
    out["loss_target"] = _jax.random.normal(_jax.random.fold_in(key, 0), shape, _jnp.float32)
    for i, name in enumerate(TWIN_WEIGHTS):
        w = inp[name].astype(_jnp.float32)
        if MOMENT_SCALE is None:
            s = _jnp.sqrt(_jnp.mean(_jnp.square(w)) + 1e-30)
        else:
            s = MOMENT_SCALE[name]
        km, kv = _jax.random.split(_jax.random.fold_in(key, i + 1))
        out[name] = w
        out["m_" + name] = s * _jax.random.normal(km, w.shape, _jnp.float32)
        out["v_" + name] = (s * s) * _jax.random.uniform(kv, w.shape, _jnp.float32, 0.5, 1.5)
    if N_MICROBATCH > 1:
        for name, axis in PER_EXAMPLE_BATCH_AXIS.items():
            out[name] = _to_microbatches(out[name], axis)
    return {'x': out['x'], 'c': out['c'], 'ctx': out['ctx'], 'c_ctx': out['c_ctx'], 'mod_w': out['mod_w'], 'mod_b': out['mod_b'], 'ffn_w_in': out['ffn_w_in'], 'ffn_w_out': out['ffn_w_out'], 'w_in': out['w_in'], 'w_out': out['w_out'], 'a_conv_w': out['a_conv_w'], 'a_conv_b': out['a_conv_b'], 'a_dt_bias': out['a_dt_bias'], 'a_log': out['a_log'], 'a_d': out['a_d'], 'a_norm_w': out['a_norm_w'], 's5_lam_re': out['s5_lam_re'], 's5_lam_im': out['s5_lam_im'], 's5_log_step': out['s5_log_step'], 's5_b_re': out['s5_b_re'], 's5_b_im': out['s5_b_im'], 's5_c_re': out['s5_c_re'], 's5_c_im': out['s5_c_im'], 's5_d': out['s5_d'], 's5_glu_w': out['s5_glu_w'], 's5_glu_b': out['s5_glu_b'], 'hg_lb_logits': out['hg_lb_logits'], 'hg_norm_w': out['hg_norm_w'], 'final_norm_w': out['final_norm_w'], 'loss_target': out['loss_target'], 'm_c_ctx': out['m_c_ctx'], 'm_mod_w': out['m_mod_w'], 'm_mod_b': out['m_mod_b'], 'm_ffn_w_in': out['m_ffn_w_in'], 'm_ffn_w_out': out['m_ffn_w_out'], 'm_w_in': out['m_w_in'], 'm_w_out': out['m_w_out'], 'm_a_conv_w': out['m_a_conv_w'], 'm_a_conv_b': out['m_a_conv_b'], 'm_a_dt_bias': out['m_a_dt_bias'], 'm_a_log': out['m_a_log'], 'm_a_d': out['m_a_d'], 'm_a_norm_w': out['m_a_norm_w'], 'm_s5_lam_re': out['m_s5_lam_re'], 'm_s5_lam_im': out['m_s5_lam_im'], 'm_s5_log_step': out['m_s5_log_step'], 'm_s5_b_re': out['m_s5_b_re'], 'm_s5_b_im': out['m_s5_b_im'], 'm_s5_c_re': out['m_s5_c_re'], 'm_s5_c_im': out['m_s5_c_im'], 'm_s5_d': out['m_s5_d'], 'm_s5_glu_w': out['m_s5_glu_w'], 'm_s5_glu_b': out['m_s5_glu_b'], 'm_hg_lb_logits': out['m_hg_lb_logits'], 'm_hg_norm_w': out['m_hg_norm_w'], 'm_final_norm_w': out['m_final_norm_w'], 'v_c_ctx': out['v_c_ctx'], 'v_mod_w': out['v_mod_w'], 'v_mod_b': out['v_mod_b'], 'v_ffn_w_in': out['v_ffn_w_in'], 'v_ffn_w_out': out['v_ffn_w_out'], 'v_w_in': out['v_w_in'], 'v_w_out': out['v_w_out'], 'v_a_conv_w': out['v_a_conv_w'], 'v_a_conv_b': out['v_a_conv_b'], 'v_a_dt_bias': out['v_a_dt_bias'], 'v_a_log': out['v_a_log'], 'v_a_d': out['v_a_d'], 'v_a_norm_w': out['v_a_norm_w'], 'v_s5_lam_re': out['v_s5_lam_re'], 'v_s5_lam_im': out['v_s5_lam_im'], 'v_s5_log_step': out['v_s5_log_step'], 'v_s5_b_re': out['v_s5_b_re'], 'v_s5_b_im': out['v_s5_b_im'], 'v_s5_c_re': out['v_s5_c_re'], 'v_s5_c_im': out['v_s5_c_im'], 'v_s5_d': out['v_s5_d'], 'v_s5_glu_w': out['v_s5_glu_w'], 'v_s5_glu_b': out['v_s5_glu_b'], 'v_hg_lb_logits': out['v_hg_lb_logits'], 'v_hg_norm_w': out['v_hg_norm_w'], 'v_final_norm_w': out['v_final_norm_w']}


def _loss(weights, diff, rest, loss_target):
    with _jax.named_scope("forward"):
        args = {**rest, TWIN_DIFF_INPUT: diff, **{k: w.astype(_WEIGHT_DTYPES[k]) for k, w in weights.items()}}
        y = _forward(args)
    with _jax.named_scope("loss_head"):
        err = _jnp.square(y.astype(_jnp.float32) - loss_target)
        return 0.5 * _jnp.sum(_jnp.mean(err, axis=-1)) if err.ndim else 0.5 * err


def _adamw(w, g, m, v):
    m = ADAM_B1 * m + (1.0 - ADAM_B1) * g
    v = ADAM_B2 * v + (1.0 - ADAM_B2) * _jnp.square(g)
    m_hat = m / (1.0 - ADAM_B1 ** ADAM_STEP)
    v_hat = v / (1.0 - ADAM_B2 ** ADAM_STEP)
    delta = -ADAM_LR * (m_hat / (_jnp.sqrt(v_hat) + ADAM_EPS) + ADAM_WD * w)
    return delta, m, v


def reference(x, c, ctx, c_ctx, mod_w, mod_b, ffn_w_in, ffn_w_out, w_in, w_out, a_conv_w, a_conv_b, a_dt_bias, a_log, a_d, a_norm_w, s5_lam_re, s5_lam_im, s5_log_step, s5_b_re, s5_b_im, s5_c_re, s5_c_im, s5_d, s5_glu_w, s5_glu_b, hg_lb_logits, hg_norm_w, final_norm_w, loss_target, m_c_ctx, m_mod_w, m_mod_b, m_ffn_w_in, m_ffn_w_out, m_w_in, m_w_out, m_a_conv_w, m_a_conv_b, m_a_dt_bias, m_a_log, m_a_d, m_a_norm_w, m_s5_lam_re, m_s5_lam_im, m_s5_log_step, m_s5_b_re, m_s5_b_im, m_s5_c_re, m_s5_c_im, m_s5_d, m_s5_glu_w, m_s5_glu_b, m_hg_lb_logits, m_hg_norm_w, m_final_norm_w, v_c_ctx, v_mod_w, v_mod_b, v_ffn_w_in, v_ffn_w_out, v_w_in, v_w_out, v_a_conv_w, v_a_conv_b, v_a_dt_bias, v_a_log, v_a_d, v_a_norm_w, v_s5_lam_re, v_s5_lam_im, v_s5_log_step, v_s5_b_re, v_s5_b_im, v_s5_c_re, v_s5_c_im, v_s5_d, v_s5_glu_w, v_s5_glu_b, v_hg_lb_logits, v_hg_norm_w, v_final_norm_w):
    given = dict(x=x, c=c, ctx=ctx, c_ctx=c_ctx, mod_w=mod_w, mod_b=mod_b, ffn_w_in=ffn_w_in, ffn_w_out=ffn_w_out, w_in=w_in, w_out=w_out, a_conv_w=a_conv_w, a_conv_b=a_conv_b, a_dt_bias=a_dt_bias, a_log=a_log, a_d=a_d, a_norm_w=a_norm_w, s5_lam_re=s5_lam_re, s5_lam_im=s5_lam_im, s5_log_step=s5_log_step, s5_b_re=s5_b_re, s5_b_im=s5_b_im, s5_c_re=s5_c_re, s5_c_im=s5_c_im, s5_d=s5_d, s5_glu_w=s5_glu_w, s5_glu_b=s5_glu_b, hg_lb_logits=hg_lb_logits, hg_norm_w=hg_norm_w, final_norm_w=final_norm_w, loss_target=loss_target, m_c_ctx=m_c_ctx, m_mod_w=m_mod_w, m_mod_b=m_mod_b, m_ffn_w_in=m_ffn_w_in, m_ffn_w_out=m_ffn_w_out, m_w_in=m_w_in, m_w_out=m_w_out, m_a_conv_w=m_a_conv_w, m_a_conv_b=m_a_conv_b, m_a_dt_bias=m_a_dt_bias, m_a_log=m_a_log, m_a_d=m_a_d, m_a_norm_w=m_a_norm_w, m_s5_lam_re=m_s5_lam_re, m_s5_lam_im=m_s5_lam_im, m_s5_log_step=m_s5_log_step, m_s5_b_re=m_s5_b_re, m_s5_b_im=m_s5_b_im, m_s5_c_re=m_s5_c_re, m_s5_c_im=m_s5_c_im, m_s5_d=m_s5_d, m_s5_glu_w=m_s5_glu_w, m_s5_glu_b=m_s5_glu_b, m_hg_lb_logits=m_hg_lb_logits, m_hg_norm_w=m_hg_norm_w, m_final_norm_w=m_final_norm_w, v_c_ctx=v_c_ctx, v_mod_w=v_mod_w, v_mod_b=v_mod_b, v_ffn_w_in=v_ffn_w_in, v_ffn_w_out=v_ffn_w_out, v_w_in=v_w_in, v_w_out=v_w_out, v_a_conv_w=v_a_conv_w, v_a_conv_b=v_a_conv_b, v_a_dt_bias=v_a_dt_bias, v_a_log=v_a_log, v_a_d=v_a_d, v_a_norm_w=v_a_norm_w, v_s5_lam_re=v_s5_lam_re, v_s5_lam_im=v_s5_lam_im, v_s5_log_step=v_s5_log_step, v_s5_b_re=v_s5_b_re, v_s5_b_im=v_s5_b_im, v_s5_c_re=v_s5_c_re, v_s5_c_im=v_s5_c_im, v_s5_d=v_s5_d, v_s5_glu_w=v_s5_glu_w, v_s5_glu_b=v_s5_glu_b, v_hg_lb_logits=v_hg_lb_logits, v_hg_norm_w=v_hg_norm_w, v_final_norm_w=v_final_norm_w)
    weights = {n: given[n] for n in TWIN_WEIGHTS}
    shared = {n: given[n] for n in SHARED_INPUTS}
    per_example = {n: given[n] for n in ['x', 'c', 'ctx']}
    grad_fn = _jax.value_and_grad(_loss, argnums=(0, 1))

    def one_microbatch(ex, loss_target):
        ex = dict(ex)
        diff = ex.pop(TWIN_DIFF_INPUT)
        return grad_fn(weights, diff, {**shared, **ex}, loss_target)

    if N_MICROBATCH == 1:
        loss, (grad_w, grad_x) = one_microbatch(per_example, given["loss_target"])
    else:
        def body(carry, xs):
            loss_sum, grad_sum = carry
            l_k, (gw_k, gx_k) = one_microbatch(xs[0], xs[1])
            with _jax.named_scope("update"):
                return (loss_sum + l_k, _jax.tree.map(_jnp.add, grad_sum, gw_k)), gx_k

        init = (_jnp.zeros((), _jnp.float32), _jax.tree.map(_jnp.zeros_like, weights))
        (loss, grad_w), grad_x = _jax.lax.scan(body, init, (per_example, given["loss_target"]))
    with _jax.named_scope("update"):
        delta_w, new_m, new_v = {}, {}, {}
        for n in TWIN_WEIGHTS:
            delta_w[n], new_m[n], new_v[n] = _adamw(weights[n], grad_w[n], given["m_" + n], given["v_" + n])
    return (loss, grad_x, *[grad_w[n] for n in TWIN_WEIGHTS], *[delta_w[n] for n in TWIN_WEIGHTS],
            *[new_m[n] for n in TWIN_WEIGHTS], *[new_v[n] for n in TWIN_WEIGHTS])
```

```python
import functools
import math

import numpy as np
import jax
import jax.numpy as jnp
from jax import lax
from jax.experimental import pallas as pl
from jax.experimental.pallas import tpu as pltpu

F32, BF16 = jnp.float32, jnp.bfloat16
EPS = 1e-6
N_MOD = 9
D_FF = 2816
A_INNER, A_HEADS, A_GROUPS, A_STATE, A_CONV, A_CONV_DIM = 512, 8, 2, 64, 5, 768
A_COLS = A_INNER + A_CONV_DIM + 2 * A_HEADS
B_WIDTH, B_GROUP, B_NGROUPS, B_STATE = 256, 16, 16, 64
C_WIDTH, C_HEADS, C_KEY = 256, 4, 64
GRID_W = 64
IN_COLS = A_COLS + B_WIDTH + 5 * C_WIDTH
IN_COLS_PAD = 3072
ADAM_LR, ADAM_B1, ADAM_B2, ADAM_EPS, ADAM_WD, ADAM_STEP = 0.001, 0.9, 0.999, 1e-08, 0.01, 10

ROW_TILE = 256
SSD_Q, S5_Q, HG_Q = 128, 64, 64
HG_BAND = 8
VMEM_LIMIT = 56 * 1024 * 1024

NT = ((1,), (1,))
TN = ((0,), (0,))


def _bdot(a, b, dims=((1,), (0,))):
    return lax.dot_general(a.astype(BF16), b.astype(BF16), (dims, ((), ())), preferred_element_type=F32)


def _split3(x):
    hi = x.astype(BF16)
    r = x - hi.astype(F32)
    mid = r.astype(BF16)
    lo = (r - mid.astype(F32)).astype(BF16)
    return hi, mid, lo


def _mask_lhs(m, x, dims):
    return sum(lax.dot_general(m, p, (dims, ((), ())), preferred_element_type=F32) for p in _split3(x))


def _mask_rhs(x, m, dims):
    return sum(lax.dot_general(p, m, (dims, ((), ())), preferred_element_type=F32) for p in _split3(x))


@jax.custom_vjp
def mdot(m, x):
    return _mask_lhs(m, x, ((1,), (0,)))


def _mdot_fwd(m, x):
    return mdot(m, x), m


def _mdot_bwd(m, g):
    return jnp.zeros_like(m), _mask_lhs(m, g, TN)


mdot.defvjp(_mdot_fwd, _mdot_bwd)


@jax.custom_vjp
def xdot(x, m):
    return _mask_rhs(x, m, ((1,), (0,)))


def _xdot_fwd(x, m):
    return xdot(x, m), m


def _xdot_bwd(m, g):
    return _mask_rhs(g, m, NT), jnp.zeros_like(m)


xdot.defvjp(_xdot_fwd, _xdot_bwd)


@functools.partial(jax.custom_vjp, nondiff_argnums=(1,))
def _roll(x, s):
    return pltpu.roll(x, s, 0)


def _roll_fwd(x, s):
    return _roll(x, s), None


def _roll_bwd(s, _, g):
    return (_roll(g, x_rows(g) - s),)


def x_rows(x):
    return x.shape[0]


_roll.defvjp(_roll_fwd, _roll_bwd)


def sroll(x, s):
    s = s % x.shape[0]
    return x if s == 0 else _roll(x, s)


def _softplus(x):
    return jnp.maximum(x, 0.0) + jnp.log1p(jnp.exp(jnp.minimum(x, -x)))


def _silu(x):
    return x * jax.nn.sigmoid(x)


def _rms(x):
    return x * lax.rsqrt(jnp.mean(x * x, axis=-1, keepdims=True) + EPS)


def _pick(n, cands):
    for c in cands:
        if n % c == 0:
            return c
    return n


def _cp(*sem):
    return pltpu.CompilerParams(dimension_semantics=sem, vmem_limit_bytes=VMEM_LIMIT)


def _mm(a, b, ta, tb, out_dtype, name):
    M, K = (a.shape[1], a.shape[0]) if ta else a.shape
    N = b.shape[0] if tb else b.shape[1]
    tm = _pick(M, (512, 256, 128, 64, 32, 16, 8))
    tn = _pick(N, (512, 256, 128))
    tk = _pick(K, (512, 256, 128, 64, 32, 16, 8))
    nk = K // tk
    dims = ((0 if ta else 1,), (1 if tb else 0,))

    def body(a_ref, b_ref, o_ref, acc_ref):
        k = pl.program_id(2)

        @pl.when(k == 0)
        def _():
            acc_ref[...] = jnp.zeros_like(acc_ref)

        acc_ref[...] += _bdot(a_ref[...], b_ref[...], dims)

        @pl.when(k == nk - 1)
        def _():
            o_ref[...] = acc_ref[...].astype(o_ref.dtype)

    a_spec = pl.BlockSpec((tk, tm), lambda i, j, k: (k, i)) if ta else pl.BlockSpec((tm, tk), lambda i, j, k: (i, k))
    b_spec = pl.BlockSpec((tn, tk), lambda i, j, k: (j, k)) if tb else pl.BlockSpec((tk, tn), lambda i, j, k: (k, j))
    return pl.pallas_call(
        body, grid=(M // tm, N // tn, nk), in_specs=[a_spec, b_spec],
        out_specs=pl.BlockSpec((tm, tn), lambda i, j, k: (i, j)),
        out_shape=jax.ShapeDtypeStruct((M, N), out_dtype),
        scratch_shapes=[pltpu.VMEM((tm, tn), F32)], name=name,
        compiler_params=_cp("parallel", "parallel", "arbitrary"))(a, b)


def linear(x, w16, wslot, name):
    @jax.custom_vjp
    def f(x, w16, wslot):
        return _mm(x, w16, False, False, F32, name + "_fwd")

    def fwd(x, w16, wslot):
        return f(x, w16, wslot), (x, w16)

    def bwd(res, dy):
        x, w16 = res
        dx = _mm(dy, w16, False, True, x.dtype, name + "_dx")
        dw = _mm(x, dy, True, False, F32, name + "_dw")
        return dx, jnp.zeros_like(w16), dw

    f.defvjp(fwd, bwd)
    return f(x, w16, wslot)


def rowwise(f, name, rows, mods, params, outs, n_ctx_blocks, tl=ROW_TILE, row_diff=None):
    rows, mods, params = tuple(rows), tuple(mods), tuple(params)
    nr, nm, npar, no = len(rows), len(mods), len(params), len(outs)
    B, L = rows[0].shape[:2]
    nblk = L // tl
    row_diff = tuple(row_diff) if row_diff is not None else (True,) * nr
    out_dtypes = [dt for _, dt in outs]

    def fcast(r, m, p):
        return tuple(o.astype(dt) for o, dt in zip(f(r, m, p), out_dtypes))

    def seg(i):
        return (i >= n_ctx_blocks).astype(jnp.int32) if n_ctx_blocks else 0

    def specs():
        row_specs = [pl.BlockSpec((1, tl, r.shape[2]), lambda b, i: (b, i, 0)) for r in rows]
        mod_specs = [pl.BlockSpec((1, 1, 1, m.shape[3]), lambda b, i: (b, seg(i), 0, 0)) for m in mods]
        par_specs = [pl.BlockSpec(p.shape, lambda b, i: (0, 0)) for p in params]
        out_specs = [pl.BlockSpec((1, tl, w), lambda b, i: (b, i, 0)) for w, _ in outs]
        return row_specs, mod_specs, par_specs, out_specs

    def load(ins):
        r = tuple(x[0] for x in ins[:nr])
        m = tuple(x[0, 0] for x in ins[nr:nr + nm])
        p = tuple(x[...] for x in ins[nr + nm:])
        return r, m, p

    def fwd_call(rows, mods, params):
        def body(*refs):
            r, m, p = load(refs[:nr + nm + npar])
            for o_ref, o in zip(refs[nr + nm + npar:], fcast(r, m, p)):
                o_ref[0] = o

        rs, ms, ps, os_ = specs()
        return pl.pallas_call(
            body, grid=(B, nblk), in_specs=rs + ms + ps, out_specs=os_,
            out_shape=[jax.ShapeDtypeStruct((B, L, w), dt) for w, dt in outs],
            name=name + "_fwd", compiler_params=_cp("parallel", "parallel"))(*rows, *mods, *params)

    def bwd_call(rows, mods, params, douts):
        didx = [j for j in range(nr) if row_diff[j]]

        def body(*refs):
            n_in = nr + nm + npar
            r, m, p = load(refs[:n_in])
            dins = refs[n_in:n_in + no]
            rest = refs[n_in + no:]
            dr_refs, dm_refs, dp_refs = rest[:len(didx)], rest[len(didx):len(didx) + nm], rest[len(didx) + nm:]
            _, vjp = jax.vjp(fcast, r, m, p)
            dr, dm, dp = vjp(tuple(d[0] for d in dins))
            for ref, j in zip(dr_refs, didx):
                ref[0] = dr[j].astype(ref.dtype)
            b, i = pl.program_id(0), pl.program_id(1)
            first_m = (i == 0) | (i == n_ctx_blocks) if n_ctx_blocks else (i == 0)
            first_p = (b == 0) & (i == 0)
            for ref, g in zip(dm_refs, dm):
                @pl.when(first_m)
                def _(ref=ref, g=g):
                    ref[0, 0] = g

                @pl.when(jnp.logical_not(first_m))
                def _(ref=ref, g=g):
                    ref[0, 0] += g
            for ref, g in zip(dp_refs, dp):
                @pl.when(first_p)
                def _(ref=ref, g=g):
                    ref[...] = g

                @pl.when(jnp.logical_not(first_p))
                def _(ref=ref, g=g):
                    ref[...] += g

        rs, ms, ps, os_ = specs()
        out_shape = ([jax.ShapeDtypeStruct(rows[j].shape, rows[j].dtype) for j in didx]
                     + [jax.ShapeDtypeStruct(m.shape, F32) for m in mods]
                     + [jax.ShapeDtypeStruct(p.shape, F32) for p in params])
        res = pl.pallas_call(
            body, grid=(B, nblk), in_specs=rs + ms + ps + os_,
            out_specs=[rs[j] for j in didx] + ms + ps, out_shape=out_shape,
            name=name + "_bwd", compiler_params=_cp("arbitrary", "arbitrary"))(*rows, *mods, *params, *douts)
        dr = [None] * nr
        for j, g in zip(didx, res[:len(didx)]):
            dr[j] = g
        dr = tuple(g if g is not None else jnp.zeros_like(rows[j]) for j, g in enumerate(dr))
        return dr, tuple(res[len(didx):len(didx) + nm]), tuple(res[len(didx) + nm:])

    @jax.custom_vjp
    def op(rows, mods, params):
        return tuple(fwd_call(rows, mods, params))

    def op_fwd(rows, mods, params):
        return op(rows, mods, params), (rows, mods, params)

    def op_bwd(res, douts):
        return bwd_call(*res, tuple(douts))

    op.defvjp(op_fwd, op_bwd)
    return op(rows, mods, params)


def chunk_scan(step, name, rev, seqs, seq_t, params, consts, state_shapes, out_widths, Q, n_ctx):
    seqs, params, consts = tuple(seqs), tuple(params), tuple(consts)
    ns, npar, nc, nst, no = len(seqs), len(params), len(consts), len(state_shapes), len(out_widths)
    B = seqs[0].shape[0]
    L = seqs[0].shape[2] if seq_t[0] else seqs[0].shape[1]
    nck, ncc = L // Q, n_ctx // Q

    def chunk_of(k):
        if not rev:
            return k
        return jnp.where(k < ncc, ncc - 1 - k, nck + ncc - 1 - k)

    def specs(order):
        def seq_spec(s, t):
            if t:
                return pl.BlockSpec((1, s.shape[1], Q), lambda b, k: (b, 0, chunk_of(order(k))))
            return pl.BlockSpec((1, Q, s.shape[2]), lambda b, k: (b, chunk_of(order(k)), 0))

        seq_specs = [seq_spec(s, t) for s, t in zip(seqs, seq_t)]
        par_specs = [pl.BlockSpec(p.shape, lambda b, k: (0, 0)) for p in params]
        con_specs = [pl.BlockSpec(c.shape, lambda b, k, nd=c.ndim: (0,) * nd) for c in consts]
        out_specs = [pl.BlockSpec((1, Q, w), lambda b, k: (b, chunk_of(order(k)), 0)) for w in out_widths]
        sav_specs = [pl.BlockSpec((1, 1) + tuple(s), lambda b, k: (b, order(k), 0, 0)) for s in state_shapes]
        return seq_specs, par_specs, con_specs, out_specs, sav_specs

    def fwd_call(seqs, params):
        def body(*refs):
            seq_refs = refs[:ns]
            par_refs = refs[ns:ns + npar]
            con_refs = refs[ns + npar:ns + npar + nc]
            rest = refs[ns + npar + nc:]
            out_refs, sav_refs, st_refs = rest[:no], rest[no:no + nst], rest[no + nst:]

            @pl.when(pl.program_id(1) == 0)
            def _():
                for s in st_refs:
                    s[...] = jnp.zeros_like(s)

            s_in = tuple(s[...] for s in st_refs)
            for sv, s in zip(sav_refs, s_in):
                sv[0, 0] = s
            s_new, outs = step(s_in, tuple(r[0] for r in seq_refs), tuple(p[...] for p in par_refs),
                               tuple(c[...] for c in con_refs))
            for s_ref, s in zip(st_refs, s_new):
                s_ref[...] = s
            for o_ref, o in zip(out_refs, outs):
                o_ref[0] = o

        ss, ps, cs, os_, vs = specs(lambda k: k)
        res = pl.pallas_call(
            body, grid=(B, nck), in_specs=ss + ps + cs, out_specs=os_ + vs,
            out_shape=[jax.ShapeDtypeStruct((B, L, w), F32) for w in out_widths]
            + [jax.ShapeDtypeStruct((B, nck) + tuple(s), F32) for s in state_shapes],
            scratch_shapes=[pltpu.VMEM(tuple(s), F32) for s in state_shapes],
            name=name + "_fwd", compiler_params=_cp("parallel", "arbitrary"))(*seqs, *params, *consts)
        return tuple(res[:no]), tuple(res[no:])

    def bwd_call(seqs, params, saved, douts):
        def body(*refs):
            seq_refs = refs[:ns]
            par_refs = refs[ns:ns + npar]
            con_refs = refs[ns + npar:ns + npar + nc]
            rest = refs[ns + npar + nc:]
            sav_refs, dout_refs = rest[:nst], rest[nst:nst + no]
            rest = rest[nst + no:]
            dseq_refs, dpar_refs, dst_refs = rest[:ns], rest[ns:ns + npar], rest[ns + npar:]
            b, k = pl.program_id(0), pl.program_id(1)

            @pl.when(k == 0)
            def _():
                for s in dst_refs:
                    s[...] = jnp.zeros_like(s)

            cvals = tuple(c[...] for c in con_refs)
            _, vjp = jax.vjp(lambda s, x, p: step(s, x, p, cvals), tuple(s[0, 0] for s in sav_refs),
                             tuple(r[0] for r in seq_refs), tuple(p[...] for p in par_refs))
            ds, dx, dp = vjp((tuple(s[...] for s in dst_refs), tuple(d[0] for d in dout_refs)))
            for s_ref, s in zip(dst_refs, ds):
                s_ref[...] = s
            for x_ref, x in zip(dseq_refs, dx):
                x_ref[0] = x
            first = (b == 0) & (k == 0)
            for ref, g in zip(dpar_refs, dp):
                @pl.when(first)
                def _(ref=ref, g=g):
                    ref[...] = g

                @pl.when(jnp.logical_not(first))
                def _(ref=ref, g=g):
                    ref[...] += g

        ss, ps, cs, os_, vs = specs(lambda k: nck - 1 - k)
        res = pl.pallas_call(
            body, grid=(B, nck), in_specs=ss + ps + cs + vs + os_, out_specs=ss + ps,
            out_shape=[jax.ShapeDtypeStruct(s.shape, F32) for s in seqs]
            + [jax.ShapeDtypeStruct(p.shape, F32) for p in params],
            scratch_shapes=[pltpu.VMEM(tuple(s), F32) for s in state_shapes],
            name=name + "_bwd", compiler_params=_cp("arbitrary", "arbitrary"))(*seqs, *params, *consts, *saved, *douts)
        return tuple(res[:ns]), tuple(res[ns:])

    @jax.custom_vjp
    def op(seqs, params):
        return fwd_call(seqs, params)[0]

    def op_fwd(seqs, params):
        outs, saved = fwd_call(seqs, params)
        return outs, (seqs, params, saved)

    def op_bwd(res, douts):
        seqs, params, saved = res
        return bwd_call(seqs, params, saved, tuple(douts))

    op.defvjp(op_fwd, op_bwd)
    return op(seqs, params)


def _positions(Q, rev):
    t = np.arange(Q)
    return (Q - 1 - t) if rev else t


def _ssd_consts(rev):
    Q = SSD_Q
    pos = _positions(Q, rev)
    tri = pos[:, None] >= pos[None, :]
    head_of = np.arange(A_INNER) // (A_INNER // A_HEADS)
    expand = np.arange(A_HEADS)[:, None] == head_of[None, :]
    group_of_row = np.arange(A_GROUPS * A_STATE) // A_STATE
    block = group_of_row[:, None] == (head_of // (A_HEADS // A_GROUPS))[None, :]
    as_bf = lambda m: jnp.asarray(m.astype(np.float32), BF16)
    return (as_bf(tri), as_bf(tri.T), jnp.asarray(tri.astype(np.float32)), as_bf(expand),
            jnp.asarray(block.astype(np.float32)))


def _ssd_step(state, seqs, params, consts):
    (st,) = state
    xs, bm, cm, dtr, dtr_t = seqs
    bias, bias_t, alog, alog_t = params
    tri_b, tri_t_b, tri_f, expand, block = consts
    dt = _softplus(dtr + bias)
    dta = dt * (-jnp.exp(alog))
    dt_t = _softplus(dtr_t + bias_t)
    dta_t = dt_t * (-jnp.exp(alog_t))
    cum = mdot(tri_b, dta)
    cum_t = xdot(dta_t, tri_t_b)
    total = jnp.sum(dta, axis=0, keepdims=True)
    w_end = jnp.exp(total - cum) * dt
    ecum_x = xdot(jnp.exp(cum), expand)
    wend_x = xdot(w_end, expand)
    dec_x = jnp.exp(jnp.sum(xdot(dta, expand), axis=0, keepdims=True))
    y = _bdot(cm, st) * ecum_x
    st_new = st * dec_x + block * _bdot(bm, xs * wend_x, TN)
    lane_g = lax.broadcasted_iota(jnp.int32, (1, A_GROUPS * A_STATE), 1) // A_STATE
    lane_h = lax.broadcasted_iota(jnp.int32, (1, A_INNER), 1) // (A_INNER // A_HEADS)
    sub_h = lax.broadcasted_iota(jnp.int32, (A_HEADS, 1), 0)
    lane8 = lax.broadcasted_iota(jnp.int32, (1, A_HEADS), 1)
    for g in range(A_GROUPS):
        cb = _bdot(jnp.where(lane_g == g, cm, 0.0), bm, NT)
        for h in range(g * (A_HEADS // A_GROUPS), (g + 1) * (A_HEADS // A_GROUPS)):
            col = jnp.sum(jnp.where(lane8 == h, cum, 0.0), axis=1, keepdims=True)
            row = jnp.sum(jnp.where(sub_h == h, cum_t, 0.0), axis=0, keepdims=True)
            dt_row = jnp.sum(jnp.where(sub_h == h, dt_t, 0.0), axis=0, keepdims=True)
            decay = tri_f * jnp.exp(tri_f * (col - row))
            y = y + _bdot(cb * decay * dt_row, jnp.where(lane_h == h, xs, 0.0))
    return (st_new,), (y,)


def ssd_scan(name, rev, xs, bm, cm, dtr, dt_bias, a_log, n_ctx):
    seqs = (xs, bm, cm, dtr, jnp.swapaxes(dtr, 1, 2))
    params = (dt_bias.reshape(1, -1), dt_bias.reshape(-1, 1), a_log.reshape(1, -1), a_log.reshape(-1, 1))
    (y,) = chunk_scan(_ssd_step, name, rev, seqs, (False, False, False, False, True), params, _ssd_consts(rev),
                      [(A_GROUPS * A_STATE, A_INNER)], [A_INNER], SSD_Q, n_ctx)
    return y


def _s5_step_fn(rev):
    def step(state, seqs, params, consts):
        sr, si = state
        (u,) = seqs
        b_re, b_im, c_re, c_im, ar, ai = params
        (pos,) = consts
        Q = u.shape[0]
        first = (pos == 0.0).astype(F32)
        xr = _bdot(u, b_re) + first * (ar * sr - ai * si)
        xi = _bdot(u, b_im) + first * (ar * si + ai * sr)
        pr, pi = ar, ai
        k = 1
        while k < Q:
            keep = (pos >= float(k)).astype(F32)
            s = -k if rev else k
            xr_s, xi_s = sroll(xr, s) * keep, sroll(xi, s) * keep
            xr, xi = xr + pr * xr_s - pi * xi_s, xi + pr * xi_s + pi * xr_s
            pr, pi = pr * pr - pi * pi, 2.0 * pr * pi
            k *= 2
        last = (pos == float(Q - 1)).astype(F32)
        sr_new = jnp.sum(xr * last, axis=0, keepdims=True)
        si_new = jnp.sum(xi * last, axis=0, keepdims=True)
        y = _bdot(xr, c_re) - _bdot(xi, c_im)
        return (sr_new, si_new), (y,)

    return step


def _block_diag(t):
    G, a, b = t.shape
    eye = jnp.eye(G, dtype=t.dtype)
    return (t[:, :, None, :] * eye[:, None, :, None]).reshape(G * a, G * b)


def s5_scan(name, rev, u, lam_re, lam_im, log_step, b_re, b_im, c_re, c_im, n_ctx):
    step = jnp.exp(log_step)[:, None]
    mag = jnp.exp(lam_re * step)
    ar = mag * jnp.cos(lam_im * step)
    ai = mag * jnp.sin(lam_im * step)
    den = lam_re * lam_re + lam_im * lam_im
    nr = ar - 1.0
    kr = (nr * lam_re + ai * lam_im) / den
    ki = (ai * lam_re - nr * lam_im) / den
    br = kr[..., None] * b_re - ki[..., None] * b_im
    bi = kr[..., None] * b_im + ki[..., None] * b_re
    params = (_block_diag(jnp.swapaxes(br, 1, 2)), _block_diag(jnp.swapaxes(bi, 1, 2)),
              _block_diag(jnp.swapaxes(c_re, 1, 2)), _block_diag(jnp.swapaxes(c_im, 1, 2)),
              ar.reshape(1, -1), ai.reshape(1, -1))
    pos = jnp.asarray(_positions(S5_Q, rev).astype(np.float32).reshape(-1, 1))
    width = B_NGROUPS * B_STATE
    (y,) = chunk_scan(_s5_step_fn(rev), name, rev, (u,), (False,), params, (pos,),
                      [(1, width), (1, width)], [B_WIDTH], S5_Q, n_ctx)
    return y


def _hg_consts(rev):
    Q = HG_Q
    pos = _positions(Q, rev)
    pi, pj = pos[:, None], pos[None, :]
    mats = [pi >= pj, pj > pi]
    pairs = []
    s = HG_BAND
    while s < Q:
        second = (pos // s) % 2 == 1
        mid = (pos // (2 * s)) * 2 * s + s
        mats.append(second[:, None] & (pj >= mid[:, None]) & (pj <= pi))
        mats.append((~second)[:, None] & (pj > pi) & (pj < mid[:, None]))
        pairs.append(second[:, None] & (~second)[None, :] & ((pi // (2 * s)) == (pj // (2 * s))))
        s *= 2
    rows = [((pos // s_) % 2 == 1) for s_ in [HG_BAND * 2 ** n for n in range(len(pairs))]]
    band = [(pos % HG_BAND) >= d for d in range(HG_BAND)]
    mats_b = jnp.asarray(np.stack(mats).astype(np.float32), BF16)
    pairs_f = jnp.asarray(np.stack(pairs).astype(np.float32))
    rows_f = jnp.asarray(np.stack(rows).astype(np.float32)[:, :, None])
    band_f = jnp.asarray(np.stack(band).astype(np.float32)[:, :, None])
    return mats_b, pairs_f, rows_f, band_f


def _hg_step_fn(rev, n_levels):
    def step(state, seqs, params, consts):
        (st,) = state
        qr, zf, v = seqs
        (lb,) = params
        mats, pairs, rows, band = consts
        W = C_WIDTH
        same_head = (lax.broadcasted_iota(jnp.int32, (W, W), 0) // C_KEY
                     == lax.broadcasted_iota(jnp.int32, (W, W), 1) // C_KEY)
        lane_h = lax.broadcasted_iota(jnp.int32, (1, W), 1) // C_KEY
        q = _silu(qr)
        f = lb + (1.0 - lb) * jax.nn.sigmoid(zf)
        logf = jnp.log(f)
        kk = 1.0 - f
        qd = q * jnp.exp(mdot(mats[0], logf))
        w = kk * jnp.exp(mdot(mats[1], logf))
        total = jnp.sum(logf, axis=0, keepdims=True)
        o = _bdot(qd, st, NT)
        st_new = st * jnp.exp(total) + jnp.where(same_head, _bdot(v, w, TN), 0.0)
        scores = [None] * C_HEADS
        for n in range(n_levels):
            a = q * jnp.exp(mdot(mats[2 + 2 * n], logf)) * rows[n]
            bk = kk * jnp.exp(mdot(mats[3 + 2 * n], logf)) * (1.0 - rows[n])
            for h in range(C_HEADS):
                sc = pairs[n] * _bdot(jnp.where(lane_h == h, a, 0.0), bk, NT)
                scores[h] = sc if scores[h] is None else scores[h] + sc
        for h in range(C_HEADS):
            o = o + _bdot(scores[h], jnp.where(lane_h == h, v, 0.0))
        same_head_b = same_head.astype(BF16)
        e = jnp.zeros_like(logf)
        for d in range(HG_BAND):
            s = -d if rev else d
            if d > 0:
                e = e + sroll(logf, s + (1 if rev else -1))
            kd, vd = (kk, v) if d == 0 else (sroll(kk, s), sroll(v, s))
            p = q * kd * jnp.exp(e)
            o = o + band[d] * (_bdot(p, same_head_b) * vd)
        return (st_new,), (o,)

    return step


def hg_scan(name, rev, qr, zf, v, lower, n_ctx):
    consts = _hg_consts(rev)
    (o,) = chunk_scan(_hg_step_fn(rev, consts[1].shape[0]), name, rev, (qr, zf, v), (False,) * 3,
                      (lower.reshape(1, -1),), consts, [(C_WIDTH, C_WIDTH)], [C_WIDTH], HG_Q, n_ctx)
    return o


def conv_silu(x, w, b, n_ctx, name):
    B, L, C = x.shape
    TC = 128
    w8 = jnp.zeros((8, C), F32).at[:A_CONV].set(w)
    b2 = b.reshape(1, C)
    pad = A_CONV // 2

    def taps(v, sign):
        t = lax.broadcasted_iota(jnp.int32, v.shape, 0)
        out = []
        for k in range(A_CONV):
            s = sign * (k - pad)
            src = t + s
            ok = (src >= 0) & (src < L) & ((t >= n_ctx) == (src >= n_ctx))
            vs = v if s == 0 else pltpu.roll(v, (-s) % L, 0)
            out.append((k, jnp.where(ok, vs, 0.0)))
        return out

    def pre(x_ref, w_ref, b_ref):
        xv = x_ref[0]
        y = b_ref[...] + sum(xs * w_ref[pl.ds(k, 1), :] for k, xs in taps(xv, 1))
        return xv, y

    def fwd_body(x_ref, w_ref, b_ref, o_ref):
        _, y = pre(x_ref, w_ref, b_ref)
        o_ref[0] = _silu(y)

    def bwd_body(x_ref, w_ref, b_ref, g_ref, dx_ref, dw_ref, db_ref):
        xv, y = pre(x_ref, w_ref, b_ref)
        sg = jax.nn.sigmoid(y)
        dy = g_ref[0] * (sg + y * sg * (1.0 - sg))
        dx_ref[0] = sum(ds * w_ref[pl.ds(k, 1), :] for k, ds in taps(dy, -1))
        first = pl.program_id(1) == 0

        @pl.when(first)
        def _():
            dw_ref[...] = jnp.zeros_like(dw_ref)
            db_ref[...] = jnp.zeros_like(db_ref)

        for k, xs in taps(xv, 1):
            dw_ref[pl.ds(k, 1), :] += jnp.sum(dy * xs, axis=0, keepdims=True)
        db_ref[...] += jnp.sum(dy, axis=0, keepdims=True)

    x_spec = pl.BlockSpec((1, L, TC), lambda j, bb: (bb, 0, j))
    w_spec = pl.BlockSpec((8, TC), lambda j, bb: (0, j))
    b_spec = pl.BlockSpec((1, TC), lambda j, bb: (0, j))

    @jax.custom_vjp
    def op(x, w8, b2):
        return pl.pallas_call(fwd_body, grid=(C // TC, B), in_specs=[x_spec, w_spec, b_spec], out_specs=x_spec,
                              out_shape=jax.ShapeDtypeStruct(x.shape, F32), name=name + "_fwd",
                              compiler_params=_cp("parallel", "parallel"))(x, w8, b2)

    def op_fwd(x, w8, b2):
        return op(x, w8, b2), (x, w8, b2)

    def op_bwd(res, g):
        x, w8, b2 = res
        return tuple(pl.pallas_call(
            bwd_body, grid=(C // TC, B), in_specs=[x_spec, w_spec, b_spec, x_spec],
            out_specs=[x_spec, w_spec, b_spec],
            out_shape=[jax.ShapeDtypeStruct(x.shape, F32), jax.ShapeDtypeStruct(w8.shape, F32),
                       jax.ShapeDtypeStruct(b2.shape, F32)],
            name=name + "_bwd", compiler_params=_cp("parallel", "arbitrary"))(x, w8, b2, g))

    op.defvjp(op_fwd, op_bwd)
    return op(x, w8, b2)


def loss_head(h, target, norm_w, n_ctx):
    B, L, Dm = h.shape
    S = target.shape[1]
    tl = ROW_TILE
    skip = n_ctx // tl

    def body(h_ref, t_ref, w_ref, loss_ref, dh_ref, dw_ref):
        b, i = pl.program_id(0), pl.program_id(1)
        fn = lambda hv, wv: _rms(hv) * wv
        y, vjp = jax.vjp(fn, h_ref[0], w_ref[...])
        err = y - t_ref[0]
        dhv, dwv = vjp(err * (1.0 / Dm))
        dh_ref[0] = dhv
        part = 0.5 * jnp.sum(jnp.mean(err * err, axis=-1, keepdims=True), axis=0, keepdims=True)
        first = (b == 0) & (i == 0)

        @pl.when(first)
        def _():
            loss_ref[...] = jnp.zeros_like(loss_ref)
            dw_ref[...] = jnp.zeros_like(dw_ref)

        loss_ref[...] += jnp.broadcast_to(part, loss_ref.shape)
        dw_ref[...] += dwv

    loss, dh, dw = pl.pallas_call(
        body, grid=(B, S // tl),
        in_specs=[pl.BlockSpec((1, tl, Dm), lambda b, i: (b, i + skip, 0)),
                  pl.BlockSpec((1, tl, Dm), lambda b, i: (b, i, 0)),
                  pl.BlockSpec((1, Dm), lambda b, i: (0, 0))],
        out_specs=[pl.BlockSpec((8, 128), lambda b, i: (0, 0)),
                   pl.BlockSpec((1, tl, Dm), lambda b, i: (b, i, 0)),
                   pl.BlockSpec((1, Dm), lambda b, i: (0, 0))],
        out_shape=[jax.ShapeDtypeStruct((8, 128), F32), jax.ShapeDtypeStruct((B, S, Dm), F32),
                   jax.ShapeDtypeStruct((1, Dm), F32)],
        name="loss_head", compiler_params=_cp("arbitrary", "arbitrary"))(h, target, norm_w.reshape(1, Dm))
    dh_full = jnp.concatenate([jnp.zeros((B, n_ctx, Dm), F32), dh], axis=1)
    return loss[0, 0], dh_full, dw.reshape(Dm)


def _modulate(h, shift, scale):
    return _rms(h) * (1.0 + scale) + shift


def _f_mod(r, m, p):
    return (_modulate(r[0], m[0], m[1]),)


def _f_resid_mod(coef):
    def f(r, m, p):
        h2 = r[0] + coef * m[0] * r[1]
        return h2, _modulate(h2, m[1], m[2])
    return f


def _f_resid(coef):
    def f(r, m, p):
        return (r[0] + coef * m[0] * r[1],)
    return f


def _f_swiglu(r, m, p):
    return (_silu(r[0][:, :D_FF]) * r[0][:, D_FF:],)


def _f_ssd_post(r, m, p):
    y_f, y_b, xs, z = r
    d_skip, norm_w = p
    y = (y_f + y_b + d_skip * xs) * _silu(z)
    return (_rms(y) * norm_w,)


def _f_s5_post(r, m, p):
    y_f, y_b, u = r
    d_skip, glu_w, glu_b = p
    y = jax.nn.gelu(y_f + y_b + d_skip * u)
    return (y * jax.nn.sigmoid(_bdot(y, glu_w) + glu_b),)


def _f_hg_post(r, m, p):
    o_f, o_b, g = r
    (norm_w,) = p
    W = C_WIDTH
    same_head = (lax.broadcasted_iota(jnp.int32, (W, W), 0) // C_KEY
                 == lax.broadcasted_iota(jnp.int32, (W, W), 1) // C_KEY).astype(BF16)
    o = o_f + o_b
    ms = xdot(o * o, same_head) * (1.0 / C_KEY)
    return (o * lax.rsqrt(ms + EPS) * norm_w * _silu(g),)


def _pad_w_in(w):
    dt0 = A_INNER + A_CONV_DIM
    zeros = jnp.zeros((w.shape[0], IN_COLS_PAD - IN_COLS), w.dtype)
    return jnp.concatenate([w[:, :dt0], w[:, A_COLS:], w[:, dt0:A_COLS], zeros], axis=1)


def _to_columns(t, n_ctx):
    B, L, W = t.shape
    rows = (L - n_ctx) // GRID_W
    lat = t[:, n_ctx:].reshape(B, rows, GRID_W, W).transpose(0, 2, 1, 3).reshape(B, L - n_ctx, W)
    return jnp.concatenate([t[:, :n_ctx], lat], axis=1)


def _to_raster(t, n_ctx):
    B, L, W = t.shape
    rows = (L - n_ctx) // GRID_W
    lat = t[:, n_ctx:].reshape(B, GRID_W, rows, W).transpose(0, 2, 1, 3).reshape(B, L - n_ctx, W)
    return jnp.concatenate([t[:, :n_ctx], lat], axis=1)


def _forward(h0, c, big, P, n_ctx):
    B, L, Dm = h0.shape
    depth = big["mod_w"].shape[0]
    ncb = n_ctx // ROW_TILE
    p_lb = jax.nn.softmax(P["hg_lb_logits"], axis=0)
    lower_bounds = jnp.cumsum(p_lb, axis=0) - p_lb[:1]
    cc = jnp.concatenate([c, P["c_ctx"][None], jnp.zeros((8 - B - 1, Dm), F32)], axis=0)
    cc = _silu(cc)

    def mods_of(l):
        m = linear(cc, big["mod_w"][l], P["mod_w"][l], f"mod{l}") + P["mod_b"][l]
        m = m.reshape(8, N_MOD, Dm)
        seg = jnp.stack([jnp.broadcast_to(m[B], (B, N_MOD, Dm)), m[:B]], axis=1)
        return [seg[:, :, j:j + 1, :] for j in range(N_MOD)]

    def ffn(u, l, j):
        pre = linear(u.reshape(B * L, Dm), big["ffn_w_in"][l, j], P["ffn_w_in"][l, j], f"ffn_in{l}{j}")
        (act,) = rowwise(_f_swiglu, f"swiglu{l}{j}", [pre.reshape(B, L, 2 * D_FF)], [], [], [(D_FF, BF16)], ncb,
                         tl=128)
        out = linear(act.reshape(B * L, D_FF), big["ffn_w_out"][l, j], P["ffn_w_out"][l, j], f"ffn_out{l}{j}")
        return out.reshape(B, L, Dm)

    h = h0
    mods = mods_of(0)
    (u,) = rowwise(_f_mod, "mod_first", [h], [mods[0], mods[1]], [], [(Dm, BF16)], ncb)
    for l in range(depth):
        col_major = l % 2 == 1
        o = ffn(u, l, 0)
        h, u = rowwise(_f_resid_mod(0.5), f"resid_a{l}", [h, o], [mods[2], mods[3], mods[4]], [],
                       [(Dm, F32), (Dm, BF16)], ncb)
        if col_major:
            u = _to_columns(u, n_ctx)
        pre = linear(u.reshape(B * L, Dm), _pad_w_in(big["w_in"][l]), _pad_w_in(P["w_in"][l]), f"w_in{l}")
        pre = pre.reshape(B, L, IN_COLS_PAD)
        o0 = 0
        z, o0 = pre[..., o0:o0 + A_INNER], o0 + A_INNER
        xbc, o0 = pre[..., o0:o0 + A_CONV_DIM], o0 + A_CONV_DIM
        pb, o0 = pre[..., o0:o0 + B_WIDTH], o0 + B_WIDTH
        pc, o0 = pre[..., o0:o0 + 5 * C_WIDTH], o0 + 5 * C_WIDTH
        dtr = pre[..., o0:o0 + 2 * A_HEADS]
        xbc = conv_silu(xbc, P["a_conv_w"][l], P["a_conv_b"][l], n_ctx, f"conv{l}")
        xs, bm, cm = xbc[..., :A_INNER], xbc[..., A_INNER:A_INNER + 128], xbc[..., A_INNER + 128:]
        ya_dir = [ssd_scan(f"ssd{l}{d}", bool(d), xs, bm, cm, dtr[..., d * A_HEADS:(d + 1) * A_HEADS],
                           P["a_dt_bias"][l, d], P["a_log"][l, d], n_ctx) for d in range(2)]
        d_skip = jnp.repeat(P["a_d"][l], A_INNER // A_HEADS).reshape(1, A_INNER)
        (ya,) = rowwise(_f_ssd_post, f"ssd_post{l}", [ya_dir[0], ya_dir[1], xs, z], [],
                        [d_skip, P["a_norm_w"][l].reshape(1, -1)], [(A_INNER, BF16)], ncb)
        yb_dir = [s5_scan(f"s5{l}{d}", bool(d), pb, P["s5_lam_re"][l, d], P["s5_lam_im"][l, d],
                          P["s5_log_step"][l, d], P["s5_b_re"][l, d], P["s5_b_im"][l, d], P["s5_c_re"][l, d],
                          P["s5_c_im"][l, d], n_ctx) for d in range(2)]
        (yb,) = rowwise(_f_s5_post, f"s5_post{l}", [yb_dir[0], yb_dir[1], pb], [],
                        [P["s5_d"][l].reshape(1, -1), P["s5_glu_w"][l], P["s5_glu_b"][l].reshape(1, -1)],
                        [(B_WIDTH, BF16)], ncb)
        qr, f_raw, vi, gate = (pc[..., :C_WIDTH], pc[..., C_WIDTH:3 * C_WIDTH], pc[..., 3 * C_WIDTH:4 * C_WIDTH],
                               pc[..., 4 * C_WIDTH:])
        yc_dir = [hg_scan(f"hg{l}{d}", bool(d), qr, f_raw[..., d * C_WIDTH:(d + 1) * C_WIDTH], vi,
                          lower_bounds[l, d], n_ctx) for d in range(2)]
        (yc,) = rowwise(_f_hg_post, f"hg_post{l}", [yc_dir[0], yc_dir[1], gate], [],
                        [P["hg_norm_w"][l].reshape(1, -1)], [(C_WIDTH, BF16)], ncb)
        mix = jnp.concatenate([ya, yb, yc], axis=-1)
        y = linear(mix.reshape(B * L, Dm), big["w_out"][l], P["w_out"][l], f"w_out{l}").reshape(B, L, Dm)
        if col_major:
            y = _to_raster(y, n_ctx)
        h, u = rowwise(_f_resid_mod(1.0), f"resid_b{l}", [h, y], [mods[5], mods[6], mods[7]], [],
                       [(Dm, F32), (Dm, BF16)], ncb)
        o = ffn(u, l, 1)
        gate8 = mods[8]
        if l + 1 < depth:
            mods = mods_of(l + 1)
            h, u = rowwise(_f_resid_mod(0.5), f"resid_c{l}", [h, o], [gate8, mods[0], mods[1]], [],
                           [(Dm, F32), (Dm, BF16)], ncb)
        else:
            (h,) = rowwise(_f_resid(0.5), f"resid_c{l}", [h, o], [gate8], [], [(Dm, F32)], ncb)
    return h


BIG = ("mod_w", "ffn_w_in", "ffn_w_out", "w_in", "w_out")


def local_step(x, c, ctx, target, big, small):
    n_ctx = ctx.shape[1]
    h0 = jnp.concatenate([ctx, x], axis=1)
    P = dict(small)
    for k in BIG:
        P[k] = jnp.zeros(big[k].shape, F32)
    h, vjp = jax.vjp(lambda h0, P: _forward(h0, c, big, P, n_ctx), h0, P)
    loss, dh, d_final = loss_head(h, target, small["final_norm_w"], n_ctx)
    dh0, grads = vjp(dh)
    grads = dict(grads)
    grads["final_norm_w"] = grads["final_norm_w"] + d_final
    return loss, dh0[:, n_ctx:], grads


MESH = pl.DeviceIdType.MESH
LANES = 1024
BIG_ROWS = 512
N_CHIPS = 4
ANY = pl.BlockSpec(memory_space=pl.ANY)


def _place():
    x, y, c = lax.axis_index("x"), lax.axis_index("y"), lax.axis_index("c")
    return x, y, c, 2 * x + y


def _other_chips(x, y):
    return [(x ^ kx, y ^ ky, 2 * (x ^ kx) + (y ^ ky)) for kx, ky in ((0, 1), (1, 0), (1, 1))]


def all_gather_shards(shard, name):
    def body(x_ref, out_ref, send_sems, recv_sems, local_sem):
        x, y, c, q = _place()
        sibling = (x, y, 1 - c)
        chips = _other_chips(x, y)

        def copy(k, src, dst, to):
            return pltpu.make_async_remote_copy(src_ref=src, dst_ref=dst, send_sem=send_sems.at[k],
                                                recv_sem=recv_sems.at[k], device_id=to, device_id_type=MESH)

        mine = pltpu.make_async_copy(x_ref, out_ref.at[q], local_sem)
        mine.start()
        first = [copy(k, x_ref.at[c], out_ref.at[q, c], (px, py, c)) for k, (px, py, _) in enumerate(chips)]
        for cp in first:
            cp.start()
        passed = [copy(3 + k, out_ref.at[pq, c], out_ref.at[pq, c], sibling) for k, (_, _, pq) in enumerate(chips)]
        for k, (_, _, pq) in enumerate(chips):
            copy(k, x_ref.at[c], out_ref.at[pq, c], sibling).wait_recv()
            passed[k].start()
        for k, (_, _, pq) in enumerate(chips):
            copy(3 + k, x_ref.at[c], out_ref.at[pq, 1 - c], sibling).wait_recv()
        for cp in first + passed:
            cp.wait_send()
        mine.wait()

    return pl.pallas_call(
        body, out_shape=jax.ShapeDtypeStruct((N_CHIPS,) + shard.shape, shard.dtype), in_specs=[ANY], out_specs=ANY,
        scratch_shapes=[pltpu.SemaphoreType.DMA((6,)), pltpu.SemaphoreType.DMA((6,)), pltpu.SemaphoreType.DMA],
        name=name)(shard)


def all_gather_devices(part, name):
    def body(x_ref, out_ref, send_sems, recv_sems, local_sem):
        x, y, c, _ = _place()
        me = 4 * x + 2 * y + c
        mine = pltpu.make_async_copy(x_ref, out_ref.at[me], local_sem)
        mine.start()
        copies = []
        for k in range(1, 8):
            px, py, pc = x ^ (k >> 2), y ^ ((k >> 1) & 1), c ^ (k & 1)
            copies.append(pltpu.make_async_remote_copy(
                src_ref=x_ref, dst_ref=out_ref.at[me], send_sem=send_sems.at[k - 1], recv_sem=recv_sems.at[k - 1],
                device_id=(px, py, pc), device_id_type=MESH))
            copies[-1].start()
        for k in range(1, 8):
            peer = 4 * (x ^ (k >> 2)) + 2 * (y ^ ((k >> 1) & 1)) + (c ^ (k & 1))
            pltpu.make_async_remote_copy(
                src_ref=x_ref, dst_ref=out_ref.at[peer], send_sem=send_sems.at[k - 1], recv_sem=recv_sems.at[k - 1],
                device_id=(x, y, c), device_id_type=MESH).wait_recv()
        for cp in copies:
            cp.wait_send()
        mine.wait()

    return pl.pallas_call(
        body, out_shape=jax.ShapeDtypeStruct((8,) + part.shape, part.dtype), in_specs=[ANY], out_specs=ANY,
        scratch_shapes=[pltpu.SemaphoreType.DMA((7,)), pltpu.SemaphoreType.DMA((7,)), pltpu.SemaphoreType.DMA],
        name=name)(part)


def send_half_to_sibling(p2):
    def body(p_ref, out_ref, send_sem, recv_sem):
        x, y, c, _ = _place()
        cp = pltpu.make_async_remote_copy(src_ref=p_ref.at[1 - c], dst_ref=out_ref, send_sem=send_sem, recv_sem=recv_sem,
                                          device_id=(x, y, 1 - c), device_id_type=MESH)
        cp.start()
        cp.wait()

    return pl.pallas_call(
        body, out_shape=jax.ShapeDtypeStruct(p2.shape[1:], p2.dtype), in_specs=[ANY], out_specs=ANY,
        scratch_shapes=[pltpu.SemaphoreType.DMA, pltpu.SemaphoreType.DMA], name="rs_sibling")(p2)


def exchange_pieces(pair):
    def body(p_ref, out_ref, send_sems, recv_sems, local_sem):
        x, y, c, q = _place()
        chips = _other_chips(x, y)
        mine = pltpu.make_async_copy(p_ref.at[q], out_ref.at[q], local_sem)
        mine.start()
        copies = [pltpu.make_async_remote_copy(
            src_ref=p_ref.at[pq], dst_ref=out_ref.at[q], send_sem=send_sems.at[k], recv_sem=recv_sems.at[k],
            device_id=(px, py, c), device_id_type=MESH) for k, (px, py, pq) in enumerate(chips)]
        for cp in copies:
            cp.start()
        for k, (_, _, pq) in enumerate(chips):
            pltpu.make_async_remote_copy(
                src_ref=p_ref.at[q], dst_ref=out_ref.at[pq], send_sem=send_sems.at[k], recv_sem=recv_sems.at[k],
                device_id=(x, y, c), device_id_type=MESH).wait_recv()
        for cp in copies:
            cp.wait_send()
        mine.wait()

    return pl.pallas_call(
        body, out_shape=jax.ShapeDtypeStruct(pair.shape, pair.dtype), in_specs=[ANY], out_specs=ANY,
        scratch_shapes=[pltpu.SemaphoreType.DMA((3,)), pltpu.SemaphoreType.DMA((3,)), pltpu.SemaphoreType.DMA],
        name="rs_chips")(pair)


def exchange_halves(red):
    def body(r_ref, out_ref, send_sem, recv_sem, local_sem):
        x, y, c, _ = _place()
        mine = pltpu.make_async_copy(r_ref, out_ref.at[c], local_sem)
        mine.start()
        cp = pltpu.make_async_remote_copy(src_ref=r_ref, dst_ref=out_ref.at[c], send_sem=send_sem, recv_sem=recv_sem,
                                          device_id=(x, y, 1 - c), device_id_type=MESH)
        cp.start()
        pltpu.make_async_remote_copy(src_ref=r_ref, dst_ref=out_ref.at[1 - c], send_sem=send_sem, recv_sem=recv_sem,
                                     device_id=(x, y, 1 - c), device_id_type=MESH).wait_recv()
        cp.wait_send()
        mine.wait()

    return pl.pallas_call(
        body, out_shape=jax.ShapeDtypeStruct((2,) + red.shape, red.dtype), in_specs=[ANY], out_specs=ANY,
        scratch_shapes=[pltpu.SemaphoreType.DMA, pltpu.SemaphoreType.DMA, pltpu.SemaphoreType.DMA],
        name="rs_halves")(red)


def _row_tile(R):
    return _pick(R, (512, 256, 128, 64, 32, 16, 8))


def add_own_half(p2, got):
    _, Q4, R, _ = p2.shape
    rt = _row_tile(R)

    def body(p_ref, g_ref, o_ref):
        c = lax.axis_index("c")

        @pl.when(c == 0)
        def _():
            o_ref[0] = p_ref[0, 0] + g_ref[0]

        @pl.when(c != 0)
        def _():
            o_ref[0] = g_ref[0] + p_ref[1, 0]

    return pl.pallas_call(
        body, grid=(Q4, R // rt),
        in_specs=[pl.BlockSpec((2, 1, rt, LANES), lambda q, i: (0, q, i, 0)),
                  pl.BlockSpec((1, rt, LANES), lambda q, i: (q, i, 0))],
        out_specs=pl.BlockSpec((1, rt, LANES), lambda q, i: (q, i, 0)),
        out_shape=jax.ShapeDtypeStruct(got.shape, F32), name="rs_add_pair",
        compiler_params=_cp("parallel", "parallel"))(p2, got)


def sum_parts(parts, name):
    P, R, _ = parts.shape
    rt = _row_tile(R)

    def body(p_ref, o_ref):
        acc = p_ref[0]
        for p in range(1, P):
            acc = acc + p_ref[p]
        o_ref[...] = acc

    return pl.pallas_call(
        body, grid=(R // rt,), in_specs=[pl.BlockSpec((P, rt, LANES), lambda i: (0, i, 0))],
        out_specs=pl.BlockSpec((rt, LANES), lambda i: (i, 0)), out_shape=jax.ShapeDtypeStruct((R, LANES), F32),
        name=name, compiler_params=_cp("parallel"))(parts)


def adamw(g, w, m, v, name):
    R = g.shape[0]
    rt = _row_tile(R)

    def body(g_ref, w_ref, m_ref, v_ref, d_ref, m2_ref, v2_ref):
        gv = g_ref[...]
        m2 = ADAM_B1 * m_ref[...] + (1.0 - ADAM_B1) * gv
        v2 = ADAM_B2 * v_ref[...] + (1.0 - ADAM_B2) * (gv * gv)
        m_hat = m2 / (1.0 - ADAM_B1 ** ADAM_STEP)
        v_hat = v2 / (1.0 - ADAM_B2 ** ADAM_STEP)
        d_ref[...] = -ADAM_LR * (m_hat / (jnp.sqrt(v_hat) + ADAM_EPS) + ADAM_WD * w_ref[...])
        m2_ref[...] = m2
        v2_ref[...] = v2

    spec = pl.BlockSpec((rt, LANES), lambda i: (i, 0))
    return pl.pallas_call(
        body, grid=(R // rt,), in_specs=[spec] * 4, out_specs=[spec] * 3,
        out_shape=[jax.ShapeDtypeStruct((R, LANES), F32)] * 3, name=name, compiler_params=_cp("parallel"))(g, w, m, v)


def _pack(arrays, rows_multiple, dtype):
    flat = jnp.concatenate([a.reshape(-1).astype(dtype) for a in arrays])
    n = flat.shape[0]
    per = rows_multiple * LANES
    total = -(-n // per) * per
    return jnp.pad(flat, (0, total - n)).reshape(total // LANES, LANES)


def _unpack(buf, shapes):
    flat = buf.reshape(-1)
    out, off = [], 0
    for s in shapes:
        n = math.prod(s)
        out.append(flat[off:off + n].reshape(s))
        off += n
    return out


SHARDED = (("mod_w", 2), ("ffn_w_in", 3), ("ffn_w_out", 2), ("w_in", 2), ("w_out", 1),
           ("a_conv_w", 2), ("s5_glu_w", 1), ("hg_lb_logits", 2))
WEIGHTS = ("c_ctx", "mod_w", "mod_b", "ffn_w_in", "ffn_w_out", "w_in", "w_out", "a_conv_w", "a_conv_b", "a_dt_bias",
           "a_log", "a_d", "a_norm_w", "s5_lam_re", "s5_lam_im", "s5_log_step", "s5_b_re", "s5_b_im", "s5_c_re",
           "s5_c_im", "s5_d", "s5_glu_w", "s5_glu_b", "hg_lb_logits", "hg_norm_w", "final_norm_w")


def _gather_full(local, names_axes, dtype, rows_multiple, name):
    buf = _pack([local[n] for n, _ in names_axes], 2 * rows_multiple, dtype)
    R = buf.shape[0] // 2
    full = all_gather_shards(buf.reshape(2, R, LANES), name).reshape(N_CHIPS, 2 * R, LANES)
    per_chip = [_unpack(full[q], [local[n].shape for n, _ in names_axes]) for q in range(N_CHIPS)]
    return {n: jnp.concatenate([per_chip[q][j] for q in range(N_CHIPS)], axis=ax)
            for j, (n, ax) in enumerate(names_axes)}


def kernel(x, c, ctx, c_ctx, mod_w, mod_b, ffn_w_in, ffn_w_out, w_in, w_out, a_conv_w, a_conv_b, a_dt_bias, a_log, a_d, a_norm_w, s5_lam_re, s5_lam_im, s5_log_step, s5_b_re, s5_b_im, s5_c_re, s5_c_im, s5_d, s5_glu_w, s5_glu_b, hg_lb_logits, hg_norm_w, final_norm_w, loss_target, m_c_ctx, m_mod_w, m_mod_b, m_ffn_w_in, m_ffn_w_out, m_w_in, m_w_out, m_a_conv_w, m_a_conv_b, m_a_dt_bias, m_a_log, m_a_d, m_a_norm_w, m_s5_lam_re, m_s5_lam_im, m_s5_log_step, m_s5_b_re, m_s5_b_im, m_s5_c_re, m_s5_c_im, m_s5_d, m_s5_glu_w, m_s5_glu_b, m_hg_lb_logits, m_hg_norm_w, m_final_norm_w, v_c_ctx, v_mod_w, v_mod_b, v_ffn_w_in, v_ffn_w_out, v_w_in, v_w_out, v_a_conv_w, v_a_conv_b, v_a_dt_bias, v_a_log, v_a_d, v_a_norm_w, v_s5_lam_re, v_s5_lam_im, v_s5_log_step, v_s5_b_re, v_s5_b_im, v_s5_c_re, v_s5_c_im, v_s5_d, v_s5_glu_w, v_s5_glu_b, v_hg_lb_logits, v_hg_norm_w, v_final_norm_w):
    given = dict(locals())
    w = {n: given[n] for n in WEIGHTS}
    m = {n: given["m_" + n] for n in WEIGHTS}
    v = {n: given["v_" + n] for n in WEIGHTS}
    sharded_names = [n for n, _ in SHARDED]
    replicated = [n for n in WEIGHTS if n not in sharded_names]

    big = _gather_full(w, SHARDED[:len(BIG)], BF16, BIG_ROWS, "gather_big")
    small = {n: w[n] for n in replicated}
    small.update(_gather_full(w, SHARDED[len(BIG):], F32, 8, "gather_small"))

    loss, grad_x, grads = local_step(x, c, ctx, loss_target, big, small)

    def pieces_for(q):
        parts = []
        for n, ax in SHARDED:
            width = grads[n].shape[ax] // N_CHIPS
            parts.append(lax.slice_in_dim(grads[n], q * width, (q + 1) * width, axis=ax))
        return _pack(parts, 2 * BIG_ROWS, F32)

    pieces = jnp.stack([pieces_for(q) for q in range(N_CHIPS)])
    R = pieces.shape[1] // 2
    p2 = pieces.reshape(N_CHIPS, 2, R, LANES).transpose(1, 0, 2, 3)
    pair = add_own_half(p2, send_half_to_sibling(p2))
    red = sum_parts(exchange_pieces(pair), "rs_sum_chips")
    g_sh = exchange_halves(red).reshape(2 * R, LANES)
    d_sh, m_sh, v_sh = adamw(g_sh, _pack([w[n] for n in sharded_names], 2 * BIG_ROWS, F32),
                             _pack([m[n] for n in sharded_names], 2 * BIG_ROWS, F32),
                             _pack([v[n] for n in sharded_names], 2 * BIG_ROWS, F32), "adamw_sharded")
    shapes = [w[n].shape for n in sharded_names]
    out = {}
    for kind, buf in (("grad", g_sh), ("delta", d_sh), ("new_m", m_sh), ("new_v", v_sh)):
        for n, a in zip(sharded_names, _unpack(buf, shapes)):
            out[kind, n] = a

    part = _pack([grads[n] for n in replicated] + [loss.reshape(1)], 8, F32)
    g_rep = sum_parts(all_gather_devices(part, "gather_parts"), "sum_parts")
    zero = jnp.zeros((1,), F32)
    d_rep, m_rep, v_rep = adamw(g_rep, _pack([w[n] for n in replicated] + [zero], 8, F32),
                                _pack([m[n] for n in replicated] + [zero], 8, F32),
                                _pack([v[n] for n in replicated] + [zero], 8, F32), "adamw_replicated")
    shapes = [w[n].shape for n in replicated] + [(1,)]
    for kind, buf in (("grad", g_rep), ("delta", d_rep), ("new_m", m_rep), ("new_v", v_rep)):
        for n, a in zip(replicated + ["loss"], _unpack(buf, shapes)):
            out[kind, n] = a

    loss_total = out["grad", "loss"].reshape(())
    return (loss_total, grad_x, *[out["grad", n] for n in WEIGHTS], *[out["delta", n] for n in WEIGHTS],
            *[out["new_m", n] for n in WEIGHTS], *[out["new_v", n] for n in WEIGHTS])
```

```python
import functools
import math

import numpy as np
import jax
import jax.numpy as jnp
from jax import lax
from jax.experimental import pallas as pl
from jax.experimental.pallas import tpu as pltpu

F32, BF16 = jnp.float32, jnp.bfloat16
EPS = 1e-6
N_MOD = 9
D_FF = 2816
A_INNER, A_HEADS, A_GROUPS, A_STATE, A_CONV, A_CONV_DIM = 512, 8, 2, 64, 5, 768
A_COLS = A_INNER + A_CONV_DIM + 2 * A_HEADS
B_WIDTH, B_GROUP, B_NGROUPS, B_STATE = 256, 16, 16, 64
C_WIDTH, C_HEADS, C_KEY = 256, 4, 64
GRID_W = 64
IN_COLS = A_COLS + B_WIDTH + 5 * C_WIDTH
IN_COLS_PAD = 3072
ADAM_LR, ADAM_B1, ADAM_B2, ADAM_EPS, ADAM_WD, ADAM_STEP = 0.001, 0.9, 0.999, 1e-08, 0.01, 10

ROW_TILE = 256
SSD_Q, S5_Q, HG_Q = 128, 64, 64
HG_BAND = 8
VMEM_LIMIT = 56 * 1024 * 1024
MM_VMEM_BUDGET = 36 * 1024 * 1024

NT = ((1,), (1,))
TN = ((0,), (0,))


def _bdot(a, b, dims=((1,), (0,))):
    return lax.dot_general(a.astype(BF16), b.astype(BF16), (dims, ((), ())), preferred_element_type=F32)


def _split3(x):
    hi = x.astype(BF16)
    r = x - hi.astype(F32)
    mid = r.astype(BF16)
    lo = (r - mid.astype(F32)).astype(BF16)
    return hi, mid, lo


def _mask_lhs(m, x, dims):
    return sum(lax.dot_general(m, p, (dims, ((), ())), preferred_element_type=F32) for p in _split3(x))


def _mask_rhs(x, m, dims):
    return sum(lax.dot_general(p, m, (dims, ((), ())), preferred_element_type=F32) for p in _split3(x))


@jax.custom_vjp
def mdot(m, x):
    return _mask_lhs(m, x, ((1,), (0,)))


def _mdot_fwd(m, x):
    return mdot(m, x), m


def _mdot_bwd(m, g):
    return jnp.zeros_like(m), _mask_lhs(m, g, TN)


mdot.defvjp(_mdot_fwd, _mdot_bwd)


@jax.custom_vjp
def xdot(x, m):
    return _mask_rhs(x, m, ((1,), (0,)))


def _xdot_fwd(x, m):
    return xdot(x, m), m


def _xdot_bwd(m, g):
    return _mask_rhs(g, m, NT), jnp.zeros_like(m)


xdot.defvjp(_xdot_fwd, _xdot_bwd)


@functools.partial(jax.custom_vjp, nondiff_argnums=(1,))
def _roll(x, s):
    return pltpu.roll(x, s, 0)


def _roll_fwd(x, s):
    return _roll(x, s), None


def _roll_bwd(s, _, g):
    return (_roll(g, x_rows(g) - s),)


def x_rows(x):
    return x.shape[0]


_roll.defvjp(_roll_fwd, _roll_bwd)


def sroll(x, s):
    s = s % x.shape[0]
    return x if s == 0 else _roll(x, s)


def _softplus(x):
    return jnp.maximum(x, 0.0) + jnp.log1p(jnp.exp(jnp.minimum(x, -x)))


def _silu(x):
    return x * jax.nn.sigmoid(x)


def _rms(x):
    return x * lax.rsqrt(jnp.mean(x * x, axis=-1, keepdims=True) + EPS)


def _pick(n, cands):
    for c in cands:
        if n % c == 0:
            return c
    return n


def _cp(*sem):
    return pltpu.CompilerParams(dimension_semantics=sem, vmem_limit_bytes=VMEM_LIMIT)


def _mm(a, b, ta, tb, out_dtype, name):
    M, K = (a.shape[1], a.shape[0]) if ta else a.shape
    N = b.shape[0] if tb else b.shape[1]
    tm = _pick(M, (1408, 1024, 512, 256, 128, 64, 32, 16, 8))
    tn = _pick(N, (1408, 1024, 512, 256, 128))
    a_bytes, b_bytes, o_bytes = a.dtype.itemsize, b.dtype.itemsize, jnp.dtype(out_dtype).itemsize
    for tk in (2816, 1408, 1024, 512, 256, 128, 64, 32, 16, 8):
        scratch = tm * tn * 4 if (K // tk > 1 and o_bytes != 4) else 0
        if K % tk == 0 and 2 * (tm * tk * a_bytes + tk * tn * b_bytes + tm * tn * o_bytes) + scratch <= MM_VMEM_BUDGET:
            break
    nk = K // tk
    dims = ((0 if ta else 1,), (1 if tb else 0,))

    def body(a_ref, b_ref, o_ref, *acc):
        part = _bdot(a_ref[...], b_ref[...], dims)
        if nk == 1:
            o_ref[...] = part.astype(o_ref.dtype)
            return
        acc_ref = acc[0] if acc else o_ref
        k = pl.program_id(2)

        @pl.when(k == 0)
        def _():
            acc_ref[...] = part

        @pl.when(k > 0)
        def _():
            acc_ref[...] += part

        if acc:
            @pl.when(k == nk - 1)
            def _():
                o_ref[...] = acc_ref[...].astype(o_ref.dtype)

    a_spec = pl.BlockSpec((tk, tm), lambda i, j, k: (k, i)) if ta else pl.BlockSpec((tm, tk), lambda i, j, k: (i, k))
    b_spec = pl.BlockSpec((tn, tk), lambda i, j, k: (j, k)) if tb else pl.BlockSpec((tk, tn), lambda i, j, k: (k, j))
    return pl.pallas_call(
        body, grid=(M // tm, N // tn, nk), in_specs=[a_spec, b_spec],
        out_specs=pl.BlockSpec((tm, tn), lambda i, j, k: (i, j)),
        out_shape=jax.ShapeDtypeStruct((M, N), out_dtype),
        scratch_shapes=[pltpu.VMEM((tm, tn), F32)] if (nk > 1 and o_bytes != 4) else [], name=name,
        compiler_params=_cp("parallel", "parallel", "arbitrary"))(a, b)


def linear(x, w16, wslot, name, out_dtype=F32):
    @jax.custom_vjp
    def f(x, w16, wslot):
        return _mm(x, w16, False, False, out_dtype, name + "_fwd")

    def fwd(x, w16, wslot):
        return f(x, w16, wslot), (x, w16)

    def bwd(res, dy):
        x, w16 = res
        dx = _mm(dy, w16, False, True, x.dtype, name + "_dx")
        dw = _mm(x, dy, True, False, F32, name + "_dw")
        return dx, jnp.zeros_like(w16), dw

    f.defvjp(fwd, bwd)
    return f(x, w16, wslot)


def rowwise(f, name, rows, mods, params, outs, n_ctx_blocks, tl=ROW_TILE, row_diff=None):
    rows, mods, params = tuple(rows), tuple(mods), tuple(params)
    nr, nm, npar, no = len(rows), len(mods), len(params), len(outs)
    B, L = rows[0].shape[:2]
    nblk = L // tl
    row_diff = tuple(row_diff) if row_diff is not None else (True,) * nr
    out_dtypes = [dt for _, dt in outs]

    def fcast(r, m, p):
        return tuple(o.astype(dt) for o, dt in zip(f(r, m, p), out_dtypes))

    def seg(i):
        return (i >= n_ctx_blocks).astype(jnp.int32) if n_ctx_blocks else 0

    def specs():
        row_specs = [pl.BlockSpec((1, tl, r.shape[2]), lambda b, i: (b, i, 0)) for r in rows]
        mod_specs = [pl.BlockSpec((1, 1, 1, m.shape[3]), lambda b, i: (b, seg(i), 0, 0)) for m in mods]
        par_specs = [pl.BlockSpec(p.shape, lambda b, i: (0, 0)) for p in params]
        out_specs = [pl.BlockSpec((1, tl, w), lambda b, i: (b, i, 0)) for w, _ in outs]
        return row_specs, mod_specs, par_specs, out_specs

    def load(ins):
        r = tuple(x[0] for x in ins[:nr])
        m = tuple(x[0, 0] for x in ins[nr:nr + nm])
        p = tuple(x[...] for x in ins[nr + nm:])
        return r, m, p

    def fwd_call(rows, mods, params):
        def body(*refs):
            r, m, p = load(refs[:nr + nm + npar])
            for o_ref, o in zip(refs[nr + nm + npar:], fcast(r, m, p)):
                o_ref[0] = o

        rs, ms, ps, os_ = specs()
        return pl.pallas_call(
            body, grid=(B, nblk), in_specs=rs + ms + ps, out_specs=os_,
            out_shape=[jax.ShapeDtypeStruct((B, L, w), dt) for w, dt in outs],
            name=name + "_fwd", compiler_params=_cp("parallel", "parallel"))(*rows, *mods, *params)

    def bwd_call(rows, mods, params, douts):
        didx = [j for j in range(nr) if row_diff[j]]

        def body(*refs):
            n_in = nr + nm + npar
            r, m, p = load(refs[:n_in])
            dins = refs[n_in:n_in + no]
            rest = refs[n_in + no:]
            dr_refs, dm_refs, dp_refs = rest[:len(didx)], rest[len(didx):len(didx) + nm], rest[len(didx) + nm:]
            _, vjp = jax.vjp(fcast, r, m, p)
            dr, dm, dp = vjp(tuple(d[0] for d in dins))
            for ref, j in zip(dr_refs, didx):
                ref[0] = dr[j].astype(ref.dtype)
            b, i = pl.program_id(0), pl.program_id(1)
            first_m = (i == 0) | (i == n_ctx_blocks) if n_ctx_blocks else (i == 0)
            first_p = (b == 0) & (i == 0)
            for ref, g in zip(dm_refs, dm):
                @pl.when(first_m)
                def _(ref=ref, g=g):
                    ref[0, 0] = g

                @pl.when(jnp.logical_not(first_m))
                def _(ref=ref, g=g):
                    ref[0, 0] += g
            for ref, g in zip(dp_refs, dp):
                @pl.when(first_p)
                def _(ref=ref, g=g):
                    ref[...] = g

                @pl.when(jnp.logical_not(first_p))
                def _(ref=ref, g=g):
                    ref[...] += g

        rs, ms, ps, os_ = specs()
        out_shape = ([jax.ShapeDtypeStruct(rows[j].shape, rows[j].dtype) for j in didx]
                     + [jax.ShapeDtypeStruct(m.shape, F32) for m in mods]
                     + [jax.ShapeDtypeStruct(p.shape, F32) for p in params])
        res = pl.pallas_call(
            body, grid=(B, nblk), in_specs=rs + ms + ps + os_,
            out_specs=[rs[j] for j in didx] + ms + ps, out_shape=out_shape,
            name=name + "_bwd", compiler_params=_cp("arbitrary", "arbitrary"))(*rows, *mods, *params, *douts)
        dr = [None] * nr
        for j, g in zip(didx, res[:len(didx)]):
            dr[j] = g
        dr = tuple(g if g is not None else jnp.zeros_like(rows[j]) for j, g in enumerate(dr))
        return dr, tuple(res[len(didx):len(didx) + nm]), tuple(res[len(didx) + nm:])

    @jax.custom_vjp
    def op(rows, mods, params):
        return tuple(fwd_call(rows, mods, params))

    def op_fwd(rows, mods, params):
        return op(rows, mods, params), (rows, mods, params)

    def op_bwd(res, douts):
        return bwd_call(*res, tuple(douts))

    op.defvjp(op_fwd, op_bwd)
    return op(rows, mods, params)


def chunk_scan(step, name, rev, seqs, seq_t, params, consts, state_shapes, out_widths, Q, n_ctx):
    seqs, params, consts = tuple(seqs), tuple(params), tuple(consts)
    ns, npar, nc, nst, no = len(seqs), len(params), len(consts), len(state_shapes), len(out_widths)
    B = seqs[0].shape[0]
    L = seqs[0].shape[2] if seq_t[0] else seqs[0].shape[1]
    nck, ncc = L // Q, n_ctx // Q

    def chunk_of(k):
        if not rev:
            return k
        return jnp.where(k < ncc, ncc - 1 - k, nck + ncc - 1 - k)

    def specs(order):
        def seq_spec(s, t):
            if t:
                return pl.BlockSpec((1, s.shape[1], Q), lambda b, k: (b, 0, chunk_of(order(k))))
            return pl.BlockSpec((1, Q, s.shape[2]), lambda b, k: (b, chunk_of(order(k)), 0))

        seq_specs = [seq_spec(s, t) for s, t in zip(seqs, seq_t)]
        par_specs = [pl.BlockSpec(p.shape, lambda b, k: (0, 0)) for p in params]
        con_specs = [pl.BlockSpec(c.shape, lambda b, k, nd=c.ndim: (0,) * nd) for c in consts]
        out_specs = [pl.BlockSpec((1, Q, w), lambda b, k: (b, chunk_of(order(k)), 0)) for w in out_widths]
        sav_specs = [pl.BlockSpec((1, 1) + tuple(s), lambda b, k: (b, order(k), 0, 0)) for s in state_shapes]
        return seq_specs, par_specs, con_specs, out_specs, sav_specs

    def fwd_call(seqs, params):
        def body(*refs):
            seq_refs = refs[:ns]
            par_refs = refs[ns:ns + npar]
            con_refs = refs[ns + npar:ns + npar + nc]
            rest = refs[ns + npar + nc:]
            out_refs, sav_refs, st_refs = rest[:no], rest[no:no + nst], rest[no + nst:]

            @pl.when(pl.program_id(1) == 0)
            def _():
                for s in st_refs:
                    s[...] = jnp.zeros_like(s)

            s_in = tuple(s[...] for s in st_refs)
            for sv, s in zip(sav_refs, s_in):
                sv[0, 0] = s
            s_new, outs = step(s_in, tuple(r[0] for r in seq_refs), tuple(p[...] for p in par_refs),
                               tuple(c[...] for c in con_refs))
            for s_ref, s in zip(st_refs, s_new):
                s_ref[...] = s
            for o_ref, o in zip(out_refs, outs):
                o_ref[0] = o

        ss, ps, cs, os_, vs = specs(lambda k: k)
        res = pl.pallas_call(
            body, grid=(B, nck), in_specs=ss + ps + cs, out_specs=os_ + vs,
            out_shape=[jax.ShapeDtypeStruct((B, L, w), F32) for w in out_widths]
            + [jax.ShapeDtypeStruct((B, nck) + tuple(s), F32) for s in state_shapes],
            scratch_shapes=[pltpu.VMEM(tuple(s), F32) for s in state_shapes],
            name=name + "_fwd", compiler_params=_cp("parallel", "arbitrary"))(*seqs, *params, *consts)
        return tuple(res[:no]), tuple(res[no:])

    def bwd_call(seqs, params, saved, douts):
        def body(*refs):
            seq_refs = refs[:ns]
            par_refs = refs[ns:ns + npar]
            con_refs = refs[ns + npar:ns + npar + nc]
            rest = refs[ns + npar + nc:]
            sav_refs, dout_refs = rest[:nst], rest[nst:nst + no]
            rest = rest[nst + no:]
            dseq_refs, dpar_refs, dst_refs = rest[:ns], rest[ns:ns + npar], rest[ns + npar:]
            b, k = pl.program_id(0), pl.program_id(1)

            @pl.when(k == 0)
            def _():
                for s in dst_refs:
                    s[...] = jnp.zeros_like(s)

            cvals = tuple(c[...] for c in con_refs)
            _, vjp = jax.vjp(lambda s, x, p: step(s, x, p, cvals), tuple(s[0, 0] for s in sav_refs),
                             tuple(r[0] for r in seq_refs), tuple(p[...] for p in par_refs))
            ds, dx, dp = vjp((tuple(s[...] for s in dst_refs), tuple(d[0] for d in dout_refs)))
            for s_ref, s in zip(dst_refs, ds):
                s_ref[...] = s
            for x_ref, x in zip(dseq_refs, dx):
                x_ref[0] = x
            first = (b == 0) & (k == 0)
            for ref, g in zip(dpar_refs, dp):
                @pl.when(first)
                def _(ref=ref, g=g):
                    ref[...] = g

                @pl.when(jnp.logical_not(first))
                def _(ref=ref, g=g):
                    ref[...] += g

        ss, ps, cs, os_, vs = specs(lambda k: nck - 1 - k)
        res = pl.pallas_call(
            body, grid=(B, nck), in_specs=ss + ps + cs + vs + os_, out_specs=ss + ps,
            out_shape=[jax.ShapeDtypeStruct(s.shape, F32) for s in seqs]
            + [jax.ShapeDtypeStruct(p.shape, F32) for p in params],
            scratch_shapes=[pltpu.VMEM(tuple(s), F32) for s in state_shapes],
            name=name + "_bwd", compiler_params=_cp("arbitrary", "arbitrary"))(*seqs, *params, *consts, *saved, *douts)
        return tuple(res[:ns]), tuple(res[ns:])

    @jax.custom_vjp
    def op(seqs, params):
        return fwd_call(seqs, params)[0]

    def op_fwd(seqs, params):
        outs, saved = fwd_call(seqs, params)
        return outs, (seqs, params, saved)

    def op_bwd(res, douts):
        seqs, params, saved = res
        return bwd_call(seqs, params, saved, tuple(douts))

    op.defvjp(op_fwd, op_bwd)
    return op(seqs, params)


def _positions(Q, rev):
    t = np.arange(Q)
    return (Q - 1 - t) if rev else t


def _ssd_consts(rev):
    Q = SSD_Q
    pos = _positions(Q, rev)
    tri = pos[:, None] >= pos[None, :]
    head_of = np.arange(A_INNER) // (A_INNER // A_HEADS)
    expand = np.arange(A_HEADS)[:, None] == head_of[None, :]
    group_of_row = np.arange(A_GROUPS * A_STATE) // A_STATE
    block = group_of_row[:, None] == (head_of // (A_HEADS // A_GROUPS))[None, :]
    as_bf = lambda m: jnp.asarray(m.astype(np.float32), BF16)
    return (as_bf(tri), as_bf(tri.T), jnp.asarray(tri.astype(np.float32)), as_bf(expand),
            jnp.asarray(block.astype(np.float32)))


def _ssd_step(state, seqs, params, consts):
    (st,) = state
    xs, bm, cm, dtr, dtr_t = seqs
    bias, bias_t, alog, alog_t = params
    tri_b, tri_t_b, tri_f, expand, block = consts
    dt = _softplus(dtr + bias)
    dta = dt * (-jnp.exp(alog))
    dt_t = _softplus(dtr_t + bias_t)
    dta_t = dt_t * (-jnp.exp(alog_t))
    cum = mdot(tri_b, dta)
    cum_t = xdot(dta_t, tri_t_b)
    total = jnp.sum(dta, axis=0, keepdims=True)
    w_end = jnp.exp(total - cum) * dt
    ecum_x = xdot(jnp.exp(cum), expand)
    wend_x = xdot(w_end, expand)
    dec_x = jnp.exp(jnp.sum(xdot(dta, expand), axis=0, keepdims=True))
    y = _bdot(cm, st) * ecum_x
    st_new = st * dec_x + block * _bdot(bm, xs * wend_x, TN)
    lane_g = lax.broadcasted_iota(jnp.int32, (1, A_GROUPS * A_STATE), 1) // A_STATE
    lane_h = lax.broadcasted_iota(jnp.int32, (1, A_INNER), 1) // (A_INNER // A_HEADS)
    sub_h = lax.broadcasted_iota(jnp.int32, (A_HEADS, 1), 0)
    lane8 = lax.broadcasted_iota(jnp.int32, (1, A_HEADS), 1)
    for g in range(A_GROUPS):
        cb = _bdot(jnp.where(lane_g == g, cm, 0.0), bm, NT)
        for h in range(g * (A_HEADS // A_GROUPS), (g + 1) * (A_HEADS // A_GROUPS)):
            col = jnp.sum(jnp.where(lane8 == h, cum, 0.0), axis=1, keepdims=True)
            row = jnp.sum(jnp.where(sub_h == h, cum_t, 0.0), axis=0, keepdims=True)
            dt_row = jnp.sum(jnp.where(sub_h == h, dt_t, 0.0), axis=0, keepdims=True)
            decay = tri_f * jnp.exp(tri_f * (col - row))
            y = y + _bdot(cb * decay * dt_row, jnp.where(lane_h == h, xs, 0.0))
    return (st_new,), (y,)


def ssd_scan(name, rev, xs, bm, cm, dtr, dt_bias, a_log, n_ctx):
    seqs = (xs, bm, cm, dtr, jnp.swapaxes(dtr, 1, 2))
    params = (dt_bias.reshape(1, -1), dt_bias.reshape(-1, 1), a_log.reshape(1, -1), a_log.reshape(-1, 1))
    (y,) = chunk_scan(_ssd_step, name, rev, seqs, (False, False, False, False, True), params, _ssd_consts(rev),
                      [(A_GROUPS * A_STATE, A_INNER)], [A_INNER], SSD_Q, n_ctx)
    return y


def _s5_step_fn(rev):
    def step(state, seqs, params, consts):
        sr, si = state
        (u,) = seqs
        b_re, b_im, c_re, c_im, ar, ai = params
        (pos,) = consts
        Q = u.shape[0]
        first = (pos == 0.0).astype(F32)
        xr = _bdot(u, b_re) + first * (ar * sr - ai * si)
        xi = _bdot(u, b_im) + first * (ar * si + ai * sr)
        pr, pi = ar, ai
        k = 1
        while k < Q:
            keep = (pos >= float(k)).astype(F32)
            s = -k if rev else k
            xr_s, xi_s = sroll(xr, s) * keep, sroll(xi, s) * keep
            xr, xi = xr + pr * xr_s - pi * xi_s, xi + pr * xi_s + pi * xr_s
            pr, pi = pr * pr - pi * pi, 2.0 * pr * pi
            k *= 2
        last = (pos == float(Q - 1)).astype(F32)
        sr_new = jnp.sum(xr * last, axis=0, keepdims=True)
        si_new = jnp.sum(xi * last, axis=0, keepdims=True)
        y = _bdot(xr, c_re) - _bdot(xi, c_im)
        return (sr_new, si_new), (y,)

    return step


def _block_diag(t):
    G, a, b = t.shape
    eye = jnp.eye(G, dtype=t.dtype)
    return (t[:, :, None, :] * eye[:, None, :, None]).reshape(G * a, G * b)


def s5_scan(name, rev, u, lam_re, lam_im, log_step, b_re, b_im, c_re, c_im, n_ctx):
    step = jnp.exp(log_step)[:, None]
    mag = jnp.exp(lam_re * step)
    ar = mag * jnp.cos(lam_im * step)
    ai = mag * jnp.sin(lam_im * step)
    den = lam_re * lam_re + lam_im * lam_im
    nr = ar - 1.0
    kr = (nr * lam_re + ai * lam_im) / den
    ki = (ai * lam_re - nr * lam_im) / den
    br = kr[..., None] * b_re - ki[..., None] * b_im
    bi = kr[..., None] * b_im + ki[..., None] * b_re
    params = (_block_diag(jnp.swapaxes(br, 1, 2)), _block_diag(jnp.swapaxes(bi, 1, 2)),
              _block_diag(jnp.swapaxes(c_re, 1, 2)), _block_diag(jnp.swapaxes(c_im, 1, 2)),
              ar.reshape(1, -1), ai.reshape(1, -1))
    pos = jnp.asarray(_positions(S5_Q, rev).astype(np.float32).reshape(-1, 1))
    width = B_NGROUPS * B_STATE
    (y,) = chunk_scan(_s5_step_fn(rev), name, rev, (u,), (False,), params, (pos,),
                      [(1, width), (1, width)], [B_WIDTH], S5_Q, n_ctx)
    return y


def _hg_consts(rev):
    Q = HG_Q
    pos = _positions(Q, rev)
    pi, pj = pos[:, None], pos[None, :]
    mats = [pi >= pj, pj > pi]
    pairs = []
    s = HG_BAND
    while s < Q:
        second = (pos // s) % 2 == 1
        mid = (pos // (2 * s)) * 2 * s + s
        mats.append(second[:, None] & (pj >= mid[:, None]) & (pj <= pi))
        mats.append((~second)[:, None] & (pj > pi) & (pj < mid[:, None]))
        pairs.append(second[:, None] & (~second)[None, :] & ((pi // (2 * s)) == (pj // (2 * s))))
        s *= 2
    rows = [((pos // s_) % 2 == 1) for s_ in [HG_BAND * 2 ** n for n in range(len(pairs))]]
    band = [(pos % HG_BAND) >= d for d in range(HG_BAND)]
    mats_b = jnp.asarray(np.stack(mats).astype(np.float32), BF16)
    pairs_f = jnp.asarray(np.stack(pairs).astype(np.float32))
    rows_f = jnp.asarray(np.stack(rows).astype(np.float32)[:, :, None])
    band_f = jnp.asarray(np.stack(band).astype(np.float32)[:, :, None])
    return mats_b, pairs_f, rows_f, band_f


def _hg_step_fn(rev, n_levels):
    def step(state, seqs, params, consts):
        (st,) = state
        qr, zf, v = seqs
        (lb,) = params
        mats, pairs, rows, band = consts
        W = C_WIDTH
        same_head = (lax.broadcasted_iota(jnp.int32, (W, W), 0) // C_KEY
                     == lax.broadcasted_iota(jnp.int32, (W, W), 1) // C_KEY)
        lane_h = lax.broadcasted_iota(jnp.int32, (1, W), 1) // C_KEY
        q = _silu(qr)
        f = lb + (1.0 - lb) * jax.nn.sigmoid(zf)
        logf = jnp.log(f)
        kk = 1.0 - f
        qd = q * jnp.exp(mdot(mats[0], logf))
        w = kk * jnp.exp(mdot(mats[1], logf))
        total = jnp.sum(logf, axis=0, keepdims=True)
        o = _bdot(qd, st, NT)
        st_new = st * jnp.exp(total) + jnp.where(same_head, _bdot(v, w, TN), 0.0)
        scores = [None] * C_HEADS
        for n in range(n_levels):
            a = q * jnp.exp(mdot(mats[2 + 2 * n], logf)) * rows[n]
            bk = kk * jnp.exp(mdot(mats[3 + 2 * n], logf)) * (1.0 - rows[n])
            for h in range(C_HEADS):
                sc = pairs[n] * _bdot(jnp.where(lane_h == h, a, 0.0), bk, NT)
                scores[h] = sc if scores[h] is None else scores[h] + sc
        for h in range(C_HEADS):
            o = o + _bdot(scores[h], jnp.where(lane_h == h, v, 0.0))
        same_head_b = same_head.astype(BF16)
        e = jnp.zeros_like(logf)
        for d in range(HG_BAND):
            s = -d if rev else d
            if d > 0:
                e = e + sroll(logf, s + (1 if rev else -1))
            kd, vd = (kk, v) if d == 0 else (sroll(kk, s), sroll(v, s))
            p = q * kd * jnp.exp(e)
            o = o + band[d] * (_bdot(p, same_head_b) * vd)
        return (st_new,), (o,)

    return step


def hg_scan(name, rev, qr, zf, v, lower, n_ctx):
    consts = _hg_consts(rev)
    (o,) = chunk_scan(_hg_step_fn(rev, consts[1].shape[0]), name, rev, (qr, zf, v), (False,) * 3,
                      (lower.reshape(1, -1),), consts, [(C_WIDTH, C_WIDTH)], [C_WIDTH], HG_Q, n_ctx)
    return o


def conv_silu(x, w, b, n_ctx, name):
    B, L, C = x.shape
    TC = 128
    w8 = jnp.zeros((8, C), F32).at[:A_CONV].set(w)
    b2 = b.reshape(1, C)
    pad = A_CONV // 2

    def taps(v, sign):
        t = lax.broadcasted_iota(jnp.int32, v.shape, 0)
        out = []
        for k in range(A_CONV):
            s = sign * (k - pad)
            src = t + s
            ok = (src >= 0) & (src < L) & ((t >= n_ctx) == (src >= n_ctx))
            vs = v if s == 0 else pltpu.roll(v, (-s) % L, 0)
            out.append((k, jnp.where(ok, vs, 0.0)))
        return out

    def pre(x_ref, w_ref, b_ref):
        xv = x_ref[0]
        y = b_ref[...] + sum(xs * w_ref[pl.ds(k, 1), :] for k, xs in taps(xv, 1))
        return xv, y

    def fwd_body(x_ref, w_ref, b_ref, o_ref):
        _, y = pre(x_ref, w_ref, b_ref)
        o_ref[0] = _silu(y)

    def bwd_body(x_ref, w_ref, b_ref, g_ref, dx_ref, dw_ref, db_ref):
        xv, y = pre(x_ref, w_ref, b_ref)
        sg = jax.nn.sigmoid(y)
        dy = g_ref[0] * (sg + y * sg * (1.0 - sg))
        dx_ref[0] = sum(ds * w_ref[pl.ds(k, 1), :] for k, ds in taps(dy, -1))
        first = pl.program_id(1) == 0

        @pl.when(first)
        def _():
            dw_ref[...] = jnp.zeros_like(dw_ref)
            db_ref[...] = jnp.zeros_like(db_ref)

        for k, xs in taps(xv, 1):
            dw_ref[pl.ds(k, 1), :] += jnp.sum(dy * xs, axis=0, keepdims=True)
        db_ref[...] += jnp.sum(dy, axis=0, keepdims=True)

    x_spec = pl.BlockSpec((1, L, TC), lambda j, bb: (bb, 0, j))
    w_spec = pl.BlockSpec((8, TC), lambda j, bb: (0, j))
    b_spec = pl.BlockSpec((1, TC), lambda j, bb: (0, j))

    @jax.custom_vjp
    def op(x, w8, b2):
        return pl.pallas_call(fwd_body, grid=(C // TC, B), in_specs=[x_spec, w_spec, b_spec], out_specs=x_spec,
                              out_shape=jax.ShapeDtypeStruct(x.shape, F32), name=name + "_fwd",
                              compiler_params=_cp("parallel", "parallel"))(x, w8, b2)

    def op_fwd(x, w8, b2):
        return op(x, w8, b2), (x, w8, b2)

    def op_bwd(res, g):
        x, w8, b2 = res
        return tuple(pl.pallas_call(
            bwd_body, grid=(C // TC, B), in_specs=[x_spec, w_spec, b_spec, x_spec],
            out_specs=[x_spec, w_spec, b_spec],
            out_shape=[jax.ShapeDtypeStruct(x.shape, F32), jax.ShapeDtypeStruct(w8.shape, F32),
                       jax.ShapeDtypeStruct(b2.shape, F32)],
            name=name + "_bwd", compiler_params=_cp("parallel", "arbitrary"))(x, w8, b2, g))

    op.defvjp(op_fwd, op_bwd)
    return op(x, w8, b2)


def loss_head(h, target, norm_w, n_ctx):
    B, L, Dm = h.shape
    S = target.shape[1]
    tl = ROW_TILE
    skip = n_ctx // tl

    def body(h_ref, t_ref, w_ref, loss_ref, dh_ref, dw_ref):
        b, i = pl.program_id(0), pl.program_id(1)
        fn = lambda hv, wv: _rms(hv) * wv
        y, vjp = jax.vjp(fn, h_ref[0], w_ref[...])
        err = y - t_ref[0]
        dhv, dwv = vjp(err * (1.0 / Dm))
        dh_ref[0] = dhv
        part = 0.5 * jnp.sum(jnp.mean(err * err, axis=-1, keepdims=True), axis=0, keepdims=True)
        first = (b == 0) & (i == 0)

        @pl.when(first)
        def _():
            loss_ref[...] = jnp.zeros_like(loss_ref)
            dw_ref[...] = jnp.zeros_like(dw_ref)

        loss_ref[...] += jnp.broadcast_to(part, loss_ref.shape)
        dw_ref[...] += dwv

    loss, dh, dw = pl.pallas_call(
        body, grid=(B, S // tl),
        in_specs=[pl.BlockSpec((1, tl, Dm), lambda b, i: (b, i + skip, 0)),
                  pl.BlockSpec((1, tl, Dm), lambda b, i: (b, i, 0)),
                  pl.BlockSpec((1, Dm), lambda b, i: (0, 0))],
        out_specs=[pl.BlockSpec((8, 128), lambda b, i: (0, 0)),
                   pl.BlockSpec((1, tl, Dm), lambda b, i: (b, i, 0)),
                   pl.BlockSpec((1, Dm), lambda b, i: (0, 0))],
        out_shape=[jax.ShapeDtypeStruct((8, 128), F32), jax.ShapeDtypeStruct((B, S, Dm), F32),
                   jax.ShapeDtypeStruct((1, Dm), F32)],
        name="loss_head", compiler_params=_cp("arbitrary", "arbitrary"))(h, target, norm_w.reshape(1, Dm))
    dh_full = jnp.concatenate([jnp.zeros((B, n_ctx, Dm), F32), dh], axis=1)
    return loss[0, 0], dh_full, dw.reshape(Dm)


def _modulate(h, shift, scale):
    return _rms(h) * (1.0 + scale) + shift


def _f_mod(r, m, p):
    return (_modulate(r[0], m[0], m[1]),)


def _f_resid_mod(coef):
    def f(r, m, p):
        h2 = r[0] + coef * m[0] * r[1]
        return h2, _modulate(h2, m[1], m[2])
    return f


def _f_resid(coef):
    def f(r, m, p):
        return (r[0] + coef * m[0] * r[1],)
    return f


def _f_swiglu(r, m, p):
    pre = r[0].astype(F32)
    return (_silu(pre[:, :D_FF]) * pre[:, D_FF:],)


def _f_ssd_post(r, m, p):
    y_f, y_b, xs, z = r
    d_skip, norm_w = p
    y = (y_f + y_b + d_skip * xs) * _silu(z)
    return (_rms(y) * norm_w,)


def _f_s5_post(r, m, p):
    y_f, y_b, u = r
    d_skip, glu_w, glu_b = p
    y = jax.nn.gelu(y_f + y_b + d_skip * u)
    return (y * jax.nn.sigmoid(_bdot(y, glu_w) + glu_b),)


def _f_hg_post(r, m, p):
    o_f, o_b, g = r
    (norm_w,) = p
    W = C_WIDTH
    same_head = (lax.broadcasted_iota(jnp.int32, (W, W), 0) // C_KEY
                 == lax.broadcasted_iota(jnp.int32, (W, W), 1) // C_KEY).astype(BF16)
    o = o_f + o_b
    ms = xdot(o * o, same_head) * (1.0 / C_KEY)
    return (o * lax.rsqrt(ms + EPS) * norm_w * _silu(g),)


def _pad_w_in(w):
    dt0 = A_INNER + A_CONV_DIM
    zeros = jnp.zeros((w.shape[0], IN_COLS_PAD - IN_COLS), w.dtype)
    return jnp.concatenate([w[:, :dt0], w[:, A_COLS:], w[:, dt0:A_COLS], zeros], axis=1)


def _to_columns(t, n_ctx):
    B, L, W = t.shape
    rows = (L - n_ctx) // GRID_W
    lat = t[:, n_ctx:].reshape(B, rows, GRID_W, W).transpose(0, 2, 1, 3).reshape(B, L - n_ctx, W)
    return jnp.concatenate([t[:, :n_ctx], lat], axis=1)


def _to_raster(t, n_ctx):
    B, L, W = t.shape
    rows = (L - n_ctx) // GRID_W
    lat = t[:, n_ctx:].reshape(B, GRID_W, rows, W).transpose(0, 2, 1, 3).reshape(B, L - n_ctx, W)
    return jnp.concatenate([t[:, :n_ctx], lat], axis=1)


def _forward(h0, c, big, P, n_ctx):
    B, L, Dm = h0.shape
    depth = big["mod_w"].shape[0]
    ncb = n_ctx // ROW_TILE
    p_lb = jax.nn.softmax(P["hg_lb_logits"], axis=0)
    lower_bounds = jnp.cumsum(p_lb, axis=0) - p_lb[:1]
    cc = jnp.concatenate([c, P["c_ctx"][None], jnp.zeros((8 - B - 1, Dm), F32)], axis=0)
    cc = _silu(cc)

    def mods_of(l):
        m = linear(cc, big["mod_w"][l], P["mod_w"][l], f"mod{l}") + P["mod_b"][l]
        m = m.reshape(8, N_MOD, Dm)
        seg = jnp.stack([jnp.broadcast_to(m[B], (B, N_MOD, Dm)), m[:B]], axis=1)
        return [seg[:, :, j:j + 1, :] for j in range(N_MOD)]

    def ffn(u, l, j):
        pre = linear(u.reshape(B * L, Dm), big["ffn_w_in"][l, j], P["ffn_w_in"][l, j], f"ffn_in{l}{j}", BF16)
        (act,) = rowwise(_f_swiglu, f"swiglu{l}{j}", [pre.reshape(B, L, 2 * D_FF)], [], [], [(D_FF, BF16)], ncb)
        out = linear(act.reshape(B * L, D_FF), big["ffn_w_out"][l, j], P["ffn_w_out"][l, j], f"ffn_out{l}{j}")
        return out.reshape(B, L, Dm)

    h = h0
    mods = mods_of(0)
    (u,) = rowwise(_f_mod, "mod_first", [h], [mods[0], mods[1]], [], [(Dm, BF16)], ncb)
    for l in range(depth):
        col_major = l % 2 == 1
        o = ffn(u, l, 0)
        h, u = rowwise(_f_resid_mod(0.5), f"resid_a{l}", [h, o], [mods[2], mods[3], mods[4]], [],
                       [(Dm, F32), (Dm, BF16)], ncb)
        if col_major:
            u = _to_columns(u, n_ctx)
        pre = linear(u.reshape(B * L, Dm), _pad_w_in(big["w_in"][l]), _pad_w_in(P["w_in"][l]), f"w_in{l}")
        pre = pre.reshape(B, L, IN_COLS_PAD)
        o0 = 0
        z, o0 = pre[..., o0:o0 + A_INNER], o0 + A_INNER
        xbc, o0 = pre[..., o0:o0 + A_CONV_DIM], o0 + A_CONV_DIM
        pb, o0 = pre[..., o0:o0 + B_WIDTH], o0 + B_WIDTH
        pc, o0 = pre[..., o0:o0 + 5 * C_WIDTH], o0 + 5 * C_WIDTH
        dtr = pre[..., o0:o0 + 2 * A_HEADS]
        xbc = conv_silu(xbc, P["a_conv_w"][l], P["a_conv_b"][l], n_ctx, f"conv{l}")
        xs, bm, cm = xbc[..., :A_INNER], xbc[..., A_INNER:A_INNER + 128], xbc[..., A_INNER + 128:]
        ya_dir = [ssd_scan(f"ssd{l}{d}", bool(d), xs, bm, cm, dtr[..., d * A_HEADS:(d + 1) * A_HEADS],
                           P["a_dt_bias"][l, d], P["a_log"][l, d], n_ctx) for d in range(2)]
        d_skip = jnp.repeat(P["a_d"][l], A_INNER // A_HEADS).reshape(1, A_INNER)
        (ya,) = rowwise(_f_ssd_post, f"ssd_post{l}", [ya_dir[0], ya_dir[1], xs, z], [],
                        [d_skip, P["a_norm_w"][l].reshape(1, -1)], [(A_INNER, BF16)], ncb)
        yb_dir = [s5_scan(f"s5{l}{d}", bool(d), pb, P["s5_lam_re"][l, d], P["s5_lam_im"][l, d],
                          P["s5_log_step"][l, d], P["s5_b_re"][l, d], P["s5_b_im"][l, d], P["s5_c_re"][l, d],
                          P["s5_c_im"][l, d], n_ctx) for d in range(2)]
        (yb,) = rowwise(_f_s5_post, f"s5_post{l}", [yb_dir[0], yb_dir[1], pb], [],
                        [P["s5_d"][l].reshape(1, -1), P["s5_glu_w"][l], P["s5_glu_b"][l].reshape(1, -1)],
                        [(B_WIDTH, BF16)], ncb)
        qr, f_raw, vi, gate = (pc[..., :C_WIDTH], pc[..., C_WIDTH:3 * C_WIDTH], pc[..., 3 * C_WIDTH:4 * C_WIDTH],
                               pc[..., 4 * C_WIDTH:])
        yc_dir = [hg_scan(f"hg{l}{d}", bool(d), qr, f_raw[..., d * C_WIDTH:(d + 1) * C_WIDTH], vi,
                          lower_bounds[l, d], n_ctx) for d in range(2)]
        (yc,) = rowwise(_f_hg_post, f"hg_post{l}", [yc_dir[0], yc_dir[1], gate], [],
                        [P["hg_norm_w"][l].reshape(1, -1)], [(C_WIDTH, BF16)], ncb)
        mix = jnp.concatenate([ya, yb, yc], axis=-1)
        y = linear(mix.reshape(B * L, Dm), big["w_out"][l], P["w_out"][l], f"w_out{l}").reshape(B, L, Dm)
        if col_major:
            y = _to_raster(y, n_ctx)
        h, u = rowwise(_f_resid_mod(1.0), f"resid_b{l}", [h, y], [mods[5], mods[6], mods[7]], [],
                       [(Dm, F32), (Dm, BF16)], ncb)
        o = ffn(u, l, 1)
        gate8 = mods[8]
        if l + 1 < depth:
            mods = mods_of(l + 1)
            h, u = rowwise(_f_resid_mod(0.5), f"resid_c{l}", [h, o], [gate8, mods[0], mods[1]], [],
                           [(Dm, F32), (Dm, BF16)], ncb)
        else:
            (h,) = rowwise(_f_resid(0.5), f"resid_c{l}", [h, o], [gate8], [], [(Dm, F32)], ncb)
    return h


BIG = ("mod_w", "ffn_w_in", "ffn_w_out", "w_in", "w_out")


def local_step(x, c, ctx, target, big, small):
    n_ctx = ctx.shape[1]
    h0 = jnp.concatenate([ctx, x], axis=1)
    P = dict(small)
    for k in BIG:
        P[k] = jnp.zeros(big[k].shape, F32)
    h, vjp = jax.vjp(lambda h0, P: _forward(h0, c, big, P, n_ctx), h0, P)
    loss, dh, d_final = loss_head(h, target, small["final_norm_w"], n_ctx)
    dh0, grads = vjp(dh)
    grads = dict(grads)
    grads["final_norm_w"] = grads["final_norm_w"] + d_final
    return loss, dh0[:, n_ctx:], grads


MESH = pl.DeviceIdType.MESH
LANES = 1024
BIG_ROWS = 512
N_CHIPS = 4
ANY = pl.BlockSpec(memory_space=pl.ANY)


def _place():
    x, y, c = lax.axis_index("x"), lax.axis_index("y"), lax.axis_index("c")
    return x, y, c, 2 * x + y


def _other_chips(x, y):
    return [(x ^ kx, y ^ ky, 2 * (x ^ kx) + (y ^ ky)) for kx, ky in ((0, 1), (1, 0), (1, 1))]


def all_gather_shards(shard, name):
    def body(x_ref, out_ref, send_sems, recv_sems, local_sem):
        x, y, c, q = _place()
        sibling = (x, y, 1 - c)
        chips = _other_chips(x, y)

        def copy(k, src, dst, to):
            return pltpu.make_async_remote_copy(src_ref=src, dst_ref=dst, send_sem=send_sems.at[k],
                                                recv_sem=recv_sems.at[k], device_id=to, device_id_type=MESH)

        mine = pltpu.make_async_copy(x_ref, out_ref.at[q], local_sem)
        mine.start()
        first = [copy(k, x_ref.at[c], out_ref.at[q, c], (px, py, c)) for k, (px, py, _) in enumerate(chips)]
        for cp in first:
            cp.start()
        passed = [copy(3 + k, out_ref.at[pq, c], out_ref.at[pq, c], sibling) for k, (_, _, pq) in enumerate(chips)]
        for k, (_, _, pq) in enumerate(chips):
            copy(k, x_ref.at[c], out_ref.at[pq, c], sibling).wait_recv()
            passed[k].start()
        for k, (_, _, pq) in enumerate(chips):
            copy(3 + k, x_ref.at[c], out_ref.at[pq, 1 - c], sibling).wait_recv()
        for cp in first + passed:
            cp.wait_send()
        mine.wait()

    return pl.pallas_call(
        body, out_shape=jax.ShapeDtypeStruct((N_CHIPS,) + shard.shape, shard.dtype), in_specs=[ANY], out_specs=ANY,
        scratch_shapes=[pltpu.SemaphoreType.DMA((6,)), pltpu.SemaphoreType.DMA((6,)), pltpu.SemaphoreType.DMA],
        name=name)(shard)


def all_gather_devices(part, name):
    def body(x_ref, out_ref, send_sems, recv_sems, local_sem):
        x, y, c, _ = _place()
        me = 4 * x + 2 * y + c
        mine = pltpu.make_async_copy(x_ref, out_ref.at[me], local_sem)
        mine.start()
        copies = []
        for k in range(1, 8):
            px, py, pc = x ^ (k >> 2), y ^ ((k >> 1) & 1), c ^ (k & 1)
            copies.append(pltpu.make_async_remote_copy(
                src_ref=x_ref, dst_ref=out_ref.at[me], send_sem=send_sems.at[k - 1], recv_sem=recv_sems.at[k - 1],
                device_id=(px, py, pc), device_id_type=MESH))
            copies[-1].start()
        for k in range(1, 8):
            peer = 4 * (x ^ (k >> 2)) + 2 * (y ^ ((k >> 1) & 1)) + (c ^ (k & 1))
            pltpu.make_async_remote_copy(
                src_ref=x_ref, dst_ref=out_ref.at[peer], send_sem=send_sems.at[k - 1], recv_sem=recv_sems.at[k - 1],
                device_id=(x, y, c), device_id_type=MESH).wait_recv()
        for cp in copies:
            cp.wait_send()
        mine.wait()

    return pl.pallas_call(
        body, out_shape=jax.ShapeDtypeStruct((8,) + part.shape, part.dtype), in_specs=[ANY], out_specs=ANY,
        scratch_shapes=[pltpu.SemaphoreType.DMA((7,)), pltpu.SemaphoreType.DMA((7,)), pltpu.SemaphoreType.DMA],
        name=name)(part)


def send_half_to_sibling(p2):
    def body(p_ref, out_ref, send_sem, recv_sem):
        x, y, c, _ = _place()
        cp = pltpu.make_async_remote_copy(src_ref=p_ref.at[1 - c], dst_ref=out_ref, send_sem=send_sem, recv_sem=recv_sem,
                                          device_id=(x, y, 1 - c), device_id_type=MESH)
        cp.start()
        cp.wait()

    return pl.pallas_call(
        body, out_shape=jax.ShapeDtypeStruct(p2.shape[1:], p2.dtype), in_specs=[ANY], out_specs=ANY,
        scratch_shapes=[pltpu.SemaphoreType.DMA, pltpu.SemaphoreType.DMA], name="rs_sibling")(p2)


def exchange_pieces(pair):
    def body(p_ref, out_ref, send_sems, recv_sems, local_sem):
        x, y, c, q = _place()
        chips = _other_chips(x, y)
        mine = pltpu.make_async_copy(p_ref.at[q], out_ref.at[q], local_sem)
        mine.start()
        copies = [pltpu.make_async_remote_copy(
            src_ref=p_ref.at[pq], dst_ref=out_ref.at[q], send_sem=send_sems.at[k], recv_sem=recv_sems.at[k],
            device_id=(px, py, c), device_id_type=MESH) for k, (px, py, pq) in enumerate(chips)]
        for cp in copies:
            cp.start()
        for k, (_, _, pq) in enumerate(chips):
            pltpu.make_async_remote_copy(
                src_ref=p_ref.at[q], dst_ref=out_ref.at[pq], send_sem=send_sems.at[k], recv_sem=recv_sems.at[k],
                device_id=(x, y, c), device_id_type=MESH).wait_recv()
        for cp in copies:
            cp.wait_send()
        mine.wait()

    return pl.pallas_call(
        body, out_shape=jax.ShapeDtypeStruct(pair.shape, pair.dtype), in_specs=[ANY], out_specs=ANY,
        scratch_shapes=[pltpu.SemaphoreType.DMA((3,)), pltpu.SemaphoreType.DMA((3,)), pltpu.SemaphoreType.DMA],
        name="rs_chips")(pair)


def exchange_halves(red):
    def body(r_ref, out_ref, send_sem, recv_sem, local_sem):
        x, y, c, _ = _place()
        mine = pltpu.make_async_copy(r_ref, out_ref.at[c], local_sem)
        mine.start()
        cp = pltpu.make_async_remote_copy(src_ref=r_ref, dst_ref=out_ref.at[c], send_sem=send_sem, recv_sem=recv_sem,
                                          device_id=(x, y, 1 - c), device_id_type=MESH)
        cp.start()
        pltpu.make_async_remote_copy(src_ref=r_ref, dst_ref=out_ref.at[1 - c], send_sem=send_sem, recv_sem=recv_sem,
                                     device_id=(x, y, 1 - c), device_id_type=MESH).wait_recv()
        cp.wait_send()
        mine.wait()

    return pl.pallas_call(
        body, out_shape=jax.ShapeDtypeStruct((2,) + red.shape, red.dtype), in_specs=[ANY], out_specs=ANY,
        scratch_shapes=[pltpu.SemaphoreType.DMA, pltpu.SemaphoreType.DMA, pltpu.SemaphoreType.DMA],
        name="rs_halves")(red)


def _row_tile(R):
    return _pick(R, (512, 256, 128, 64, 32, 16, 8))


def add_own_half(p2, got):
    _, Q4, R, _ = p2.shape
    rt = _row_tile(R)

    def body(p_ref, g_ref, o_ref):
        c = lax.axis_index("c")

        @pl.when(c == 0)
        def _():
            o_ref[0] = p_ref[0, 0] + g_ref[0]

        @pl.when(c != 0)
        def _():
            o_ref[0] = g_ref[0] + p_ref[1, 0]

    return pl.pallas_call(
        body, grid=(Q4, R // rt),
        in_specs=[pl.BlockSpec((2, 1, rt, LANES), lambda q, i: (0, q, i, 0)),
                  pl.BlockSpec((1, rt, LANES), lambda q, i: (q, i, 0))],
        out_specs=pl.BlockSpec((1, rt, LANES), lambda q, i: (q, i, 0)),
        out_shape=jax.ShapeDtypeStruct(got.shape, F32), name="rs_add_pair",
        compiler_params=_cp("parallel", "parallel"))(p2, got)


def sum_parts(parts, name):
    P, R, _ = parts.shape
    rt = _row_tile(R)

    def body(p_ref, o_ref):
        acc = p_ref[0]
        for p in range(1, P):
            acc = acc + p_ref[p]
        o_ref[...] = acc

    return pl.pallas_call(
        body, grid=(R // rt,), in_specs=[pl.BlockSpec((P, rt, LANES), lambda i: (0, i, 0))],
        out_specs=pl.BlockSpec((rt, LANES), lambda i: (i, 0)), out_shape=jax.ShapeDtypeStruct((R, LANES), F32),
        name=name, compiler_params=_cp("parallel"))(parts)


def adamw(g, w, m, v, name):
    R = g.shape[0]
    rt = _row_tile(R)

    def body(g_ref, w_ref, m_ref, v_ref, d_ref, m2_ref, v2_ref):
        gv = g_ref[...]
        m2 = ADAM_B1 * m_ref[...] + (1.0 - ADAM_B1) * gv
        v2 = ADAM_B2 * v_ref[...] + (1.0 - ADAM_B2) * (gv * gv)
        m_hat = m2 / (1.0 - ADAM_B1 ** ADAM_STEP)
        v_hat = v2 / (1.0 - ADAM_B2 ** ADAM_STEP)
        d_ref[...] = -ADAM_LR * (m_hat / (jnp.sqrt(v_hat) + ADAM_EPS) + ADAM_WD * w_ref[...])
        m2_ref[...] = m2
        v2_ref[...] = v2

    spec = pl.BlockSpec((rt, LANES), lambda i: (i, 0))
    return pl.pallas_call(
        body, grid=(R // rt,), in_specs=[spec] * 4, out_specs=[spec] * 3,
        out_shape=[jax.ShapeDtypeStruct((R, LANES), F32)] * 3, name=name, compiler_params=_cp("parallel"))(g, w, m, v)


def _pack(arrays, rows_multiple, dtype):
    flat = jnp.concatenate([a.reshape(-1).astype(dtype) for a in arrays])
    n = flat.shape[0]
    per = rows_multiple * LANES
    total = -(-n // per) * per
    return jnp.pad(flat, (0, total - n)).reshape(total // LANES, LANES)


def _unpack(buf, shapes):
    flat = buf.reshape(-1)
    out, off = [], 0
    for s in shapes:
        n = math.prod(s)
        out.append(flat[off:off + n].reshape(s))
        off += n
    return out


SHARDED = (("mod_w", 2), ("ffn_w_in", 3), ("ffn_w_out", 2), ("w_in", 2), ("w_out", 1),
           ("a_conv_w", 2), ("s5_glu_w", 1), ("hg_lb_logits", 2))
WEIGHTS = ("c_ctx", "mod_w", "mod_b", "ffn_w_in", "ffn_w_out", "w_in", "w_out", "a_conv_w", "a_conv_b", "a_dt_bias",
           "a_log", "a_d", "a_norm_w", "s5_lam_re", "s5_lam_im", "s5_log_step", "s5_b_re", "s5_b_im", "s5_c_re",
           "s5_c_im", "s5_d", "s5_glu_w", "s5_glu_b", "hg_lb_logits", "hg_norm_w", "final_norm_w")


def _gather_full(local, names_axes, dtype, rows_multiple, name):
    buf = _pack([local[n] for n, _ in names_axes], 2 * rows_multiple, dtype)
    R = buf.shape[0] // 2
    full = all_gather_shards(buf.reshape(2, R, LANES), name).reshape(N_CHIPS, 2 * R, LANES)
    per_chip = [_unpack(full[q], [local[n].shape for n, _ in names_axes]) for q in range(N_CHIPS)]
    return {n: jnp.concatenate([per_chip[q][j] for q in range(N_CHIPS)], axis=ax)
            for j, (n, ax) in enumerate(names_axes)}


def kernel(x, c, ctx, c_ctx, mod_w, mod_b, ffn_w_in, ffn_w_out, w_in, w_out, a_conv_w, a_conv_b, a_dt_bias, a_log, a_d, a_norm_w, s5_lam_re, s5_lam_im, s5_log_step, s5_b_re, s5_b_im, s5_c_re, s5_c_im, s5_d, s5_glu_w, s5_glu_b, hg_lb_logits, hg_norm_w, final_norm_w, loss_target, m_c_ctx, m_mod_w, m_mod_b, m_ffn_w_in, m_ffn_w_out, m_w_in, m_w_out, m_a_conv_w, m_a_conv_b, m_a_dt_bias, m_a_log, m_a_d, m_a_norm_w, m_s5_lam_re, m_s5_lam_im, m_s5_log_step, m_s5_b_re, m_s5_b_im, m_s5_c_re, m_s5_c_im, m_s5_d, m_s5_glu_w, m_s5_glu_b, m_hg_lb_logits, m_hg_norm_w, m_final_norm_w, v_c_ctx, v_mod_w, v_mod_b, v_ffn_w_in, v_ffn_w_out, v_w_in, v_w_out, v_a_conv_w, v_a_conv_b, v_a_dt_bias, v_a_log, v_a_d, v_a_norm_w, v_s5_lam_re, v_s5_lam_im, v_s5_log_step, v_s5_b_re, v_s5_b_im, v_s5_c_re, v_s5_c_im, v_s5_d, v_s5_glu_w, v_s5_glu_b, v_hg_lb_logits, v_hg_norm_w, v_final_norm_w):
    given = dict(locals())
    w = {n: given[n] for n in WEIGHTS}
    m = {n: given["m_" + n] for n in WEIGHTS}
    v = {n: given["v_" + n] for n in WEIGHTS}
    sharded_names = [n for n, _ in SHARDED]
    replicated = [n for n in WEIGHTS if n not in sharded_names]

    big = _gather_full(w, SHARDED[:len(BIG)], BF16, BIG_ROWS, "gather_big")
    small = {n: w[n] for n in replicated}
    small.update(_gather_full(w, SHARDED[len(BIG):], F32, 8, "gather_small"))

    loss, grad_x, grads = local_step(x, c, ctx, loss_target, big, small)

    def pieces_for(q):
        parts = []
        for n, ax in SHARDED:
            width = grads[n].shape[ax] // N_CHIPS
            parts.append(lax.slice_in_dim(grads[n], q * width, (q + 1) * width, axis=ax))
        return _pack(parts, 2 * BIG_ROWS, F32)

    pieces = jnp.stack([pieces_for(q) for q in range(N_CHIPS)])
    R = pieces.shape[1] // 2
    p2 = pieces.reshape(N_CHIPS, 2, R, LANES).transpose(1, 0, 2, 3)
    pair = add_own_half(p2, send_half_to_sibling(p2))
    red = sum_parts(exchange_pieces(pair), "rs_sum_chips")
    g_sh = exchange_halves(red).reshape(2 * R, LANES)
    d_sh, m_sh, v_sh = adamw(g_sh, _pack([w[n] for n in sharded_names], 2 * BIG_ROWS, F32),
                             _pack([m[n] for n in sharded_names], 2 * BIG_ROWS, F32),
                             _pack([v[n] for n in sharded_names], 2 * BIG_ROWS, F32), "adamw_sharded")
    shapes = [w[n].shape for n in sharded_names]
    out = {}
    for kind, buf in (("grad", g_sh), ("delta", d_sh), ("new_m", m_sh), ("new_v", v_sh)):
        for n, a in zip(sharded_names, _unpack(buf, shapes)):
            out[kind, n] = a

    part = _pack([grads[n] for n in replicated] + [loss.reshape(1)], 8, F32)
    g_rep = sum_parts(all_gather_devices(part, "gather_parts"), "sum_parts")
    zero = jnp.zeros((1,), F32)
    d_rep, m_rep, v_rep = adamw(g_rep, _pack([w[n] for n in replicated] + [zero], 8, F32),
                                _pack([m[n] for n in replicated] + [zero], 8, F32),
                                _pack([v[n] for n in replicated] + [zero], 8, F32), "adamw_replicated")
    shapes = [w[n].shape for n in replicated] + [(1,)]
    for kind, buf in (("grad", g_rep), ("delta", d_rep), ("new_m", m_rep), ("new_v", v_rep)):
        for n, a in zip(replicated + ["loss"], _unpack(buf, shapes)):
            out[kind, n] = a

    loss_total = out["grad", "loss"].reshape(())
    return (loss_total, grad_x, *[out["grad", n] for n in WEIGHTS], *[out["delta", n] for n in WEIGHTS],
            *[out["new_m", n] for n in WEIGHTS], *[out["new_v", n] for n in WEIGHTS])
```

```python
import functools
import math

import numpy as np
import jax
import jax.numpy as jnp
from jax import lax
from jax.experimental import pallas as pl
from jax.experimental.pallas import tpu as pltpu

F32, BF16 = jnp.float32, jnp.bfloat16
EPS = 1e-6
N_MOD = 9
D_FF = 2816
A_INNER, A_HEADS, A_GROUPS, A_STATE, A_CONV, A_CONV_DIM = 512, 8, 2, 64, 5, 768
A_COLS = A_INNER + A_CONV_DIM + 2 * A_HEADS
B_WIDTH, B_GROUP, B_NGROUPS, B_STATE = 256, 16, 16, 64
C_WIDTH, C_HEADS, C_KEY = 256, 4, 64
GRID_W = 64
IN_COLS = A_COLS + B_WIDTH + 5 * C_WIDTH
IN_COLS_PAD = 3072
ADAM_LR, ADAM_B1, ADAM_B2, ADAM_EPS, ADAM_WD, ADAM_STEP = 0.001, 0.9, 0.999, 1e-08, 0.01, 10

ROW_TILE = 256
SSD_Q, S5_Q, HG_Q = 128, 64, 64
HG_BAND = 8
VMEM_LIMIT = 56 * 1024 * 1024
MM_VMEM_BUDGET = 36 * 1024 * 1024

NT = ((1,), (1,))
TN = ((0,), (0,))


def _bdot(a, b, dims=((1,), (0,))):
    return lax.dot_general(a.astype(BF16), b.astype(BF16), (dims, ((), ())), preferred_element_type=F32)


def _split3(x):
    hi = x.astype(BF16)
    r = x - hi.astype(F32)
    mid = r.astype(BF16)
    lo = (r - mid.astype(F32)).astype(BF16)
    return hi, mid, lo


def _mask_lhs(m, x, dims):
    return sum(lax.dot_general(m, p, (dims, ((), ())), preferred_element_type=F32) for p in _split3(x))


def _mask_rhs(x, m, dims):
    return sum(lax.dot_general(p, m, (dims, ((), ())), preferred_element_type=F32) for p in _split3(x))


@jax.custom_vjp
def mdot(m, x):
    return _mask_lhs(m, x, ((1,), (0,)))


def _mdot_fwd(m, x):
    return mdot(m, x), m


def _mdot_bwd(m, g):
    return jnp.zeros_like(m), _mask_lhs(m, g, TN)


mdot.defvjp(_mdot_fwd, _mdot_bwd)


@jax.custom_vjp
def xdot(x, m):
    return _mask_rhs(x, m, ((1,), (0,)))


def _xdot_fwd(x, m):
    return xdot(x, m), m


def _xdot_bwd(m, g):
    return _mask_rhs(g, m, NT), jnp.zeros_like(m)


xdot.defvjp(_xdot_fwd, _xdot_bwd)


@functools.partial(jax.custom_vjp, nondiff_argnums=(1,))
def _roll(x, s):
    return pltpu.roll(x, s, 0)


def _roll_fwd(x, s):
    return _roll(x, s), None


def _roll_bwd(s, _, g):
    return (_roll(g, x_rows(g) - s),)


def x_rows(x):
    return x.shape[0]


_roll.defvjp(_roll_fwd, _roll_bwd)


def sroll(x, s):
    s = s % x.shape[0]
    return x if s == 0 else _roll(x, s)


def _softplus(x):
    return jnp.maximum(x, 0.0) + jnp.log1p(jnp.exp(jnp.minimum(x, -x)))


def _silu(x):
    return x * jax.nn.sigmoid(x)


def _rms(x):
    return x * lax.rsqrt(jnp.mean(x * x, axis=-1, keepdims=True) + EPS)


def _pick(n, cands):
    for c in cands:
        if n % c == 0:
            return c
    return n


def _cp(*sem):
    return pltpu.CompilerParams(dimension_semantics=sem, vmem_limit_bytes=VMEM_LIMIT)


def _tile(n, unit, cap):
    best = None
    for d in range(unit, min(n, cap) + 1, unit):
        if n % d == 0:
            best = d
    return best if best is not None else n


def _mm(a, b, ta, tb, out_dtype, name, b_pieces=False, out_pieces=False):
    M, K = (a.shape[1], a.shape[0]) if ta else a.shape
    n_div = k_div = None
    if b_pieces and tb:
        N, k_div = b.shape[1], b.shape[2]
    elif b_pieces:
        N, n_div = N_CHIPS * b.shape[2], b.shape[2]
    else:
        N = b.shape[0] if tb else b.shape[1]
    if out_pieces:
        n_div = N // N_CHIPS
    n_div, k_div = n_div or N, k_div or K
    tm = _tile(M, 128 if ta else 8, 1536 if ta else 1024)
    tn = _tile(n_div, 128, 1536)
    a_bytes, b_bytes, o_bytes = a.dtype.itemsize, b.dtype.itemsize, jnp.dtype(out_dtype).itemsize
    cands = {d for d in range(128, min(k_div, 2816) + 1, 128) if k_div % d == 0}
    if k_div <= 2816 or not cands:
        cands.add(k_div)
    for tk in sorted(cands, reverse=True):
        scratch = tm * tn * 4 if (K // tk > 1 and o_bytes != 4) else 0
        if 2 * (tm * tk * a_bytes + tk * tn * b_bytes + tm * tn * o_bytes) + scratch <= MM_VMEM_BUDGET:
            break
    nk = K // tk
    nq_n, nq_k = n_div // tn, k_div // tk
    dims = ((0 if ta else 1,), (1 if tb else 0,))

    def body(a_ref, b_ref, o_ref, *acc):
        part = _bdot(a_ref[...], b_ref[...], dims)
        if nk == 1:
            o_ref[...] = part.astype(o_ref.dtype)
            return
        acc_ref = acc[0] if acc else o_ref
        k = pl.program_id(2)

        @pl.when(k == 0)
        def _():
            acc_ref[...] = part

        @pl.when(k > 0)
        def _():
            acc_ref[...] += part

        if acc:
            @pl.when(k == nk - 1)
            def _():
                o_ref[...] = acc_ref[...].astype(o_ref.dtype)

    a_spec = pl.BlockSpec((tk, tm), lambda i, j, k: (k, i)) if ta else pl.BlockSpec((tm, tk), lambda i, j, k: (i, k))
    if b_pieces and tb:
        b_spec = pl.BlockSpec((None, tn, tk), lambda i, j, k: (k // nq_k, j, k % nq_k))
    elif b_pieces:
        b_spec = pl.BlockSpec((None, tk, tn), lambda i, j, k: (j // nq_n, k, j % nq_n))
    elif tb:
        b_spec = pl.BlockSpec((tn, tk), lambda i, j, k: (j, k))
    else:
        b_spec = pl.BlockSpec((tk, tn), lambda i, j, k: (k, j))
    if out_pieces:
        o_spec = pl.BlockSpec((None, tm, tn), lambda i, j, k: (j // nq_n, i, j % nq_n))
        o_shape = jax.ShapeDtypeStruct((N_CHIPS, M, N // N_CHIPS), out_dtype)
    else:
        o_spec = pl.BlockSpec((tm, tn), lambda i, j, k: (i, j))
        o_shape = jax.ShapeDtypeStruct((M, N), out_dtype)
    return pl.pallas_call(
        body, grid=(M // tm, N // tn, nk), in_specs=[a_spec, b_spec], out_specs=o_spec, out_shape=o_shape,
        scratch_shapes=[pltpu.VMEM((tm, tn), F32)] if (nk > 1 and o_bytes != 4) else [], name=name,
        compiler_params=_cp("parallel", "parallel", "arbitrary"))(a, b)


def linear(x, w16, wslot, name, out_dtype=F32, pieces=False):
    @jax.custom_vjp
    def f(x, w16, wslot):
        return _mm(x, w16, False, False, out_dtype, name + "_fwd", b_pieces=pieces)

    def fwd(x, w16, wslot):
        return f(x, w16, wslot), (x, w16)

    def bwd(res, dy):
        x, w16 = res
        dx = _mm(dy, w16, False, True, x.dtype, name + "_dx", b_pieces=pieces)
        dw = _mm(x, dy, True, False, F32, name + "_dw", out_pieces=pieces)
        return dx, jnp.zeros_like(w16), dw

    f.defvjp(fwd, bwd)
    return f(x, w16, wslot)


def rowwise(f, name, rows, mods, params, outs, n_ctx_blocks, tl=ROW_TILE, row_diff=None):
    rows, mods, params = tuple(rows), tuple(mods), tuple(params)
    nr, nm, npar, no = len(rows), len(mods), len(params), len(outs)
    B, L = rows[0].shape[:2]
    nblk = L // tl
    row_diff = tuple(row_diff) if row_diff is not None else (True,) * nr
    out_dtypes = [dt for _, dt in outs]

    def fcast(r, m, p):
        return tuple(o.astype(dt) for o, dt in zip(f(r, m, p), out_dtypes))

    def seg(i):
        return (i >= n_ctx_blocks).astype(jnp.int32) if n_ctx_blocks else 0

    def specs():
        row_specs = [pl.BlockSpec((1, tl, r.shape[2]), lambda b, i: (b, i, 0)) for r in rows]
        mod_specs = [pl.BlockSpec((1, 1, 1, m.shape[3]), lambda b, i: (b, seg(i), 0, 0)) for m in mods]
        par_specs = [pl.BlockSpec(p.shape, lambda b, i: (0, 0)) for p in params]
        out_specs = [pl.BlockSpec((1, tl, w), lambda b, i: (b, i, 0)) for w, _ in outs]
        return row_specs, mod_specs, par_specs, out_specs

    def load(ins):
        r = tuple(x[0] for x in ins[:nr])
        m = tuple(x[0, 0] for x in ins[nr:nr + nm])
        p = tuple(x[...] for x in ins[nr + nm:])
        return r, m, p

    def fwd_call(rows, mods, params):
        def body(*refs):
            r, m, p = load(refs[:nr + nm + npar])
            for o_ref, o in zip(refs[nr + nm + npar:], fcast(r, m, p)):
                o_ref[0] = o

        rs, ms, ps, os_ = specs()
        return pl.pallas_call(
            body, grid=(B, nblk), in_specs=rs + ms + ps, out_specs=os_,
            out_shape=[jax.ShapeDtypeStruct((B, L, w), dt) for w, dt in outs],
            name=name + "_fwd", compiler_params=_cp("parallel", "parallel"))(*rows, *mods, *params)

    def bwd_call(rows, mods, params, douts):
        didx = [j for j in range(nr) if row_diff[j]]

        def body(*refs):
            n_in = nr + nm + npar
            r, m, p = load(refs[:n_in])
            dins = refs[n_in:n_in + no]
            rest = refs[n_in + no:]
            dr_refs, dm_refs, dp_refs = rest[:len(didx)], rest[len(didx):len(didx) + nm], rest[len(didx) + nm:]
            _, vjp = jax.vjp(fcast, r, m, p)
            dr, dm, dp = vjp(tuple(d[0] for d in dins))
            for ref, j in zip(dr_refs, didx):
                ref[0] = dr[j].astype(ref.dtype)
            b, i = pl.program_id(0), pl.program_id(1)
            first_m = (i == 0) | (i == n_ctx_blocks) if n_ctx_blocks else (i == 0)
            first_p = (b == 0) & (i == 0)
            for ref, g in zip(dm_refs, dm):
                @pl.when(first_m)
                def _(ref=ref, g=g):
                    ref[0, 0] = g

                @pl.when(jnp.logical_not(first_m))
                def _(ref=ref, g=g):
                    ref[0, 0] += g
            for ref, g in zip(dp_refs, dp):
                @pl.when(first_p)
                def _(ref=ref, g=g):
                    ref[...] = g

                @pl.when(jnp.logical_not(first_p))
                def _(ref=ref, g=g):
                    ref[...] += g

        rs, ms, ps, os_ = specs()
        out_shape = ([jax.ShapeDtypeStruct(rows[j].shape, rows[j].dtype) for j in didx]
                     + [jax.ShapeDtypeStruct(m.shape, F32) for m in mods]
                     + [jax.ShapeDtypeStruct(p.shape, F32) for p in params])
        res = pl.pallas_call(
            body, grid=(B, nblk), in_specs=rs + ms + ps + os_,
            out_specs=[rs[j] for j in didx] + ms + ps, out_shape=out_shape,
            name=name + "_bwd", compiler_params=_cp("arbitrary", "arbitrary"))(*rows, *mods, *params, *douts)
        dr = [None] * nr
        for j, g in zip(didx, res[:len(didx)]):
            dr[j] = g
        dr = tuple(g if g is not None else jnp.zeros_like(rows[j]) for j, g in enumerate(dr))
        return dr, tuple(res[len(didx):len(didx) + nm]), tuple(res[len(didx) + nm:])

    @jax.custom_vjp
    def op(rows, mods, params):
        return tuple(fwd_call(rows, mods, params))

    def op_fwd(rows, mods, params):
        return op(rows, mods, params), (rows, mods, params)

    def op_bwd(res, douts):
        return bwd_call(*res, tuple(douts))

    op.defvjp(op_fwd, op_bwd)
    return op(rows, mods, params)


def chunk_scan(step, name, rev, seqs, seq_t, params, consts, state_shapes, out_widths, Q, n_ctx):
    seqs, params, consts = tuple(seqs), tuple(params), tuple(consts)
    ns, npar, nc, nst, no = len(seqs), len(params), len(consts), len(state_shapes), len(out_widths)
    B = seqs[0].shape[0]
    L = seqs[0].shape[2] if seq_t[0] else seqs[0].shape[1]
    nck, ncc = L // Q, n_ctx // Q

    def chunk_of(k):
        if not rev:
            return k
        return jnp.where(k < ncc, ncc - 1 - k, nck + ncc - 1 - k)

    def specs(order):
        def seq_spec(s, t):
            if t:
                return pl.BlockSpec((1, s.shape[1], Q), lambda b, k: (b, 0, chunk_of(order(k))))
            return pl.BlockSpec((1, Q, s.shape[2]), lambda b, k: (b, chunk_of(order(k)), 0))

        seq_specs = [seq_spec(s, t) for s, t in zip(seqs, seq_t)]
        par_specs = [pl.BlockSpec(p.shape, lambda b, k: (0, 0)) for p in params]
        con_specs = [pl.BlockSpec(c.shape, lambda b, k, nd=c.ndim: (0,) * nd) for c in consts]
        out_specs = [pl.BlockSpec((1, Q, w), lambda b, k: (b, chunk_of(order(k)), 0)) for w in out_widths]
        sav_specs = [pl.BlockSpec((1, 1) + tuple(s), lambda b, k: (b, order(k), 0, 0)) for s in state_shapes]
        return seq_specs, par_specs, con_specs, out_specs, sav_specs

    def fwd_call(seqs, params):
        def body(*refs):
            seq_refs = refs[:ns]
            par_refs = refs[ns:ns + npar]
            con_refs = refs[ns + npar:ns + npar + nc]
            rest = refs[ns + npar + nc:]
            out_refs, sav_refs, st_refs = rest[:no], rest[no:no + nst], rest[no + nst:]

            @pl.when(pl.program_id(1) == 0)
            def _():
                for s in st_refs:
                    s[...] = jnp.zeros_like(s)

            s_in = tuple(s[...] for s in st_refs)
            for sv, s in zip(sav_refs, s_in):
                sv[0, 0] = s
            s_new, outs = step(s_in, tuple(r[0] for r in seq_refs), tuple(p[...] for p in par_refs),
                               tuple(c[...] for c in con_refs))
            for s_ref, s in zip(st_refs, s_new):
                s_ref[...] = s
            for o_ref, o in zip(out_refs, outs):
                o_ref[0] = o

        ss, ps, cs, os_, vs = specs(lambda k: k)
        res = pl.pallas_call(
            body, grid=(B, nck), in_specs=ss + ps + cs, out_specs=os_ + vs,
            out_shape=[jax.ShapeDtypeStruct((B, L, w), F32) for w in out_widths]
            + [jax.ShapeDtypeStruct((B, nck) + tuple(s), F32) for s in state_shapes],
            scratch_shapes=[pltpu.VMEM(tuple(s), F32) for s in state_shapes],
            name=name + "_fwd", compiler_params=_cp("parallel", "arbitrary"))(*seqs, *params, *consts)
        return tuple(res[:no]), tuple(res[no:])

    def bwd_call(seqs, params, saved, douts):
        def body(*refs):
            seq_refs = refs[:ns]
            par_refs = refs[ns:ns + npar]
            con_refs = refs[ns + npar:ns + npar + nc]
            rest = refs[ns + npar + nc:]
            sav_refs, dout_refs = rest[:nst], rest[nst:nst + no]
            rest = rest[nst + no:]
            dseq_refs, dpar_refs, dst_refs = rest[:ns], rest[ns:ns + npar], rest[ns + npar:]
            b, k = pl.program_id(0), pl.program_id(1)

            @pl.when(k == 0)
            def _():
                for s in dst_refs:
                    s[...] = jnp.zeros_like(s)

            cvals = tuple(c[...] for c in con_refs)
            _, vjp = jax.vjp(lambda s, x, p: step(s, x, p, cvals), tuple(s[0, 0] for s in sav_refs),
                             tuple(r[0] for r in seq_refs), tuple(p[...] for p in par_refs))
            ds, dx, dp = vjp((tuple(s[...] for s in dst_refs), tuple(d[0] for d in dout_refs)))
            for s_ref, s in zip(dst_refs, ds):
                s_ref[...] = s
            for x_ref, x in zip(dseq_refs, dx):
                x_ref[0] = x
            first = (b == 0) & (k == 0)
            for ref, g in zip(dpar_refs, dp):
                @pl.when(first)
                def _(ref=ref, g=g):
                    ref[...] = g

                @pl.when(jnp.logical_not(first))
                def _(ref=ref, g=g):
                    ref[...] += g

        ss, ps, cs, os_, vs = specs(lambda k: nck - 1 - k)
        res = pl.pallas_call(
            body, grid=(B, nck), in_specs=ss + ps + cs + vs + os_, out_specs=ss + ps,
            out_shape=[jax.ShapeDtypeStruct(s.shape, F32) for s in seqs]
            + [jax.ShapeDtypeStruct(p.shape, F32) for p in params],
            scratch_shapes=[pltpu.VMEM(tuple(s), F32) for s in state_shapes],
            name=name + "_bwd", compiler_params=_cp("arbitrary", "arbitrary"))(*seqs, *params, *consts, *saved, *douts)
        return tuple(res[:ns]), tuple(res[ns:])

    @jax.custom_vjp
    def op(seqs, params):
        return fwd_call(seqs, params)[0]

    def op_fwd(seqs, params):
        outs, saved = fwd_call(seqs, params)
        return outs, (seqs, params, saved)

    def op_bwd(res, douts):
        seqs, params, saved = res
        return bwd_call(seqs, params, saved, tuple(douts))

    op.defvjp(op_fwd, op_bwd)
    return op(seqs, params)


def _positions(Q, rev):
    t = np.arange(Q)
    return (Q - 1 - t) if rev else t


def _ssd_consts(rev):
    Q = SSD_Q
    pos = _positions(Q, rev)
    tri = pos[:, None] >= pos[None, :]
    head_of = np.arange(A_INNER) // (A_INNER // A_HEADS)
    expand = np.arange(A_HEADS)[:, None] == head_of[None, :]
    group_of_row = np.arange(A_GROUPS * A_STATE) // A_STATE
    block = group_of_row[:, None] == (head_of // (A_HEADS // A_GROUPS))[None, :]
    as_bf = lambda m: jnp.asarray(m.astype(np.float32), BF16)
    return (as_bf(tri), as_bf(tri.T), jnp.asarray(tri.astype(np.float32)), as_bf(expand),
            jnp.asarray(block.astype(np.float32)))


def _ssd_step(state, seqs, params, consts):
    (st,) = state
    xs, bm, cm, dtr, dtr_t = seqs
    bias, bias_t, alog, alog_t = params
    tri_b, tri_t_b, tri_f, expand, block = consts
    dt = _softplus(dtr + bias)
    dta = dt * (-jnp.exp(alog))
    dt_t = _softplus(dtr_t + bias_t)
    dta_t = dt_t * (-jnp.exp(alog_t))
    cum = mdot(tri_b, dta)
    cum_t = xdot(dta_t, tri_t_b)
    total = jnp.sum(dta, axis=0, keepdims=True)
    w_end = jnp.exp(total - cum) * dt
    ecum_x = xdot(jnp.exp(cum), expand)
    wend_x = xdot(w_end, expand)
    dec_x = jnp.exp(jnp.sum(xdot(dta, expand), axis=0, keepdims=True))
    y = _bdot(cm, st) * ecum_x
    st_new = st * dec_x + block * _bdot(bm, xs * wend_x, TN)
    lane_g = lax.broadcasted_iota(jnp.int32, (1, A_GROUPS * A_STATE), 1) // A_STATE
    lane_h = lax.broadcasted_iota(jnp.int32, (1, A_INNER), 1) // (A_INNER // A_HEADS)
    sub_h = lax.broadcasted_iota(jnp.int32, (A_HEADS, 1), 0)
    lane8 = lax.broadcasted_iota(jnp.int32, (1, A_HEADS), 1)
    for g in range(A_GROUPS):
        cb = _bdot(jnp.where(lane_g == g, cm, 0.0), bm, NT)
        for h in range(g * (A_HEADS // A_GROUPS), (g + 1) * (A_HEADS // A_GROUPS)):
            col = jnp.sum(jnp.where(lane8 == h, cum, 0.0), axis=1, keepdims=True)
            row = jnp.sum(jnp.where(sub_h == h, cum_t, 0.0), axis=0, keepdims=True)
            dt_row = jnp.sum(jnp.where(sub_h == h, dt_t, 0.0), axis=0, keepdims=True)
            decay = tri_f * jnp.exp(tri_f * (col - row))
            y = y + _bdot(cb * decay * dt_row, jnp.where(lane_h == h, xs, 0.0))
    return (st_new,), (y,)


def ssd_scan(name, rev, xs, bm, cm, dtr, dt_bias, a_log, n_ctx):
    seqs = (xs, bm, cm, dtr, jnp.swapaxes(dtr, 1, 2))
    params = (dt_bias.reshape(1, -1), dt_bias.reshape(-1, 1), a_log.reshape(1, -1), a_log.reshape(-1, 1))
    (y,) = chunk_scan(_ssd_step, name, rev, seqs, (False, False, False, False, True), params, _ssd_consts(rev),
                      [(A_GROUPS * A_STATE, A_INNER)], [A_INNER], SSD_Q, n_ctx)
    return y


def _s5_step_fn(rev):
    def step(state, seqs, params, consts):
        sr, si = state
        (u,) = seqs
        b_re, b_im, c_re, c_im, ar, ai = params
        (pos,) = consts
        Q = u.shape[0]
        first = (pos == 0.0).astype(F32)
        xr = _bdot(u, b_re) + first * (ar * sr - ai * si)
        xi = _bdot(u, b_im) + first * (ar * si + ai * sr)
        pr, pi = ar, ai
        k = 1
        while k < Q:
            keep = (pos >= float(k)).astype(F32)
            s = -k if rev else k
            xr_s, xi_s = sroll(xr, s) * keep, sroll(xi, s) * keep
            xr, xi = xr + pr * xr_s - pi * xi_s, xi + pr * xi_s + pi * xr_s
            pr, pi = pr * pr - pi * pi, 2.0 * pr * pi
            k *= 2
        last = (pos == float(Q - 1)).astype(F32)
        sr_new = jnp.sum(xr * last, axis=0, keepdims=True)
        si_new = jnp.sum(xi * last, axis=0, keepdims=True)
        y = _bdot(xr, c_re) - _bdot(xi, c_im)
        return (sr_new, si_new), (y,)

    return step


def _block_diag(t):
    G, a, b = t.shape
    eye = jnp.eye(G, dtype=t.dtype)
    return (t[:, :, None, :] * eye[:, None, :, None]).reshape(G * a, G * b)


def s5_scan(name, rev, u, lam_re, lam_im, log_step, b_re, b_im, c_re, c_im, n_ctx):
    step = jnp.exp(log_step)[:, None]
    mag = jnp.exp(lam_re * step)
    ar = mag * jnp.cos(lam_im * step)
    ai = mag * jnp.sin(lam_im * step)
    den = lam_re * lam_re + lam_im * lam_im
    nr = ar - 1.0
    kr = (nr * lam_re + ai * lam_im) / den
    ki = (ai * lam_re - nr * lam_im) / den
    br = kr[..., None] * b_re - ki[..., None] * b_im
    bi = kr[..., None] * b_im + ki[..., None] * b_re
    params = (_block_diag(jnp.swapaxes(br, 1, 2)), _block_diag(jnp.swapaxes(bi, 1, 2)),
              _block_diag(jnp.swapaxes(c_re, 1, 2)), _block_diag(jnp.swapaxes(c_im, 1, 2)),
              ar.reshape(1, -1), ai.reshape(1, -1))
    pos = jnp.asarray(_positions(S5_Q, rev).astype(np.float32).reshape(-1, 1))
    width = B_NGROUPS * B_STATE
    (y,) = chunk_scan(_s5_step_fn(rev), name, rev, (u,), (False,), params, (pos,),
                      [(1, width), (1, width)], [B_WIDTH], S5_Q, n_ctx)
    return y


def _hg_consts(rev):
    Q = HG_Q
    pos = _positions(Q, rev)
    pi, pj = pos[:, None], pos[None, :]
    mats = [pi >= pj, pj > pi]
    pairs = []
    s = HG_BAND
    while s < Q:
        second = (pos // s) % 2 == 1
        mid = (pos // (2 * s)) * 2 * s + s
        mats.append(second[:, None] & (pj >= mid[:, None]) & (pj <= pi))
        mats.append((~second)[:, None] & (pj > pi) & (pj < mid[:, None]))
        pairs.append(second[:, None] & (~second)[None, :] & ((pi // (2 * s)) == (pj // (2 * s))))
        s *= 2
    rows = [((pos // s_) % 2 == 1) for s_ in [HG_BAND * 2 ** n for n in range(len(pairs))]]
    band = [(pos % HG_BAND) >= d for d in range(HG_BAND)]
    mats_b = jnp.asarray(np.stack(mats).astype(np.float32), BF16)
    pairs_f = jnp.asarray(np.stack(pairs).astype(np.float32))
    rows_f = jnp.asarray(np.stack(rows).astype(np.float32)[:, :, None])
    band_f = jnp.asarray(np.stack(band).astype(np.float32)[:, :, None])
    return mats_b, pairs_f, rows_f, band_f


def _hg_step_fn(rev, n_levels):
    def step(state, seqs, params, consts):
        (st,) = state
        qr, zf, v = seqs
        (lb,) = params
        mats, pairs, rows, band = consts
        W = C_WIDTH
        same_head = (lax.broadcasted_iota(jnp.int32, (W, W), 0) // C_KEY
                     == lax.broadcasted_iota(jnp.int32, (W, W), 1) // C_KEY)
        lane_h = lax.broadcasted_iota(jnp.int32, (1, W), 1) // C_KEY
        q = _silu(qr)
        f = lb + (1.0 - lb) * jax.nn.sigmoid(zf)
        logf = jnp.log(f)
        kk = 1.0 - f
        qd = q * jnp.exp(mdot(mats[0], logf))
        w = kk * jnp.exp(mdot(mats[1], logf))
        total = jnp.sum(logf, axis=0, keepdims=True)
        o = _bdot(qd, st, NT)
        st_new = st * jnp.exp(total) + jnp.where(same_head, _bdot(v, w, TN), 0.0)
        scores = [None] * C_HEADS
        for n in range(n_levels):
            a = q * jnp.exp(mdot(mats[2 + 2 * n], logf)) * rows[n]
            bk = kk * jnp.exp(mdot(mats[3 + 2 * n], logf)) * (1.0 - rows[n])
            for h in range(C_HEADS):
                sc = pairs[n] * _bdot(jnp.where(lane_h == h, a, 0.0), bk, NT)
                scores[h] = sc if scores[h] is None else scores[h] + sc
        for h in range(C_HEADS):
            o = o + _bdot(scores[h], jnp.where(lane_h == h, v, 0.0))
        same_head_b = same_head.astype(BF16)
        e = jnp.zeros_like(logf)
        for d in range(HG_BAND):
            s = -d if rev else d
            if d > 0:
                e = e + sroll(logf, s + (1 if rev else -1))
            kd, vd = (kk, v) if d == 0 else (sroll(kk, s), sroll(v, s))
            p = q * kd * jnp.exp(e)
            o = o + band[d] * (_bdot(p, same_head_b) * vd)
        return (st_new,), (o,)

    return step


def hg_scan(name, rev, qr, zf, v, lower, n_ctx):
    consts = _hg_consts(rev)
    (o,) = chunk_scan(_hg_step_fn(rev, consts[1].shape[0]), name, rev, (qr, zf, v), (False,) * 3,
                      (lower.reshape(1, -1),), consts, [(C_WIDTH, C_WIDTH)], [C_WIDTH], HG_Q, n_ctx)
    return o


def conv_silu(x, w, b, n_ctx, name):
    B, L, C = x.shape
    TC = 128
    w8 = jnp.zeros((8, C), F32).at[:A_CONV].set(w)
    b2 = b.reshape(1, C)
    pad = A_CONV // 2

    def taps(v, sign):
        t = lax.broadcasted_iota(jnp.int32, v.shape, 0)
        out = []
        for k in range(A_CONV):
            s = sign * (k - pad)
            src = t + s
            ok = (src >= 0) & (src < L) & ((t >= n_ctx) == (src >= n_ctx))
            vs = v if s == 0 else pltpu.roll(v, (-s) % L, 0)
            out.append((k, jnp.where(ok, vs, 0.0)))
        return out

    def pre(x_ref, w_ref, b_ref):
        xv = x_ref[0]
        y = b_ref[...] + sum(xs * w_ref[pl.ds(k, 1), :] for k, xs in taps(xv, 1))
        return xv, y

    def fwd_body(x_ref, w_ref, b_ref, o_ref):
        _, y = pre(x_ref, w_ref, b_ref)
        o_ref[0] = _silu(y)

    def bwd_body(x_ref, w_ref, b_ref, g_ref, dx_ref, dw_ref, db_ref):
        xv, y = pre(x_ref, w_ref, b_ref)
        sg = jax.nn.sigmoid(y)
        dy = g_ref[0] * (sg + y * sg * (1.0 - sg))
        dx_ref[0] = sum(ds * w_ref[pl.ds(k, 1), :] for k, ds in taps(dy, -1))
        first = pl.program_id(1) == 0

        @pl.when(first)
        def _():
            dw_ref[...] = jnp.zeros_like(dw_ref)
            db_ref[...] = jnp.zeros_like(db_ref)

        for k, xs in taps(xv, 1):
            dw_ref[pl.ds(k, 1), :] += jnp.sum(dy * xs, axis=0, keepdims=True)
        db_ref[...] += jnp.sum(dy, axis=0, keepdims=True)

    x_spec = pl.BlockSpec((1, L, TC), lambda j, bb: (bb, 0, j))
    w_spec = pl.BlockSpec((8, TC), lambda j, bb: (0, j))
    b_spec = pl.BlockSpec((1, TC), lambda j, bb: (0, j))

    @jax.custom_vjp
    def op(x, w8, b2):
        return pl.pallas_call(fwd_body, grid=(C // TC, B), in_specs=[x_spec, w_spec, b_spec], out_specs=x_spec,
                              out_shape=jax.ShapeDtypeStruct(x.shape, F32), name=name + "_fwd",
                              compiler_params=_cp("parallel", "parallel"))(x, w8, b2)

    def op_fwd(x, w8, b2):
        return op(x, w8, b2), (x, w8, b2)

    def op_bwd(res, g):
        x, w8, b2 = res
        return tuple(pl.pallas_call(
            bwd_body, grid=(C // TC, B), in_specs=[x_spec, w_spec, b_spec, x_spec],
            out_specs=[x_spec, w_spec, b_spec],
            out_shape=[jax.ShapeDtypeStruct(x.shape, F32), jax.ShapeDtypeStruct(w8.shape, F32),
                       jax.ShapeDtypeStruct(b2.shape, F32)],
            name=name + "_bwd", compiler_params=_cp("parallel", "arbitrary"))(x, w8, b2, g))

    op.defvjp(op_fwd, op_bwd)
    return op(x, w8, b2)


def loss_head(h, target, norm_w, n_ctx):
    B, L, Dm = h.shape
    S = target.shape[1]
    tl = ROW_TILE
    skip = n_ctx // tl

    def body(h_ref, t_ref, w_ref, loss_ref, dh_ref, dw_ref):
        b, i = pl.program_id(0), pl.program_id(1)
        fn = lambda hv, wv: _rms(hv) * wv
        y, vjp = jax.vjp(fn, h_ref[0], w_ref[...])
        err = y - t_ref[0]
        dhv, dwv = vjp(err * (1.0 / Dm))
        dh_ref[0] = dhv
        part = 0.5 * jnp.sum(jnp.mean(err * err, axis=-1, keepdims=True), axis=0, keepdims=True)
        first = (b == 0) & (i == 0)

        @pl.when(first)
        def _():
            loss_ref[...] = jnp.zeros_like(loss_ref)
            dw_ref[...] = jnp.zeros_like(dw_ref)

        loss_ref[...] += jnp.broadcast_to(part, loss_ref.shape)
        dw_ref[...] += dwv

    loss, dh, dw = pl.pallas_call(
        body, grid=(B, S // tl),
        in_specs=[pl.BlockSpec((1, tl, Dm), lambda b, i: (b, i + skip, 0)),
                  pl.BlockSpec((1, tl, Dm), lambda b, i: (b, i, 0)),
                  pl.BlockSpec((1, Dm), lambda b, i: (0, 0))],
        out_specs=[pl.BlockSpec((8, 128), lambda b, i: (0, 0)),
                   pl.BlockSpec((1, tl, Dm), lambda b, i: (b, i, 0)),
                   pl.BlockSpec((1, Dm), lambda b, i: (0, 0))],
        out_shape=[jax.ShapeDtypeStruct((8, 128), F32), jax.ShapeDtypeStruct((B, S, Dm), F32),
                   jax.ShapeDtypeStruct((1, Dm), F32)],
        name="loss_head", compiler_params=_cp("arbitrary", "arbitrary"))(h, target, norm_w.reshape(1, Dm))
    dh_full = jnp.concatenate([jnp.zeros((B, n_ctx, Dm), F32), dh], axis=1)
    return loss[0, 0], dh_full, dw.reshape(Dm)


def _modulate(h, shift, scale):
    return _rms(h) * (1.0 + scale) + shift


def _f_mod(r, m, p):
    return (_modulate(r[0], m[0], m[1]),)


def _f_resid_mod(coef):
    def f(r, m, p):
        h2 = r[0] + coef * m[0] * r[1]
        return h2, _modulate(h2, m[1], m[2])
    return f


def _f_resid(coef):
    def f(r, m, p):
        return (r[0] + coef * m[0] * r[1],)
    return f


def _f_swiglu(r, m, p):
    pre = r[0].astype(F32)
    return (_silu(pre[:, :D_FF]) * pre[:, D_FF:],)


def _f_ssd_post(r, m, p):
    y_f, y_b, xs, z = r
    d_skip, norm_w = p
    y = (y_f + y_b + d_skip * xs) * _silu(z)
    return (_rms(y) * norm_w,)


def _f_s5_post(r, m, p):
    y_f, y_b, u = r
    d_skip, glu_w, glu_b = p
    y = jax.nn.gelu(y_f + y_b + d_skip * u)
    return (y * jax.nn.sigmoid(_bdot(y, glu_w) + glu_b),)


def _f_hg_post(r, m, p):
    o_f, o_b, g = r
    (norm_w,) = p
    W = C_WIDTH
    same_head = (lax.broadcasted_iota(jnp.int32, (W, W), 0) // C_KEY
                 == lax.broadcasted_iota(jnp.int32, (W, W), 1) // C_KEY).astype(BF16)
    o = o_f + o_b
    ms = xdot(o * o, same_head) * (1.0 / C_KEY)
    return (o * lax.rsqrt(ms + EPS) * norm_w * _silu(g),)


def _pad_w_in(w):
    dt0 = A_INNER + A_CONV_DIM
    zeros = jnp.zeros((w.shape[0], IN_COLS_PAD - IN_COLS), w.dtype)
    return jnp.concatenate([w[:, :dt0], w[:, A_COLS:], w[:, dt0:A_COLS], zeros], axis=1)


def _to_columns(t, n_ctx):
    B, L, W = t.shape
    rows = (L - n_ctx) // GRID_W
    lat = t[:, n_ctx:].reshape(B, rows, GRID_W, W).transpose(0, 2, 1, 3).reshape(B, L - n_ctx, W)
    return jnp.concatenate([t[:, :n_ctx], lat], axis=1)


def _to_raster(t, n_ctx):
    B, L, W = t.shape
    rows = (L - n_ctx) // GRID_W
    lat = t[:, n_ctx:].reshape(B, GRID_W, rows, W).transpose(0, 2, 1, 3).reshape(B, L - n_ctx, W)
    return jnp.concatenate([t[:, :n_ctx], lat], axis=1)


def _forward(h0, c, big, P, n_ctx):
    B, L, Dm = h0.shape
    depth = len(big["mod_w"])
    ncb = n_ctx // ROW_TILE

    def rows_of(t):
        return t.reshape(t.shape[0] * t.shape[1], t.shape[2])

    def w_in_of(t):
        return _pad_w_in(jnp.moveaxis(t, 0, 1).reshape(t.shape[1], IN_COLS))

    p_lb = jax.nn.softmax(P["hg_lb_logits"], axis=0)
    lower_bounds = jnp.cumsum(p_lb, axis=0) - p_lb[:1]
    cc = jnp.concatenate([c, P["c_ctx"][None], jnp.zeros((8 - B - 1, Dm), F32)], axis=0)
    cc = _silu(cc)

    def mods_of(l):
        m = linear(cc, big["mod_w"][l], P["mod_w"][l], f"mod{l}", pieces=True) + P["mod_b"][l]
        m = m.reshape(8, N_MOD, Dm)
        seg = jnp.stack([jnp.broadcast_to(m[B], (B, N_MOD, Dm)), m[:B]], axis=1)
        return [seg[:, :, j:j + 1, :] for j in range(N_MOD)]

    def ffn(u, l, j):
        pre = linear(u.reshape(B * L, Dm), big["ffn_w_in"][l][j], P["ffn_w_in"][l][j], f"ffn_in{l}{j}", BF16,
                     pieces=True)
        (act,) = rowwise(_f_swiglu, f"swiglu{l}{j}", [pre.reshape(B, L, 2 * D_FF)], [], [], [(D_FF, BF16)], ncb)
        out = linear(act.reshape(B * L, D_FF), rows_of(big["ffn_w_out"][l][j]), rows_of(P["ffn_w_out"][l][j]),
                     f"ffn_out{l}{j}")
        return out.reshape(B, L, Dm)

    h = h0
    mods = mods_of(0)
    (u,) = rowwise(_f_mod, "mod_first", [h], [mods[0], mods[1]], [], [(Dm, BF16)], ncb)
    for l in range(depth):
        col_major = l % 2 == 1
        o = ffn(u, l, 0)
        h, u = rowwise(_f_resid_mod(0.5), f"resid_a{l}", [h, o], [mods[2], mods[3], mods[4]], [],
                       [(Dm, F32), (Dm, BF16)], ncb)
        if col_major:
            u = _to_columns(u, n_ctx)
        pre = linear(u.reshape(B * L, Dm), w_in_of(big["w_in"][l]), w_in_of(P["w_in"][l]), f"w_in{l}")
        pre = pre.reshape(B, L, IN_COLS_PAD)
        o0 = 0
        z, o0 = pre[..., o0:o0 + A_INNER], o0 + A_INNER
        xbc, o0 = pre[..., o0:o0 + A_CONV_DIM], o0 + A_CONV_DIM
        pb, o0 = pre[..., o0:o0 + B_WIDTH], o0 + B_WIDTH
        pc, o0 = pre[..., o0:o0 + 5 * C_WIDTH], o0 + 5 * C_WIDTH
        dtr = pre[..., o0:o0 + 2 * A_HEADS]
        xbc = conv_silu(xbc, P["a_conv_w"][l], P["a_conv_b"][l], n_ctx, f"conv{l}")
        xs, bm, cm = xbc[..., :A_INNER], xbc[..., A_INNER:A_INNER + 128], xbc[..., A_INNER + 128:]
        ya_dir = [ssd_scan(f"ssd{l}{d}", bool(d), xs, bm, cm, dtr[..., d * A_HEADS:(d + 1) * A_HEADS],
                           P["a_dt_bias"][l, d], P["a_log"][l, d], n_ctx) for d in range(2)]
        d_skip = jnp.repeat(P["a_d"][l], A_INNER // A_HEADS).reshape(1, A_INNER)
        (ya,) = rowwise(_f_ssd_post, f"ssd_post{l}", [ya_dir[0], ya_dir[1], xs, z], [],
                        [d_skip, P["a_norm_w"][l].reshape(1, -1)], [(A_INNER, BF16)], ncb)
        yb_dir = [s5_scan(f"s5{l}{d}", bool(d), pb, P["s5_lam_re"][l, d], P["s5_lam_im"][l, d],
                          P["s5_log_step"][l, d], P["s5_b_re"][l, d], P["s5_b_im"][l, d], P["s5_c_re"][l, d],
                          P["s5_c_im"][l, d], n_ctx) for d in range(2)]
        (yb,) = rowwise(_f_s5_post, f"s5_post{l}", [yb_dir[0], yb_dir[1], pb], [],
                        [P["s5_d"][l].reshape(1, -1), P["s5_glu_w"][l], P["s5_glu_b"][l].reshape(1, -1)],
                        [(B_WIDTH, BF16)], ncb)
        qr, f_raw, vi, gate = (pc[..., :C_WIDTH], pc[..., C_WIDTH:3 * C_WIDTH], pc[..., 3 * C_WIDTH:4 * C_WIDTH],
                               pc[..., 4 * C_WIDTH:])
        yc_dir = [hg_scan(f"hg{l}{d}", bool(d), qr, f_raw[..., d * C_WIDTH:(d + 1) * C_WIDTH], vi,
                          lower_bounds[l, d], n_ctx) for d in range(2)]
        (yc,) = rowwise(_f_hg_post, f"hg_post{l}", [yc_dir[0], yc_dir[1], gate], [],
                        [P["hg_norm_w"][l].reshape(1, -1)], [(C_WIDTH, BF16)], ncb)
        mix = jnp.concatenate([ya, yb, yc], axis=-1)
        y = linear(mix.reshape(B * L, Dm), rows_of(big["w_out"][l]), rows_of(P["w_out"][l]), f"w_out{l}")
        y = y.reshape(B, L, Dm)
        if col_major:
            y = _to_raster(y, n_ctx)
        h, u = rowwise(_f_resid_mod(1.0), f"resid_b{l}", [h, y], [mods[5], mods[6], mods[7]], [],
                       [(Dm, F32), (Dm, BF16)], ncb)
        o = ffn(u, l, 1)
        gate8 = mods[8]
        if l + 1 < depth:
            mods = mods_of(l + 1)
            h, u = rowwise(_f_resid_mod(0.5), f"resid_c{l}", [h, o], [gate8, mods[0], mods[1]], [],
                           [(Dm, F32), (Dm, BF16)], ncb)
        else:
            (h,) = rowwise(_f_resid(0.5), f"resid_c{l}", [h, o], [gate8], [], [(Dm, F32)], ncb)
    return h


BIG = ("mod_w", "ffn_w_in", "ffn_w_out", "w_in", "w_out")


def local_step(x, c, ctx, target, big, small):
    n_ctx = ctx.shape[1]
    h0 = jnp.concatenate([ctx, x], axis=1)
    P = dict(small)
    for k in BIG:
        P[k] = jax.tree.map(lambda t: jnp.zeros(t.shape, F32), big[k])
    h, vjp = jax.vjp(lambda h0, P: _forward(h0, c, big, P, n_ctx), h0, P)
    loss, dh, d_final = loss_head(h, target, small["final_norm_w"], n_ctx)
    dh0, grads = vjp(dh)
    grads = dict(grads)
    grads["final_norm_w"] = grads["final_norm_w"] + d_final
    return loss, dh0[:, n_ctx:], grads


MESH = pl.DeviceIdType.MESH
LANES = 1024
N_CHIPS = 4
ELEMENTWISE_BLOCK_BYTES = 2 * 1024 * 1024
ANY = pl.BlockSpec(memory_space=pl.ANY)


def _place():
    x, y, c = lax.axis_index("x"), lax.axis_index("y"), lax.axis_index("c")
    return x, y, c, 2 * x + y


def _other_chips(x, y):
    return [(x ^ kx, y ^ ky, 2 * (x ^ kx) + (y ^ ky)) for kx, ky in ((0, 1), (1, 0), (1, 1))]


def all_gather_shards(shard, name):
    def body(x_ref, out_ref, send_sems, recv_sems, local_sem):
        x, y, c, q = _place()
        sibling = (x, y, 1 - c)
        chips = _other_chips(x, y)

        def copy(k, src, dst, to):
            return pltpu.make_async_remote_copy(src_ref=src, dst_ref=dst, send_sem=send_sems.at[k],
                                                recv_sem=recv_sems.at[k], device_id=to, device_id_type=MESH)

        mine = pltpu.make_async_copy(x_ref, out_ref.at[q], local_sem)
        mine.start()
        first = [copy(k, x_ref.at[c], out_ref.at[q, c], (px, py, c)) for k, (px, py, _) in enumerate(chips)]
        for cp in first:
            cp.start()
        passed = [copy(3 + k, out_ref.at[pq, c], out_ref.at[pq, c], sibling) for k, (_, _, pq) in enumerate(chips)]
        for k, (_, _, pq) in enumerate(chips):
            copy(k, x_ref.at[c], out_ref.at[pq, c], sibling).wait_recv()
            passed[k].start()
        for k, (_, _, pq) in enumerate(chips):
            copy(3 + k, x_ref.at[c], out_ref.at[pq, 1 - c], sibling).wait_recv()
        for cp in first + passed:
            cp.wait_send()
        mine.wait()

    return pl.pallas_call(
        body, out_shape=jax.ShapeDtypeStruct((N_CHIPS,) + shard.shape, shard.dtype), in_specs=[ANY], out_specs=ANY,
        scratch_shapes=[pltpu.SemaphoreType.DMA((6,)), pltpu.SemaphoreType.DMA((6,)), pltpu.SemaphoreType.DMA],
        name=name)(shard)


def all_gather_devices(part, name):
    def body(x_ref, out_ref, send_sems, recv_sems, local_sem):
        x, y, c, _ = _place()
        me = 4 * x + 2 * y + c
        mine = pltpu.make_async_copy(x_ref, out_ref.at[me], local_sem)
        mine.start()
        copies = []
        for k in range(1, 8):
            px, py, pc = x ^ (k >> 2), y ^ ((k >> 1) & 1), c ^ (k & 1)
            copies.append(pltpu.make_async_remote_copy(
                src_ref=x_ref, dst_ref=out_ref.at[me], send_sem=send_sems.at[k - 1], recv_sem=recv_sems.at[k - 1],
                device_id=(px, py, pc), device_id_type=MESH))
            copies[-1].start()
        for k in range(1, 8):
            peer = 4 * (x ^ (k >> 2)) + 2 * (y ^ ((k >> 1) & 1)) + (c ^ (k & 1))
            pltpu.make_async_remote_copy(
                src_ref=x_ref, dst_ref=out_ref.at[peer], send_sem=send_sems.at[k - 1], recv_sem=recv_sems.at[k - 1],
                device_id=(x, y, c), device_id_type=MESH).wait_recv()
        for cp in copies:
            cp.wait_send()
        mine.wait()

    return pl.pallas_call(
        body, out_shape=jax.ShapeDtypeStruct((8,) + part.shape, part.dtype), in_specs=[ANY], out_specs=ANY,
        scratch_shapes=[pltpu.SemaphoreType.DMA((7,)), pltpu.SemaphoreType.DMA((7,)), pltpu.SemaphoreType.DMA],
        name=name)(part)


def _half(ref, c, axis):
    h = ref.shape[axis] // 2
    return ref.at[(slice(None),) * axis + (pl.ds(pl.multiple_of(c * h, 8), h),)]


def _remote(src, dst, send_sems, recv_sems, s, to):
    return pltpu.make_async_remote_copy(src_ref=src, dst_ref=dst, send_sem=send_sems.at[s], recv_sem=recv_sems.at[s],
                                        device_id=to, device_id_type=MESH)


def gather_pieces(arrays, name):
    n = len(arrays)

    def body(*refs):
        ins, outs, (send_sems, recv_sems) = refs[:n], refs[n:2 * n], refs[2 * n:]
        x, y, c, q = _place()
        sibling = (x, y, 1 - c)
        chips = _other_chips(x, y)
        started = []

        def go(src, dst, s, to):
            started.append(_remote(src, dst, send_sems, recv_sems, s, to))
            started[-1].start()

        for i, (a, o) in enumerate(zip(ins, outs)):
            go(a, o.at[q], 7 * i + 6, sibling)
            for k, (px, py, _) in enumerate(chips):
                go(_half(a, c, 0), _half(o.at[q], c, 0), 7 * i + k, (px, py, c))
        for i, o in enumerate(outs):
            for k, (_, _, pq) in enumerate(chips):
                land = _half(o.at[pq], c, 0)
                _remote(land, land, send_sems, recv_sems, 7 * i + k, sibling).wait_recv()
                go(land, land, 7 * i + 3 + k, sibling)
        for i, (a, o) in enumerate(zip(ins, outs)):
            for k, (_, _, pq) in enumerate(chips):
                land = _half(o.at[pq], 1 - c, 0)
                _remote(land, land, send_sems, recv_sems, 7 * i + 3 + k, sibling).wait_recv()
            _remote(a, o.at[q], send_sems, recv_sems, 7 * i + 6, sibling).wait_recv()
        for cp in started:
            cp.wait_send()

    return pl.pallas_call(
        body, out_shape=[jax.ShapeDtypeStruct((N_CHIPS,) + a.shape, a.dtype) for a in arrays], in_specs=[ANY] * n,
        out_specs=[ANY] * n, scratch_shapes=[pltpu.SemaphoreType.DMA((7 * n,)), pltpu.SemaphoreType.DMA((7 * n,))],
        name=name)(*arrays)


def swap_with_sibling(arrays, name, send_other_half_of_axis=None):
    n = len(arrays)
    ax = send_other_half_of_axis

    def body(*refs):
        ins, outs, (send_sems, recv_sems) = refs[:n], refs[n:2 * n], refs[2 * n:]
        x, y, c, _ = _place()
        copies = [_remote(a if ax is None else _half(a, 1 - c, ax), o, send_sems, recv_sems, i, (x, y, 1 - c))
                  for i, (a, o) in enumerate(zip(ins, outs))]
        for cp in copies:
            cp.start()
        for cp in copies:
            cp.wait()

    def out_of(a):
        shape = list(a.shape)
        if ax is not None:
            shape[ax] //= 2
        return jax.ShapeDtypeStruct(tuple(shape), a.dtype)

    return pl.pallas_call(
        body, out_shape=[out_of(a) for a in arrays], in_specs=[ANY] * n, out_specs=[ANY] * n,
        scratch_shapes=[pltpu.SemaphoreType.DMA((n,)), pltpu.SemaphoreType.DMA((n,))], name=name)(*arrays)


def pieces_to_chips(pairs, name):
    n = len(pairs)

    def body(*refs):
        ins, outs, (send_sems, recv_sems) = refs[:n], refs[n:2 * n], refs[2 * n:]
        x, y, c, q = _place()
        chips = _other_chips(x, y)
        copies = [_remote(a.at[pq], o.at[q], send_sems, recv_sems, 3 * i + k, (px, py, c))
                  for i, (a, o) in enumerate(zip(ins, outs)) for k, (px, py, pq) in enumerate(chips)]
        for cp in copies:
            cp.start()
        for i, (a, o) in enumerate(zip(ins, outs)):
            for k, (_, _, pq) in enumerate(chips):
                _remote(a.at[pq], o.at[pq], send_sems, recv_sems, 3 * i + k, (x, y, c)).wait_recv()
        for cp in copies:
            cp.wait_send()

    return pl.pallas_call(
        body, out_shape=[jax.ShapeDtypeStruct(a.shape, a.dtype) for a in pairs], in_specs=[ANY] * n,
        out_specs=[ANY] * n, scratch_shapes=[pltpu.SemaphoreType.DMA((3 * n,)), pltpu.SemaphoreType.DMA((3 * n,))],
        name=name)(*pairs)


def _rows_block(h, n_cols, itemsize=4):
    return _tile(h, 8, max(8, ELEMENTWISE_BLOCK_BYTES // (n_cols * itemsize)))


def add_pair(core, g, got, name):
    _, K, N = g.shape
    h = K // 2
    th = _rows_block(h, N)
    nb = h // th

    def body(c_ref, g_ref, got_ref, o_ref):
        o_ref[...] = g_ref[...] + got_ref[...]

    grid_spec = pltpu.PrefetchScalarGridSpec(
        num_scalar_prefetch=1, grid=(N_CHIPS, nb),
        in_specs=[pl.BlockSpec((None, th, N), lambda p, i, c_ref: (p, c_ref[0] * nb + i, 0)),
                  pl.BlockSpec((None, th, N), lambda p, i, c_ref: (p, i, 0))],
        out_specs=pl.BlockSpec((None, th, N), lambda p, i, c_ref: (p, i, 0)))
    return pl.pallas_call(body, grid_spec=grid_spec, out_shape=jax.ShapeDtypeStruct(got.shape, F32), name=name,
                          compiler_params=_cp("parallel", "parallel"))(core, g, got)


def sum_pieces(chip, pair, got, name):
    _, h, N = pair.shape
    th = _rows_block(h, N)

    def body(q_ref, pair_ref, *rest):
        got_refs, o_ref = rest[:N_CHIPS], rest[N_CHIPS]
        q = q_ref[0]
        for p in range(N_CHIPS):
            def put(val, p=p):
                if p == 0:
                    o_ref[...] = val
                else:
                    o_ref[...] += val

            @pl.when(q == p)
            def _():
                put(pair_ref[...])

            @pl.when(q != p)
            def _(p=p):
                put(got_refs[p][...])

    def got_spec(p):
        return pl.BlockSpec((None, th, N), lambda i, q_ref: (jnp.where(q_ref[0] == p, (p + 1) % N_CHIPS, p), i, 0))

    grid_spec = pltpu.PrefetchScalarGridSpec(
        num_scalar_prefetch=1, grid=(h // th,),
        in_specs=[pl.BlockSpec((None, th, N), lambda i, q_ref: (q_ref[0], i, 0))] + [got_spec(p) for p in range(N_CHIPS)],
        out_specs=pl.BlockSpec((th, N), lambda i, q_ref: (i, 0)))
    return pl.pallas_call(body, grid_spec=grid_spec, out_shape=jax.ShapeDtypeStruct((h, N), F32), name=name,
                          compiler_params=_cp("parallel"))(chip, pair, *([got] * N_CHIPS))


def adamw_halves(core, mine, other, w, m, v, name):
    n, K, N = w.shape
    h = K // 2
    th = _rows_block(h, N)
    nb = h // th

    def body(c_ref, *refs):
        mine_refs, other_refs = refs[:n], refs[n:2 * n]
        w_ref, m_ref, v_ref, g_ref, d_ref, m2_ref, v2_ref = refs[2 * n:]
        a, s, c = pl.program_id(0), pl.program_id(1), c_ref[0]
        for idx in range(n):
            @pl.when((a == idx) & (s == c))
            def _(idx=idx):
                g_ref[...] = mine_refs[idx][...]

            @pl.when((a == idx) & (s != c))
            def _(idx=idx):
                g_ref[...] = other_refs[idx][...]
        _adamw_math(g_ref[...], w_ref, m_ref, v_ref, d_ref, m2_ref, v2_ref)

    def half_spec(idx, is_mine):
        def index(a, s, i, c_ref):
            right_half = (s == c_ref[0]) if is_mine else (s != c_ref[0])
            return (jnp.where((a == idx) & right_half, i, 0), 0)
        return pl.BlockSpec((th, N), index)

    full = pl.BlockSpec((None, th, N), lambda a, s, i, c_ref: (a, s * nb + i, 0))
    grid_spec = pltpu.PrefetchScalarGridSpec(
        num_scalar_prefetch=1, grid=(n, 2, nb),
        in_specs=[half_spec(idx, True) for idx in range(n)] + [half_spec(idx, False) for idx in range(n)] + [full] * 3,
        out_specs=[full] * 4)
    return pl.pallas_call(body, grid_spec=grid_spec, out_shape=[jax.ShapeDtypeStruct((n, K, N), F32)] * 4, name=name,
                          compiler_params=_cp("arbitrary", "arbitrary", "arbitrary"))(core, *mine, *other, w, m, v)


def _adamw_math(gv, w_ref, m_ref, v_ref, d_ref, m2_ref, v2_ref):
    m2 = ADAM_B1 * m_ref[...] + (1.0 - ADAM_B1) * gv
    v2 = ADAM_B2 * v_ref[...] + (1.0 - ADAM_B2) * (gv * gv)
    m_hat = m2 / (1.0 - ADAM_B1 ** ADAM_STEP)
    v_hat = v2 / (1.0 - ADAM_B2 ** ADAM_STEP)
    d_ref[...] = -ADAM_LR * (m_hat / (jnp.sqrt(v_hat) + ADAM_EPS) + ADAM_WD * w_ref[...])
    m2_ref[...] = m2
    v2_ref[...] = v2


def _row_tile(R):
    return _pick(R, (512, 256, 128, 64, 32, 16, 8))


def sum_parts(parts, name):
    P, R, _ = parts.shape
    rt = _row_tile(R)

    def body(p_ref, o_ref):
        acc = p_ref[0]
        for p in range(1, P):
            acc = acc + p_ref[p]
        o_ref[...] = acc

    return pl.pallas_call(
        body, grid=(R // rt,), in_specs=[pl.BlockSpec((P, rt, LANES), lambda i: (0, i, 0))],
        out_specs=pl.BlockSpec((rt, LANES), lambda i: (i, 0)), out_shape=jax.ShapeDtypeStruct((R, LANES), F32),
        name=name, compiler_params=_cp("parallel"))(parts)


def adamw(g, w, m, v, name):
    R = g.shape[0]
    rt = _row_tile(R)

    def body(g_ref, w_ref, m_ref, v_ref, d_ref, m2_ref, v2_ref):
        _adamw_math(g_ref[...], w_ref, m_ref, v_ref, d_ref, m2_ref, v2_ref)

    spec = pl.BlockSpec((rt, LANES), lambda i: (i, 0))
    return pl.pallas_call(
        body, grid=(R // rt,), in_specs=[spec] * 4, out_specs=[spec] * 3,
        out_shape=[jax.ShapeDtypeStruct((R, LANES), F32)] * 3, name=name, compiler_params=_cp("parallel"))(g, w, m, v)


def _pack(arrays, rows_multiple, dtype):
    flat = jnp.concatenate([a.reshape(-1).astype(dtype) for a in arrays])
    n = flat.shape[0]
    per = rows_multiple * LANES
    total = -(-n // per) * per
    return jnp.pad(flat, (0, total - n)).reshape(total // LANES, LANES)


def _unpack(buf, shapes):
    flat = buf.reshape(-1)
    out, off = [], 0
    for s in shapes:
        n = math.prod(s)
        out.append(flat[off:off + n].reshape(s))
        off += n
    return out


SHARDED = (("mod_w", 2), ("ffn_w_in", 3), ("ffn_w_out", 2), ("w_in", 2), ("w_out", 1),
           ("a_conv_w", 2), ("s5_glu_w", 1), ("hg_lb_logits", 2))
WEIGHTS = ("c_ctx", "mod_w", "mod_b", "ffn_w_in", "ffn_w_out", "w_in", "w_out", "a_conv_w", "a_conv_b", "a_dt_bias",
           "a_log", "a_d", "a_norm_w", "s5_lam_re", "s5_lam_im", "s5_log_step", "s5_b_re", "s5_b_im", "s5_c_re",
           "s5_c_im", "s5_d", "s5_glu_w", "s5_glu_b", "hg_lb_logits", "hg_norm_w", "final_norm_w")


def _gather_full(local, names_axes, dtype, rows_multiple, name):
    buf = _pack([local[n] for n, _ in names_axes], 2 * rows_multiple, dtype)
    R = buf.shape[0] // 2
    full = all_gather_shards(buf.reshape(2, R, LANES), name).reshape(N_CHIPS, 2 * R, LANES)
    per_chip = [_unpack(full[q], [local[n].shape for n, _ in names_axes]) for q in range(N_CHIPS)]
    return {n: jnp.concatenate([per_chip[q][j] for q in range(N_CHIPS)], axis=ax)
            for j, (n, ax) in enumerate(names_axes)}


def kernel(x, c, ctx, c_ctx, mod_w, mod_b, ffn_w_in, ffn_w_out, w_in, w_out, a_conv_w, a_conv_b, a_dt_bias, a_log, a_d, a_norm_w, s5_lam_re, s5_lam_im, s5_log_step, s5_b_re, s5_b_im, s5_c_re, s5_c_im, s5_d, s5_glu_w, s5_glu_b, hg_lb_logits, hg_norm_w, final_norm_w, loss_target, m_c_ctx, m_mod_w, m_mod_b, m_ffn_w_in, m_ffn_w_out, m_w_in, m_w_out, m_a_conv_w, m_a_conv_b, m_a_dt_bias, m_a_log, m_a_d, m_a_norm_w, m_s5_lam_re, m_s5_lam_im, m_s5_log_step, m_s5_b_re, m_s5_b_im, m_s5_c_re, m_s5_c_im, m_s5_d, m_s5_glu_w, m_s5_glu_b, m_hg_lb_logits, m_hg_norm_w, m_final_norm_w, v_c_ctx, v_mod_w, v_mod_b, v_ffn_w_in, v_ffn_w_out, v_w_in, v_w_out, v_a_conv_w, v_a_conv_b, v_a_dt_bias, v_a_log, v_a_d, v_a_norm_w, v_s5_lam_re, v_s5_lam_im, v_s5_log_step, v_s5_b_re, v_s5_b_im, v_s5_c_re, v_s5_c_im, v_s5_d, v_s5_glu_w, v_s5_glu_b, v_hg_lb_logits, v_hg_norm_w, v_final_norm_w):
    given = dict(locals())
    w = {n: given[n] for n in WEIGHTS}
    m = {n: given["m_" + n] for n in WEIGHTS}
    v = {n: given["v_" + n] for n in WEIGHTS}
    small_sharded = SHARDED[len(BIG):]
    replicated = [n for n in WEIGHTS if n not in [s for s, _ in SHARDED]]
    core = lax.axis_index("c").astype(jnp.int32).reshape(1)
    chip = (2 * lax.axis_index("x") + lax.axis_index("y")).astype(jnp.int32)

    def leaves(t):
        return list(t.reshape((-1,) + t.shape[-2:]))

    def nest(kind, flat):
        flat = list(flat)
        return [flat[2 * l:2 * l + 2] for l in range(len(flat) // 2)] if w[kind].ndim == 4 else flat

    counts = [len(leaves(w[k])) for k in BIG]
    gathered = gather_pieces([a.astype(BF16) for k in BIG for a in leaves(w[k])], "gather_big")
    big, off = {}, 0
    for k, cnt in zip(BIG, counts):
        big[k], off = nest(k, gathered[off:off + cnt]), off + cnt
    small = {n: w[n] for n in replicated}
    small.update(_gather_full(w, small_sharded, F32, 8, "gather_small"))

    loss, grad_x, grads = local_step(x, c, ctx, loss_target, big, small)

    g_big = [g for k in BIG for g in jax.tree.leaves(grads[k])]
    got = swap_with_sibling(g_big, "rs_sibling", send_other_half_of_axis=1)
    pairs = [add_pair(core, g, t, f"rs_add_pair{i}") for i, (g, t) in enumerate(zip(g_big, got))]
    from_chips = pieces_to_chips(pairs, "rs_chips")
    mine = [sum_pieces(chip.reshape(1), p, t, f"rs_sum_chips{i}") for i, (p, t) in enumerate(zip(pairs, from_chips))]
    other = swap_with_sibling(mine, "rs_halves")
    out, off = {}, 0
    for k, cnt in zip(BIG, counts):
        stack = lambda t: t.reshape((cnt,) + t.shape[-2:])
        res = adamw_halves(core, mine[off:off + cnt], other[off:off + cnt], stack(w[k]), stack(m[k]), stack(v[k]),
                           "adamw_" + k)
        off += cnt
        for kind, a in zip(("grad", "delta", "new_m", "new_v"), res):
            out[kind, k] = a.reshape(w[k].shape)

    names = replicated + [n for n, _ in small_sharded]
    part = _pack([grads[n] for n in names] + [loss.reshape(1)], 8, F32)
    g_all = _unpack(sum_parts(all_gather_devices(part, "gather_parts"), "sum_parts"),
                    [grads[n].shape for n in names] + [(1,)])
    g_of = dict(zip(names + ["loss"], g_all))
    for n, ax in small_sharded:
        g_of[n] = lax.dynamic_slice_in_dim(g_of[n], chip * w[n].shape[ax], w[n].shape[ax], axis=ax)
    rest = lambda d: _pack([d[n] for n in names], 8, F32)
    g_rest = rest(g_of)
    d_rest, m_rest, v_rest = adamw(g_rest, rest(w), rest(m), rest(v), "adamw_rest")
    shapes = [w[n].shape for n in names]
    for kind, buf in (("grad", g_rest), ("delta", d_rest), ("new_m", m_rest), ("new_v", v_rest)):
        for n, a in zip(names, _unpack(buf, shapes)):
            out[kind, n] = a

    loss_total = g_of["loss"].reshape(())
    return (loss_total, grad_x, *[out["grad", n] for n in WEIGHTS], *[out["delta", n] for n in WEIGHTS],
            *[out["new_m", n] for n in WEIGHTS], *[out["new_v", n] for n in WEIGHTS])
```

```python
import functools
import math

import numpy as np
import jax
import jax.numpy as jnp
from jax import lax
from jax.experimental import pallas as pl
from jax.experimental.pallas import tpu as pltpu

F32, BF16 = jnp.float32, jnp.bfloat16
EPS = 1e-6
N_MOD = 9
D_FF = 2816
A_INNER, A_HEADS, A_GROUPS, A_STATE, A_CONV, A_CONV_DIM = 512, 8, 2, 64, 5, 768
A_COLS = A_INNER + A_CONV_DIM + 2 * A_HEADS
B_WIDTH, B_GROUP, B_NGROUPS, B_STATE = 256, 16, 16, 64
C_WIDTH, C_HEADS, C_KEY = 256, 4, 64
GRID_W = 64
IN_COLS = A_COLS + B_WIDTH + 5 * C_WIDTH
IN_COLS_PAD = 3072
ADAM_LR, ADAM_B1, ADAM_B2, ADAM_EPS, ADAM_WD, ADAM_STEP = 0.001, 0.9, 0.999, 1e-08, 0.01, 10

ROW_TILE = 256
SSD_Q, HG_Q = 128, 64
REC_T = 128
HG_BAND = 8
VMEM_LIMIT = 56 * 1024 * 1024
MM_VMEM_BUDGET = 36 * 1024 * 1024

NT = ((1,), (1,))
TN = ((0,), (0,))


def _bdot(a, b, dims=((1,), (0,))):
    return lax.dot_general(a.astype(BF16), b.astype(BF16), (dims, ((), ())), preferred_element_type=F32)


def _split3(x):
    hi = x.astype(BF16)
    r = x - hi.astype(F32)
    mid = r.astype(BF16)
    lo = (r - mid.astype(F32)).astype(BF16)
    return hi, mid, lo


def _mask_lhs(m, x, dims):
    return sum(lax.dot_general(m, p, (dims, ((), ())), preferred_element_type=F32) for p in _split3(x))


def _mask_rhs(x, m, dims):
    return sum(lax.dot_general(p, m, (dims, ((), ())), preferred_element_type=F32) for p in _split3(x))


@jax.custom_vjp
def mdot(m, x):
    return _mask_lhs(m, x, ((1,), (0,)))


def _mdot_fwd(m, x):
    return mdot(m, x), m


def _mdot_bwd(m, g):
    return jnp.zeros_like(m), _mask_lhs(m, g, TN)


mdot.defvjp(_mdot_fwd, _mdot_bwd)


@jax.custom_vjp
def xdot(x, m):
    return _mask_rhs(x, m, ((1,), (0,)))


def _xdot_fwd(x, m):
    return xdot(x, m), m


def _xdot_bwd(m, g):
    return _mask_rhs(g, m, NT), jnp.zeros_like(m)


xdot.defvjp(_xdot_fwd, _xdot_bwd)


@functools.partial(jax.custom_vjp, nondiff_argnums=(1,))
def _roll(x, s):
    return pltpu.roll(x, s, 0)


def _roll_fwd(x, s):
    return _roll(x, s), None


def _roll_bwd(s, _, g):
    return (_roll(g, x_rows(g) - s),)


def x_rows(x):
    return x.shape[0]


_roll.defvjp(_roll_fwd, _roll_bwd)


def sroll(x, s):
    s = s % x.shape[0]
    return x if s == 0 else _roll(x, s)


def _softplus(x):
    return jnp.maximum(x, 0.0) + jnp.log1p(jnp.exp(jnp.minimum(x, -x)))


def _silu(x):
    return x * jax.nn.sigmoid(x)


def _rms(x):
    return x * lax.rsqrt(jnp.mean(x * x, axis=-1, keepdims=True) + EPS)


def _pick(n, cands):
    for c in cands:
        if n % c == 0:
            return c
    return n


def _cp(*sem):
    return pltpu.CompilerParams(dimension_semantics=sem, vmem_limit_bytes=VMEM_LIMIT)


def _tile(n, unit, cap):
    best = None
    for d in range(unit, min(n, cap) + 1, unit):
        if n % d == 0:
            best = d
    return best if best is not None else n


def _mm(a, b, ta, tb, out_dtype, name, b_pieces=False, out_pieces=False):
    M, K = (a.shape[1], a.shape[0]) if ta else a.shape
    n_div = k_div = None
    if b_pieces and tb:
        N, k_div = b.shape[1], b.shape[2]
    elif b_pieces:
        N, n_div = N_CHIPS * b.shape[2], b.shape[2]
    else:
        N = b.shape[0] if tb else b.shape[1]
    if out_pieces:
        n_div = N // N_CHIPS
    n_div, k_div = n_div or N, k_div or K
    tm = _tile(M, 128 if ta else 8, 1536 if ta else 1024)
    tn = _tile(n_div, 128, 1536)
    a_bytes, b_bytes, o_bytes = a.dtype.itemsize, b.dtype.itemsize, jnp.dtype(out_dtype).itemsize
    cands = {d for d in range(128, min(k_div, 2816) + 1, 128) if k_div % d == 0}
    if k_div <= 2816 or not cands:
        cands.add(k_div)
    for tk in sorted(cands, reverse=True):
        scratch = tm * tn * 4 if (K // tk > 1 and o_bytes != 4) else 0
        if 2 * (tm * tk * a_bytes + tk * tn * b_bytes + tm * tn * o_bytes) + scratch <= MM_VMEM_BUDGET:
            break
    nk = K // tk
    nq_n, nq_k = n_div // tn, k_div // tk
    dims = ((0 if ta else 1,), (1 if tb else 0,))

    def body(a_ref, b_ref, o_ref, *acc):
        part = _bdot(a_ref[...], b_ref[...], dims)
        if nk == 1:
            o_ref[...] = part.astype(o_ref.dtype)
            return
        acc_ref = acc[0] if acc else o_ref
        k = pl.program_id(2)

        @pl.when(k == 0)
        def _():
            acc_ref[...] = part

        @pl.when(k > 0)
        def _():
            acc_ref[...] += part

        if acc:
            @pl.when(k == nk - 1)
            def _():
                o_ref[...] = acc_ref[...].astype(o_ref.dtype)

    a_spec = pl.BlockSpec((tk, tm), lambda i, j, k: (k, i)) if ta else pl.BlockSpec((tm, tk), lambda i, j, k: (i, k))
    if b_pieces and tb:
        b_spec = pl.BlockSpec((None, tn, tk), lambda i, j, k: (k // nq_k, j, k % nq_k))
    elif b_pieces:
        b_spec = pl.BlockSpec((None, tk, tn), lambda i, j, k: (j // nq_n, k, j % nq_n))
    elif tb:
        b_spec = pl.BlockSpec((tn, tk), lambda i, j, k: (j, k))
    else:
        b_spec = pl.BlockSpec((tk, tn), lambda i, j, k: (k, j))
    if out_pieces:
        o_spec = pl.BlockSpec((None, tm, tn), lambda i, j, k: (j // nq_n, i, j % nq_n))
        o_shape = jax.ShapeDtypeStruct((N_CHIPS, M, N // N_CHIPS), out_dtype)
    else:
        o_spec = pl.BlockSpec((tm, tn), lambda i, j, k: (i, j))
        o_shape = jax.ShapeDtypeStruct((M, N), out_dtype)
    return pl.pallas_call(
        body, grid=(M // tm, N // tn, nk), in_specs=[a_spec, b_spec], out_specs=o_spec, out_shape=o_shape,
        scratch_shapes=[pltpu.VMEM((tm, tn), F32)] if (nk > 1 and o_bytes != 4) else [], name=name,
        compiler_params=_cp("parallel", "parallel", "arbitrary"))(a, b)


def linear(x, w16, wslot, name, out_dtype=F32, pieces=False):
    @jax.custom_vjp
    def f(x, w16, wslot):
        return _mm(x, w16, False, False, out_dtype, name + "_fwd", b_pieces=pieces)

    def fwd(x, w16, wslot):
        return f(x, w16, wslot), (x, w16)

    grad_dtype = wslot.dtype

    def bwd(res, dy):
        x, w16 = res
        dx = _mm(dy, w16, False, True, x.dtype, name + "_dx", b_pieces=pieces)
        dw = _mm(x, dy, True, False, grad_dtype, name + "_dw", out_pieces=pieces)
        return dx, jnp.zeros_like(w16), dw

    f.defvjp(fwd, bwd)
    return f(x, w16, wslot)


def rowwise(f, name, rows, mods, params, outs, n_ctx_blocks, tl=ROW_TILE, row_diff=None):
    rows, mods, params = tuple(rows), tuple(mods), tuple(params)
    nr, nm, npar, no = len(rows), len(mods), len(params), len(outs)
    B, L = rows[0].shape[:2]
    nblk = L // tl
    row_diff = tuple(row_diff) if row_diff is not None else (True,) * nr
    out_dtypes = [dt for _, dt in outs]

    def fcast(r, m, p):
        return tuple(o.astype(dt) for o, dt in zip(f(r, m, p), out_dtypes))

    def seg(i):
        return (i >= n_ctx_blocks).astype(jnp.int32) if n_ctx_blocks else 0

    def specs():
        row_specs = [pl.BlockSpec((1, tl, r.shape[2]), lambda b, i: (b, i, 0)) for r in rows]
        mod_specs = [pl.BlockSpec((1, 1, 1, m.shape[3]), lambda b, i: (b, seg(i), 0, 0)) for m in mods]
        par_specs = [pl.BlockSpec(p.shape, lambda b, i: (0, 0)) for p in params]
        out_specs = [pl.BlockSpec((1, tl, w), lambda b, i: (b, i, 0)) for w, _ in outs]
        return row_specs, mod_specs, par_specs, out_specs

    def load(ins):
        r = tuple(x[0] for x in ins[:nr])
        m = tuple(x[0, 0] for x in ins[nr:nr + nm])
        p = tuple(x[...] for x in ins[nr + nm:])
        return r, m, p

    def fwd_call(rows, mods, params):
        def body(*refs):
            r, m, p = load(refs[:nr + nm + npar])
            for o_ref, o in zip(refs[nr + nm + npar:], fcast(r, m, p)):
                o_ref[0] = o

        rs, ms, ps, os_ = specs()
        return pl.pallas_call(
            body, grid=(B, nblk), in_specs=rs + ms + ps, out_specs=os_,
            out_shape=[jax.ShapeDtypeStruct((B, L, w), dt) for w, dt in outs],
            name=name + "_fwd", compiler_params=_cp("parallel", "parallel"))(*rows, *mods, *params)

    def bwd_call(rows, mods, params, douts):
        didx = [j for j in range(nr) if row_diff[j]]

        def body(*refs):
            n_in = nr + nm + npar
            r, m, p = load(refs[:n_in])
            dins = refs[n_in:n_in + no]
            rest = refs[n_in + no:]
            dr_refs, dm_refs, dp_refs = rest[:len(didx)], rest[len(didx):len(didx) + nm], rest[len(didx) + nm:]
            _, vjp = jax.vjp(fcast, r, m, p)
            dr, dm, dp = vjp(tuple(d[0] for d in dins))
            for ref, j in zip(dr_refs, didx):
                ref[0] = dr[j].astype(ref.dtype)
            b, i = pl.program_id(0), pl.program_id(1)
            first_m = (i == 0) | (i == n_ctx_blocks) if n_ctx_blocks else (i == 0)
            first_p = (b == 0) & (i == 0)
            for ref, g in zip(dm_refs, dm):
                @pl.when(first_m)
                def _(ref=ref, g=g):
                    ref[0, 0] = g

                @pl.when(jnp.logical_not(first_m))
                def _(ref=ref, g=g):
                    ref[0, 0] += g
            for ref, g in zip(dp_refs, dp):
                @pl.when(first_p)
                def _(ref=ref, g=g):
                    ref[...] = g

                @pl.when(jnp.logical_not(first_p))
                def _(ref=ref, g=g):
                    ref[...] += g

        rs, ms, ps, os_ = specs()
        out_shape = ([jax.ShapeDtypeStruct(rows[j].shape, rows[j].dtype) for j in didx]
                     + [jax.ShapeDtypeStruct(m.shape, F32) for m in mods]
                     + [jax.ShapeDtypeStruct(p.shape, F32) for p in params])
        res = pl.pallas_call(
            body, grid=(B, nblk), in_specs=rs + ms + ps + os_,
            out_specs=[rs[j] for j in didx] + ms + ps, out_shape=out_shape,
            name=name + "_bwd", compiler_params=_cp("arbitrary", "arbitrary"))(*rows, *mods, *params, *douts)
        dr = [None] * nr
        for j, g in zip(didx, res[:len(didx)]):
            dr[j] = g
        dr = tuple(g if g is not None else jnp.zeros_like(rows[j]) for j, g in enumerate(dr))
        return dr, tuple(res[len(didx):len(didx) + nm]), tuple(res[len(didx) + nm:])

    @jax.custom_vjp
    def op(rows, mods, params):
        return tuple(fwd_call(rows, mods, params))

    def op_fwd(rows, mods, params):
        return op(rows, mods, params), (rows, mods, params)

    def op_bwd(res, douts):
        return bwd_call(*res, tuple(douts))

    op.defvjp(op_fwd, op_bwd)
    return op(rows, mods, params)


def chunk_scan(step, name, rev, seqs, seq_t, params, consts, state_shapes, out_widths, Q, n_ctx):
    seqs, params, consts = tuple(seqs), tuple(params), tuple(consts)
    ns, npar, nc, nst, no = len(seqs), len(params), len(consts), len(state_shapes), len(out_widths)
    B = seqs[0].shape[0]
    L = seqs[0].shape[2] if seq_t[0] else seqs[0].shape[1]
    nck, ncc = L // Q, n_ctx // Q

    def chunk_of(k):
        if not rev:
            return k
        return jnp.where(k < ncc, ncc - 1 - k, nck + ncc - 1 - k)

    def specs(order):
        def seq_spec(s, t):
            if t:
                return pl.BlockSpec((1, s.shape[1], Q), lambda b, k: (b, 0, chunk_of(order(k))))
            return pl.BlockSpec((1, Q, s.shape[2]), lambda b, k: (b, chunk_of(order(k)), 0))

        seq_specs = [seq_spec(s, t) for s, t in zip(seqs, seq_t)]
        par_specs = [pl.BlockSpec(p.shape, lambda b, k: (0, 0)) for p in params]
        con_specs = [pl.BlockSpec(c.shape, lambda b, k, nd=c.ndim: (0,) * nd) for c in consts]
        out_specs = [pl.BlockSpec((1, Q, w), lambda b, k: (b, chunk_of(order(k)), 0)) for w in out_widths]
        sav_specs = [pl.BlockSpec((1, 1) + tuple(s), lambda b, k: (b, order(k), 0, 0)) for s in state_shapes]
        return seq_specs, par_specs, con_specs, out_specs, sav_specs

    def fwd_call(seqs, params):
        def body(*refs):
            seq_refs = refs[:ns]
            par_refs = refs[ns:ns + npar]
            con_refs = refs[ns + npar:ns + npar + nc]
            rest = refs[ns + npar + nc:]
            out_refs, sav_refs, st_refs = rest[:no], rest[no:no + nst], rest[no + nst:]

            @pl.when(pl.program_id(1) == 0)
            def _():
                for s in st_refs:
                    s[...] = jnp.zeros_like(s)

            s_in = tuple(s[...] for s in st_refs)
            for sv, s in zip(sav_refs, s_in):
                sv[0, 0] = s
            s_new, outs = step(s_in, tuple(r[0] for r in seq_refs), tuple(p[...] for p in par_refs),
                               tuple(c[...] for c in con_refs))
            for s_ref, s in zip(st_refs, s_new):
                s_ref[...] = s
            for o_ref, o in zip(out_refs, outs):
                o_ref[0] = o

        ss, ps, cs, os_, vs = specs(lambda k: k)
        res = pl.pallas_call(
            body, grid=(B, nck), in_specs=ss + ps + cs, out_specs=os_ + vs,
            out_shape=[jax.ShapeDtypeStruct((B, L, w), F32) for w in out_widths]
            + [jax.ShapeDtypeStruct((B, nck) + tuple(s), F32) for s in state_shapes],
            scratch_shapes=[pltpu.VMEM(tuple(s), F32) for s in state_shapes],
            name=name + "_fwd", compiler_params=_cp("parallel", "arbitrary"))(*seqs, *params, *consts)
        return tuple(res[:no]), tuple(res[no:])

    def bwd_call(seqs, params, saved, douts):
        def body(*refs):
            seq_refs = refs[:ns]
            par_refs = refs[ns:ns + npar]
            con_refs = refs[ns + npar:ns + npar + nc]
            rest = refs[ns + npar + nc:]
            sav_refs, dout_refs = rest[:nst], rest[nst:nst + no]
            rest = rest[nst + no:]
            dseq_refs, dpar_refs, dst_refs = rest[:ns], rest[ns:ns + npar], rest[ns + npar:]
            b, k = pl.program_id(0), pl.program_id(1)

            @pl.when(k == 0)
            def _():
                for s in dst_refs:
                    s[...] = jnp.zeros_like(s)

            cvals = tuple(c[...] for c in con_refs)
            _, vjp = jax.vjp(lambda s, x, p: step(s, x, p, cvals), tuple(s[0, 0] for s in sav_refs),
                             tuple(r[0] for r in seq_refs), tuple(p[...] for p in par_refs))
            ds, dx, dp = vjp((tuple(s[...] for s in dst_refs), tuple(d[0] for d in dout_refs)))
            for s_ref, s in zip(dst_refs, ds):
                s_ref[...] = s
            for x_ref, x in zip(dseq_refs, dx):
                x_ref[0] = x
            first = (b == 0) & (k == 0)
            for ref, g in zip(dpar_refs, dp):
                @pl.when(first)
                def _(ref=ref, g=g):
                    ref[...] = g

                @pl.when(jnp.logical_not(first))
                def _(ref=ref, g=g):
                    ref[...] += g

        ss, ps, cs, os_, vs = specs(lambda k: nck - 1 - k)
        res = pl.pallas_call(
            body, grid=(B, nck), in_specs=ss + ps + cs + vs + os_, out_specs=ss + ps,
            out_shape=[jax.ShapeDtypeStruct(s.shape, F32) for s in seqs]
            + [jax.ShapeDtypeStruct(p.shape, F32) for p in params],
            scratch_shapes=[pltpu.VMEM(tuple(s), F32) for s in state_shapes],
            name=name + "_bwd", compiler_params=_cp("arbitrary", "arbitrary"))(*seqs, *params, *consts, *saved, *douts)
        return tuple(res[:ns]), tuple(res[ns:])

    @jax.custom_vjp
    def op(seqs, params):
        return fwd_call(seqs, params)[0]

    def op_fwd(seqs, params):
        outs, saved = fwd_call(seqs, params)
        return outs, (seqs, params, saved)

    def op_bwd(res, douts):
        seqs, params, saved = res
        return bwd_call(seqs, params, saved, tuple(douts))

    op.defvjp(op_fwd, op_bwd)
    return op(seqs, params)


def _positions(Q, rev):
    t = np.arange(Q)
    return (Q - 1 - t) if rev else t


def _ssd_consts(rev):
    Q = SSD_Q
    pos = _positions(Q, rev)
    tri = pos[:, None] >= pos[None, :]
    head_of = np.arange(A_INNER) // (A_INNER // A_HEADS)
    expand = np.arange(A_HEADS)[:, None] == head_of[None, :]
    group_of_row = np.arange(A_GROUPS * A_STATE) // A_STATE
    block = group_of_row[:, None] == (head_of // (A_HEADS // A_GROUPS))[None, :]
    as_bf = lambda m: jnp.asarray(m.astype(np.float32), BF16)
    return (as_bf(tri), as_bf(tri.T), jnp.asarray(tri.astype(np.float32)), as_bf(expand),
            jnp.asarray(block.astype(np.float32)))


def _ssd_step(state, seqs, params, consts):
    (st,) = state
    xs, bm, cm, dtr, dtr_t = seqs
    bias, bias_t, alog, alog_t = params
    tri_b, tri_t_b, tri_f, expand, block = consts
    dt = _softplus(dtr + bias)
    dta = dt * (-jnp.exp(alog))
    dt_t = _softplus(dtr_t + bias_t)
    dta_t = dt_t * (-jnp.exp(alog_t))
    cum = mdot(tri_b, dta)
    cum_t = xdot(dta_t, tri_t_b)
    total = jnp.sum(dta, axis=0, keepdims=True)
    w_end = jnp.exp(total - cum) * dt
    ecum_x = xdot(jnp.exp(cum), expand)
    wend_x = xdot(w_end, expand)
    dec_x = jnp.exp(jnp.sum(xdot(dta, expand), axis=0, keepdims=True))
    y = _bdot(cm, st) * ecum_x
    st_new = st * dec_x + block * _bdot(bm, xs * wend_x, TN)
    lane_g = lax.broadcasted_iota(jnp.int32, (1, A_GROUPS * A_STATE), 1) // A_STATE
    lane_h = lax.broadcasted_iota(jnp.int32, (1, A_INNER), 1) // (A_INNER // A_HEADS)
    sub_h = lax.broadcasted_iota(jnp.int32, (A_HEADS, 1), 0)
    lane8 = lax.broadcasted_iota(jnp.int32, (1, A_HEADS), 1)
    for g in range(A_GROUPS):
        cb = _bdot(jnp.where(lane_g == g, cm, 0.0), bm, NT)
        for h in range(g * (A_HEADS // A_GROUPS), (g + 1) * (A_HEADS // A_GROUPS)):
            col = jnp.sum(jnp.where(lane8 == h, cum, 0.0), axis=1, keepdims=True)
            row = jnp.sum(jnp.where(sub_h == h, cum_t, 0.0), axis=0, keepdims=True)
            dt_row = jnp.sum(jnp.where(sub_h == h, dt_t, 0.0), axis=0, keepdims=True)
            decay = tri_f * jnp.exp(tri_f * (col - row))
            y = y + _bdot(cb * decay * dt_row, jnp.where(lane_h == h, xs, 0.0))
    return (st_new,), (y,)


def ssd_scan(name, rev, xs, bm, cm, dtr, dt_bias, a_log, n_ctx):
    seqs = (xs, bm, cm, dtr, jnp.swapaxes(dtr, 1, 2))
    params = (dt_bias.reshape(1, -1), dt_bias.reshape(-1, 1), a_log.reshape(1, -1), a_log.reshape(-1, 1))
    (y,) = chunk_scan(_ssd_step, name, rev, seqs, (False, False, False, False, True), params, _ssd_consts(rev),
                      [(A_GROUPS * A_STATE, A_INNER)], [A_INNER], SSD_Q, n_ctx)
    return y


def _recurrence(v, a, order, n_ctx, name, x_fwd=None):
    B, L = v.shape[:2]
    T = REC_T
    nck, ncc = L // T, n_ctx // T
    nlat = nck - ncc
    chunk_of = {"F": lambda k: k,
                "R": lambda k: jnp.where(k < ncc, ncc - 1 - k, nck + ncc - 1 - k),
                "FT": lambda k: nck - 1 - k,
                "RT": lambda k: jnp.where(k < nlat, ncc + k, k - nlat)}[order]
    descending = order in ("R", "FT")
    with_da = x_fwd is not None
    H = 8

    def body(*refs):
        if with_da:
            v_ref, a_ref, xf_ref, x_ref, da_ref, st_ref = refs
        else:
            v_ref, a_ref, x_ref, st_ref = refs
        k = pl.program_id(0)

        @pl.when(k == 0)
        def _():
            st_ref[...] = jnp.zeros_like(st_ref)
            if with_da:
                da_ref[...] = jnp.zeros_like(da_ref)

        ar, ai = a_ref[pl.ds(0, H), :], a_ref[pl.ds(H, H), :]
        zero = jnp.zeros((H, 128), F32)

        def step(i, carry):
            chains, acc_r, acc_i = carry
            t = (T - 1 - i) if descending else i
            out = []
            for b, (xr, xi) in enumerate(chains):
                if with_da:
                    fr, fi = xf_ref[b, t, pl.ds(0, H), :], xf_ref[b, t, pl.ds(H, H), :]
                    acc_r = acc_r + xr * fr + xi * fi
                    acc_i = acc_i + xi * fr - xr * fi
                nr = ar * xr - ai * xi + v_ref[b, t, pl.ds(0, H), :]
                ni = ar * xi + ai * xr + v_ref[b, t, pl.ds(H, H), :]
                x_ref[b, t, pl.ds(0, H), :] = nr
                x_ref[b, t, pl.ds(H, H), :] = ni
                out.append((nr, ni))
            return tuple(out), acc_r, acc_i

        init = tuple((st_ref[b, pl.ds(0, H), :], st_ref[b, pl.ds(H, H), :]) for b in range(B))
        chains, acc_r, acc_i = lax.fori_loop(0, T, step, (init, zero, zero), unroll=8)
        for b, (xr, xi) in enumerate(chains):
            st_ref[b, pl.ds(0, H), :] = xr
            st_ref[b, pl.ds(H, H), :] = xi
        if with_da:
            da_ref[pl.ds(0, H), :] += acc_r
            da_ref[pl.ds(H, H), :] += acc_i

    seq = pl.BlockSpec((B, T, 2 * H, 128), lambda k: (0, chunk_of(k), 0, 0))
    par = pl.BlockSpec((2 * H, 128), lambda k: (0, 0))
    x_shape = jax.ShapeDtypeStruct(v.shape, F32)
    if with_da:
        return pl.pallas_call(
            body, grid=(nck,), in_specs=[seq, par, seq], out_specs=[seq, par],
            out_shape=[x_shape, jax.ShapeDtypeStruct((2 * H, 128), F32)],
            scratch_shapes=[pltpu.VMEM((B, 2 * H, 128), F32)], name=name, compiler_params=_cp("arbitrary"))(v, a, x_fwd)
    return pl.pallas_call(
        body, grid=(nck,), in_specs=[seq, par], out_specs=seq, out_shape=x_shape,
        scratch_shapes=[pltpu.VMEM((B, 2 * H, 128), F32)], name=name, compiler_params=_cp("arbitrary"))(v, a)


def lin_rec(v, a, rev, n_ctx, name):
    fwd_order, bwd_order = ("R", "RT") if rev else ("F", "FT")
    conj = jnp.concatenate([jnp.ones((8, 128), F32), -jnp.ones((8, 128), F32)], axis=0)

    @jax.custom_vjp
    def f(v, a):
        return _recurrence(v, a, fwd_order, n_ctx, name + "_fwd")

    def fwd(v, a):
        x = f(v, a)
        return x, (x, a)

    def bwd(res, dx):
        x, a = res
        g, da = _recurrence(dx, a * conj, bwd_order, n_ctx, name + "_bwd", x_fwd=x)
        return g, da

    f.defvjp(fwd, bwd)
    return f(v, a)


def _block_diag(t):
    G, a, b = t.shape
    eye = jnp.eye(G, dtype=t.dtype)
    return (t[:, :, None, :] * eye[:, None, :, None]).reshape(G * a, G * b)


def s5_scan(name, rev, u, lam_re, lam_im, log_step, b_re, b_im, c_re, c_im, n_ctx):
    step = jnp.exp(log_step)[:, None]
    mag = jnp.exp(lam_re * step)
    ar = mag * jnp.cos(lam_im * step)
    ai = mag * jnp.sin(lam_im * step)
    den = lam_re * lam_re + lam_im * lam_im
    nr = ar - 1.0
    kr = (nr * lam_re + ai * lam_im) / den
    ki = (ai * lam_re - nr * lam_im) / den
    br = kr[..., None] * b_re - ki[..., None] * b_im
    bi = kr[..., None] * b_im + ki[..., None] * b_re
    B, L, _ = u.shape
    width = B_NGROUPS * B_STATE
    w_b = jnp.concatenate([_block_diag(jnp.swapaxes(br, 1, 2)), _block_diag(jnp.swapaxes(bi, 1, 2))], axis=1)
    w_c = jnp.concatenate([_block_diag(jnp.swapaxes(c_re, 1, 2)), -_block_diag(jnp.swapaxes(c_im, 1, 2))], axis=0)
    a = jnp.concatenate([ar.reshape(8, 128), ai.reshape(8, 128)], axis=0)
    v = linear(u.reshape(B * L, B_WIDTH), w_b.astype(BF16), w_b, name + "_in")
    x = lin_rec(v.reshape(B, L, 16, 128), a, rev, n_ctx, name)
    y = linear(x.reshape(B * L, 2 * width), w_c.astype(BF16), w_c, name + "_out")
    return y.reshape(B, L, B_WIDTH)


def _hg_consts(rev):
    Q = HG_Q
    pos = _positions(Q, rev)
    pi, pj = pos[:, None], pos[None, :]
    mats = [pi >= pj, pj > pi]
    pairs = []
    s = HG_BAND
    while s < Q:
        second = (pos // s) % 2 == 1
        mid = (pos // (2 * s)) * 2 * s + s
        mats.append(second[:, None] & (pj >= mid[:, None]) & (pj <= pi))
        mats.append((~second)[:, None] & (pj > pi) & (pj < mid[:, None]))
        pairs.append(second[:, None] & (~second)[None, :] & ((pi // (2 * s)) == (pj // (2 * s))))
        s *= 2
    rows = [((pos // s_) % 2 == 1) for s_ in [HG_BAND * 2 ** n for n in range(len(pairs))]]
    band = [(pos % HG_BAND) >= d for d in range(HG_BAND)]
    mats_b = jnp.asarray(np.stack(mats).astype(np.float32), BF16)
    pairs_f = jnp.asarray(np.stack(pairs).astype(np.float32))
    rows_f = jnp.asarray(np.stack(rows).astype(np.float32)[:, :, None])
    band_f = jnp.asarray(np.stack(band).astype(np.float32)[:, :, None])
    return mats_b, pairs_f, rows_f, band_f


def _hg_step_fn(rev, n_levels):
    def step(state, seqs, params, consts):
        (st,) = state
        qr, zf, v = seqs
        (lb,) = params
        mats, pairs, rows, band = consts
        W = C_WIDTH
        same_head = (lax.broadcasted_iota(jnp.int32, (W, W), 0) // C_KEY
                     == lax.broadcasted_iota(jnp.int32, (W, W), 1) // C_KEY)
        lane_h = lax.broadcasted_iota(jnp.int32, (1, W), 1) // C_KEY
        q = _silu(qr)
        f = lb + (1.0 - lb) * jax.nn.sigmoid(zf)
        logf = jnp.log(f)
        kk = 1.0 - f
        qd = q * jnp.exp(mdot(mats[0], logf))
        w = kk * jnp.exp(mdot(mats[1], logf))
        total = jnp.sum(logf, axis=0, keepdims=True)
        o = _bdot(qd, st, NT)
        st_new = st * jnp.exp(total) + jnp.where(same_head, _bdot(v, w, TN), 0.0)
        scores = [None] * C_HEADS
        for n in range(n_levels):
            a = q * jnp.exp(mdot(mats[2 + 2 * n], logf)) * rows[n]
            bk = kk * jnp.exp(mdot(mats[3 + 2 * n], logf)) * (1.0 - rows[n])
            for h in range(C_HEADS):
                sc = pairs[n] * _bdot(jnp.where(lane_h == h, a, 0.0), bk, NT)
                scores[h] = sc if scores[h] is None else scores[h] + sc
        for h in range(C_HEADS):
            o = o + _bdot(scores[h], jnp.where(lane_h == h, v, 0.0))
        same_head_b = same_head.astype(BF16)
        e = jnp.zeros_like(logf)
        for d in range(HG_BAND):
            s = -d if rev else d
            if d > 0:
                e = e + sroll(logf, s + (1 if rev else -1))
            kd, vd = (kk, v) if d == 0 else (sroll(kk, s), sroll(v, s))
            p = q * kd * jnp.exp(e)
            o = o + band[d] * (_bdot(p, same_head_b) * vd)
        return (st_new,), (o,)

    return step


def hg_scan(name, rev, qr, zf, v, lower, n_ctx):
    consts = _hg_consts(rev)
    (o,) = chunk_scan(_hg_step_fn(rev, consts[1].shape[0]), name, rev, (qr, zf, v), (False,) * 3,
                      (lower.reshape(1, -1),), consts, [(C_WIDTH, C_WIDTH)], [C_WIDTH], HG_Q, n_ctx)
    return o


def conv_silu(x, w, b, n_ctx, name):
    B, L, C = x.shape
    TC = 128
    w8 = jnp.zeros((8, C), F32).at[:A_CONV].set(w)
    b2 = b.reshape(1, C)
    pad = A_CONV // 2

    def taps(v, sign):
        t = lax.broadcasted_iota(jnp.int32, v.shape, 0)
        out = []
        for k in range(A_CONV):
            s = sign * (k - pad)
            src = t + s
            ok = (src >= 0) & (src < L) & ((t >= n_ctx) == (src >= n_ctx))
            vs = v if s == 0 else pltpu.roll(v, (-s) % L, 0)
            out.append((k, jnp.where(ok, vs, 0.0)))
        return out

    def pre(x_ref, w_ref, b_ref):
        xv = x_ref[0]
        y = b_ref[...] + sum(xs * w_ref[pl.ds(k, 1), :] for k, xs in taps(xv, 1))
        return xv, y

    def fwd_body(x_ref, w_ref, b_ref, o_ref):
        _, y = pre(x_ref, w_ref, b_ref)
        o_ref[0] = _silu(y)

    def bwd_body(x_ref, w_ref, b_ref, g_ref, dx_ref, dw_ref, db_ref):
        xv, y = pre(x_ref, w_ref, b_ref)
        sg = jax.nn.sigmoid(y)
        dy = g_ref[0] * (sg + y * sg * (1.0 - sg))
        dx_ref[0] = sum(ds * w_ref[pl.ds(k, 1), :] for k, ds in taps(dy, -1))
        first = pl.program_id(1) == 0

        @pl.when(first)
        def _():
            dw_ref[...] = jnp.zeros_like(dw_ref)
            db_ref[...] = jnp.zeros_like(db_ref)

        for k, xs in taps(xv, 1):
            dw_ref[pl.ds(k, 1), :] += jnp.sum(dy * xs, axis=0, keepdims=True)
        db_ref[...] += jnp.sum(dy, axis=0, keepdims=True)

    x_spec = pl.BlockSpec((1, L, TC), lambda j, bb: (bb, 0, j))
    w_spec = pl.BlockSpec((8, TC), lambda j, bb: (0, j))
    b_spec = pl.BlockSpec((1, TC), lambda j, bb: (0, j))

    @jax.custom_vjp
    def op(x, w8, b2):
        return pl.pallas_call(fwd_body, grid=(C // TC, B), in_specs=[x_spec, w_spec, b_spec], out_specs=x_spec,
                              out_shape=jax.ShapeDtypeStruct(x.shape, F32), name=name + "_fwd",
                              compiler_params=_cp("parallel", "parallel"))(x, w8, b2)

    def op_fwd(x, w8, b2):
        return op(x, w8, b2), (x, w8, b2)

    def op_bwd(res, g):
        x, w8, b2 = res
        return tuple(pl.pallas_call(
            bwd_body, grid=(C // TC, B), in_specs=[x_spec, w_spec, b_spec, x_spec],
            out_specs=[x_spec, w_spec, b_spec],
            out_shape=[jax.ShapeDtypeStruct(x.shape, F32), jax.ShapeDtypeStruct(w8.shape, F32),
                       jax.ShapeDtypeStruct(b2.shape, F32)],
            name=name + "_bwd", compiler_params=_cp("parallel", "arbitrary"))(x, w8, b2, g))

    op.defvjp(op_fwd, op_bwd)
    return op(x, w8, b2)


def loss_head(h, target, norm_w, n_ctx):
    B, L, Dm = h.shape
    S = target.shape[1]
    tl = ROW_TILE
    skip = n_ctx // tl

    def body(h_ref, t_ref, w_ref, loss_ref, dh_ref, dw_ref):
        b, i = pl.program_id(0), pl.program_id(1)
        fn = lambda hv, wv: _rms(hv) * wv
        y, vjp = jax.vjp(fn, h_ref[0], w_ref[...])
        err = y - t_ref[0]
        dhv, dwv = vjp(err * (1.0 / Dm))
        dh_ref[0] = dhv
        part = 0.5 * jnp.sum(jnp.mean(err * err, axis=-1, keepdims=True), axis=0, keepdims=True)
        first = (b == 0) & (i == 0)

        @pl.when(first)
        def _():
            loss_ref[...] = jnp.zeros_like(loss_ref)
            dw_ref[...] = jnp.zeros_like(dw_ref)

        loss_ref[...] += jnp.broadcast_to(part, loss_ref.shape)
        dw_ref[...] += dwv

    loss, dh, dw = pl.pallas_call(
        body, grid=(B, S // tl),
        in_specs=[pl.BlockSpec((1, tl, Dm), lambda b, i: (b, i + skip, 0)),
                  pl.BlockSpec((1, tl, Dm), lambda b, i: (b, i, 0)),
                  pl.BlockSpec((1, Dm), lambda b, i: (0, 0))],
        out_specs=[pl.BlockSpec((8, 128), lambda b, i: (0, 0)),
                   pl.BlockSpec((1, tl, Dm), lambda b, i: (b, i, 0)),
                   pl.BlockSpec((1, Dm), lambda b, i: (0, 0))],
        out_shape=[jax.ShapeDtypeStruct((8, 128), F32), jax.ShapeDtypeStruct((B, S, Dm), F32),
                   jax.ShapeDtypeStruct((1, Dm), F32)],
        name="loss_head", compiler_params=_cp("arbitrary", "arbitrary"))(h, target, norm_w.reshape(1, Dm))
    dh_full = jnp.concatenate([jnp.zeros((B, n_ctx, Dm), F32), dh], axis=1)
    return loss[0, 0], dh_full, dw.reshape(Dm)


def _modulate(h, shift, scale):
    return _rms(h) * (1.0 + scale) + shift


def _f_mod(r, m, p):
    return (_modulate(r[0], m[0], m[1]),)


def _f_resid_mod(coef):
    def f(r, m, p):
        h2 = r[0] + coef * m[0] * r[1]
        return h2, _modulate(h2, m[1], m[2])
    return f


def _f_resid(coef):
    def f(r, m, p):
        return (r[0] + coef * m[0] * r[1],)
    return f


def _f_swiglu(r, m, p):
    pre = r[0].astype(F32)
    return (_silu(pre[:, :D_FF]) * pre[:, D_FF:],)


def _f_ssd_post(r, m, p):
    y_f, y_b, xs, z = r
    d_skip, norm_w = p
    y = (y_f + y_b + d_skip * xs) * _silu(z)
    return (_rms(y) * norm_w,)


def _f_s5_post(r, m, p):
    y_f, y_b, u = r
    d_skip, glu_w, glu_b = p
    y = jax.nn.gelu(y_f + y_b + d_skip * u)
    return (y * jax.nn.sigmoid(_bdot(y, glu_w) + glu_b),)


def _f_hg_post(r, m, p):
    o_f, o_b, g = r
    (norm_w,) = p
    W = C_WIDTH
    same_head = (lax.broadcasted_iota(jnp.int32, (W, W), 0) // C_KEY
                 == lax.broadcasted_iota(jnp.int32, (W, W), 1) // C_KEY).astype(BF16)
    o = o_f + o_b
    ms = xdot(o * o, same_head) * (1.0 / C_KEY)
    return (o * lax.rsqrt(ms + EPS) * norm_w * _silu(g),)


def _pad_w_in(w):
    dt0 = A_INNER + A_CONV_DIM
    zeros = jnp.zeros((w.shape[0], IN_COLS_PAD - IN_COLS), w.dtype)
    return jnp.concatenate([w[:, :dt0], w[:, A_COLS:], w[:, dt0:A_COLS], zeros], axis=1)


def _to_columns(t, n_ctx):
    B, L, W = t.shape
    rows = (L - n_ctx) // GRID_W
    lat = t[:, n_ctx:].reshape(B, rows, GRID_W, W).transpose(0, 2, 1, 3).reshape(B, L - n_ctx, W)
    return jnp.concatenate([t[:, :n_ctx], lat], axis=1)


def _to_raster(t, n_ctx):
    B, L, W = t.shape
    rows = (L - n_ctx) // GRID_W
    lat = t[:, n_ctx:].reshape(B, GRID_W, rows, W).transpose(0, 2, 1, 3).reshape(B, L - n_ctx, W)
    return jnp.concatenate([t[:, :n_ctx], lat], axis=1)


def _forward(h0, c, big, P, n_ctx):
    B, L, Dm = h0.shape
    depth = len(big["mod_w"])
    ncb = n_ctx // ROW_TILE

    def rows_of(t):
        return t.reshape(t.shape[0] * t.shape[1], t.shape[2])

    def w_in_of(t):
        return _pad_w_in(jnp.moveaxis(t, 0, 1).reshape(t.shape[1], IN_COLS))

    p_lb = jax.nn.softmax(P["hg_lb_logits"], axis=0)
    lower_bounds = jnp.cumsum(p_lb, axis=0) - p_lb[:1]
    cc = jnp.concatenate([c, P["c_ctx"][None], jnp.zeros((8 - B - 1, Dm), F32)], axis=0)
    cc = _silu(cc)

    def mods_of(l):
        m = linear(cc, big["mod_w"][l], P["mod_w"][l], f"mod{l}", pieces=True) + P["mod_b"][l]
        m = m.reshape(8, N_MOD, Dm)
        seg = jnp.stack([jnp.broadcast_to(m[B], (B, N_MOD, Dm)), m[:B]], axis=1)
        return [seg[:, :, j:j + 1, :] for j in range(N_MOD)]

    def ffn(u, l, j):
        pre = linear(u.reshape(B * L, Dm), big["ffn_w_in"][l][j], P["ffn_w_in"][l][j], f"ffn_in{l}{j}", BF16,
                     pieces=True)
        (act,) = rowwise(_f_swiglu, f"swiglu{l}{j}", [pre.reshape(B, L, 2 * D_FF)], [], [], [(D_FF, BF16)], ncb)
        out = linear(act.reshape(B * L, D_FF), rows_of(big["ffn_w_out"][l][j]), rows_of(P["ffn_w_out"][l][j]),
                     f"ffn_out{l}{j}")
        return out.reshape(B, L, Dm)

    h = h0
    mods = mods_of(0)
    (u,) = rowwise(_f_mod, "mod_first", [h], [mods[0], mods[1]], [], [(Dm, BF16)], ncb)
    for l in range(depth):
        col_major = l % 2 == 1
        o = ffn(u, l, 0)
        h, u = rowwise(_f_resid_mod(0.5), f"resid_a{l}", [h, o], [mods[2], mods[3], mods[4]], [],
                       [(Dm, F32), (Dm, BF16)], ncb)
        if col_major:
            u = _to_columns(u, n_ctx)
        pre = linear(u.reshape(B * L, Dm), w_in_of(big["w_in"][l]), w_in_of(P["w_in"][l]), f"w_in{l}")
        pre = pre.reshape(B, L, IN_COLS_PAD)
        o0 = 0
        z, o0 = pre[..., o0:o0 + A_INNER], o0 + A_INNER
        xbc, o0 = pre[..., o0:o0 + A_CONV_DIM], o0 + A_CONV_DIM
        pb, o0 = pre[..., o0:o0 + B_WIDTH], o0 + B_WIDTH
        pc, o0 = pre[..., o0:o0 + 5 * C_WIDTH], o0 + 5 * C_WIDTH
        dtr = pre[..., o0:o0 + 2 * A_HEADS]
        xbc = conv_silu(xbc, P["a_conv_w"][l], P["a_conv_b"][l], n_ctx, f"conv{l}")
        xs, bm, cm = xbc[..., :A_INNER], xbc[..., A_INNER:A_INNER + 128], xbc[..., A_INNER + 128:]
        ya_dir = [ssd_scan(f"ssd{l}{d}", bool(d), xs, bm, cm, dtr[..., d * A_HEADS:(d + 1) * A_HEADS],
                           P["a_dt_bias"][l, d], P["a_log"][l, d], n_ctx) for d in range(2)]
        d_skip = jnp.repeat(P["a_d"][l], A_INNER // A_HEADS).reshape(1, A_INNER)
        (ya,) = rowwise(_f_ssd_post, f"ssd_post{l}", [ya_dir[0], ya_dir[1], xs, z], [],
                        [d_skip, P["a_norm_w"][l].reshape(1, -1)], [(A_INNER, BF16)], ncb)
        yb_dir = [s5_scan(f"s5{l}{d}", bool(d), pb, P["s5_lam_re"][l, d], P["s5_lam_im"][l, d],
                          P["s5_log_step"][l, d], P["s5_b_re"][l, d], P["s5_b_im"][l, d], P["s5_c_re"][l, d],
                          P["s5_c_im"][l, d], n_ctx) for d in range(2)]
        (yb,) = rowwise(_f_s5_post, f"s5_post{l}", [yb_dir[0], yb_dir[1], pb], [],
                        [P["s5_d"][l].reshape(1, -1), P["s5_glu_w"][l], P["s5_glu_b"][l].reshape(1, -1)],
                        [(B_WIDTH, BF16)], ncb)
        qr, f_raw, vi, gate = (pc[..., :C_WIDTH], pc[..., C_WIDTH:3 * C_WIDTH], pc[..., 3 * C_WIDTH:4 * C_WIDTH],
                               pc[..., 4 * C_WIDTH:])
        yc_dir = [hg_scan(f"hg{l}{d}", bool(d), qr, f_raw[..., d * C_WIDTH:(d + 1) * C_WIDTH], vi,
                          lower_bounds[l, d], n_ctx) for d in range(2)]
        (yc,) = rowwise(_f_hg_post, f"hg_post{l}", [yc_dir[0], yc_dir[1], gate], [],
                        [P["hg_norm_w"][l].reshape(1, -1)], [(C_WIDTH, BF16)], ncb)
        mix = jnp.concatenate([ya, yb, yc], axis=-1)
        y = linear(mix.reshape(B * L, Dm), rows_of(big["w_out"][l]), rows_of(P["w_out"][l]), f"w_out{l}")
        y = y.reshape(B, L, Dm)
        if col_major:
            y = _to_raster(y, n_ctx)
        h, u = rowwise(_f_resid_mod(1.0), f"resid_b{l}", [h, y], [mods[5], mods[6], mods[7]], [],
                       [(Dm, F32), (Dm, BF16)], ncb)
        o = ffn(u, l, 1)
        gate8 = mods[8]
        if l + 1 < depth:
            mods = mods_of(l + 1)
            h, u = rowwise(_f_resid_mod(0.5), f"resid_c{l}", [h, o], [gate8, mods[0], mods[1]], [],
                           [(Dm, F32), (Dm, BF16)], ncb)
        else:
            (h,) = rowwise(_f_resid(0.5), f"resid_c{l}", [h, o], [gate8], [], [(Dm, F32)], ncb)
    return h


BIG = ("mod_w", "ffn_w_in", "ffn_w_out", "w_in", "w_out")


def local_step(x, c, ctx, target, big, small):
    n_ctx = ctx.shape[1]
    h0 = jnp.concatenate([ctx, x], axis=1)
    P = dict(small)
    for k in BIG:
        P[k] = jax.tree.map(lambda t: jnp.zeros(t.shape, BF16), big[k])
    h, vjp = jax.vjp(lambda h0, P: _forward(h0, c, big, P, n_ctx), h0, P)
    loss, dh, d_final = loss_head(h, target, small["final_norm_w"], n_ctx)
    dh0, grads = vjp(dh)
    grads = dict(grads)
    grads["final_norm_w"] = grads["final_norm_w"] + d_final
    return loss, dh0[:, n_ctx:], grads


MESH = pl.DeviceIdType.MESH
LANES = 1024
N_CHIPS = 4
ELEMENTWISE_BLOCK_BYTES = 2 * 1024 * 1024
ANY = pl.BlockSpec(memory_space=pl.ANY)


def _place():
    x, y, c = lax.axis_index("x"), lax.axis_index("y"), lax.axis_index("c")
    return x, y, c, 2 * x + y


def _other_chips(x, y):
    return [(x ^ kx, y ^ ky, 2 * (x ^ kx) + (y ^ ky)) for kx, ky in ((0, 1), (1, 0), (1, 1))]


def all_gather_shards(shard, name):
    def body(x_ref, out_ref, send_sems, recv_sems, local_sem):
        x, y, c, q = _place()
        sibling = (x, y, 1 - c)
        chips = _other_chips(x, y)

        def copy(k, src, dst, to):
            return pltpu.make_async_remote_copy(src_ref=src, dst_ref=dst, send_sem=send_sems.at[k],
                                                recv_sem=recv_sems.at[k], device_id=to, device_id_type=MESH)

        mine = pltpu.make_async_copy(x_ref, out_ref.at[q], local_sem)
        mine.start()
        first = [copy(k, x_ref.at[c], out_ref.at[q, c], (px, py, c)) for k, (px, py, _) in enumerate(chips)]
        for cp in first:
            cp.start()
        passed = [copy(3 + k, out_ref.at[pq, c], out_ref.at[pq, c], sibling) for k, (_, _, pq) in enumerate(chips)]
        for k, (_, _, pq) in enumerate(chips):
            copy(k, x_ref.at[c], out_ref.at[pq, c], sibling).wait_recv()
            passed[k].start()
        for k, (_, _, pq) in enumerate(chips):
            copy(3 + k, x_ref.at[c], out_ref.at[pq, 1 - c], sibling).wait_recv()
        for cp in first + passed:
            cp.wait_send()
        mine.wait()

    return pl.pallas_call(
        body, out_shape=jax.ShapeDtypeStruct((N_CHIPS,) + shard.shape, shard.dtype), in_specs=[ANY], out_specs=ANY,
        scratch_shapes=[pltpu.SemaphoreType.DMA((6,)), pltpu.SemaphoreType.DMA((6,)), pltpu.SemaphoreType.DMA],
        name=name)(shard)


def all_gather_devices(part, name):
    def body(x_ref, out_ref, send_sems, recv_sems, local_sem):
        x, y, c, _ = _place()
        me = 4 * x + 2 * y + c
        mine = pltpu.make_async_copy(x_ref, out_ref.at[me], local_sem)
        mine.start()
        copies = []
        for k in range(1, 8):
            px, py, pc = x ^ (k >> 2), y ^ ((k >> 1) & 1), c ^ (k & 1)
            copies.append(pltpu.make_async_remote_copy(
                src_ref=x_ref, dst_ref=out_ref.at[me], send_sem=send_sems.at[k - 1], recv_sem=recv_sems.at[k - 1],
                device_id=(px, py, pc), device_id_type=MESH))
            copies[-1].start()
        for k in range(1, 8):
            peer = 4 * (x ^ (k >> 2)) + 2 * (y ^ ((k >> 1) & 1)) + (c ^ (k & 1))
            pltpu.make_async_remote_copy(
                src_ref=x_ref, dst_ref=out_ref.at[peer], send_sem=send_sems.at[k - 1], recv_sem=recv_sems.at[k - 1],
                device_id=(x, y, c), device_id_type=MESH).wait_recv()
        for cp in copies:
            cp.wait_send()
        mine.wait()

    return pl.pallas_call(
        body, out_shape=jax.ShapeDtypeStruct((8,) + part.shape, part.dtype), in_specs=[ANY], out_specs=ANY,
        scratch_shapes=[pltpu.SemaphoreType.DMA((7,)), pltpu.SemaphoreType.DMA((7,)), pltpu.SemaphoreType.DMA],
        name=name)(part)


def _half(ref, c, axis):
    h = ref.shape[axis] // 2
    return ref.at[(slice(None),) * axis + (pl.ds(pl.multiple_of(c * h, 8), h),)]


def _remote(src, dst, send_sems, recv_sems, s, to):
    return pltpu.make_async_remote_copy(src_ref=src, dst_ref=dst, send_sem=send_sems.at[s], recv_sem=recv_sems.at[s],
                                        device_id=to, device_id_type=MESH)


def gather_pieces(arrays, name):
    n = len(arrays)

    def body(*refs):
        ins, outs, (send_sems, recv_sems) = refs[:n], refs[n:2 * n], refs[2 * n:]
        x, y, c, q = _place()
        sibling = (x, y, 1 - c)
        chips = _other_chips(x, y)
        started = []

        def go(src, dst, s, to):
            started.append(_remote(src, dst, send_sems, recv_sems, s, to))
            started[-1].start()

        for i, (a, o) in enumerate(zip(ins, outs)):
            go(a, o.at[q], 7 * i + 6, sibling)
            for k, (px, py, _) in enumerate(chips):
                go(_half(a, c, 0), _half(o.at[q], c, 0), 7 * i + k, (px, py, c))
        for i, o in enumerate(outs):
            for k, (_, _, pq) in enumerate(chips):
                land = _half(o.at[pq], c, 0)
                _remote(land, land, send_sems, recv_sems, 7 * i + k, sibling).wait_recv()
                go(land, land, 7 * i + 3 + k, sibling)
        for i, (a, o) in enumerate(zip(ins, outs)):
            for k, (_, _, pq) in enumerate(chips):
                land = _half(o.at[pq], 1 - c, 0)
                _remote(land, land, send_sems, recv_sems, 7 * i + 3 + k, sibling).wait_recv()
            _remote(a, o.at[q], send_sems, recv_sems, 7 * i + 6, sibling).wait_recv()
        for cp in started:
            cp.wait_send()

    return pl.pallas_call(
        body, out_shape=[jax.ShapeDtypeStruct((N_CHIPS,) + a.shape, a.dtype) for a in arrays], in_specs=[ANY] * n,
        out_specs=[ANY] * n, scratch_shapes=[pltpu.SemaphoreType.DMA((7 * n,)), pltpu.SemaphoreType.DMA((7 * n,))],
        name=name)(*arrays)


def swap_with_sibling(arrays, name, send_other_half_of_axis=None):
    n = len(arrays)
    ax = send_other_half_of_axis

    def body(*refs):
        ins, outs, (send_sems, recv_sems) = refs[:n], refs[n:2 * n], refs[2 * n:]
        x, y, c, _ = _place()
        copies = [_remote(a if ax is None else _half(a, 1 - c, ax), o, send_sems, recv_sems, i, (x, y, 1 - c))
                  for i, (a, o) in enumerate(zip(ins, outs))]
        for cp in copies:
            cp.start()
        for cp in copies:
            cp.wait()

    def out_of(a):
        shape = list(a.shape)
        if ax is not None:
            shape[ax] //= 2
        return jax.ShapeDtypeStruct(tuple(shape), a.dtype)

    return pl.pallas_call(
        body, out_shape=[out_of(a) for a in arrays], in_specs=[ANY] * n, out_specs=[ANY] * n,
        scratch_shapes=[pltpu.SemaphoreType.DMA((n,)), pltpu.SemaphoreType.DMA((n,))], name=name)(*arrays)


def pieces_to_chips(pairs, name):
    n = len(pairs)

    def body(*refs):
        ins, outs, (send_sems, recv_sems) = refs[:n], refs[n:2 * n], refs[2 * n:]
        x, y, c, q = _place()
        chips = _other_chips(x, y)
        copies = [_remote(a.at[pq], o.at[q], send_sems, recv_sems, 3 * i + k, (px, py, c))
                  for i, (a, o) in enumerate(zip(ins, outs)) for k, (px, py, pq) in enumerate(chips)]
        for cp in copies:
            cp.start()
        for i, (a, o) in enumerate(zip(ins, outs)):
            for k, (_, _, pq) in enumerate(chips):
                _remote(a.at[pq], o.at[pq], send_sems, recv_sems, 3 * i + k, (x, y, c)).wait_recv()
        for cp in copies:
            cp.wait_send()

    return pl.pallas_call(
        body, out_shape=[jax.ShapeDtypeStruct(a.shape, a.dtype) for a in pairs], in_specs=[ANY] * n,
        out_specs=[ANY] * n, scratch_shapes=[pltpu.SemaphoreType.DMA((3 * n,)), pltpu.SemaphoreType.DMA((3 * n,))],
        name=name)(*pairs)


def _rows_block(h, n_cols):
    return _tile(h, 16, max(16, ELEMENTWISE_BLOCK_BYTES // (n_cols * 4)))


def add_pair(core, g, got, name):
    _, K, N = g.shape
    h = K // 2
    th = _rows_block(h, N)
    nb = h // th

    def body(c_ref, g_ref, got_ref, o_ref):
        o_ref[...] = (g_ref[...].astype(F32) + got_ref[...].astype(F32)).astype(o_ref.dtype)

    grid_spec = pltpu.PrefetchScalarGridSpec(
        num_scalar_prefetch=1, grid=(N_CHIPS, nb),
        in_specs=[pl.BlockSpec((None, th, N), lambda p, i, c_ref: (p, c_ref[0] * nb + i, 0)),
                  pl.BlockSpec((None, th, N), lambda p, i, c_ref: (p, i, 0))],
        out_specs=pl.BlockSpec((None, th, N), lambda p, i, c_ref: (p, i, 0)))
    return pl.pallas_call(body, grid_spec=grid_spec, out_shape=jax.ShapeDtypeStruct(got.shape, g.dtype), name=name,
                          compiler_params=_cp("parallel", "parallel"))(core, g, got)


def sum_pieces(chip, pair, got, name):
    _, h, N = pair.shape
    th = _rows_block(h, N)

    def body(q_ref, pair_ref, *rest):
        got_refs, o_ref = rest[:N_CHIPS], rest[N_CHIPS]
        q = q_ref[0]
        for p in range(N_CHIPS):
            def put(val, p=p):
                if p == 0:
                    o_ref[...] = val
                else:
                    o_ref[...] += val

            @pl.when(q == p)
            def _():
                put(pair_ref[...].astype(F32))

            @pl.when(q != p)
            def _(p=p):
                put(got_refs[p][...].astype(F32))

    def got_spec(p):
        return pl.BlockSpec((None, th, N), lambda i, q_ref: (jnp.where(q_ref[0] == p, (p + 1) % N_CHIPS, p), i, 0))

    grid_spec = pltpu.PrefetchScalarGridSpec(
        num_scalar_prefetch=1, grid=(h // th,),
        in_specs=[pl.BlockSpec((None, th, N), lambda i, q_ref: (q_ref[0], i, 0))] + [got_spec(p) for p in range(N_CHIPS)],
        out_specs=pl.BlockSpec((th, N), lambda i, q_ref: (i, 0)))
    return pl.pallas_call(body, grid_spec=grid_spec, out_shape=jax.ShapeDtypeStruct((h, N), F32), name=name,
                          compiler_params=_cp("parallel"))(chip, pair, *([got] * N_CHIPS))


def adamw_halves(core, mine, other, w, m, v, name):
    n, K, N = w.shape
    h = K // 2
    th = _rows_block(h, N)
    nb = h // th

    def body(c_ref, *refs):
        mine_refs, other_refs = refs[:n], refs[n:2 * n]
        w_ref, m_ref, v_ref, g_ref, d_ref, m2_ref, v2_ref = refs[2 * n:]
        a, s, c = pl.program_id(0), pl.program_id(1), c_ref[0]
        for idx in range(n):
            @pl.when((a == idx) & (s == c))
            def _(idx=idx):
                g_ref[...] = mine_refs[idx][...]

            @pl.when((a == idx) & (s != c))
            def _(idx=idx):
                g_ref[...] = other_refs[idx][...]
        _adamw_math(g_ref[...], w_ref, m_ref, v_ref, d_ref, m2_ref, v2_ref)

    def half_spec(idx, is_mine):
        def index(a, s, i, c_ref):
            right_half = (s == c_ref[0]) if is_mine else (s != c_ref[0])
            return (jnp.where((a == idx) & right_half, i, 0), 0)
        return pl.BlockSpec((th, N), index)

    full = pl.BlockSpec((None, th, N), lambda a, s, i, c_ref: (a, s * nb + i, 0))
    grid_spec = pltpu.PrefetchScalarGridSpec(
        num_scalar_prefetch=1, grid=(n, 2, nb),
        in_specs=[half_spec(idx, True) for idx in range(n)] + [half_spec(idx, False) for idx in range(n)] + [full] * 3,
        out_specs=[full] * 4)
    return pl.pallas_call(body, grid_spec=grid_spec, out_shape=[jax.ShapeDtypeStruct((n, K, N), F32)] * 4, name=name,
                          compiler_params=_cp("arbitrary", "arbitrary", "arbitrary"))(core, *mine, *other, w, m, v)


def _adamw_math(gv, w_ref, m_ref, v_ref, d_ref, m2_ref, v2_ref):
    m2 = ADAM_B1 * m_ref[...] + (1.0 - ADAM_B1) * gv
    v2 = ADAM_B2 * v_ref[...] + (1.0 - ADAM_B2) * (gv * gv)
    m_hat = m2 / (1.0 - ADAM_B1 ** ADAM_STEP)
    v_hat = v2 / (1.0 - ADAM_B2 ** ADAM_STEP)
    d_ref[...] = -ADAM_LR * (m_hat / (jnp.sqrt(v_hat) + ADAM_EPS) + ADAM_WD * w_ref[...])
    m2_ref[...] = m2
    v2_ref[...] = v2


def _row_tile(R):
    return _pick(R, (512, 256, 128, 64, 32, 16, 8))


def sum_parts(parts, name):
    P, R, _ = parts.shape
    rt = _row_tile(R)

    def body(p_ref, o_ref):
        acc = p_ref[0]
        for p in range(1, P):
            acc = acc + p_ref[p]
        o_ref[...] = acc

    return pl.pallas_call(
        body, grid=(R // rt,), in_specs=[pl.BlockSpec((P, rt, LANES), lambda i: (0, i, 0))],
        out_specs=pl.BlockSpec((rt, LANES), lambda i: (i, 0)), out_shape=jax.ShapeDtypeStruct((R, LANES), F32),
        name=name, compiler_params=_cp("parallel"))(parts)


def adamw(g, w, m, v, name):
    R = g.shape[0]
    rt = _row_tile(R)

    def body(g_ref, w_ref, m_ref, v_ref, d_ref, m2_ref, v2_ref):
        _adamw_math(g_ref[...], w_ref, m_ref, v_ref, d_ref, m2_ref, v2_ref)

    spec = pl.BlockSpec((rt, LANES), lambda i: (i, 0))
    return pl.pallas_call(
        body, grid=(R // rt,), in_specs=[spec] * 4, out_specs=[spec] * 3,
        out_shape=[jax.ShapeDtypeStruct((R, LANES), F32)] * 3, name=name, compiler_params=_cp("parallel"))(g, w, m, v)


def _pack(arrays, rows_multiple, dtype):
    flat = jnp.concatenate([a.reshape(-1).astype(dtype) for a in arrays])
    n = flat.shape[0]
    per = rows_multiple * LANES
    total = -(-n // per) * per
    return jnp.pad(flat, (0, total - n)).reshape(total // LANES, LANES)


def _unpack(buf, shapes):
    flat = buf.reshape(-1)
    out, off = [], 0
    for s in shapes:
        n = math.prod(s)
        out.append(flat[off:off + n].reshape(s))
        off += n
    return out


SHARDED = (("mod_w", 2), ("ffn_w_in", 3), ("ffn_w_out", 2), ("w_in", 2), ("w_out", 1),
           ("a_conv_w", 2), ("s5_glu_w", 1), ("hg_lb_logits", 2))
WEIGHTS = ("c_ctx", "mod_w", "mod_b", "ffn_w_in", "ffn_w_out", "w_in", "w_out", "a_conv_w", "a_conv_b", "a_dt_bias",
           "a_log", "a_d", "a_norm_w", "s5_lam_re", "s5_lam_im", "s5_log_step", "s5_b_re", "s5_b_im", "s5_c_re",
           "s5_c_im", "s5_d", "s5_glu_w", "s5_glu_b", "hg_lb_logits", "hg_norm_w", "final_norm_w")


def _gather_full(local, names_axes, dtype, rows_multiple, name):
    buf = _pack([local[n] for n, _ in names_axes], 2 * rows_multiple, dtype)
    R = buf.shape[0] // 2
    full = all_gather_shards(buf.reshape(2, R, LANES), name).reshape(N_CHIPS, 2 * R, LANES)
    per_chip = [_unpack(full[q], [local[n].shape for n, _ in names_axes]) for q in range(N_CHIPS)]
    return {n: jnp.concatenate([per_chip[q][j] for q in range(N_CHIPS)], axis=ax)
            for j, (n, ax) in enumerate(names_axes)}


def kernel(x, c, ctx, c_ctx, mod_w, mod_b, ffn_w_in, ffn_w_out, w_in, w_out, a_conv_w, a_conv_b, a_dt_bias, a_log, a_d, a_norm_w, s5_lam_re, s5_lam_im, s5_log_step, s5_b_re, s5_b_im, s5_c_re, s5_c_im, s5_d, s5_glu_w, s5_glu_b, hg_lb_logits, hg_norm_w, final_norm_w, loss_target, m_c_ctx, m_mod_w, m_mod_b, m_ffn_w_in, m_ffn_w_out, m_w_in, m_w_out, m_a_conv_w, m_a_conv_b, m_a_dt_bias, m_a_log, m_a_d, m_a_norm_w, m_s5_lam_re, m_s5_lam_im, m_s5_log_step, m_s5_b_re, m_s5_b_im, m_s5_c_re, m_s5_c_im, m_s5_d, m_s5_glu_w, m_s5_glu_b, m_hg_lb_logits, m_hg_norm_w, m_final_norm_w, v_c_ctx, v_mod_w, v_mod_b, v_ffn_w_in, v_ffn_w_out, v_w_in, v_w_out, v_a_conv_w, v_a_conv_b, v_a_dt_bias, v_a_log, v_a_d, v_a_norm_w, v_s5_lam_re, v_s5_lam_im, v_s5_log_step, v_s5_b_re, v_s5_b_im, v_s5_c_re, v_s5_c_im, v_s5_d, v_s5_glu_w, v_s5_glu_b, v_hg_lb_logits, v_hg_norm_w, v_final_norm_w):
    given = dict(locals())
    w = {n: given[n] for n in WEIGHTS}
    m = {n: given["m_" + n] for n in WEIGHTS}
    v = {n: given["v_" + n] for n in WEIGHTS}
    small_sharded = SHARDED[len(BIG):]
    replicated = [n for n in WEIGHTS if n not in [s for s, _ in SHARDED]]
    core = lax.axis_index("c").astype(jnp.int32).reshape(1)
    chip = (2 * lax.axis_index("x") + lax.axis_index("y")).astype(jnp.int32)

    def leaves(t):
        return list(t.reshape((-1,) + t.shape[-2:]))

    def nest(kind, flat):
        flat = list(flat)
        return [flat[2 * l:2 * l + 2] for l in range(len(flat) // 2)] if w[kind].ndim == 4 else flat

    counts = [len(leaves(w[k])) for k in BIG]
    gathered = gather_pieces([a.astype(BF16) for k in BIG for a in leaves(w[k])], "gather_big")
    big, off = {}, 0
    for k, cnt in zip(BIG, counts):
        big[k], off = nest(k, gathered[off:off + cnt]), off + cnt
    small = {n: w[n] for n in replicated}
    small.update(_gather_full(w, small_sharded, F32, 8, "gather_small"))

    loss, grad_x, grads = local_step(x, c, ctx, loss_target, big, small)

    g_big = [g for k in BIG for g in jax.tree.leaves(grads[k])]
    got = swap_with_sibling(g_big, "rs_sibling", send_other_half_of_axis=1)
    pairs = [add_pair(core, g, t, f"rs_add_pair{i}") for i, (g, t) in enumerate(zip(g_big, got))]
    from_chips = pieces_to_chips(pairs, "rs_chips")
    mine = [sum_pieces(chip.reshape(1), p, t, f"rs_sum_chips{i}") for i, (p, t) in enumerate(zip(pairs, from_chips))]
    other = swap_with_sibling(mine, "rs_halves")
    out, off = {}, 0
    for k, cnt in zip(BIG, counts):
        stack = lambda t: t.reshape((cnt,) + t.shape[-2:])
        res = adamw_halves(core, mine[off:off + cnt], other[off:off + cnt], stack(w[k]), stack(m[k]), stack(v[k]),
                           "adamw_" + k)
        off += cnt
        for kind, a in zip(("grad", "delta", "new_m", "new_v"), res):
            out[kind, k] = a.reshape(w[k].shape)

    names = replicated + [n for n, _ in small_sharded]
    part = _pack([grads[n] for n in names] + [loss.reshape(1)], 8, F32)
    g_all = _unpack(sum_parts(all_gather_devices(part, "gather_parts"), "sum_parts"),
                    [grads[n].shape for n in names] + [(1,)])
    g_of = dict(zip(names + ["loss"], g_all))
    for n, ax in small_sharded:
        g_of[n] = lax.dynamic_slice_in_dim(g_of[n], chip * w[n].shape[ax], w[n].shape[ax], axis=ax)
    rest = lambda d: _pack([d[n] for n in names], 8, F32)
    g_rest = rest(g_of)
    d_rest, m_rest, v_rest = adamw(g_rest, rest(w), rest(m), rest(v), "adamw_rest")
    shapes = [w[n].shape for n in names]
    for kind, buf in (("grad", g_rest), ("delta", d_rest), ("new_m", m_rest), ("new_v", v_rest)):
        for n, a in zip(names, _unpack(buf, shapes)):
            out[kind, n] = a

    loss_total = g_of["loss"].reshape(())
    return (loss_total, grad_x, *[out["grad", n] for n in WEIGHTS], *[out["delta", n] for n in WEIGHTS],
            *[out["new_m", n] for n in WEIGHTS], *[out["new_v", n] for n in WEIGHTS])
```

```python
import functools
import math

import numpy as np
import jax
import jax.numpy as jnp
from jax import lax
from jax.experimental import pallas as pl
from jax.experimental.pallas import tpu as pltpu

F32, BF16 = jnp.float32, jnp.bfloat16
EPS = 1e-6
N_MOD = 9
D_FF = 2816
A_INNER, A_HEADS, A_GROUPS, A_STATE, A_CONV, A_CONV_DIM = 512, 8, 2, 64, 5, 768
A_COLS = A_INNER + A_CONV_DIM + 2 * A_HEADS
B_WIDTH, B_GROUP, B_NGROUPS, B_STATE = 256, 16, 16, 64
C_WIDTH, C_HEADS, C_KEY = 256, 4, 64
GRID_W = 64
IN_COLS = A_COLS + B_WIDTH + 5 * C_WIDTH
IN_COLS_PAD = 3072
ADAM_LR, ADAM_B1, ADAM_B2, ADAM_EPS, ADAM_WD, ADAM_STEP = 0.001, 0.9, 0.999, 1e-08, 0.01, 10

ROW_TILE = 256
SSD_Q, HG_Q = 128, 64
REC_T = 128
SCAN_SAMPLES = 2
HG_BAND = 8
VMEM_LIMIT = 56 * 1024 * 1024
MM_VMEM_BUDGET = 36 * 1024 * 1024

NT = ((1,), (1,))
TN = ((0,), (0,))


def _bdot(a, b, dims=((1,), (0,))):
    return lax.dot_general(a.astype(BF16), b.astype(BF16), (dims, ((), ())), preferred_element_type=F32)


def _split3(x):
    hi = x.astype(BF16)
    r = x - hi.astype(F32)
    mid = r.astype(BF16)
    lo = (r - mid.astype(F32)).astype(BF16)
    return hi, mid, lo


def _mask_lhs(m, x, dims):
    return sum(lax.dot_general(m, p, (dims, ((), ())), preferred_element_type=F32) for p in _split3(x))


def _mask_rhs(x, m, dims):
    return sum(lax.dot_general(p, m, (dims, ((), ())), preferred_element_type=F32) for p in _split3(x))


@jax.custom_vjp
def mdot(m, x):
    return _mask_lhs(m, x, ((1,), (0,)))


def _mdot_fwd(m, x):
    return mdot(m, x), m


def _mdot_bwd(m, g):
    return jnp.zeros_like(m), _mask_lhs(m, g, TN)


mdot.defvjp(_mdot_fwd, _mdot_bwd)


@jax.custom_vjp
def xdot(x, m):
    return _mask_rhs(x, m, ((1,), (0,)))


def _xdot_fwd(x, m):
    return xdot(x, m), m


def _xdot_bwd(m, g):
    return _mask_rhs(g, m, NT), jnp.zeros_like(m)


xdot.defvjp(_xdot_fwd, _xdot_bwd)


@functools.partial(jax.custom_vjp, nondiff_argnums=(1,))
def _roll(x, s):
    return pltpu.roll(x, s, 0)


def _roll_fwd(x, s):
    return _roll(x, s), None


def _roll_bwd(s, _, g):
    return (_roll(g, x_rows(g) - s),)


def x_rows(x):
    return x.shape[0]


_roll.defvjp(_roll_fwd, _roll_bwd)


def sroll(x, s):
    s = s % x.shape[0]
    return x if s == 0 else _roll(x, s)


def _softplus(x):
    return jnp.maximum(x, 0.0) + jnp.log1p(jnp.exp(jnp.minimum(x, -x)))


def _sigmoid(x):
    return 0.5 * (jnp.tanh(0.5 * x) + 1.0)


def _silu(x):
    return x * _sigmoid(x)


def _rms(x):
    return x * lax.rsqrt(jnp.mean(x * x, axis=-1, keepdims=True) + EPS)


def _pick(n, cands):
    for c in cands:
        if n % c == 0:
            return c
    return n


def _cp(*sem):
    return pltpu.CompilerParams(dimension_semantics=sem, vmem_limit_bytes=VMEM_LIMIT)


def _tile(n, unit, cap):
    best = None
    for d in range(unit, min(n, cap) + 1, unit):
        if n % d == 0:
            best = d
    return best if best is not None else n


def _mm(a, b, ta, tb, out_dtype, name, b_pieces=False, out_pieces=False):
    M, K = (a.shape[1], a.shape[0]) if ta else a.shape
    n_div = k_div = None
    if b_pieces and tb:
        N, k_div = b.shape[1], b.shape[2]
    elif b_pieces:
        N, n_div = N_CHIPS * b.shape[2], b.shape[2]
    else:
        N = b.shape[0] if tb else b.shape[1]
    if out_pieces:
        n_div = N // N_CHIPS
    n_div, k_div = n_div or N, k_div or K
    tm = _tile(M, 128 if ta else 8, 1536 if ta else 1024)
    tn = _tile(n_div, 128, 1536)
    a_bytes, b_bytes, o_bytes = a.dtype.itemsize, b.dtype.itemsize, jnp.dtype(out_dtype).itemsize
    cands = {d for d in range(128, min(k_div, 2816) + 1, 128) if k_div % d == 0}
    if k_div <= 2816 or not cands:
        cands.add(k_div)
    for tk in sorted(cands, reverse=True):
        scratch = tm * tn * 4 if (K // tk > 1 and o_bytes != 4) else 0
        if 2 * (tm * tk * a_bytes + tk * tn * b_bytes + tm * tn * o_bytes) + scratch <= MM_VMEM_BUDGET:
            break
    nk = K // tk
    nq_n, nq_k = n_div // tn, k_div // tk
    dims = ((0 if ta else 1,), (1 if tb else 0,))

    def body(a_ref, b_ref, o_ref, *acc):
        part = _bdot(a_ref[...], b_ref[...], dims)
        if nk == 1:
            o_ref[...] = part.astype(o_ref.dtype)
            return
        acc_ref = acc[0] if acc else o_ref
        k = pl.program_id(2)

        @pl.when(k == 0)
        def _():
            acc_ref[...] = part

        @pl.when(k > 0)
        def _():
            acc_ref[...] += part

        if acc:
            @pl.when(k == nk - 1)
            def _():
                o_ref[...] = acc_ref[...].astype(o_ref.dtype)

    a_spec = pl.BlockSpec((tk, tm), lambda i, j, k: (k, i)) if ta else pl.BlockSpec((tm, tk), lambda i, j, k: (i, k))
    if b_pieces and tb:
        b_spec = pl.BlockSpec((None, tn, tk), lambda i, j, k: (k // nq_k, j, k % nq_k))
    elif b_pieces:
        b_spec = pl.BlockSpec((None, tk, tn), lambda i, j, k: (j // nq_n, k, j % nq_n))
    elif tb:
        b_spec = pl.BlockSpec((tn, tk), lambda i, j, k: (j, k))
    else:
        b_spec = pl.BlockSpec((tk, tn), lambda i, j, k: (k, j))
    if out_pieces:
        o_spec = pl.BlockSpec((None, tm, tn), lambda i, j, k: (j // nq_n, i, j % nq_n))
        o_shape = jax.ShapeDtypeStruct((N_CHIPS, M, N // N_CHIPS), out_dtype)
    else:
        o_spec = pl.BlockSpec((tm, tn), lambda i, j, k: (i, j))
        o_shape = jax.ShapeDtypeStruct((M, N), out_dtype)
    return pl.pallas_call(
        body, grid=(M // tm, N // tn, nk), in_specs=[a_spec, b_spec], out_specs=o_spec, out_shape=o_shape,
        scratch_shapes=[pltpu.VMEM((tm, tn), F32)] if (nk > 1 and o_bytes != 4) else [], name=name,
        compiler_params=_cp("parallel", "parallel", "arbitrary"))(a, b)


def linear(x, w16, wslot, name, out_dtype=F32, pieces=False):
    @jax.custom_vjp
    def f(x, w16, wslot):
        return _mm(x, w16, False, False, out_dtype, name + "_fwd", b_pieces=pieces)

    def fwd(x, w16, wslot):
        return f(x, w16, wslot), (x, w16)

    grad_dtype = wslot.dtype

    def bwd(res, dy):
        x, w16 = res
        dx = _mm(dy, w16, False, True, x.dtype, name + "_dx", b_pieces=pieces)
        dw = _mm(x, dy, True, False, grad_dtype, name + "_dw", out_pieces=pieces)
        return dx, jnp.zeros_like(w16), dw

    f.defvjp(fwd, bwd)
    return f(x, w16, wslot)


def rowwise(f, name, rows, mods, params, outs, n_ctx_blocks, tl=ROW_TILE, row_diff=None):
    rows, mods, params = tuple(rows), tuple(mods), tuple(params)
    nr, nm, npar, no = len(rows), len(mods), len(params), len(outs)
    B, L = rows[0].shape[:2]
    nblk = L // tl
    row_diff = tuple(row_diff) if row_diff is not None else (True,) * nr
    out_dtypes = [dt for _, dt in outs]

    def fcast(r, m, p):
        return tuple(o.astype(dt) for o, dt in zip(f(r, m, p), out_dtypes))

    def seg(i):
        return (i >= n_ctx_blocks).astype(jnp.int32) if n_ctx_blocks else 0

    def specs():
        row_specs = [pl.BlockSpec((1, tl, r.shape[2]), lambda b, i: (b, i, 0)) for r in rows]
        mod_specs = [pl.BlockSpec((1, 1, 1, m.shape[3]), lambda b, i: (b, seg(i), 0, 0)) for m in mods]
        par_specs = [pl.BlockSpec(p.shape, lambda b, i: (0, 0)) for p in params]
        out_specs = [pl.BlockSpec((1, tl, w), lambda b, i: (b, i, 0)) for w, _ in outs]
        return row_specs, mod_specs, par_specs, out_specs

    def load(ins):
        r = tuple(x[0] for x in ins[:nr])
        m = tuple(x[0, 0] for x in ins[nr:nr + nm])
        p = tuple(x[...] for x in ins[nr + nm:])
        return r, m, p

    def fwd_call(rows, mods, params):
        def body(*refs):
            r, m, p = load(refs[:nr + nm + npar])
            for o_ref, o in zip(refs[nr + nm + npar:], fcast(r, m, p)):
                o_ref[0] = o

        rs, ms, ps, os_ = specs()
        return pl.pallas_call(
            body, grid=(B, nblk), in_specs=rs + ms + ps, out_specs=os_,
            out_shape=[jax.ShapeDtypeStruct((B, L, w), dt) for w, dt in outs],
            name=name + "_fwd", compiler_params=_cp("parallel", "parallel"))(*rows, *mods, *params)

    def bwd_call(rows, mods, params, douts):
        didx = [j for j in range(nr) if row_diff[j]]

        def body(*refs):
            n_in = nr + nm + npar
            r, m, p = load(refs[:n_in])
            dins = refs[n_in:n_in + no]
            rest = refs[n_in + no:]
            dr_refs, dm_refs, dp_refs = rest[:len(didx)], rest[len(didx):len(didx) + nm], rest[len(didx) + nm:]
            _, vjp = jax.vjp(fcast, r, m, p)
            dr, dm, dp = vjp(tuple(d[0] for d in dins))
            for ref, j in zip(dr_refs, didx):
                ref[0] = dr[j].astype(ref.dtype)
            b, i = pl.program_id(0), pl.program_id(1)
            first_m = (i == 0) | (i == n_ctx_blocks) if n_ctx_blocks else (i == 0)
            first_p = (b == 0) & (i == 0)
            for ref, g in zip(dm_refs, dm):
                @pl.when(first_m)
                def _(ref=ref, g=g):
                    ref[0, 0] = g

                @pl.when(jnp.logical_not(first_m))
                def _(ref=ref, g=g):
                    ref[0, 0] += g
            for ref, g in zip(dp_refs, dp):
                @pl.when(first_p)
                def _(ref=ref, g=g):
                    ref[...] = g

                @pl.when(jnp.logical_not(first_p))
                def _(ref=ref, g=g):
                    ref[...] += g

        rs, ms, ps, os_ = specs()
        out_shape = ([jax.ShapeDtypeStruct(rows[j].shape, rows[j].dtype) for j in didx]
                     + [jax.ShapeDtypeStruct(m.shape, F32) for m in mods]
                     + [jax.ShapeDtypeStruct(p.shape, F32) for p in params])
        res = pl.pallas_call(
            body, grid=(B, nblk), in_specs=rs + ms + ps + os_,
            out_specs=[rs[j] for j in didx] + ms + ps, out_shape=out_shape,
            name=name + "_bwd", compiler_params=_cp("arbitrary", "arbitrary"))(*rows, *mods, *params, *douts)
        dr = [None] * nr
        for j, g in zip(didx, res[:len(didx)]):
            dr[j] = g
        dr = tuple(g if g is not None else jnp.zeros_like(rows[j]) for j, g in enumerate(dr))
        return dr, tuple(res[len(didx):len(didx) + nm]), tuple(res[len(didx) + nm:])

    @jax.custom_vjp
    def op(rows, mods, params):
        return tuple(fwd_call(rows, mods, params))

    def op_fwd(rows, mods, params):
        return op(rows, mods, params), (rows, mods, params)

    def op_bwd(res, douts):
        return bwd_call(*res, tuple(douts))

    op.defvjp(op_fwd, op_bwd)
    return op(rows, mods, params)


def chunk_scan(step, name, rev, seqs, seq_t, params, consts, state_shapes, out_widths, Q, n_ctx):
    seqs, params, consts = tuple(seqs), tuple(params), tuple(consts)
    ns, npar, nc, nst, no = len(seqs), len(params), len(consts), len(state_shapes), len(out_widths)
    B = seqs[0].shape[0]
    L = seqs[0].shape[2] if seq_t[0] else seqs[0].shape[1]
    nck, ncc = L // Q, n_ctx // Q
    nb = SCAN_SAMPLES if B % SCAN_SAMPLES == 0 else 1

    def chunk_of(k):
        if not rev:
            return k
        return jnp.where(k < ncc, ncc - 1 - k, nck + ncc - 1 - k)

    def specs(order):
        def seq_spec(s, t):
            if t:
                return pl.BlockSpec((nb, s.shape[1], Q), lambda b, k: (b, 0, chunk_of(order(k))))
            return pl.BlockSpec((nb, Q, s.shape[2]), lambda b, k: (b, chunk_of(order(k)), 0))

        seq_specs = [seq_spec(s, t) for s, t in zip(seqs, seq_t)]
        par_specs = [pl.BlockSpec(p.shape, lambda b, k: (0, 0)) for p in params]
        con_specs = [pl.BlockSpec(c.shape, lambda b, k, nd=c.ndim: (0,) * nd) for c in consts]
        out_specs = [pl.BlockSpec((nb, Q, w), lambda b, k: (b, chunk_of(order(k)), 0)) for w in out_widths]
        sav_specs = [pl.BlockSpec((nb, 1) + tuple(s), lambda b, k: (b, order(k), 0, 0)) for s in state_shapes]
        return seq_specs, par_specs, con_specs, out_specs, sav_specs

    def fwd_call(seqs, params):
        def body(*refs):
            seq_refs = refs[:ns]
            par_refs = refs[ns:ns + npar]
            con_refs = refs[ns + npar:ns + npar + nc]
            rest = refs[ns + npar + nc:]
            out_refs, sav_refs, st_refs = rest[:no], rest[no:no + nst], rest[no + nst:]

            @pl.when(pl.program_id(1) == 0)
            def _():
                for s in st_refs:
                    s[...] = jnp.zeros_like(s)

            pvals, cvals = tuple(p[...] for p in par_refs), tuple(c[...] for c in con_refs)
            for i in range(nb):
                s_in = tuple(s[i] for s in st_refs)
                for sv, s in zip(sav_refs, s_in):
                    sv[i, 0] = s
                s_new, outs = step(s_in, tuple(r[i] for r in seq_refs), pvals, cvals)
                for s_ref, s in zip(st_refs, s_new):
                    s_ref[i] = s
                for o_ref, o in zip(out_refs, outs):
                    o_ref[i] = o

        ss, ps, cs, os_, vs = specs(lambda k: k)
        res = pl.pallas_call(
            body, grid=(B // nb, nck), in_specs=ss + ps + cs, out_specs=os_ + vs,
            out_shape=[jax.ShapeDtypeStruct((B, L, w), F32) for w in out_widths]
            + [jax.ShapeDtypeStruct((B, nck) + tuple(s), F32) for s in state_shapes],
            scratch_shapes=[pltpu.VMEM((nb,) + tuple(s), F32) for s in state_shapes],
            name=name + "_fwd", compiler_params=_cp("parallel", "arbitrary"))(*seqs, *params, *consts)
        return tuple(res[:no]), tuple(res[no:])

    def bwd_call(seqs, params, saved, douts):
        def body(*refs):
            seq_refs = refs[:ns]
            par_refs = refs[ns:ns + npar]
            con_refs = refs[ns + npar:ns + npar + nc]
            rest = refs[ns + npar + nc:]
            sav_refs, dout_refs = rest[:nst], rest[nst:nst + no]
            rest = rest[nst + no:]
            dseq_refs, dpar_refs, dst_refs = rest[:ns], rest[ns:ns + npar], rest[ns + npar:]
            b, k = pl.program_id(0), pl.program_id(1)

            @pl.when(k == 0)
            def _():
                for s in dst_refs:
                    s[...] = jnp.zeros_like(s)

            cvals, pvals = tuple(c[...] for c in con_refs), tuple(p[...] for p in par_refs)
            dp = None
            for i in range(nb):
                _, vjp = jax.vjp(lambda s, x, p: step(s, x, p, cvals), tuple(s[i, 0] for s in sav_refs),
                                 tuple(r[i] for r in seq_refs), pvals)
                ds, dx, dp_i = vjp((tuple(s[i] for s in dst_refs), tuple(d[i] for d in dout_refs)))
                for s_ref, s in zip(dst_refs, ds):
                    s_ref[i] = s
                for x_ref, x in zip(dseq_refs, dx):
                    x_ref[i] = x
                dp = dp_i if dp is None else tuple(a + g for a, g in zip(dp, dp_i))
            first = (b == 0) & (k == 0)
            for ref, g in zip(dpar_refs, dp):
                @pl.when(first)
                def _(ref=ref, g=g):
                    ref[...] = g

                @pl.when(jnp.logical_not(first))
                def _(ref=ref, g=g):
                    ref[...] += g

        ss, ps, cs, os_, vs = specs(lambda k: nck - 1 - k)
        res = pl.pallas_call(
            body, grid=(B // nb, nck), in_specs=ss + ps + cs + vs + os_, out_specs=ss + ps,
            out_shape=[jax.ShapeDtypeStruct(s.shape, F32) for s in seqs]
            + [jax.ShapeDtypeStruct(p.shape, F32) for p in params],
            scratch_shapes=[pltpu.VMEM((nb,) + tuple(s), F32) for s in state_shapes],
            name=name + "_bwd", compiler_params=_cp("arbitrary", "arbitrary"))(*seqs, *params, *consts, *saved, *douts)
        return tuple(res[:ns]), tuple(res[ns:])

    @jax.custom_vjp
    def op(seqs, params):
        return fwd_call(seqs, params)[0]

    def op_fwd(seqs, params):
        outs, saved = fwd_call(seqs, params)
        return outs, (seqs, params, saved)

    def op_bwd(res, douts):
        seqs, params, saved = res
        return bwd_call(seqs, params, saved, tuple(douts))

    op.defvjp(op_fwd, op_bwd)
    return op(seqs, params)


def _positions(Q, rev):
    t = np.arange(Q)
    return (Q - 1 - t) if rev else t


def _ssd_consts(rev):
    Q = SSD_Q
    pos = _positions(Q, rev)
    tri = pos[:, None] >= pos[None, :]
    head_of = np.arange(A_INNER) // (A_INNER // A_HEADS)
    expand = np.arange(A_HEADS)[:, None] == head_of[None, :]
    group_of_row = np.arange(A_GROUPS * A_STATE) // A_STATE
    block = group_of_row[:, None] == (head_of // (A_HEADS // A_GROUPS))[None, :]
    as_bf = lambda m: jnp.asarray(m.astype(np.float32), BF16)
    return (as_bf(tri), as_bf(tri.T), jnp.asarray(tri.astype(np.float32)), as_bf(expand),
            jnp.asarray(block.astype(np.float32)), jnp.asarray(expand.astype(np.float32)[:, None, :]))


def _ssd_step(state, seqs, params, consts):
    (st,) = state
    xs, bm, cm, dtr, dtr_t = seqs
    bias, bias_t, alog, alog_t = params
    tri_b, tri_t_b, tri_f, expand, block, head_lanes = consts
    dt = _softplus(dtr + bias)
    dta = dt * (-jnp.exp(alog))
    dt_t = _softplus(dtr_t + bias_t)
    dta_t = dt_t * (-jnp.exp(alog_t))
    cum = mdot(tri_b, dta)
    cum_t = xdot(dta_t, tri_t_b)
    total = jnp.sum(dta, axis=0, keepdims=True)
    w_end = jnp.exp(total - cum) * dt
    ecum_x = xdot(jnp.exp(cum), expand)
    wend_x = xdot(w_end, expand)
    dec_x = jnp.exp(jnp.sum(xdot(dta, expand), axis=0, keepdims=True))
    y = _bdot(cm, st) * ecum_x
    st_new = st * dec_x + block * _bdot(bm, xs * wend_x, TN)
    lane_g = lax.broadcasted_iota(jnp.int32, (1, A_GROUPS * A_STATE), 1) // A_STATE
    sub_h = lax.broadcasted_iota(jnp.int32, (A_HEADS, 1), 0)
    lane8 = lax.broadcasted_iota(jnp.int32, (1, A_HEADS), 1)
    Q = xs.shape[0]
    scores = []
    for g in range(A_GROUPS):
        cb = _bdot(jnp.where(lane_g == g, cm, 0.0), bm, NT)
        for h in range(g * (A_HEADS // A_GROUPS), (g + 1) * (A_HEADS // A_GROUPS)):
            col = jnp.sum(jnp.where(lane8 == h, cum, 0.0), axis=1, keepdims=True)
            row = jnp.sum(jnp.where(sub_h == h, cum_t, 0.0), axis=0, keepdims=True)
            dt_row = jnp.sum(jnp.where(sub_h == h, dt_t, 0.0), axis=0, keepdims=True)
            scores.append(cb * (tri_f * jnp.exp(tri_f * (col - row))) * dt_row)
    y_all = _bdot(jnp.stack(scores).reshape(A_HEADS * Q, Q), xs).reshape(A_HEADS, Q, A_INNER)
    y = y + jnp.sum(y_all * head_lanes, axis=0)
    return (st_new,), (y,)


def ssd_scan(name, rev, xs, bm, cm, dtr, dt_bias, a_log, n_ctx):
    seqs = (xs, bm, cm, dtr, jnp.swapaxes(dtr, 1, 2))
    params = (dt_bias.reshape(1, -1), dt_bias.reshape(-1, 1), a_log.reshape(1, -1), a_log.reshape(-1, 1))
    (y,) = chunk_scan(_ssd_step, name, rev, seqs, (False, False, False, False, True), params, _ssd_consts(rev),
                      [(A_GROUPS * A_STATE, A_INNER)], [A_INNER], SSD_Q, n_ctx)
    return y


def _recurrence(v, a, order, n_ctx, name, x_fwd=None):
    B, L = v.shape[:2]
    T = REC_T
    nck, ncc = L // T, n_ctx // T
    nlat = nck - ncc
    chunk_of = {"F": lambda k: k,
                "R": lambda k: jnp.where(k < ncc, ncc - 1 - k, nck + ncc - 1 - k),
                "FT": lambda k: nck - 1 - k,
                "RT": lambda k: jnp.where(k < nlat, ncc + k, k - nlat)}[order]
    descending = order in ("R", "FT")
    with_da = x_fwd is not None
    H = 8

    def body(*refs):
        if with_da:
            v_ref, a_ref, xf_ref, x_ref, da_ref, st_ref = refs
        else:
            v_ref, a_ref, x_ref, st_ref = refs
        k = pl.program_id(0)

        @pl.when(k == 0)
        def _():
            st_ref[...] = jnp.zeros_like(st_ref)
            if with_da:
                da_ref[...] = jnp.zeros_like(da_ref)

        ar, ai = a_ref[pl.ds(0, H), :], a_ref[pl.ds(H, H), :]
        zero = jnp.zeros((H, 128), F32)

        def step(i, carry):
            chains, acc_r, acc_i = carry
            t = (T - 1 - i) if descending else i
            out = []
            for b, (xr, xi) in enumerate(chains):
                if with_da:
                    fr, fi = xf_ref[b, t, pl.ds(0, H), :], xf_ref[b, t, pl.ds(H, H), :]
                    acc_r = acc_r + xr * fr + xi * fi
                    acc_i = acc_i + xi * fr - xr * fi
                nr = ar * xr - ai * xi + v_ref[b, t, pl.ds(0, H), :]
                ni = ar * xi + ai * xr + v_ref[b, t, pl.ds(H, H), :]
                x_ref[b, t, pl.ds(0, H), :] = nr
                x_ref[b, t, pl.ds(H, H), :] = ni
                out.append((nr, ni))
            return tuple(out), acc_r, acc_i

        init = tuple((st_ref[b, pl.ds(0, H), :], st_ref[b, pl.ds(H, H), :]) for b in range(B))
        chains, acc_r, acc_i = lax.fori_loop(0, T, step, (init, zero, zero), unroll=8)
        for b, (xr, xi) in enumerate(chains):
            st_ref[b, pl.ds(0, H), :] = xr
            st_ref[b, pl.ds(H, H), :] = xi
        if with_da:
            da_ref[pl.ds(0, H), :] += acc_r
            da_ref[pl.ds(H, H), :] += acc_i

    seq = pl.BlockSpec((B, T, 2 * H, 128), lambda k: (0, chunk_of(k), 0, 0))
    par = pl.BlockSpec((2 * H, 128), lambda k: (0, 0))
    x_shape = jax.ShapeDtypeStruct(v.shape, F32)
    if with_da:
        return pl.pallas_call(
            body, grid=(nck,), in_specs=[seq, par, seq], out_specs=[seq, par],
            out_shape=[x_shape, jax.ShapeDtypeStruct((2 * H, 128), F32)],
            scratch_shapes=[pltpu.VMEM((B, 2 * H, 128), F32)], name=name, compiler_params=_cp("arbitrary"))(v, a, x_fwd)
    return pl.pallas_call(
        body, grid=(nck,), in_specs=[seq, par], out_specs=seq, out_shape=x_shape,
        scratch_shapes=[pltpu.VMEM((B, 2 * H, 128), F32)], name=name, compiler_params=_cp("arbitrary"))(v, a)


def lin_rec(v, a, rev, n_ctx, name):
    fwd_order, bwd_order = ("R", "RT") if rev else ("F", "FT")
    conj = jnp.concatenate([jnp.ones((8, 128), F32), -jnp.ones((8, 128), F32)], axis=0)

    @jax.custom_vjp
    def f(v, a):
        return _recurrence(v, a, fwd_order, n_ctx, name + "_fwd")

    def fwd(v, a):
        x = f(v, a)
        return x, (x, a)

    def bwd(res, dx):
        x, a = res
        g, da = _recurrence(dx, a * conj, bwd_order, n_ctx, name + "_bwd", x_fwd=x)
        return g, da

    f.defvjp(fwd, bwd)
    return f(v, a)


def _block_diag(t):
    G, a, b = t.shape
    eye = jnp.eye(G, dtype=t.dtype)
    return (t[:, :, None, :] * eye[:, None, :, None]).reshape(G * a, G * b)


def s5_scan(name, rev, u, lam_re, lam_im, log_step, b_re, b_im, c_re, c_im, n_ctx):
    step = jnp.exp(log_step)[:, None]
    mag = jnp.exp(lam_re * step)
    ar = mag * jnp.cos(lam_im * step)
    ai = mag * jnp.sin(lam_im * step)
    den = lam_re * lam_re + lam_im * lam_im
    nr = ar - 1.0
    kr = (nr * lam_re + ai * lam_im) / den
    ki = (ai * lam_re - nr * lam_im) / den
    br = kr[..., None] * b_re - ki[..., None] * b_im
    bi = kr[..., None] * b_im + ki[..., None] * b_re
    B, L, _ = u.shape
    width = B_NGROUPS * B_STATE
    w_b = jnp.concatenate([_block_diag(jnp.swapaxes(br, 1, 2)), _block_diag(jnp.swapaxes(bi, 1, 2))], axis=1)
    w_c = jnp.concatenate([_block_diag(jnp.swapaxes(c_re, 1, 2)), -_block_diag(jnp.swapaxes(c_im, 1, 2))], axis=0)
    a = jnp.concatenate([ar.reshape(8, 128), ai.reshape(8, 128)], axis=0)
    v = linear(u.reshape(B * L, B_WIDTH), w_b.astype(BF16), w_b, name + "_in")
    x = lin_rec(v.reshape(B, L, 16, 128), a, rev, n_ctx, name)
    y = linear(x.reshape(B * L, 2 * width), w_c.astype(BF16), w_c, name + "_out")
    return y.reshape(B, L, B_WIDTH)


def _hg_consts(rev):
    Q = HG_Q
    pos = _positions(Q, rev)
    pi, pj = pos[:, None], pos[None, :]
    mats = [pi >= pj, pj > pi]
    pairs = []
    s = HG_BAND
    while s < Q:
        second = (pos // s) % 2 == 1
        mid = (pos // (2 * s)) * 2 * s + s
        mats.append(second[:, None] & (pj >= mid[:, None]) & (pj <= pi))
        mats.append((~second)[:, None] & (pj > pi) & (pj < mid[:, None]))
        pairs.append(second[:, None] & (~second)[None, :] & ((pi // (2 * s)) == (pj // (2 * s))))
        s *= 2
    rows = [((pos // s_) % 2 == 1) for s_ in [HG_BAND * 2 ** n for n in range(len(pairs))]]
    band = [(pos % HG_BAND) >= d for d in range(HG_BAND)]
    mats_b = jnp.asarray(np.stack(mats).astype(np.float32), BF16)
    pairs_f = jnp.asarray(np.tile(np.stack(pairs).astype(np.float32), (1, C_HEADS, 1)))
    rows_f = jnp.asarray(np.stack(rows).astype(np.float32)[:, :, None])
    band_f = jnp.asarray(np.stack(band).astype(np.float32)[:, :, None])
    head_lanes = np.arange(C_HEADS)[:, None] == (np.arange(C_WIDTH) // C_KEY)[None, :]
    return mats_b, pairs_f, rows_f, band_f, jnp.asarray(head_lanes.astype(np.float32)[:, None, :])


def _hg_step_fn(rev, n_levels):
    def step(state, seqs, params, consts):
        (st,) = state
        qr, zf, v = seqs
        (lb,) = params
        mats, pairs, rows, band, head_lanes = consts
        W = C_WIDTH
        Q = qr.shape[0]
        same_head = (lax.broadcasted_iota(jnp.int32, (W, W), 0) // C_KEY
                     == lax.broadcasted_iota(jnp.int32, (W, W), 1) // C_KEY)
        q = _silu(qr)
        f = lb + (1.0 - lb) * jax.nn.sigmoid(zf)
        logf = jnp.log(f)
        kk = 1.0 - f
        qd = q * jnp.exp(mdot(mats[0], logf))
        w = kk * jnp.exp(mdot(mats[1], logf))
        total = jnp.sum(logf, axis=0, keepdims=True)
        o = _bdot(qd, st, NT)
        st_new = st * jnp.exp(total) + jnp.where(same_head, _bdot(v, w, TN), 0.0)
        scores = None
        for n in range(n_levels):
            a = q * jnp.exp(mdot(mats[2 + 2 * n], logf)) * rows[n]
            bk = kk * jnp.exp(mdot(mats[3 + 2 * n], logf)) * (1.0 - rows[n])
            sc = pairs[n] * _bdot((a[None] * head_lanes).reshape(C_HEADS * Q, W), bk, NT)
            scores = sc if scores is None else scores + sc
        o = o + jnp.sum(_bdot(scores, v).reshape(C_HEADS, Q, W) * head_lanes, axis=0)
        same_head_b = same_head.astype(BF16)
        e = jnp.zeros_like(logf)
        for d in range(HG_BAND):
            s = -d if rev else d
            if d > 0:
                e = e + sroll(logf, s + (1 if rev else -1))
            kd, vd = (kk, v) if d == 0 else (sroll(kk, s), sroll(v, s))
            p = q * kd * jnp.exp(e)
            o = o + band[d] * (_bdot(p, same_head_b) * vd)
        return (st_new,), (o,)

    return step


def hg_scan(name, rev, qr, zf, v, lower, n_ctx):
    consts = _hg_consts(rev)
    (o,) = chunk_scan(_hg_step_fn(rev, consts[1].shape[0]), name, rev, (qr, zf, v), (False,) * 3,
                      (lower.reshape(1, -1),), consts, [(C_WIDTH, C_WIDTH)], [C_WIDTH], HG_Q, n_ctx)
    return o


def conv_silu(x, w, b, n_ctx, name):
    B, L, C = x.shape
    TC = 128
    w8 = jnp.zeros((8, C), F32).at[:A_CONV].set(w)
    b2 = b.reshape(1, C)
    pad = A_CONV // 2

    def taps(v, sign):
        t = lax.broadcasted_iota(jnp.int32, v.shape, 0)
        out = []
        for k in range(A_CONV):
            s = sign * (k - pad)
            src = t + s
            ok = (src >= 0) & (src < L) & ((t >= n_ctx) == (src >= n_ctx))
            vs = v if s == 0 else pltpu.roll(v, (-s) % L, 0)
            out.append((k, jnp.where(ok, vs, 0.0)))
        return out

    def pre(x_ref, w_ref, b_ref):
        xv = x_ref[0]
        y = b_ref[...] + sum(xs * w_ref[pl.ds(k, 1), :] for k, xs in taps(xv, 1))
        return xv, y

    def fwd_body(x_ref, w_ref, b_ref, o_ref):
        _, y = pre(x_ref, w_ref, b_ref)
        o_ref[0] = _silu(y)

    def bwd_body(x_ref, w_ref, b_ref, g_ref, dx_ref, dw_ref, db_ref):
        xv, y = pre(x_ref, w_ref, b_ref)
        sg = _sigmoid(y)
        dy = g_ref[0] * (sg + y * sg * (1.0 - sg))
        dx_ref[0] = sum(ds * w_ref[pl.ds(k, 1), :] for k, ds in taps(dy, -1))
        first = pl.program_id(1) == 0

        @pl.when(first)
        def _():
            dw_ref[...] = jnp.zeros_like(dw_ref)
            db_ref[...] = jnp.zeros_like(db_ref)

        for k, xs in taps(xv, 1):
            dw_ref[pl.ds(k, 1), :] += jnp.sum(dy * xs, axis=0, keepdims=True)
        db_ref[...] += jnp.sum(dy, axis=0, keepdims=True)

    x_spec = pl.BlockSpec((1, L, TC), lambda j, bb: (bb, 0, j))
    w_spec = pl.BlockSpec((8, TC), lambda j, bb: (0, j))
    b_spec = pl.BlockSpec((1, TC), lambda j, bb: (0, j))

    @jax.custom_vjp
    def op(x, w8, b2):
        return pl.pallas_call(fwd_body, grid=(C // TC, B), in_specs=[x_spec, w_spec, b_spec], out_specs=x_spec,
                              out_shape=jax.ShapeDtypeStruct(x.shape, F32), name=name + "_fwd",
                              compiler_params=_cp("parallel", "parallel"))(x, w8, b2)

    def op_fwd(x, w8, b2):
        return op(x, w8, b2), (x, w8, b2)

    def op_bwd(res, g):
        x, w8, b2 = res
        return tuple(pl.pallas_call(
            bwd_body, grid=(C // TC, B), in_specs=[x_spec, w_spec, b_spec, x_spec],
            out_specs=[x_spec, w_spec, b_spec],
            out_shape=[jax.ShapeDtypeStruct(x.shape, F32), jax.ShapeDtypeStruct(w8.shape, F32),
                       jax.ShapeDtypeStruct(b2.shape, F32)],
            name=name + "_bwd", compiler_params=_cp("parallel", "arbitrary"))(x, w8, b2, g))

    op.defvjp(op_fwd, op_bwd)
    return op(x, w8, b2)


def loss_head(h, target, norm_w, n_ctx):
    B, L, Dm = h.shape
    S = target.shape[1]
    tl = ROW_TILE
    skip = n_ctx // tl

    def body(h_ref, t_ref, w_ref, loss_ref, dh_ref, dw_ref):
        b, i = pl.program_id(0), pl.program_id(1)
        fn = lambda hv, wv: _rms(hv) * wv
        y, vjp = jax.vjp(fn, h_ref[0], w_ref[...])
        err = y - t_ref[0]
        dhv, dwv = vjp(err * (1.0 / Dm))
        dh_ref[0] = dhv
        part = 0.5 * jnp.sum(jnp.mean(err * err, axis=-1, keepdims=True), axis=0, keepdims=True)
        first = (b == 0) & (i == 0)

        @pl.when(first)
        def _():
            loss_ref[...] = jnp.zeros_like(loss_ref)
            dw_ref[...] = jnp.zeros_like(dw_ref)

        loss_ref[...] += jnp.broadcast_to(part, loss_ref.shape)
        dw_ref[...] += dwv

    loss, dh, dw = pl.pallas_call(
        body, grid=(B, S // tl),
        in_specs=[pl.BlockSpec((1, tl, Dm), lambda b, i: (b, i + skip, 0)),
                  pl.BlockSpec((1, tl, Dm), lambda b, i: (b, i, 0)),
                  pl.BlockSpec((1, Dm), lambda b, i: (0, 0))],
        out_specs=[pl.BlockSpec((8, 128), lambda b, i: (0, 0)),
                   pl.BlockSpec((1, tl, Dm), lambda b, i: (b, i, 0)),
                   pl.BlockSpec((1, Dm), lambda b, i: (0, 0))],
        out_shape=[jax.ShapeDtypeStruct((8, 128), F32), jax.ShapeDtypeStruct((B, S, Dm), F32),
                   jax.ShapeDtypeStruct((1, Dm), F32)],
        name="loss_head", compiler_params=_cp("arbitrary", "arbitrary"))(h, target, norm_w.reshape(1, Dm))
    dh_full = jnp.concatenate([jnp.zeros((B, n_ctx, Dm), F32), dh], axis=1)
    return loss[0, 0], dh_full, dw.reshape(Dm)


def _modulate(h, shift, scale):
    return _rms(h) * (1.0 + scale) + shift


def _f_mod(r, m, p):
    return (_modulate(r[0], m[0], m[1]),)


def _f_resid_mod(coef):
    def f(r, m, p):
        h2 = r[0] + coef * m[0] * r[1]
        return h2, _modulate(h2, m[1], m[2])
    return f


def _f_resid(coef):
    def f(r, m, p):
        return (r[0] + coef * m[0] * r[1],)
    return f


def _f_swiglu(r, m, p):
    pre = r[0].astype(F32)
    return (_silu(pre[:, :D_FF]) * pre[:, D_FF:],)


def _f_ssd_post(r, m, p):
    y_f, y_b, xs, z = r
    d_skip, norm_w = p
    y = (y_f + y_b + d_skip * xs) * _silu(z)
    return (_rms(y) * norm_w,)


def _f_s5_post(r, m, p):
    y_f, y_b, u = r
    d_skip, glu_w, glu_b = p
    y = jax.nn.gelu(y_f + y_b + d_skip * u)
    return (y * _sigmoid(_bdot(y, glu_w) + glu_b),)


def _f_hg_post(r, m, p):
    o_f, o_b, g = r
    (norm_w,) = p
    W = C_WIDTH
    same_head = (lax.broadcasted_iota(jnp.int32, (W, W), 0) // C_KEY
                 == lax.broadcasted_iota(jnp.int32, (W, W), 1) // C_KEY).astype(BF16)
    o = o_f + o_b
    ms = xdot(o * o, same_head) * (1.0 / C_KEY)
    return (o * lax.rsqrt(ms + EPS) * norm_w * _silu(g),)


def _pad_w_in(w):
    dt0 = A_INNER + A_CONV_DIM
    zeros = jnp.zeros((w.shape[0], IN_COLS_PAD - IN_COLS), w.dtype)
    return jnp.concatenate([w[:, :dt0], w[:, A_COLS:], w[:, dt0:A_COLS], zeros], axis=1)


def _to_columns(t, n_ctx):
    B, L, W = t.shape
    rows = (L - n_ctx) // GRID_W
    lat = t[:, n_ctx:].reshape(B, rows, GRID_W, W).transpose(0, 2, 1, 3).reshape(B, L - n_ctx, W)
    return jnp.concatenate([t[:, :n_ctx], lat], axis=1)


def _to_raster(t, n_ctx):
    B, L, W = t.shape
    rows = (L - n_ctx) // GRID_W
    lat = t[:, n_ctx:].reshape(B, GRID_W, rows, W).transpose(0, 2, 1, 3).reshape(B, L - n_ctx, W)
    return jnp.concatenate([t[:, :n_ctx], lat], axis=1)


def _forward(h0, c, big, P, n_ctx):
    B, L, Dm = h0.shape
    depth = len(big["mod_w"])
    ncb = n_ctx // ROW_TILE

    def rows_of(t):
        return t.reshape(t.shape[0] * t.shape[1], t.shape[2])

    def w_in_of(t):
        return _pad_w_in(jnp.moveaxis(t, 0, 1).reshape(t.shape[1], IN_COLS))

    p_lb = jax.nn.softmax(P["hg_lb_logits"], axis=0)
    lower_bounds = jnp.cumsum(p_lb, axis=0) - p_lb[:1]
    cc = jnp.concatenate([c, P["c_ctx"][None], jnp.zeros((8 - B - 1, Dm), F32)], axis=0)
    cc = _silu(cc)

    def mods_of(l):
        m = linear(cc, big["mod_w"][l], P["mod_w"][l], f"mod{l}", pieces=True) + P["mod_b"][l]
        m = m.reshape(8, N_MOD, Dm)
        seg = jnp.stack([jnp.broadcast_to(m[B], (B, N_MOD, Dm)), m[:B]], axis=1)
        return [seg[:, :, j:j + 1, :] for j in range(N_MOD)]

    def ffn(u, l, j):
        pre = linear(u.reshape(B * L, Dm), big["ffn_w_in"][l][j], P["ffn_w_in"][l][j], f"ffn_in{l}{j}", BF16,
                     pieces=True)
        (act,) = rowwise(_f_swiglu, f"swiglu{l}{j}", [pre.reshape(B, L, 2 * D_FF)], [], [], [(D_FF, BF16)], ncb)
        out = linear(act.reshape(B * L, D_FF), rows_of(big["ffn_w_out"][l][j]), rows_of(P["ffn_w_out"][l][j]),
                     f"ffn_out{l}{j}")
        return out.reshape(B, L, Dm)

    h = h0
    mods = mods_of(0)
    (u,) = rowwise(_f_mod, "mod_first", [h], [mods[0], mods[1]], [], [(Dm, BF16)], ncb)
    for l in range(depth):
        col_major = l % 2 == 1
        o = ffn(u, l, 0)
        h, u = rowwise(_f_resid_mod(0.5), f"resid_a{l}", [h, o], [mods[2], mods[3], mods[4]], [],
                       [(Dm, F32), (Dm, BF16)], ncb)
        if col_major:
            u = _to_columns(u, n_ctx)
        pre = linear(u.reshape(B * L, Dm), w_in_of(big["w_in"][l]), w_in_of(P["w_in"][l]), f"w_in{l}")
        pre = pre.reshape(B, L, IN_COLS_PAD)
        o0 = 0
        z, o0 = pre[..., o0:o0 + A_INNER], o0 + A_INNER
        xbc, o0 = pre[..., o0:o0 + A_CONV_DIM], o0 + A_CONV_DIM
        pb, o0 = pre[..., o0:o0 + B_WIDTH], o0 + B_WIDTH
        pc, o0 = pre[..., o0:o0 + 5 * C_WIDTH], o0 + 5 * C_WIDTH
        dtr = pre[..., o0:o0 + 2 * A_HEADS]
        xbc = conv_silu(xbc, P["a_conv_w"][l], P["a_conv_b"][l], n_ctx, f"conv{l}")
        xs, bm, cm = xbc[..., :A_INNER], xbc[..., A_INNER:A_INNER + 128], xbc[..., A_INNER + 128:]
        ya_dir = [ssd_scan(f"ssd{l}{d}", bool(d), xs, bm, cm, dtr[..., d * A_HEADS:(d + 1) * A_HEADS],
                           P["a_dt_bias"][l, d], P["a_log"][l, d], n_ctx) for d in range(2)]
        d_skip = jnp.repeat(P["a_d"][l], A_INNER // A_HEADS).reshape(1, A_INNER)
        (ya,) = rowwise(_f_ssd_post, f"ssd_post{l}", [ya_dir[0], ya_dir[1], xs, z], [],
                        [d_skip, P["a_norm_w"][l].reshape(1, -1)], [(A_INNER, BF16)], ncb)
        yb_dir = [s5_scan(f"s5{l}{d}", bool(d), pb, P["s5_lam_re"][l, d], P["s5_lam_im"][l, d],
                          P["s5_log_step"][l, d], P["s5_b_re"][l, d], P["s5_b_im"][l, d], P["s5_c_re"][l, d],
                          P["s5_c_im"][l, d], n_ctx) for d in range(2)]
        (yb,) = rowwise(_f_s5_post, f"s5_post{l}", [yb_dir[0], yb_dir[1], pb], [],
                        [P["s5_d"][l].reshape(1, -1), P["s5_glu_w"][l], P["s5_glu_b"][l].reshape(1, -1)],
                        [(B_WIDTH, BF16)], ncb)
        qr, f_raw, vi, gate = (pc[..., :C_WIDTH], pc[..., C_WIDTH:3 * C_WIDTH], pc[..., 3 * C_WIDTH:4 * C_WIDTH],
                               pc[..., 4 * C_WIDTH:])
        yc_dir = [hg_scan(f"hg{l}{d}", bool(d), qr, f_raw[..., d * C_WIDTH:(d + 1) * C_WIDTH], vi,
                          lower_bounds[l, d], n_ctx) for d in range(2)]
        (yc,) = rowwise(_f_hg_post, f"hg_post{l}", [yc_dir[0], yc_dir[1], gate], [],
                        [P["hg_norm_w"][l].reshape(1, -1)], [(C_WIDTH, BF16)], ncb)
        mix = jnp.concatenate([ya, yb, yc], axis=-1)
        y = linear(mix.reshape(B * L, Dm), rows_of(big["w_out"][l]), rows_of(P["w_out"][l]), f"w_out{l}")
        y = y.reshape(B, L, Dm)
        if col_major:
            y = _to_raster(y, n_ctx)
        h, u = rowwise(_f_resid_mod(1.0), f"resid_b{l}", [h, y], [mods[5], mods[6], mods[7]], [],
                       [(Dm, F32), (Dm, BF16)], ncb)
        o = ffn(u, l, 1)
        gate8 = mods[8]
        if l + 1 < depth:
            mods = mods_of(l + 1)
            h, u = rowwise(_f_resid_mod(0.5), f"resid_c{l}", [h, o], [gate8, mods[0], mods[1]], [],
                           [(Dm, F32), (Dm, BF16)], ncb)
        else:
            (h,) = rowwise(_f_resid(0.5), f"resid_c{l}", [h, o], [gate8], [], [(Dm, F32)], ncb)
    return h


BIG = ("mod_w", "ffn_w_in", "ffn_w_out", "w_in", "w_out")


def local_step(x, c, ctx, target, big, small):
    n_ctx = ctx.shape[1]
    h0 = jnp.concatenate([ctx, x], axis=1)
    P = dict(small)
    for k in BIG:
        P[k] = jax.tree.map(lambda t: jnp.zeros(t.shape, BF16), big[k])
    h, vjp = jax.vjp(lambda h0, P: _forward(h0, c, big, P, n_ctx), h0, P)
    loss, dh, d_final = loss_head(h, target, small["final_norm_w"], n_ctx)
    dh0, grads = vjp(dh)
    grads = dict(grads)
    grads["final_norm_w"] = grads["final_norm_w"] + d_final
    return loss, dh0[:, n_ctx:], grads


MESH = pl.DeviceIdType.MESH
LANES = 1024
N_CHIPS = 4
ELEMENTWISE_BLOCK_BYTES = 2 * 1024 * 1024
ANY = pl.BlockSpec(memory_space=pl.ANY)


def _place():
    x, y, c = lax.axis_index("x"), lax.axis_index("y"), lax.axis_index("c")
    return x, y, c, 2 * x + y


def _other_chips(x, y):
    return [(x ^ kx, y ^ ky, 2 * (x ^ kx) + (y ^ ky)) for kx, ky in ((0, 1), (1, 0), (1, 1))]


def all_gather_shards(shard, name):
    def body(x_ref, out_ref, send_sems, recv_sems, local_sem):
        x, y, c, q = _place()
        sibling = (x, y, 1 - c)
        chips = _other_chips(x, y)

        def copy(k, src, dst, to):
            return pltpu.make_async_remote_copy(src_ref=src, dst_ref=dst, send_sem=send_sems.at[k],
                                                recv_sem=recv_sems.at[k], device_id=to, device_id_type=MESH)

        mine = pltpu.make_async_copy(x_ref, out_ref.at[q], local_sem)
        mine.start()
        first = [copy(k, x_ref.at[c], out_ref.at[q, c], (px, py, c)) for k, (px, py, _) in enumerate(chips)]
        for cp in first:
            cp.start()
        passed = [copy(3 + k, out_ref.at[pq, c], out_ref.at[pq, c], sibling) for k, (_, _, pq) in enumerate(chips)]
        for k, (_, _, pq) in enumerate(chips):
            copy(k, x_ref.at[c], out_ref.at[pq, c], sibling).wait_recv()
            passed[k].start()
        for k, (_, _, pq) in enumerate(chips):
            copy(3 + k, x_ref.at[c], out_ref.at[pq, 1 - c], sibling).wait_recv()
        for cp in first + passed:
            cp.wait_send()
        mine.wait()

    return pl.pallas_call(
        body, out_shape=jax.ShapeDtypeStruct((N_CHIPS,) + shard.shape, shard.dtype), in_specs=[ANY], out_specs=ANY,
        scratch_shapes=[pltpu.SemaphoreType.DMA((6,)), pltpu.SemaphoreType.DMA((6,)), pltpu.SemaphoreType.DMA],
        name=name)(shard)


def all_gather_devices(part, name):
    def body(x_ref, out_ref, send_sems, recv_sems, local_sem):
        x, y, c, _ = _place()
        me = 4 * x + 2 * y + c
        mine = pltpu.make_async_copy(x_ref, out_ref.at[me], local_sem)
        mine.start()
        copies = []
        for k in range(1, 8):
            px, py, pc = x ^ (k >> 2), y ^ ((k >> 1) & 1), c ^ (k & 1)
            copies.append(pltpu.make_async_remote_copy(
                src_ref=x_ref, dst_ref=out_ref.at[me], send_sem=send_sems.at[k - 1], recv_sem=recv_sems.at[k - 1],
                device_id=(px, py, pc), device_id_type=MESH))
            copies[-1].start()
        for k in range(1, 8):
            peer = 4 * (x ^ (k >> 2)) + 2 * (y ^ ((k >> 1) & 1)) + (c ^ (k & 1))
            pltpu.make_async_remote_copy(
                src_ref=x_ref, dst_ref=out_ref.at[peer], send_sem=send_sems.at[k - 1], recv_sem=recv_sems.at[k - 1],
                device_id=(x, y, c), device_id_type=MESH).wait_recv()
        for cp in copies:
            cp.wait_send()
        mine.wait()

    return pl.pallas_call(
        body, out_shape=jax.ShapeDtypeStruct((8,) + part.shape, part.dtype), in_specs=[ANY], out_specs=ANY,
        scratch_shapes=[pltpu.SemaphoreType.DMA((7,)), pltpu.SemaphoreType.DMA((7,)), pltpu.SemaphoreType.DMA],
        name=name)(part)


def _half(ref, c, axis):
    h = ref.shape[axis] // 2
    return ref.at[(slice(None),) * axis + (pl.ds(pl.multiple_of(c * h, 8), h),)]


def _remote(src, dst, send_sems, recv_sems, s, to):
    return pltpu.make_async_remote_copy(src_ref=src, dst_ref=dst, send_sem=send_sems.at[s], recv_sem=recv_sems.at[s],
                                        device_id=to, device_id_type=MESH)


def gather_pieces(arrays, name):
    n = len(arrays)

    def body(*refs):
        ins, outs, (send_sems, recv_sems) = refs[:n], refs[n:2 * n], refs[2 * n:]
        x, y, c, q = _place()
        sibling = (x, y, 1 - c)
        chips = _other_chips(x, y)
        started = []

        def go(src, dst, s, to):
            started.append(_remote(src, dst, send_sems, recv_sems, s, to))
            started[-1].start()

        for i, (a, o) in enumerate(zip(ins, outs)):
            go(a, o.at[q], 7 * i + 6, sibling)
            for k, (px, py, _) in enumerate(chips):
                go(_half(a, c, 0), _half(o.at[q], c, 0), 7 * i + k, (px, py, c))
        for i, o in enumerate(outs):
            for k, (_, _, pq) in enumerate(chips):
                land = _half(o.at[pq], c, 0)
                _remote(land, land, send_sems, recv_sems, 7 * i + k, sibling).wait_recv()
                go(land, land, 7 * i + 3 + k, sibling)
        for i, (a, o) in enumerate(zip(ins, outs)):
            for k, (_, _, pq) in enumerate(chips):
                land = _half(o.at[pq], 1 - c, 0)
                _remote(land, land, send_sems, recv_sems, 7 * i + 3 + k, sibling).wait_recv()
            _remote(a, o.at[q], send_sems, recv_sems, 7 * i + 6, sibling).wait_recv()
        for cp in started:
            cp.wait_send()

    return pl.pallas_call(
        body, out_shape=[jax.ShapeDtypeStruct((N_CHIPS,) + a.shape, a.dtype) for a in arrays], in_specs=[ANY] * n,
        out_specs=[ANY] * n, scratch_shapes=[pltpu.SemaphoreType.DMA((7 * n,)), pltpu.SemaphoreType.DMA((7 * n,))],
        name=name)(*arrays)


def swap_with_sibling(arrays, name, send_other_half_of_axis=None):
    n = len(arrays)
    ax = send_other_half_of_axis

    def body(*refs):
        ins, outs, (send_sems, recv_sems) = refs[:n], refs[n:2 * n], refs[2 * n:]
        x, y, c, _ = _place()
        copies = [_remote(a if ax is None else _half(a, 1 - c, ax), o, send_sems, recv_sems, i, (x, y, 1 - c))
                  for i, (a, o) in enumerate(zip(ins, outs))]
        for cp in copies:
            cp.start()
        for cp in copies:
            cp.wait()

    def out_of(a):
        shape = list(a.shape)
        if ax is not None:
            shape[ax] //= 2
        return jax.ShapeDtypeStruct(tuple(shape), a.dtype)

    return pl.pallas_call(
        body, out_shape=[out_of(a) for a in arrays], in_specs=[ANY] * n, out_specs=[ANY] * n,
        scratch_shapes=[pltpu.SemaphoreType.DMA((n,)), pltpu.SemaphoreType.DMA((n,))], name=name)(*arrays)


def pieces_to_chips(pairs, name):
    n = len(pairs)

    def body(*refs):
        ins, outs, (send_sems, recv_sems) = refs[:n], refs[n:2 * n], refs[2 * n:]
        x, y, c, q = _place()
        chips = _other_chips(x, y)
        copies = [_remote(a.at[pq], o.at[q], send_sems, recv_sems, 3 * i + k, (px, py, c))
                  for i, (a, o) in enumerate(zip(ins, outs)) for k, (px, py, pq) in enumerate(chips)]
        for cp in copies:
            cp.start()
        for i, (a, o) in enumerate(zip(ins, outs)):
            for k, (_, _, pq) in enumerate(chips):
                _remote(a.at[pq], o.at[pq], send_sems, recv_sems, 3 * i + k, (x, y, c)).wait_recv()
        for cp in copies:
            cp.wait_send()

    return pl.pallas_call(
        body, out_shape=[jax.ShapeDtypeStruct(a.shape, a.dtype) for a in pairs], in_specs=[ANY] * n,
        out_specs=[ANY] * n, scratch_shapes=[pltpu.SemaphoreType.DMA((3 * n,)), pltpu.SemaphoreType.DMA((3 * n,))],
        name=name)(*pairs)


def _rows_block(h, n_cols):
    return _tile(h, 16, max(16, ELEMENTWISE_BLOCK_BYTES // (n_cols * 4)))


def add_pair(core, g, got, name):
    _, K, N = g.shape
    h = K // 2
    th = _rows_block(h, N)
    nb = h // th

    def body(c_ref, g_ref, got_ref, o_ref):
        o_ref[...] = (g_ref[...].astype(F32) + got_ref[...].astype(F32)).astype(o_ref.dtype)

    grid_spec = pltpu.PrefetchScalarGridSpec(
        num_scalar_prefetch=1, grid=(N_CHIPS, nb),
        in_specs=[pl.BlockSpec((None, th, N), lambda p, i, c_ref: (p, c_ref[0] * nb + i, 0)),
                  pl.BlockSpec((None, th, N), lambda p, i, c_ref: (p, i, 0))],
        out_specs=pl.BlockSpec((None, th, N), lambda p, i, c_ref: (p, i, 0)))
    return pl.pallas_call(body, grid_spec=grid_spec, out_shape=jax.ShapeDtypeStruct(got.shape, g.dtype), name=name,
                          compiler_params=_cp("parallel", "parallel"))(core, g, got)


def sum_pieces(chip, pair, got, name):
    _, h, N = pair.shape
    th = _rows_block(h, N)

    def body(q_ref, pair_ref, *rest):
        got_refs, o_ref = rest[:N_CHIPS], rest[N_CHIPS]
        q = q_ref[0]
        for p in range(N_CHIPS):
            def put(val, p=p):
                if p == 0:
                    o_ref[...] = val
                else:
                    o_ref[...] += val

            @pl.when(q == p)
            def _():
                put(pair_ref[...].astype(F32))

            @pl.when(q != p)
            def _(p=p):
                put(got_refs[p][...].astype(F32))

    def got_spec(p):
        return pl.BlockSpec((None, th, N), lambda i, q_ref: (jnp.where(q_ref[0] == p, (p + 1) % N_CHIPS, p), i, 0))

    grid_spec = pltpu.PrefetchScalarGridSpec(
        num_scalar_prefetch=1, grid=(h // th,),
        in_specs=[pl.BlockSpec((None, th, N), lambda i, q_ref: (q_ref[0], i, 0))] + [got_spec(p) for p in range(N_CHIPS)],
        out_specs=pl.BlockSpec((th, N), lambda i, q_ref: (i, 0)))
    return pl.pallas_call(body, grid_spec=grid_spec, out_shape=jax.ShapeDtypeStruct((h, N), F32), name=name,
                          compiler_params=_cp("parallel"))(chip, pair, *([got] * N_CHIPS))


def adamw_halves(core, mine, other, w, m, v, name):
    n, K, N = w.shape
    h = K // 2
    th = _rows_block(h, N)
    nb = h // th

    def body(c_ref, *refs):
        mine_refs, other_refs = refs[:n], refs[n:2 * n]
        w_ref, m_ref, v_ref, g_ref, d_ref, m2_ref, v2_ref = refs[2 * n:]
        a, s, c = pl.program_id(0), pl.program_id(1), c_ref[0]
        for idx in range(n):
            @pl.when((a == idx) & (s == c))
            def _(idx=idx):
                g_ref[...] = mine_refs[idx][...]

            @pl.when((a == idx) & (s != c))
            def _(idx=idx):
                g_ref[...] = other_refs[idx][...]
        _adamw_math(g_ref[...], w_ref, m_ref, v_ref, d_ref, m2_ref, v2_ref)

    def half_spec(idx, is_mine):
        def index(a, s, i, c_ref):
            right_half = (s == c_ref[0]) if is_mine else (s != c_ref[0])
            return (jnp.where((a == idx) & right_half, i, 0), 0)
        return pl.BlockSpec((th, N), index)

    full = pl.BlockSpec((None, th, N), lambda a, s, i, c_ref: (a, s * nb + i, 0))
    grid_spec = pltpu.PrefetchScalarGridSpec(
        num_scalar_prefetch=1, grid=(n, 2, nb),
        in_specs=[half_spec(idx, True) for idx in range(n)] + [half_spec(idx, False) for idx in range(n)] + [full] * 3,
        out_specs=[full] * 4)
    return pl.pallas_call(body, grid_spec=grid_spec, out_shape=[jax.ShapeDtypeStruct((n, K, N), F32)] * 4, name=name,
                          compiler_params=_cp("arbitrary", "arbitrary", "arbitrary"))(core, *mine, *other, w, m, v)


def _adamw_math(gv, w_ref, m_ref, v_ref, d_ref, m2_ref, v2_ref):
    m2 = ADAM_B1 * m_ref[...] + (1.0 - ADAM_B1) * gv
    v2 = ADAM_B2 * v_ref[...] + (1.0 - ADAM_B2) * (gv * gv)
    m_hat = m2 / (1.0 - ADAM_B1 ** ADAM_STEP)
    v_hat = v2 / (1.0 - ADAM_B2 ** ADAM_STEP)
    d_ref[...] = -ADAM_LR * (m_hat / (jnp.sqrt(v_hat) + ADAM_EPS) + ADAM_WD * w_ref[...])
    m2_ref[...] = m2
    v2_ref[...] = v2


def _row_tile(R):
    return _pick(R, (512, 256, 128, 64, 32, 16, 8))


def sum_parts(parts, name):
    P, R, _ = parts.shape
    rt = _row_tile(R)

    def body(p_ref, o_ref):
        acc = p_ref[0]
        for p in range(1, P):
            acc = acc + p_ref[p]
        o_ref[...] = acc

    return pl.pallas_call(
        body, grid=(R // rt,), in_specs=[pl.BlockSpec((P, rt, LANES), lambda i: (0, i, 0))],
        out_specs=pl.BlockSpec((rt, LANES), lambda i: (i, 0)), out_shape=jax.ShapeDtypeStruct((R, LANES), F32),
        name=name, compiler_params=_cp("parallel"))(parts)


def adamw(g, w, m, v, name):
    R = g.shape[0]
    rt = _row_tile(R)

    def body(g_ref, w_ref, m_ref, v_ref, d_ref, m2_ref, v2_ref):
        _adamw_math(g_ref[...], w_ref, m_ref, v_ref, d_ref, m2_ref, v2_ref)

    spec = pl.BlockSpec((rt, LANES), lambda i: (i, 0))
    return pl.pallas_call(
        body, grid=(R // rt,), in_specs=[spec] * 4, out_specs=[spec] * 3,
        out_shape=[jax.ShapeDtypeStruct((R, LANES), F32)] * 3, name=name, compiler_params=_cp("parallel"))(g, w, m, v)


def _pack(arrays, rows_multiple, dtype):
    flat = jnp.concatenate([a.reshape(-1).astype(dtype) for a in arrays])
    n = flat.shape[0]
    per = rows_multiple * LANES
    total = -(-n // per) * per
    return jnp.pad(flat, (0, total - n)).reshape(total // LANES, LANES)


def _unpack(buf, shapes):
    flat = buf.reshape(-1)
    out, off = [], 0
    for s in shapes:
        n = math.prod(s)
        out.append(flat[off:off + n].reshape(s))
        off += n
    return out


SHARDED = (("mod_w", 2), ("ffn_w_in", 3), ("ffn_w_out", 2), ("w_in", 2), ("w_out", 1),
           ("a_conv_w", 2), ("s5_glu_w", 1), ("hg_lb_logits", 2))
WEIGHTS = ("c_ctx", "mod_w", "mod_b", "ffn_w_in", "ffn_w_out", "w_in", "w_out", "a_conv_w", "a_conv_b", "a_dt_bias",
           "a_log", "a_d", "a_norm_w", "s5_lam_re", "s5_lam_im", "s5_log_step", "s5_b_re", "s5_b_im", "s5_c_re",
           "s5_c_im", "s5_d", "s5_glu_w", "s5_glu_b", "hg_lb_logits", "hg_norm_w", "final_norm_w")


def _gather_full(local, names_axes, dtype, rows_multiple, name):
    buf = _pack([local[n] for n, _ in names_axes], 2 * rows_multiple, dtype)
    R = buf.shape[0] // 2
    full = all_gather_shards(buf.reshape(2, R, LANES), name).reshape(N_CHIPS, 2 * R, LANES)
    per_chip = [_unpack(full[q], [local[n].shape for n, _ in names_axes]) for q in range(N_CHIPS)]
    return {n: jnp.concatenate([per_chip[q][j] for q in range(N_CHIPS)], axis=ax)
            for j, (n, ax) in enumerate(names_axes)}


def kernel(x, c, ctx, c_ctx, mod_w, mod_b, ffn_w_in, ffn_w_out, w_in, w_out, a_conv_w, a_conv_b, a_dt_bias, a_log, a_d, a_norm_w, s5_lam_re, s5_lam_im, s5_log_step, s5_b_re, s5_b_im, s5_c_re, s5_c_im, s5_d, s5_glu_w, s5_glu_b, hg_lb_logits, hg_norm_w, final_norm_w, loss_target, m_c_ctx, m_mod_w, m_mod_b, m_ffn_w_in, m_ffn_w_out, m_w_in, m_w_out, m_a_conv_w, m_a_conv_b, m_a_dt_bias, m_a_log, m_a_d, m_a_norm_w, m_s5_lam_re, m_s5_lam_im, m_s5_log_step, m_s5_b_re, m_s5_b_im, m_s5_c_re, m_s5_c_im, m_s5_d, m_s5_glu_w, m_s5_glu_b, m_hg_lb_logits, m_hg_norm_w, m_final_norm_w, v_c_ctx, v_mod_w, v_mod_b, v_ffn_w_in, v_ffn_w_out, v_w_in, v_w_out, v_a_conv_w, v_a_conv_b, v_a_dt_bias, v_a_log, v_a_d, v_a_norm_w, v_s5_lam_re, v_s5_lam_im, v_s5_log_step, v_s5_b_re, v_s5_b_im, v_s5_c_re, v_s5_c_im, v_s5_d, v_s5_glu_w, v_s5_glu_b, v_hg_lb_logits, v_hg_norm_w, v_final_norm_w):
    given = dict(locals())
    w = {n: given[n] for n in WEIGHTS}
    m = {n: given["m_" + n] for n in WEIGHTS}
    v = {n: given["v_" + n] for n in WEIGHTS}
    small_sharded = SHARDED[len(BIG):]
    replicated = [n for n in WEIGHTS if n not in [s for s, _ in SHARDED]]
    core = lax.axis_index("c").astype(jnp.int32).reshape(1)
    chip = (2 * lax.axis_index("x") + lax.axis_index("y")).astype(jnp.int32)

    def leaves(t):
        return list(t.reshape((-1,) + t.shape[-2:]))

    def nest(kind, flat):
        flat = list(flat)
        return [flat[2 * l:2 * l + 2] for l in range(len(flat) // 2)] if w[kind].ndim == 4 else flat

    counts = [len(leaves(w[k])) for k in BIG]
    gathered = gather_pieces([a.astype(BF16) for k in BIG for a in leaves(w[k])], "gather_big")
    big, off = {}, 0
    for k, cnt in zip(BIG, counts):
        big[k], off = nest(k, gathered[off:off + cnt]), off + cnt
    small = {n: w[n] for n in replicated}
    small.update(_gather_full(w, small_sharded, F32, 8, "gather_small"))

    loss, grad_x, grads = local_step(x, c, ctx, loss_target, big, small)

    g_big = [g for k in BIG for g in jax.tree.leaves(grads[k])]
    got = swap_with_sibling(g_big, "rs_sibling", send_other_half_of_axis=1)
    pairs = [add_pair(core, g, t, f"rs_add_pair{i}") for i, (g, t) in enumerate(zip(g_big, got))]
    from_chips = pieces_to_chips(pairs, "rs_chips")
    mine = [sum_pieces(chip.reshape(1), p, t, f"rs_sum_chips{i}") for i, (p, t) in enumerate(zip(pairs, from_chips))]
    other = swap_with_sibling(mine, "rs_halves")
    out, off = {}, 0
    for k, cnt in zip(BIG, counts):
        stack = lambda t: t.reshape((cnt,) + t.shape[-2:])
        res = adamw_halves(core, mine[off:off + cnt], other[off:off + cnt], stack(w[k]), stack(m[k]), stack(v[k]),
                           "adamw_" + k)
        off += cnt
        for kind, a in zip(("grad", "delta", "new_m", "new_v"), res):
            out[kind, k] = a.reshape(w[k].shape)

    names = replicated + [n for n, _ in small_sharded]
    part = _pack([grads[n] for n in names] + [loss.reshape(1)], 8, F32)
    g_all = _unpack(sum_parts(all_gather_devices(part, "gather_parts"), "sum_parts"),
                    [grads[n].shape for n in names] + [(1,)])
    g_of = dict(zip(names + ["loss"], g_all))
    for n, ax in small_sharded:
        g_of[n] = lax.dynamic_slice_in_dim(g_of[n], chip * w[n].shape[ax], w[n].shape[ax], axis=ax)
    rest = lambda d: _pack([d[n] for n in names], 8, F32)
    g_rest = rest(g_of)
    d_rest, m_rest, v_rest = adamw(g_rest, rest(w), rest(m), rest(v), "adamw_rest")
    shapes = [w[n].shape for n in names]
    for kind, buf in (("grad", g_rest), ("delta", d_rest), ("new_m", m_rest), ("new_v", v_rest)):
        for n, a in zip(names, _unpack(buf, shapes)):
            out[kind, n] = a

    loss_total = g_of["loss"].reshape(())
    return (loss_total, grad_x, *[out["grad", n] for n in WEIGHTS], *[out["delta", n] for n in WEIGHTS],
            *[out["new_m", n] for n in WEIGHTS], *[out["new_v", n] for n in WEIGHTS])
```

```python
import functools
import math

import numpy as np
import jax
import jax.numpy as jnp
from jax import lax
from jax.experimental import pallas as pl
from jax.experimental.pallas import tpu as pltpu

F32, BF16 = jnp.float32, jnp.bfloat16
EPS = 1e-6
N_MOD = 9
D_FF = 2816
A_INNER, A_HEADS, A_GROUPS, A_STATE, A_CONV, A_CONV_DIM = 512, 8, 2, 64, 5, 768
A_COLS = A_INNER + A_CONV_DIM + 2 * A_HEADS
B_WIDTH, B_GROUP, B_NGROUPS, B_STATE = 256, 16, 16, 64
C_WIDTH, C_HEADS, C_KEY = 256, 4, 64
GRID_W = 64
IN_COLS = A_COLS + B_WIDTH + 5 * C_WIDTH
IN_COLS_PAD = 3072
ADAM_LR, ADAM_B1, ADAM_B2, ADAM_EPS, ADAM_WD, ADAM_STEP = 0.001, 0.9, 0.999, 1e-08, 0.01, 10

ROW_TILE = 256
SSD_Q, HG_Q = 128, 64
REC_T = 128
SCAN_SAMPLES = 4
HG_BAND = 8
VMEM_LIMIT = 56 * 1024 * 1024
MM_VMEM_BUDGET = 36 * 1024 * 1024

NT = ((1,), (1,))
TN = ((0,), (0,))


def _bdot(a, b, dims=((1,), (0,))):
    return lax.dot_general(a.astype(BF16), b.astype(BF16), (dims, ((), ())), preferred_element_type=F32)


def _split3(x):
    hi = x.astype(BF16)
    r = x - hi.astype(F32)
    mid = r.astype(BF16)
    lo = (r - mid.astype(F32)).astype(BF16)
    return hi, mid, lo


def _mask_lhs(m, x, dims):
    return sum(lax.dot_general(m, p, (dims, ((), ())), preferred_element_type=F32) for p in _split3(x))


def _mask_rhs(x, m, dims):
    return sum(lax.dot_general(p, m, (dims, ((), ())), preferred_element_type=F32) for p in _split3(x))


@jax.custom_vjp
def mdot(m, x):
    return _mask_lhs(m, x, ((1,), (0,)))


def _mdot_fwd(m, x):
    return mdot(m, x), m


def _mdot_bwd(m, g):
    return jnp.zeros_like(m), _mask_lhs(m, g, TN)


mdot.defvjp(_mdot_fwd, _mdot_bwd)


@jax.custom_vjp
def xdot(x, m):
    return _mask_rhs(x, m, ((1,), (0,)))


def _xdot_fwd(x, m):
    return xdot(x, m), m


def _xdot_bwd(m, g):
    return _mask_rhs(g, m, NT), jnp.zeros_like(m)


xdot.defvjp(_xdot_fwd, _xdot_bwd)


@functools.partial(jax.custom_vjp, nondiff_argnums=(1,))
def _roll(x, s):
    return pltpu.roll(x, s, 0)


def _roll_fwd(x, s):
    return _roll(x, s), None


def _roll_bwd(s, _, g):
    return (_roll(g, x_rows(g) - s),)


def x_rows(x):
    return x.shape[0]


_roll.defvjp(_roll_fwd, _roll_bwd)


def sroll(x, s):
    s = s % x.shape[0]
    return x if s == 0 else _roll(x, s)


def _softplus(x):
    return jnp.maximum(x, 0.0) + jnp.log1p(jnp.exp(jnp.minimum(x, -x)))


def _sigmoid(x):
    return 0.5 * (jnp.tanh(0.5 * x) + 1.0)


def _silu(x):
    return x * _sigmoid(x)


def _rms(x):
    return x * lax.rsqrt(jnp.mean(x * x, axis=-1, keepdims=True) + EPS)


def _pick(n, cands):
    for c in cands:
        if n % c == 0:
            return c
    return n


def _cp(*sem):
    return pltpu.CompilerParams(dimension_semantics=sem, vmem_limit_bytes=VMEM_LIMIT)


def _tile(n, unit, cap):
    best = None
    for d in range(unit, min(n, cap) + 1, unit):
        if n % d == 0:
            best = d
    return best if best is not None else n


def _mm(a, b, ta, tb, out_dtype, name, b_pieces=False, out_pieces=False):
    M, K = (a.shape[1], a.shape[0]) if ta else a.shape
    n_div = k_div = None
    if b_pieces and tb:
        N, k_div = b.shape[1], b.shape[2]
    elif b_pieces:
        N, n_div = N_CHIPS * b.shape[2], b.shape[2]
    else:
        N = b.shape[0] if tb else b.shape[1]
    if out_pieces:
        n_div = N // N_CHIPS
    n_div, k_div = n_div or N, k_div or K
    tm = _tile(M, 128 if ta else 8, 1536 if ta else 1024)
    tn = _tile(n_div, 128, 1536)
    a_bytes, b_bytes, o_bytes = a.dtype.itemsize, b.dtype.itemsize, jnp.dtype(out_dtype).itemsize
    cands = {d for d in range(128, min(k_div, 2816) + 1, 128) if k_div % d == 0}
    if k_div <= 2816 or not cands:
        cands.add(k_div)
    for tk in sorted(cands, reverse=True):
        scratch = tm * tn * 4 if (K // tk > 1 and o_bytes != 4) else 0
        if 2 * (tm * tk * a_bytes + tk * tn * b_bytes + tm * tn * o_bytes) + scratch <= MM_VMEM_BUDGET:
            break
    nk = K // tk
    nq_n, nq_k = n_div // tn, k_div // tk
    dims = ((0 if ta else 1,), (1 if tb else 0,))

    def body(a_ref, b_ref, o_ref, *acc):
        part = _bdot(a_ref[...], b_ref[...], dims)
        if nk == 1:
            o_ref[...] = part.astype(o_ref.dtype)
            return
        acc_ref = acc[0] if acc else o_ref
        k = pl.program_id(2)

        @pl.when(k == 0)
        def _():
            acc_ref[...] = part

        @pl.when(k > 0)
        def _():
            acc_ref[...] += part

        if acc:
            @pl.when(k == nk - 1)
            def _():
                o_ref[...] = acc_ref[...].astype(o_ref.dtype)

    a_spec = pl.BlockSpec((tk, tm), lambda i, j, k: (k, i)) if ta else pl.BlockSpec((tm, tk), lambda i, j, k: (i, k))
    if b_pieces and tb:
        b_spec = pl.BlockSpec((None, tn, tk), lambda i, j, k: (k // nq_k, j, k % nq_k))
    elif b_pieces:
        b_spec = pl.BlockSpec((None, tk, tn), lambda i, j, k: (j // nq_n, k, j % nq_n))
    elif tb:
        b_spec = pl.BlockSpec((tn, tk), lambda i, j, k: (j, k))
    else:
        b_spec = pl.BlockSpec((tk, tn), lambda i, j, k: (k, j))
    if out_pieces:
        o_spec = pl.BlockSpec((None, tm, tn), lambda i, j, k: (j // nq_n, i, j % nq_n))
        o_shape = jax.ShapeDtypeStruct((N_CHIPS, M, N // N_CHIPS), out_dtype)
    else:
        o_spec = pl.BlockSpec((tm, tn), lambda i, j, k: (i, j))
        o_shape = jax.ShapeDtypeStruct((M, N), out_dtype)
    return pl.pallas_call(
        body, grid=(M // tm, N // tn, nk), in_specs=[a_spec, b_spec], out_specs=o_spec, out_shape=o_shape,
        scratch_shapes=[pltpu.VMEM((tm, tn), F32)] if (nk > 1 and o_bytes != 4) else [], name=name,
        compiler_params=_cp("parallel", "parallel", "arbitrary"))(a, b)


def linear(x, w16, wslot, name, out_dtype=F32, pieces=False):
    @jax.custom_vjp
    def f(x, w16, wslot):
        return _mm(x, w16, False, False, out_dtype, name + "_fwd", b_pieces=pieces)

    def fwd(x, w16, wslot):
        return f(x, w16, wslot), (x, w16)

    grad_dtype = wslot.dtype

    def bwd(res, dy):
        x, w16 = res
        dx = _mm(dy, w16, False, True, x.dtype, name + "_dx", b_pieces=pieces)
        dw = _mm(x, dy, True, False, grad_dtype, name + "_dw", out_pieces=pieces)
        return dx, jnp.zeros_like(w16), dw

    f.defvjp(fwd, bwd)
    return f(x, w16, wslot)


def rowwise(f, name, rows, mods, params, outs, n_ctx_blocks, tl=ROW_TILE, row_diff=None):
    rows, mods, params = tuple(rows), tuple(mods), tuple(params)
    nr, nm, npar, no = len(rows), len(mods), len(params), len(outs)
    B, L = rows[0].shape[:2]
    nblk = L // tl
    row_diff = tuple(row_diff) if row_diff is not None else (True,) * nr
    out_dtypes = [dt for _, dt in outs]

    def fcast(r, m, p):
        return tuple(o.astype(dt) for o, dt in zip(f(r, m, p), out_dtypes))

    def seg(i):
        return (i >= n_ctx_blocks).astype(jnp.int32) if n_ctx_blocks else 0

    def specs():
        row_specs = [pl.BlockSpec((1, tl, r.shape[2]), lambda b, i: (b, i, 0)) for r in rows]
        mod_specs = [pl.BlockSpec((1, 1, 1, m.shape[3]), lambda b, i: (b, seg(i), 0, 0)) for m in mods]
        par_specs = [pl.BlockSpec(p.shape, lambda b, i: (0, 0)) for p in params]
        out_specs = [pl.BlockSpec((1, tl, w), lambda b, i: (b, i, 0)) for w, _ in outs]
        return row_specs, mod_specs, par_specs, out_specs

    def load(ins):
        r = tuple(x[0] for x in ins[:nr])
        m = tuple(x[0, 0] for x in ins[nr:nr + nm])
        p = tuple(x[...] for x in ins[nr + nm:])
        return r, m, p

    def fwd_call(rows, mods, params):
        def body(*refs):
            r, m, p = load(refs[:nr + nm + npar])
            for o_ref, o in zip(refs[nr + nm + npar:], fcast(r, m, p)):
                o_ref[0] = o

        rs, ms, ps, os_ = specs()
        return pl.pallas_call(
            body, grid=(B, nblk), in_specs=rs + ms + ps, out_specs=os_,
            out_shape=[jax.ShapeDtypeStruct((B, L, w), dt) for w, dt in outs],
            name=name + "_fwd", compiler_params=_cp("parallel", "parallel"))(*rows, *mods, *params)

    def bwd_call(rows, mods, params, douts):
        didx = [j for j in range(nr) if row_diff[j]]

        def body(*refs):
            n_in = nr + nm + npar
            r, m, p = load(refs[:n_in])
            dins = refs[n_in:n_in + no]
            rest = refs[n_in + no:]
            dr_refs, dm_refs, dp_refs = rest[:len(didx)], rest[len(didx):len(didx) + nm], rest[len(didx) + nm:]
            _, vjp = jax.vjp(fcast, r, m, p)
            dr, dm, dp = vjp(tuple(d[0] for d in dins))
            for ref, j in zip(dr_refs, didx):
                ref[0] = dr[j].astype(ref.dtype)
            b, i = pl.program_id(0), pl.program_id(1)
            first_m = (i == 0) | (i == n_ctx_blocks) if n_ctx_blocks else (i == 0)
            first_p = (b == 0) & (i == 0)
            for ref, g in zip(dm_refs, dm):
                @pl.when(first_m)
                def _(ref=ref, g=g):
                    ref[0, 0] = g

                @pl.when(jnp.logical_not(first_m))
                def _(ref=ref, g=g):
                    ref[0, 0] += g
            for ref, g in zip(dp_refs, dp):
                @pl.when(first_p)
                def _(ref=ref, g=g):
                    ref[...] = g

                @pl.when(jnp.logical_not(first_p))
                def _(ref=ref, g=g):
                    ref[...] += g

        rs, ms, ps, os_ = specs()
        out_shape = ([jax.ShapeDtypeStruct(rows[j].shape, rows[j].dtype) for j in didx]
                     + [jax.ShapeDtypeStruct(m.shape, F32) for m in mods]
                     + [jax.ShapeDtypeStruct(p.shape, F32) for p in params])
        res = pl.pallas_call(
            body, grid=(B, nblk), in_specs=rs + ms + ps + os_,
            out_specs=[rs[j] for j in didx] + ms + ps, out_shape=out_shape,
            name=name + "_bwd", compiler_params=_cp("arbitrary", "arbitrary"))(*rows, *mods, *params, *douts)
        dr = [None] * nr
        for j, g in zip(didx, res[:len(didx)]):
            dr[j] = g
        dr = tuple(g if g is not None else jnp.zeros_like(rows[j]) for j, g in enumerate(dr))
        return dr, tuple(res[len(didx):len(didx) + nm]), tuple(res[len(didx) + nm:])

    @jax.custom_vjp
    def op(rows, mods, params):
        return tuple(fwd_call(rows, mods, params))

    def op_fwd(rows, mods, params):
        return op(rows, mods, params), (rows, mods, params)

    def op_bwd(res, douts):
        return bwd_call(*res, tuple(douts))

    op.defvjp(op_fwd, op_bwd)
    return op(rows, mods, params)


def chunk_scan(step, name, rev, seqs, seq_t, params, consts, state_shapes, out_widths, Q, n_ctx):
    seqs, params, consts = tuple(seqs), tuple(params), tuple(consts)
    ns, npar, nc, nst, no = len(seqs), len(params), len(consts), len(state_shapes), len(out_widths)
    B = seqs[0].shape[0]
    L = seqs[0].shape[2] if seq_t[0] else seqs[0].shape[1]
    nck, ncc = L // Q, n_ctx // Q
    nb = SCAN_SAMPLES if B % SCAN_SAMPLES == 0 else 1

    def chunk_of(k):
        if not rev:
            return k
        return jnp.where(k < ncc, ncc - 1 - k, nck + ncc - 1 - k)

    def specs(order):
        def seq_spec(s, t):
            if t:
                return pl.BlockSpec((nb, s.shape[1], Q), lambda b, k: (b, 0, chunk_of(order(k))))
            return pl.BlockSpec((nb, Q, s.shape[2]), lambda b, k: (b, chunk_of(order(k)), 0))

        seq_specs = [seq_spec(s, t) for s, t in zip(seqs, seq_t)]
        par_specs = [pl.BlockSpec(p.shape, lambda b, k: (0, 0)) for p in params]
        con_specs = [pl.BlockSpec(c.shape, lambda b, k, nd=c.ndim: (0,) * nd) for c in consts]
        out_specs = [pl.BlockSpec((nb, Q, w), lambda b, k: (b, chunk_of(order(k)), 0)) for w in out_widths]
        sav_specs = [pl.BlockSpec((nb, 1) + tuple(s), lambda b, k: (b, order(k), 0, 0)) for s in state_shapes]
        return seq_specs, par_specs, con_specs, out_specs, sav_specs

    def fwd_call(seqs, params):
        def body(*refs):
            seq_refs = refs[:ns]
            par_refs = refs[ns:ns + npar]
            con_refs = refs[ns + npar:ns + npar + nc]
            rest = refs[ns + npar + nc:]
            out_refs, sav_refs, st_refs = rest[:no], rest[no:no + nst], rest[no + nst:]

            @pl.when(pl.program_id(1) == 0)
            def _():
                for s in st_refs:
                    s[...] = jnp.zeros_like(s)

            pvals, cvals = tuple(p[...] for p in par_refs), tuple(c[...] for c in con_refs)
            for i in range(nb):
                s_in = tuple(s[i] for s in st_refs)
                for sv, s in zip(sav_refs, s_in):
                    sv[i, 0] = s
                s_new, outs = step(s_in, tuple(r[i] for r in seq_refs), pvals, cvals)
                for s_ref, s in zip(st_refs, s_new):
                    s_ref[i] = s
                for o_ref, o in zip(out_refs, outs):
                    o_ref[i] = o

        ss, ps, cs, os_, vs = specs(lambda k: k)
        res = pl.pallas_call(
            body, grid=(B // nb, nck), in_specs=ss + ps + cs, out_specs=os_ + vs,
            out_shape=[jax.ShapeDtypeStruct((B, L, w), F32) for w in out_widths]
            + [jax.ShapeDtypeStruct((B, nck) + tuple(s), F32) for s in state_shapes],
            scratch_shapes=[pltpu.VMEM((nb,) + tuple(s), F32) for s in state_shapes],
            name=name + "_fwd", compiler_params=_cp("parallel", "arbitrary"))(*seqs, *params, *consts)
        return tuple(res[:no]), tuple(res[no:])

    def bwd_call(seqs, params, saved, douts):
        def body(*refs):
            seq_refs = refs[:ns]
            par_refs = refs[ns:ns + npar]
            con_refs = refs[ns + npar:ns + npar + nc]
            rest = refs[ns + npar + nc:]
            sav_refs, dout_refs = rest[:nst], rest[nst:nst + no]
            rest = rest[nst + no:]
            dseq_refs, dpar_refs, dst_refs = rest[:ns], rest[ns:ns + npar], rest[ns + npar:]
            b, k = pl.program_id(0), pl.program_id(1)

            @pl.when(k == 0)
            def _():
                for s in dst_refs:
                    s[...] = jnp.zeros_like(s)

            cvals, pvals = tuple(c[...] for c in con_refs), tuple(p[...] for p in par_refs)
            dp = None
            for i in range(nb):
                _, vjp = jax.vjp(lambda s, x, p: step(s, x, p, cvals), tuple(s[i, 0] for s in sav_refs),
                                 tuple(r[i] for r in seq_refs), pvals)
                ds, dx, dp_i = vjp((tuple(s[i] for s in dst_refs), tuple(d[i] for d in dout_refs)))
                for s_ref, s in zip(dst_refs, ds):
                    s_ref[i] = s
                for x_ref, x in zip(dseq_refs, dx):
                    x_ref[i] = x
                dp = dp_i if dp is None else tuple(a + g for a, g in zip(dp, dp_i))
            first = (b == 0) & (k == 0)
            for ref, g in zip(dpar_refs, dp):
                @pl.when(first)
                def _(ref=ref, g=g):
                    ref[...] = g

                @pl.when(jnp.logical_not(first))
                def _(ref=ref, g=g):
                    ref[...] += g

        ss, ps, cs, os_, vs = specs(lambda k: nck - 1 - k)
        res = pl.pallas_call(
            body, grid=(B // nb, nck), in_specs=ss + ps + cs + vs + os_, out_specs=ss + ps,
            out_shape=[jax.ShapeDtypeStruct(s.shape, F32) for s in seqs]
            + [jax.ShapeDtypeStruct(p.shape, F32) for p in params],
            scratch_shapes=[pltpu.VMEM((nb,) + tuple(s), F32) for s in state_shapes],
            name=name + "_bwd", compiler_params=_cp("arbitrary", "arbitrary"))(*seqs, *params, *consts, *saved, *douts)
        return tuple(res[:ns]), tuple(res[ns:])

    @jax.custom_vjp
    def op(seqs, params):
        return fwd_call(seqs, params)[0]

    def op_fwd(seqs, params):
        outs, saved = fwd_call(seqs, params)
        return outs, (seqs, params, saved)

    def op_bwd(res, douts):
        seqs, params, saved = res
        return bwd_call(seqs, params, saved, tuple(douts))

    op.defvjp(op_fwd, op_bwd)
    return op(seqs, params)


def _positions(Q, rev):
    t = np.arange(Q)
    return (Q - 1 - t) if rev else t


def _ssd_consts(rev):
    Q = SSD_Q
    pos = _positions(Q, rev)
    tri = pos[:, None] >= pos[None, :]
    head_of = np.arange(A_INNER) // (A_INNER // A_HEADS)
    expand = np.arange(A_HEADS)[:, None] == head_of[None, :]
    group_of_row = np.arange(A_GROUPS * A_STATE) // A_STATE
    block = group_of_row[:, None] == (head_of // (A_HEADS // A_GROUPS))[None, :]
    as_bf = lambda m: jnp.asarray(m.astype(np.float32), BF16)
    return (as_bf(tri), as_bf(tri.T), jnp.asarray(tri.astype(np.float32)), as_bf(expand),
            jnp.asarray(block.astype(np.float32)), jnp.asarray(expand.astype(np.float32)[:, None, :]))


def _ssd_step(state, seqs, params, consts):
    (st,) = state
    xs, bm, cm, dtr, dtr_t = seqs
    bias, bias_t, alog, alog_t = params
    tri_b, tri_t_b, tri_f, expand, block, head_lanes = consts
    dt = _softplus(dtr + bias)
    dta = dt * (-jnp.exp(alog))
    dt_t = _softplus(dtr_t + bias_t)
    dta_t = dt_t * (-jnp.exp(alog_t))
    cum = mdot(tri_b, dta)
    cum_t = xdot(dta_t, tri_t_b)
    total = jnp.sum(dta, axis=0, keepdims=True)
    w_end = jnp.exp(total - cum) * dt
    ecum_x = xdot(jnp.exp(cum), expand)
    wend_x = xdot(w_end, expand)
    dec_x = jnp.exp(jnp.sum(xdot(dta, expand), axis=0, keepdims=True))
    y = _bdot(cm, st) * ecum_x
    st_new = st * dec_x + block * _bdot(bm, xs * wend_x, TN)
    lane_g = lax.broadcasted_iota(jnp.int32, (1, A_GROUPS * A_STATE), 1) // A_STATE
    sub_h = lax.broadcasted_iota(jnp.int32, (A_HEADS, 1), 0)
    lane8 = lax.broadcasted_iota(jnp.int32, (1, A_HEADS), 1)
    Q = xs.shape[0]
    scores = []
    for g in range(A_GROUPS):
        cb = _bdot(jnp.where(lane_g == g, cm, 0.0), bm, NT)
        for h in range(g * (A_HEADS // A_GROUPS), (g + 1) * (A_HEADS // A_GROUPS)):
            col = jnp.sum(jnp.where(lane8 == h, cum, 0.0), axis=1, keepdims=True)
            row = jnp.sum(jnp.where(sub_h == h, cum_t, 0.0), axis=0, keepdims=True)
            dt_row = jnp.sum(jnp.where(sub_h == h, dt_t, 0.0), axis=0, keepdims=True)
            scores.append(cb * (tri_f * jnp.exp(tri_f * (col - row))) * dt_row)
    y_all = _bdot(jnp.stack(scores).reshape(A_HEADS * Q, Q), xs).reshape(A_HEADS, Q, A_INNER)
    y = y + jnp.sum(y_all * head_lanes, axis=0)
    return (st_new,), (y,)


def ssd_scan(name, rev, xs, bm, cm, dtr, dt_bias, a_log, n_ctx):
    seqs = (xs, bm, cm, dtr, jnp.swapaxes(dtr, 1, 2))
    params = (dt_bias.reshape(1, -1), dt_bias.reshape(-1, 1), a_log.reshape(1, -1), a_log.reshape(-1, 1))
    (y,) = chunk_scan(_ssd_step, name, rev, seqs, (False, False, False, False, True), params, _ssd_consts(rev),
                      [(A_GROUPS * A_STATE, A_INNER)], [A_INNER], SSD_Q, n_ctx)
    return y


def _recurrence(v, a, order, n_ctx, name, x_fwd=None):
    B, L = v.shape[:2]
    T = REC_T
    nck, ncc = L // T, n_ctx // T
    nlat = nck - ncc
    chunk_of = {"F": lambda k: k,
                "R": lambda k: jnp.where(k < ncc, ncc - 1 - k, nck + ncc - 1 - k),
                "FT": lambda k: nck - 1 - k,
                "RT": lambda k: jnp.where(k < nlat, ncc + k, k - nlat)}[order]
    descending = order in ("R", "FT")
    with_da = x_fwd is not None
    H = 8

    def body(*refs):
        if with_da:
            v_ref, a_ref, xf_ref, x_ref, da_ref, st_ref = refs
        else:
            v_ref, a_ref, x_ref, st_ref = refs
        k = pl.program_id(0)

        @pl.when(k == 0)
        def _():
            st_ref[...] = jnp.zeros_like(st_ref)
            if with_da:
                da_ref[...] = jnp.zeros_like(da_ref)

        ar, ai = a_ref[pl.ds(0, H), :], a_ref[pl.ds(H, H), :]
        zero = jnp.zeros((H, 128), F32)

        def step(i, carry):
            chains, acc_r, acc_i = carry
            t = (T - 1 - i) if descending else i
            out = []
            for b, (xr, xi) in enumerate(chains):
                if with_da:
                    f = xf_ref[b, t].astype(F32)
                    fr, fi = f[:H], f[H:]
                    acc_r = acc_r + xr * fr + xi * fi
                    acc_i = acc_i + xi * fr - xr * fi
                vt = v_ref[b, t].astype(F32)
                nr = ar * xr - ai * xi + vt[:H]
                ni = ar * xi + ai * xr + vt[H:]
                x_ref[b, t] = jnp.concatenate([nr, ni], axis=0).astype(x_ref.dtype)
                out.append((nr, ni))
            return tuple(out), acc_r, acc_i

        init = tuple((st_ref[b, pl.ds(0, H), :], st_ref[b, pl.ds(H, H), :]) for b in range(B))
        chains, acc_r, acc_i = lax.fori_loop(0, T, step, (init, zero, zero), unroll=8)
        for b, (xr, xi) in enumerate(chains):
            st_ref[b, pl.ds(0, H), :] = xr
            st_ref[b, pl.ds(H, H), :] = xi
        if with_da:
            da_ref[pl.ds(0, H), :] += acc_r
            da_ref[pl.ds(H, H), :] += acc_i

    seq = pl.BlockSpec((B, T, 2 * H, 128), lambda k: (0, chunk_of(k), 0, 0))
    par = pl.BlockSpec((2 * H, 128), lambda k: (0, 0))
    x_shape = jax.ShapeDtypeStruct(v.shape, v.dtype)
    if with_da:
        return pl.pallas_call(
            body, grid=(nck,), in_specs=[seq, par, seq], out_specs=[seq, par],
            out_shape=[x_shape, jax.ShapeDtypeStruct((2 * H, 128), F32)],
            scratch_shapes=[pltpu.VMEM((B, 2 * H, 128), F32)], name=name, compiler_params=_cp("arbitrary"))(v, a, x_fwd)
    return pl.pallas_call(
        body, grid=(nck,), in_specs=[seq, par], out_specs=seq, out_shape=x_shape,
        scratch_shapes=[pltpu.VMEM((B, 2 * H, 128), F32)], name=name, compiler_params=_cp("arbitrary"))(v, a)


def lin_rec(v, a, rev, n_ctx, name):
    fwd_order, bwd_order = ("R", "RT") if rev else ("F", "FT")
    conj = jnp.concatenate([jnp.ones((8, 128), F32), -jnp.ones((8, 128), F32)], axis=0)

    @jax.custom_vjp
    def f(v, a):
        return _recurrence(v, a, fwd_order, n_ctx, name + "_fwd")

    def fwd(v, a):
        x = f(v, a)
        return x, (x, a)

    def bwd(res, dx):
        x, a = res
        g, da = _recurrence(dx, a * conj, bwd_order, n_ctx, name + "_bwd", x_fwd=x)
        return g, da

    f.defvjp(fwd, bwd)
    return f(v, a)


def _block_diag(t):
    G, a, b = t.shape
    eye = jnp.eye(G, dtype=t.dtype)
    return (t[:, :, None, :] * eye[:, None, :, None]).reshape(G * a, G * b)


def s5_scan(name, rev, u, lam_re, lam_im, log_step, b_re, b_im, c_re, c_im, n_ctx):
    step = jnp.exp(log_step)[:, None]
    mag = jnp.exp(lam_re * step)
    ar = mag * jnp.cos(lam_im * step)
    ai = mag * jnp.sin(lam_im * step)
    den = lam_re * lam_re + lam_im * lam_im
    nr = ar - 1.0
    kr = (nr * lam_re + ai * lam_im) / den
    ki = (ai * lam_re - nr * lam_im) / den
    br = kr[..., None] * b_re - ki[..., None] * b_im
    bi = kr[..., None] * b_im + ki[..., None] * b_re
    B, L, _ = u.shape
    width = B_NGROUPS * B_STATE
    w_b = jnp.concatenate([_block_diag(jnp.swapaxes(br, 1, 2)), _block_diag(jnp.swapaxes(bi, 1, 2))], axis=1)
    w_c = jnp.concatenate([_block_diag(jnp.swapaxes(c_re, 1, 2)), -_block_diag(jnp.swapaxes(c_im, 1, 2))], axis=0)
    a = jnp.concatenate([ar.reshape(8, 128), ai.reshape(8, 128)], axis=0)
    v = linear(u.reshape(B * L, B_WIDTH), w_b.astype(BF16), w_b, name + "_in", BF16)
    x = lin_rec(v.reshape(B, L, 16, 128), a, rev, n_ctx, name)
    y = linear(x.reshape(B * L, 2 * width), w_c.astype(BF16), w_c, name + "_out")
    return y.reshape(B, L, B_WIDTH)


def _hg_consts(rev):
    Q = HG_Q
    pos = _positions(Q, rev)
    pi, pj = pos[:, None], pos[None, :]
    mats = [pi >= pj, pj > pi]
    pairs = []
    s = HG_BAND
    while s < Q:
        second = (pos // s) % 2 == 1
        mid = (pos // (2 * s)) * 2 * s + s
        mats.append(second[:, None] & (pj >= mid[:, None]) & (pj <= pi))
        mats.append((~second)[:, None] & (pj > pi) & (pj < mid[:, None]))
        pairs.append(second[:, None] & (~second)[None, :] & ((pi // (2 * s)) == (pj // (2 * s))))
        s *= 2
    rows = [((pos // s_) % 2 == 1) for s_ in [HG_BAND * 2 ** n for n in range(len(pairs))]]
    band = [(pos % HG_BAND) >= d for d in range(HG_BAND)]
    mats_b = jnp.asarray(np.stack(mats).astype(np.float32), BF16)
    pairs_f = jnp.asarray(np.tile(np.stack(pairs).astype(np.float32), (1, C_HEADS, 1)))
    rows_f = jnp.asarray(np.stack(rows).astype(np.float32)[:, :, None])
    band_f = jnp.asarray(np.stack(band).astype(np.float32)[:, :, None])
    head_lanes = np.arange(C_HEADS)[:, None] == (np.arange(C_WIDTH) // C_KEY)[None, :]
    return mats_b, pairs_f, rows_f, band_f, jnp.asarray(head_lanes.astype(np.float32)[:, None, :])


def _hg_step_fn(rev, n_levels):
    def step(state, seqs, params, consts):
        (st,) = state
        qr, zf, v = seqs
        (lb,) = params
        mats, pairs, rows, band, head_lanes = consts
        W = C_WIDTH
        Q = qr.shape[0]
        same_head = (lax.broadcasted_iota(jnp.int32, (W, W), 0) // C_KEY
                     == lax.broadcasted_iota(jnp.int32, (W, W), 1) // C_KEY)
        q = _silu(qr)
        f = lb + (1.0 - lb) * jax.nn.sigmoid(zf)
        logf = jnp.log(f)
        kk = 1.0 - f
        qd = q * jnp.exp(mdot(mats[0], logf))
        w = kk * jnp.exp(mdot(mats[1], logf))
        total = jnp.sum(logf, axis=0, keepdims=True)
        o = _bdot(qd, st, NT)
        st_new = st * jnp.exp(total) + jnp.where(same_head, _bdot(v, w, TN), 0.0)
        scores = None
        for n in range(n_levels):
            a = q * jnp.exp(mdot(mats[2 + 2 * n], logf)) * rows[n]
            bk = kk * jnp.exp(mdot(mats[3 + 2 * n], logf)) * (1.0 - rows[n])
            sc = pairs[n] * _bdot((a[None] * head_lanes).reshape(C_HEADS * Q, W), bk, NT)
            scores = sc if scores is None else scores + sc
        o = o + jnp.sum(_bdot(scores, v).reshape(C_HEADS, Q, W) * head_lanes, axis=0)
        same_head_b = same_head.astype(BF16)
        e = jnp.zeros_like(logf)
        for d in range(HG_BAND):
            s = -d if rev else d
            if d > 0:
                e = e + sroll(logf, s + (1 if rev else -1))
            kd, vd = (kk, v) if d == 0 else (sroll(kk, s), sroll(v, s))
            p = q * kd * jnp.exp(e)
            o = o + band[d] * (_bdot(p, same_head_b) * vd)
        return (st_new,), (o,)

    return step


def hg_scan(name, rev, qr, zf, v, lower, n_ctx):
    consts = _hg_consts(rev)
    (o,) = chunk_scan(_hg_step_fn(rev, consts[1].shape[0]), name, rev, (qr, zf, v), (False,) * 3,
                      (lower.reshape(1, -1),), consts, [(C_WIDTH, C_WIDTH)], [C_WIDTH], HG_Q, n_ctx)
    return o


def conv_silu(x, w, b, n_ctx, name):
    B, L, C = x.shape
    TC = 128
    w8 = jnp.zeros((8, C), F32).at[:A_CONV].set(w)
    b2 = b.reshape(1, C)
    pad = A_CONV // 2

    def taps(v, sign):
        t = lax.broadcasted_iota(jnp.int32, v.shape, 0)
        out = []
        for k in range(A_CONV):
            s = sign * (k - pad)
            src = t + s
            ok = (src >= 0) & (src < L) & ((t >= n_ctx) == (src >= n_ctx))
            vs = v if s == 0 else pltpu.roll(v, (-s) % L, 0)
            out.append((k, jnp.where(ok, vs, 0.0)))
        return out

    def pre(x_ref, w_ref, b_ref):
        xv = x_ref[0]
        y = b_ref[...] + sum(xs * w_ref[pl.ds(k, 1), :] for k, xs in taps(xv, 1))
        return xv, y

    def fwd_body(x_ref, w_ref, b_ref, o_ref):
        _, y = pre(x_ref, w_ref, b_ref)
        o_ref[0] = _silu(y)

    def bwd_body(x_ref, w_ref, b_ref, g_ref, dx_ref, dw_ref, db_ref):
        xv, y = pre(x_ref, w_ref, b_ref)
        sg = _sigmoid(y)
        dy = g_ref[0] * (sg + y * sg * (1.0 - sg))
        dx_ref[0] = sum(ds * w_ref[pl.ds(k, 1), :] for k, ds in taps(dy, -1))
        first = pl.program_id(1) == 0

        @pl.when(first)
        def _():
            dw_ref[...] = jnp.zeros_like(dw_ref)
            db_ref[...] = jnp.zeros_like(db_ref)

        for k, xs in taps(xv, 1):
            dw_ref[pl.ds(k, 1), :] += jnp.sum(dy * xs, axis=0, keepdims=True)
        db_ref[...] += jnp.sum(dy, axis=0, keepdims=True)

    x_spec = pl.BlockSpec((1, L, TC), lambda j, bb: (bb, 0, j))
    w_spec = pl.BlockSpec((8, TC), lambda j, bb: (0, j))
    b_spec = pl.BlockSpec((1, TC), lambda j, bb: (0, j))

    @jax.custom_vjp
    def op(x, w8, b2):
        return pl.pallas_call(fwd_body, grid=(C // TC, B), in_specs=[x_spec, w_spec, b_spec], out_specs=x_spec,
                              out_shape=jax.ShapeDtypeStruct(x.shape, F32), name=name + "_fwd",
                              compiler_params=_cp("parallel", "parallel"))(x, w8, b2)

    def op_fwd(x, w8, b2):
        return op(x, w8, b2), (x, w8, b2)

    def op_bwd(res, g):
        x, w8, b2 = res
        return tuple(pl.pallas_call(
            bwd_body, grid=(C // TC, B), in_specs=[x_spec, w_spec, b_spec, x_spec],
            out_specs=[x_spec, w_spec, b_spec],
            out_shape=[jax.ShapeDtypeStruct(x.shape, F32), jax.ShapeDtypeStruct(w8.shape, F32),
                       jax.ShapeDtypeStruct(b2.shape, F32)],
            name=name + "_bwd", compiler_params=_cp("parallel", "arbitrary"))(x, w8, b2, g))

    op.defvjp(op_fwd, op_bwd)
    return op(x, w8, b2)


def loss_head(h, target, norm_w, n_ctx):
    B, L, Dm = h.shape
    S = target.shape[1]
    tl = ROW_TILE
    skip = n_ctx // tl

    def body(h_ref, t_ref, w_ref, loss_ref, dh_ref, dw_ref):
        b, i = pl.program_id(0), pl.program_id(1)
        fn = lambda hv, wv: _rms(hv) * wv
        y, vjp = jax.vjp(fn, h_ref[0], w_ref[...])
        err = y - t_ref[0]
        dhv, dwv = vjp(err * (1.0 / Dm))
        dh_ref[0] = dhv
        part = 0.5 * jnp.sum(jnp.mean(err * err, axis=-1, keepdims=True), axis=0, keepdims=True)
        first = (b == 0) & (i == 0)

        @pl.when(first)
        def _():
            loss_ref[...] = jnp.zeros_like(loss_ref)
            dw_ref[...] = jnp.zeros_like(dw_ref)

        loss_ref[...] += jnp.broadcast_to(part, loss_ref.shape)
        dw_ref[...] += dwv

    loss, dh, dw = pl.pallas_call(
        body, grid=(B, S // tl),
        in_specs=[pl.BlockSpec((1, tl, Dm), lambda b, i: (b, i + skip, 0)),
                  pl.BlockSpec((1, tl, Dm), lambda b, i: (b, i, 0)),
                  pl.BlockSpec((1, Dm), lambda b, i: (0, 0))],
        out_specs=[pl.BlockSpec((8, 128), lambda b, i: (0, 0)),
                   pl.BlockSpec((1, tl, Dm), lambda b, i: (b, i, 0)),
                   pl.BlockSpec((1, Dm), lambda b, i: (0, 0))],
        out_shape=[jax.ShapeDtypeStruct((8, 128), F32), jax.ShapeDtypeStruct((B, S, Dm), F32),
                   jax.ShapeDtypeStruct((1, Dm), F32)],
        name="loss_head", compiler_params=_cp("arbitrary", "arbitrary"))(h, target, norm_w.reshape(1, Dm))
    dh_full = jnp.concatenate([jnp.zeros((B, n_ctx, Dm), F32), dh], axis=1)
    return loss[0, 0], dh_full, dw.reshape(Dm)


def _modulate(h, shift, scale):
    return _rms(h) * (1.0 + scale) + shift


def _f_mod(r, m, p):
    return (_modulate(r[0], m[0], m[1]),)


def _f_resid_mod(coef):
    def f(r, m, p):
        h2 = r[0] + coef * m[0] * r[1]
        return h2, _modulate(h2, m[1], m[2])
    return f


def _f_resid(coef):
    def f(r, m, p):
        return (r[0] + coef * m[0] * r[1],)
    return f


def _f_swiglu(r, m, p):
    pre = r[0].astype(F32)
    return (_silu(pre[:, :D_FF]) * pre[:, D_FF:],)


def _f_ssd_post(r, m, p):
    y_f, y_b, xs, z = r
    d_skip, norm_w = p
    y = (y_f + y_b + d_skip * xs) * _silu(z)
    return (_rms(y) * norm_w,)


def _f_s5_post(r, m, p):
    y_f, y_b, u = r
    d_skip, glu_w, glu_b = p
    y = jax.nn.gelu(y_f + y_b + d_skip * u)
    return (y * _sigmoid(_bdot(y, glu_w) + glu_b),)


def _f_hg_post(r, m, p):
    o_f, o_b, g = r
    (norm_w,) = p
    W = C_WIDTH
    same_head = (lax.broadcasted_iota(jnp.int32, (W, W), 0) // C_KEY
                 == lax.broadcasted_iota(jnp.int32, (W, W), 1) // C_KEY).astype(BF16)
    o = o_f + o_b
    ms = xdot(o * o, same_head) * (1.0 / C_KEY)
    return (o * lax.rsqrt(ms + EPS) * norm_w * _silu(g),)


def _pad_w_in(w):
    dt0 = A_INNER + A_CONV_DIM
    zeros = jnp.zeros((w.shape[0], IN_COLS_PAD - IN_COLS), w.dtype)
    return jnp.concatenate([w[:, :dt0], w[:, A_COLS:], w[:, dt0:A_COLS], zeros], axis=1)


def _to_columns(t, n_ctx):
    B, L, W = t.shape
    rows = (L - n_ctx) // GRID_W
    lat = t[:, n_ctx:].reshape(B, rows, GRID_W, W).transpose(0, 2, 1, 3).reshape(B, L - n_ctx, W)
    return jnp.concatenate([t[:, :n_ctx], lat], axis=1)


def _to_raster(t, n_ctx):
    B, L, W = t.shape
    rows = (L - n_ctx) // GRID_W
    lat = t[:, n_ctx:].reshape(B, GRID_W, rows, W).transpose(0, 2, 1, 3).reshape(B, L - n_ctx, W)
    return jnp.concatenate([t[:, :n_ctx], lat], axis=1)


def _forward(h0, c, big, P, n_ctx):
    B, L, Dm = h0.shape
    depth = len(big["mod_w"])
    ncb = n_ctx // ROW_TILE

    def rows_of(t):
        return t.reshape(t.shape[0] * t.shape[1], t.shape[2])

    def w_in_of(t):
        return _pad_w_in(jnp.moveaxis(t, 0, 1).reshape(t.shape[1], IN_COLS))

    p_lb = jax.nn.softmax(P["hg_lb_logits"], axis=0)
    lower_bounds = jnp.cumsum(p_lb, axis=0) - p_lb[:1]
    cc = jnp.concatenate([c, P["c_ctx"][None], jnp.zeros((8 - B - 1, Dm), F32)], axis=0)
    cc = _silu(cc)

    def mods_of(l):
        m = linear(cc, big["mod_w"][l], P["mod_w"][l], f"mod{l}", pieces=True) + P["mod_b"][l]
        m = m.reshape(8, N_MOD, Dm)
        seg = jnp.stack([jnp.broadcast_to(m[B], (B, N_MOD, Dm)), m[:B]], axis=1)
        return [seg[:, :, j:j + 1, :] for j in range(N_MOD)]

    def ffn(u, l, j):
        pre = linear(u.reshape(B * L, Dm), big["ffn_w_in"][l][j], P["ffn_w_in"][l][j], f"ffn_in{l}{j}", BF16,
                     pieces=True)
        (act,) = rowwise(_f_swiglu, f"swiglu{l}{j}", [pre.reshape(B, L, 2 * D_FF)], [], [], [(D_FF, BF16)], ncb)
        out = linear(act.reshape(B * L, D_FF), rows_of(big["ffn_w_out"][l][j]), rows_of(P["ffn_w_out"][l][j]),
                     f"ffn_out{l}{j}")
        return out.reshape(B, L, Dm)

    h = h0
    mods = mods_of(0)
    (u,) = rowwise(_f_mod, "mod_first", [h], [mods[0], mods[1]], [], [(Dm, BF16)], ncb)
    for l in range(depth):
        col_major = l % 2 == 1
        o = ffn(u, l, 0)
        h, u = rowwise(_f_resid_mod(0.5), f"resid_a{l}", [h, o], [mods[2], mods[3], mods[4]], [],
                       [(Dm, F32), (Dm, BF16)], ncb)
        if col_major:
            u = _to_columns(u, n_ctx)
        pre = linear(u.reshape(B * L, Dm), w_in_of(big["w_in"][l]), w_in_of(P["w_in"][l]), f"w_in{l}")
        pre = pre.reshape(B, L, IN_COLS_PAD)
        o0 = 0
        z, o0 = pre[..., o0:o0 + A_INNER], o0 + A_INNER
        xbc, o0 = pre[..., o0:o0 + A_CONV_DIM], o0 + A_CONV_DIM
        pb, o0 = pre[..., o0:o0 + B_WIDTH], o0 + B_WIDTH
        pc, o0 = pre[..., o0:o0 + 5 * C_WIDTH], o0 + 5 * C_WIDTH
        dtr = pre[..., o0:o0 + 2 * A_HEADS]
        xbc = conv_silu(xbc, P["a_conv_w"][l], P["a_conv_b"][l], n_ctx, f"conv{l}")
        xs, bm, cm = xbc[..., :A_INNER], xbc[..., A_INNER:A_INNER + 128], xbc[..., A_INNER + 128:]
        ya_dir = [ssd_scan(f"ssd{l}{d}", bool(d), xs, bm, cm, dtr[..., d * A_HEADS:(d + 1) * A_HEADS],
                           P["a_dt_bias"][l, d], P["a_log"][l, d], n_ctx) for d in range(2)]
        d_skip = jnp.repeat(P["a_d"][l], A_INNER // A_HEADS).reshape(1, A_INNER)
        (ya,) = rowwise(_f_ssd_post, f"ssd_post{l}", [ya_dir[0], ya_dir[1], xs, z], [],
                        [d_skip, P["a_norm_w"][l].reshape(1, -1)], [(A_INNER, BF16)], ncb)
        yb_dir = [s5_scan(f"s5{l}{d}", bool(d), pb, P["s5_lam_re"][l, d], P["s5_lam_im"][l, d],
                          P["s5_log_step"][l, d], P["s5_b_re"][l, d], P["s5_b_im"][l, d], P["s5_c_re"][l, d],
                          P["s5_c_im"][l, d], n_ctx) for d in range(2)]
        (yb,) = rowwise(_f_s5_post, f"s5_post{l}", [yb_dir[0], yb_dir[1], pb], [],
                        [P["s5_d"][l].reshape(1, -1), P["s5_glu_w"][l], P["s5_glu_b"][l].reshape(1, -1)],
                        [(B_WIDTH, BF16)], ncb)
        qr, f_raw, vi, gate = (pc[..., :C_WIDTH], pc[..., C_WIDTH:3 * C_WIDTH], pc[..., 3 * C_WIDTH:4 * C_WIDTH],
                               pc[..., 4 * C_WIDTH:])
        yc_dir = [hg_scan(f"hg{l}{d}", bool(d), qr, f_raw[..., d * C_WIDTH:(d + 1) * C_WIDTH], vi,
                          lower_bounds[l, d], n_ctx) for d in range(2)]
        (yc,) = rowwise(_f_hg_post, f"hg_post{l}", [yc_dir[0], yc_dir[1], gate], [],
                        [P["hg_norm_w"][l].reshape(1, -1)], [(C_WIDTH, BF16)], ncb)
        mix = jnp.concatenate([ya, yb, yc], axis=-1)
        y = linear(mix.reshape(B * L, Dm), rows_of(big["w_out"][l]), rows_of(P["w_out"][l]), f"w_out{l}")
        y = y.reshape(B, L, Dm)
        if col_major:
            y = _to_raster(y, n_ctx)
        h, u = rowwise(_f_resid_mod(1.0), f"resid_b{l}", [h, y], [mods[5], mods[6], mods[7]], [],
                       [(Dm, F32), (Dm, BF16)], ncb)
        o = ffn(u, l, 1)
        gate8 = mods[8]
        if l + 1 < depth:
            mods = mods_of(l + 1)
            h, u = rowwise(_f_resid_mod(0.5), f"resid_c{l}", [h, o], [gate8, mods[0], mods[1]], [],
                           [(Dm, F32), (Dm, BF16)], ncb)
        else:
            (h,) = rowwise(_f_resid(0.5), f"resid_c{l}", [h, o], [gate8], [], [(Dm, F32)], ncb)
    return h


BIG = ("mod_w", "ffn_w_in", "ffn_w_out", "w_in", "w_out")


def local_step(x, c, ctx, target, big, small):
    n_ctx = ctx.shape[1]
    h0 = jnp.concatenate([ctx, x], axis=1)
    P = dict(small)
    for k in BIG:
        P[k] = jax.tree.map(lambda t: jnp.zeros(t.shape, BF16), big[k])
    h, vjp = jax.vjp(lambda h0, P: _forward(h0, c, big, P, n_ctx), h0, P)
    loss, dh, d_final = loss_head(h, target, small["final_norm_w"], n_ctx)
    dh0, grads = vjp(dh)
    grads = dict(grads)
    grads["final_norm_w"] = grads["final_norm_w"] + d_final
    return loss, dh0[:, n_ctx:], grads


MESH = pl.DeviceIdType.MESH
LANES = 1024
N_CHIPS = 4
ELEMENTWISE_BLOCK_BYTES = 2 * 1024 * 1024
ANY = pl.BlockSpec(memory_space=pl.ANY)


def _place():
    x, y, c = lax.axis_index("x"), lax.axis_index("y"), lax.axis_index("c")
    return x, y, c, 2 * x + y


def _other_chips(x, y):
    return [(x ^ kx, y ^ ky, 2 * (x ^ kx) + (y ^ ky)) for kx, ky in ((0, 1), (1, 0), (1, 1))]


def all_gather_shards(shard, name):
    def body(x_ref, out_ref, send_sems, recv_sems, local_sem):
        x, y, c, q = _place()
        sibling = (x, y, 1 - c)
        chips = _other_chips(x, y)

        def copy(k, src, dst, to):
            return pltpu.make_async_remote_copy(src_ref=src, dst_ref=dst, send_sem=send_sems.at[k],
                                                recv_sem=recv_sems.at[k], device_id=to, device_id_type=MESH)

        mine = pltpu.make_async_copy(x_ref, out_ref.at[q], local_sem)
        mine.start()
        first = [copy(k, x_ref.at[c], out_ref.at[q, c], (px, py, c)) for k, (px, py, _) in enumerate(chips)]
        for cp in first:
            cp.start()
        passed = [copy(3 + k, out_ref.at[pq, c], out_ref.at[pq, c], sibling) for k, (_, _, pq) in enumerate(chips)]
        for k, (_, _, pq) in enumerate(chips):
            copy(k, x_ref.at[c], out_ref.at[pq, c], sibling).wait_recv()
            passed[k].start()
        for k, (_, _, pq) in enumerate(chips):
            copy(3 + k, x_ref.at[c], out_ref.at[pq, 1 - c], sibling).wait_recv()
        for cp in first + passed:
            cp.wait_send()
        mine.wait()

    return pl.pallas_call(
        body, out_shape=jax.ShapeDtypeStruct((N_CHIPS,) + shard.shape, shard.dtype), in_specs=[ANY], out_specs=ANY,
        scratch_shapes=[pltpu.SemaphoreType.DMA((6,)), pltpu.SemaphoreType.DMA((6,)), pltpu.SemaphoreType.DMA],
        name=name)(shard)


def all_gather_devices(part, name):
    def body(x_ref, out_ref, send_sems, recv_sems, local_sem):
        x, y, c, _ = _place()
        me = 4 * x + 2 * y + c
        mine = pltpu.make_async_copy(x_ref, out_ref.at[me], local_sem)
        mine.start()
        copies = []
        for k in range(1, 8):
            px, py, pc = x ^ (k >> 2), y ^ ((k >> 1) & 1), c ^ (k & 1)
            copies.append(pltpu.make_async_remote_copy(
                src_ref=x_ref, dst_ref=out_ref.at[me], send_sem=send_sems.at[k - 1], recv_sem=recv_sems.at[k - 1],
                device_id=(px, py, pc), device_id_type=MESH))
            copies[-1].start()
        for k in range(1, 8):
            peer = 4 * (x ^ (k >> 2)) + 2 * (y ^ ((k >> 1) & 1)) + (c ^ (k & 1))
            pltpu.make_async_remote_copy(
                src_ref=x_ref, dst_ref=out_ref.at[peer], send_sem=send_sems.at[k - 1], recv_sem=recv_sems.at[k - 1],
                device_id=(x, y, c), device_id_type=MESH).wait_recv()
        for cp in copies:
            cp.wait_send()
        mine.wait()

    return pl.pallas_call(
        body, out_shape=jax.ShapeDtypeStruct((8,) + part.shape, part.dtype), in_specs=[ANY], out_specs=ANY,
        scratch_shapes=[pltpu.SemaphoreType.DMA((7,)), pltpu.SemaphoreType.DMA((7,)), pltpu.SemaphoreType.DMA],
        name=name)(part)


def _half(ref, c, axis):
    h = ref.shape[axis] // 2
    return ref.at[(slice(None),) * axis + (pl.ds(pl.multiple_of(c * h, 8), h),)]


def _remote(src, dst, send_sems, recv_sems, s, to):
    return pltpu.make_async_remote_copy(src_ref=src, dst_ref=dst, send_sem=send_sems.at[s], recv_sem=recv_sems.at[s],
                                        device_id=to, device_id_type=MESH)


def gather_pieces(arrays, name):
    n = len(arrays)

    def body(*refs):
        ins, outs, (send_sems, recv_sems) = refs[:n], refs[n:2 * n], refs[2 * n:]
        x, y, c, q = _place()
        sibling = (x, y, 1 - c)
        chips = _other_chips(x, y)
        started = []

        def go(src, dst, s, to):
            started.append(_remote(src, dst, send_sems, recv_sems, s, to))
            started[-1].start()

        for i, (a, o) in enumerate(zip(ins, outs)):
            go(a, o.at[q], 7 * i + 6, sibling)
            for k, (px, py, _) in enumerate(chips):
                go(_half(a, c, 0), _half(o.at[q], c, 0), 7 * i + k, (px, py, c))
        for i, o in enumerate(outs):
            for k, (_, _, pq) in enumerate(chips):
                land = _half(o.at[pq], c, 0)
                _remote(land, land, send_sems, recv_sems, 7 * i + k, sibling).wait_recv()
                go(land, land, 7 * i + 3 + k, sibling)
        for i, (a, o) in enumerate(zip(ins, outs)):
            for k, (_, _, pq) in enumerate(chips):
                land = _half(o.at[pq], 1 - c, 0)
                _remote(land, land, send_sems, recv_sems, 7 * i + 3 + k, sibling).wait_recv()
            _remote(a, o.at[q], send_sems, recv_sems, 7 * i + 6, sibling).wait_recv()
        for cp in started:
            cp.wait_send()

    return pl.pallas_call(
        body, out_shape=[jax.ShapeDtypeStruct((N_CHIPS,) + a.shape, a.dtype) for a in arrays], in_specs=[ANY] * n,
        out_specs=[ANY] * n, scratch_shapes=[pltpu.SemaphoreType.DMA((7 * n,)), pltpu.SemaphoreType.DMA((7 * n,))],
        name=name)(*arrays)


def swap_with_sibling(arrays, name, send_other_half_of_axis=None):
    n = len(arrays)
    ax = send_other_half_of_axis

    def body(*refs):
        ins, outs, (send_sems, recv_sems) = refs[:n], refs[n:2 * n], refs[2 * n:]
        x, y, c, _ = _place()
        copies = [_remote(a if ax is None else _half(a, 1 - c, ax), o, send_sems, recv_sems, i, (x, y, 1 - c))
                  for i, (a, o) in enumerate(zip(ins, outs))]
        for cp in copies:
            cp.start()
        for cp in copies:
            cp.wait()

    def out_of(a):
        shape = list(a.shape)
        if ax is not None:
            shape[ax] //= 2
        return jax.ShapeDtypeStruct(tuple(shape), a.dtype)

    return pl.pallas_call(
        body, out_shape=[out_of(a) for a in arrays], in_specs=[ANY] * n, out_specs=[ANY] * n,
        scratch_shapes=[pltpu.SemaphoreType.DMA((n,)), pltpu.SemaphoreType.DMA((n,))], name=name)(*arrays)


def pieces_to_chips(pairs, name):
    n = len(pairs)

    def body(*refs):
        ins, outs, (send_sems, recv_sems) = refs[:n], refs[n:2 * n], refs[2 * n:]
        x, y, c, q = _place()
        chips = _other_chips(x, y)
        copies = [_remote(a.at[pq], o.at[q], send_sems, recv_sems, 3 * i + k, (px, py, c))
                  for i, (a, o) in enumerate(zip(ins, outs)) for k, (px, py, pq) in enumerate(chips)]
        for cp in copies:
            cp.start()
        for i, (a, o) in enumerate(zip(ins, outs)):
            for k, (_, _, pq) in enumerate(chips):
                _remote(a.at[pq], o.at[pq], send_sems, recv_sems, 3 * i + k, (x, y, c)).wait_recv()
        for cp in copies:
            cp.wait_send()

    return pl.pallas_call(
        body, out_shape=[jax.ShapeDtypeStruct(a.shape, a.dtype) for a in pairs], in_specs=[ANY] * n,
        out_specs=[ANY] * n, scratch_shapes=[pltpu.SemaphoreType.DMA((3 * n,)), pltpu.SemaphoreType.DMA((3 * n,))],
        name=name)(*pairs)


def _rows_block(h, n_cols):
    return _tile(h, 16, max(16, ELEMENTWISE_BLOCK_BYTES // (n_cols * 4)))


def add_pair(core, g, got, name):
    _, K, N = g.shape
    h = K // 2
    th = _rows_block(h, N)
    nb = h // th

    def body(c_ref, g_ref, got_ref, o_ref):
        o_ref[...] = (g_ref[...].astype(F32) + got_ref[...].astype(F32)).astype(o_ref.dtype)

    grid_spec = pltpu.PrefetchScalarGridSpec(
        num_scalar_prefetch=1, grid=(N_CHIPS, nb),
        in_specs=[pl.BlockSpec((None, th, N), lambda p, i, c_ref: (p, c_ref[0] * nb + i, 0)),
                  pl.BlockSpec((None, th, N), lambda p, i, c_ref: (p, i, 0))],
        out_specs=pl.BlockSpec((None, th, N), lambda p, i, c_ref: (p, i, 0)))
    return pl.pallas_call(body, grid_spec=grid_spec, out_shape=jax.ShapeDtypeStruct(got.shape, g.dtype), name=name,
                          compiler_params=_cp("parallel", "parallel"))(core, g, got)


def sum_pieces(chip, pair, got, name):
    _, h, N = pair.shape
    th = _rows_block(h, N)

    def body(q_ref, pair_ref, *rest):
        got_refs, o_ref = rest[:N_CHIPS], rest[N_CHIPS]
        q = q_ref[0]
        for p in range(N_CHIPS):
            def put(val, p=p):
                if p == 0:
                    o_ref[...] = val
                else:
                    o_ref[...] += val

            @pl.when(q == p)
            def _():
                put(pair_ref[...].astype(F32))

            @pl.when(q != p)
            def _(p=p):
                put(got_refs[p][...].astype(F32))

    def got_spec(p):
        return pl.BlockSpec((None, th, N), lambda i, q_ref: (jnp.where(q_ref[0] == p, (p + 1) % N_CHIPS, p), i, 0))

    grid_spec = pltpu.PrefetchScalarGridSpec(
        num_scalar_prefetch=1, grid=(h // th,),
        in_specs=[pl.BlockSpec((None, th, N), lambda i, q_ref: (q_ref[0], i, 0))] + [got_spec(p) for p in range(N_CHIPS)],
        out_specs=pl.BlockSpec((th, N), lambda i, q_ref: (i, 0)))
    return pl.pallas_call(body, grid_spec=grid_spec, out_shape=jax.ShapeDtypeStruct((h, N), F32), name=name,
                          compiler_params=_cp("parallel"))(chip, pair, *([got] * N_CHIPS))


def adamw_halves(core, mine, other, w, m, v, name):
    n, K, N = w.shape
    h = K // 2
    th = _rows_block(h, N)
    nb = h // th

    def body(c_ref, *refs):
        mine_refs, other_refs = refs[:n], refs[n:2 * n]
        w_ref, m_ref, v_ref, g_ref, d_ref, m2_ref, v2_ref = refs[2 * n:]
        a, s, c = pl.program_id(0), pl.program_id(1), c_ref[0]
        for idx in range(n):
            @pl.when((a == idx) & (s == c))
            def _(idx=idx):
                g_ref[...] = mine_refs[idx][...]

            @pl.when((a == idx) & (s != c))
            def _(idx=idx):
                g_ref[...] = other_refs[idx][...]
        _adamw_math(g_ref[...], w_ref, m_ref, v_ref, d_ref, m2_ref, v2_ref)

    def half_spec(idx, is_mine):
        def index(a, s, i, c_ref):
            right_half = (s == c_ref[0]) if is_mine else (s != c_ref[0])
            return (jnp.where((a == idx) & right_half, i, 0), 0)
        return pl.BlockSpec((th, N), index)

    full = pl.BlockSpec((None, th, N), lambda a, s, i, c_ref: (a, s * nb + i, 0))
    grid_spec = pltpu.PrefetchScalarGridSpec(
        num_scalar_prefetch=1, grid=(n, 2, nb),
        in_specs=[half_spec(idx, True) for idx in range(n)] + [half_spec(idx, False) for idx in range(n)] + [full] * 3,
        out_specs=[full] * 4)
    return pl.pallas_call(body, grid_spec=grid_spec, out_shape=[jax.ShapeDtypeStruct((n, K, N), F32)] * 4, name=name,
                          compiler_params=_cp("arbitrary", "arbitrary", "arbitrary"))(core, *mine, *other, w, m, v)


def _adamw_math(gv, w_ref, m_ref, v_ref, d_ref, m2_ref, v2_ref):
    m2 = ADAM_B1 * m_ref[...] + (1.0 - ADAM_B1) * gv
    v2 = ADAM_B2 * v_ref[...] + (1.0 - ADAM_B2) * (gv * gv)
    m_hat = m2 / (1.0 - ADAM_B1 ** ADAM_STEP)
    v_hat = v2 / (1.0 - ADAM_B2 ** ADAM_STEP)
    d_ref[...] = -ADAM_LR * (m_hat / (jnp.sqrt(v_hat) + ADAM_EPS) + ADAM_WD * w_ref[...])
    m2_ref[...] = m2
    v2_ref[...] = v2


def _row_tile(R):
    return _pick(R, (512, 256, 128, 64, 32, 16, 8))


def sum_parts(parts, name):
    P, R, _ = parts.shape
    rt = _row_tile(R)

    def body(p_ref, o_ref):
        acc = p_ref[0]
        for p in range(1, P):
            acc = acc + p_ref[p]
        o_ref[...] = acc

    return pl.pallas_call(
        body, grid=(R // rt,), in_specs=[pl.BlockSpec((P, rt, LANES), lambda i: (0, i, 0))],
        out_specs=pl.BlockSpec((rt, LANES), lambda i: (i, 0)), out_shape=jax.ShapeDtypeStruct((R, LANES), F32),
        name=name, compiler_params=_cp("parallel"))(parts)


def adamw(g, w, m, v, name):
    R = g.shape[0]
    rt = _row_tile(R)

    def body(g_ref, w_ref, m_ref, v_ref, d_ref, m2_ref, v2_ref):
        _adamw_math(g_ref[...], w_ref, m_ref, v_ref, d_ref, m2_ref, v2_ref)

    spec = pl.BlockSpec((rt, LANES), lambda i: (i, 0))
    return pl.pallas_call(
        body, grid=(R // rt,), in_specs=[spec] * 4, out_specs=[spec] * 3,
        out_shape=[jax.ShapeDtypeStruct((R, LANES), F32)] * 3, name=name, compiler_params=_cp("parallel"))(g, w, m, v)


def _pack(arrays, rows_multiple, dtype):
    flat = jnp.concatenate([a.reshape(-1).astype(dtype) for a in arrays])
    n = flat.shape[0]
    per = rows_multiple * LANES
    total = -(-n // per) * per
    return jnp.pad(flat, (0, total - n)).reshape(total // LANES, LANES)


def _unpack(buf, shapes):
    flat = buf.reshape(-1)
    out, off = [], 0
    for s in shapes:
        n = math.prod(s)
        out.append(flat[off:off + n].reshape(s))
        off += n
    return out


SHARDED = (("mod_w", 2), ("ffn_w_in", 3), ("ffn_w_out", 2), ("w_in", 2), ("w_out", 1),
           ("a_conv_w", 2), ("s5_glu_w", 1), ("hg_lb_logits", 2))
WEIGHTS = ("c_ctx", "mod_w", "mod_b", "ffn_w_in", "ffn_w_out", "w_in", "w_out", "a_conv_w", "a_conv_b", "a_dt_bias",
           "a_log", "a_d", "a_norm_w", "s5_lam_re", "s5_lam_im", "s5_log_step", "s5_b_re", "s5_b_im", "s5_c_re",
           "s5_c_im", "s5_d", "s5_glu_w", "s5_glu_b", "hg_lb_logits", "hg_norm_w", "final_norm_w")


def _gather_full(local, names_axes, dtype, rows_multiple, name):
    buf = _pack([local[n] for n, _ in names_axes], 2 * rows_multiple, dtype)
    R = buf.shape[0] // 2
    full = all_gather_shards(buf.reshape(2, R, LANES), name).reshape(N_CHIPS, 2 * R, LANES)
    per_chip = [_unpack(full[q], [local[n].shape for n, _ in names_axes]) for q in range(N_CHIPS)]
    return {n: jnp.concatenate([per_chip[q][j] for q in range(N_CHIPS)], axis=ax)
            for j, (n, ax) in enumerate(names_axes)}


def kernel(x, c, ctx, c_ctx, mod_w, mod_b, ffn_w_in, ffn_w_out, w_in, w_out, a_conv_w, a_conv_b, a_dt_bias, a_log, a_d, a_norm_w, s5_lam_re, s5_lam_im, s5_log_step, s5_b_re, s5_b_im, s5_c_re, s5_c_im, s5_d, s5_glu_w, s5_glu_b, hg_lb_logits, hg_norm_w, final_norm_w, loss_target, m_c_ctx, m_mod_w, m_mod_b, m_ffn_w_in, m_ffn_w_out, m_w_in, m_w_out, m_a_conv_w, m_a_conv_b, m_a_dt_bias, m_a_log, m_a_d, m_a_norm_w, m_s5_lam_re, m_s5_lam_im, m_s5_log_step, m_s5_b_re, m_s5_b_im, m_s5_c_re, m_s5_c_im, m_s5_d, m_s5_glu_w, m_s5_glu_b, m_hg_lb_logits, m_hg_norm_w, m_final_norm_w, v_c_ctx, v_mod_w, v_mod_b, v_ffn_w_in, v_ffn_w_out, v_w_in, v_w_out, v_a_conv_w, v_a_conv_b, v_a_dt_bias, v_a_log, v_a_d, v_a_norm_w, v_s5_lam_re, v_s5_lam_im, v_s5_log_step, v_s5_b_re, v_s5_b_im, v_s5_c_re, v_s5_c_im, v_s5_d, v_s5_glu_w, v_s5_glu_b, v_hg_lb_logits, v_hg_norm_w, v_final_norm_w):
    given = dict(locals())
    w = {n: given[n] for n in WEIGHTS}
    m = {n: given["m_" + n] for n in WEIGHTS}
    v = {n: given["v_" + n] for n in WEIGHTS}
    small_sharded = SHARDED[len(BIG):]
    replicated = [n for n in WEIGHTS if n not in [s for s, _ in SHARDED]]
    core = lax.axis_index("c").astype(jnp.int32).reshape(1)
    chip = (2 * lax.axis_index("x") + lax.axis_index("y")).astype(jnp.int32)

    def leaves(t):
        return list(t.reshape((-1,) + t.shape[-2:]))

    def nest(kind, flat):
        flat = list(flat)
        return [flat[2 * l:2 * l + 2] for l in range(len(flat) // 2)] if w[kind].ndim == 4 else flat

    counts = [len(leaves(w[k])) for k in BIG]
    gathered = gather_pieces([a.astype(BF16) for k in BIG for a in leaves(w[k])], "gather_big")
    big, off = {}, 0
    for k, cnt in zip(BIG, counts):
        big[k], off = nest(k, gathered[off:off + cnt]), off + cnt
    small = {n: w[n] for n in replicated}
    small.update(_gather_full(w, small_sharded, F32, 8, "gather_small"))

    loss, grad_x, grads = local_step(x, c, ctx, loss_target, big, small)

    g_big = [g for k in BIG for g in jax.tree.leaves(grads[k])]
    got = swap_with_sibling(g_big, "rs_sibling", send_other_half_of_axis=1)
    pairs = [add_pair(core, g, t, f"rs_add_pair{i}") for i, (g, t) in enumerate(zip(g_big, got))]
    from_chips = pieces_to_chips(pairs, "rs_chips")
    mine = [sum_pieces(chip.reshape(1), p, t, f"rs_sum_chips{i}") for i, (p, t) in enumerate(zip(pairs, from_chips))]
    other = swap_with_sibling(mine, "rs_halves")
    out, off = {}, 0
    for k, cnt in zip(BIG, counts):
        stack = lambda t: t.reshape((cnt,) + t.shape[-2:])
        res = adamw_halves(core, mine[off:off + cnt], other[off:off + cnt], stack(w[k]), stack(m[k]), stack(v[k]),
                           "adamw_" + k)
        off += cnt
        for kind, a in zip(("grad", "delta", "new_m", "new_v"), res):
            out[kind, k] = a.reshape(w[k].shape)

    names = replicated + [n for n, _ in small_sharded]
    part = _pack([grads[n] for n in names] + [loss.reshape(1)], 8, F32)
    g_all = _unpack(sum_parts(all_gather_devices(part, "gather_parts"), "sum_parts"),
                    [grads[n].shape for n in names] + [(1,)])
    g_of = dict(zip(names + ["loss"], g_all))
    for n, ax in small_sharded:
        g_of[n] = lax.dynamic_slice_in_dim(g_of[n], chip * w[n].shape[ax], w[n].shape[ax], axis=ax)
    rest = lambda d: _pack([d[n] for n in names], 8, F32)
    g_rest = rest(g_of)
    d_rest, m_rest, v_rest = adamw(g_rest, rest(w), rest(m), rest(v), "adamw_rest")
    shapes = [w[n].shape for n in names]
    for kind, buf in (("grad", g_rest), ("delta", d_rest), ("new_m", m_rest), ("new_v", v_rest)):
        for n, a in zip(names, _unpack(buf, shapes)):
            out[kind, n] = a

    loss_total = g_of["loss"].reshape(())
    return (loss_total, grad_x, *[out["grad", n] for n in WEIGHTS], *[out["delta", n] for n in WEIGHTS],
            *[out["new_m", n] for n in WEIGHTS], *[out["new_v", n] for n in WEIGHTS])
```

```python
import functools
import math

import numpy as np
import jax
import jax.numpy as jnp
from jax import lax
from jax.experimental import pallas as pl
from jax.experimental.pallas import tpu as pltpu

F32, BF16 = jnp.float32, jnp.bfloat16
EPS = 1e-6
N_MOD = 9
D_FF = 2816
A_INNER, A_HEADS, A_GROUPS, A_STATE, A_CONV, A_CONV_DIM = 512, 8, 2, 64, 5, 768
A_COLS = A_INNER + A_CONV_DIM + 2 * A_HEADS
B_WIDTH, B_GROUP, B_NGROUPS, B_STATE = 256, 16, 16, 64
C_WIDTH, C_HEADS, C_KEY = 256, 4, 64
GRID_W = 64
IN_COLS = A_COLS + B_WIDTH + 5 * C_WIDTH
IN_COLS_PAD = 3072
ADAM_LR, ADAM_B1, ADAM_B2, ADAM_EPS, ADAM_WD, ADAM_STEP = 0.001, 0.9, 0.999, 1e-08, 0.01, 10

ROW_TILE = 256
SSD_Q, HG_Q = 128, 64
REC_T = 128
SCAN_SAMPLES = 4
HG_BAND = 8
VMEM_LIMIT = 56 * 1024 * 1024
MM_VMEM_BUDGET = 36 * 1024 * 1024

NT = ((1,), (1,))
TN = ((0,), (0,))


def _bdot(a, b, dims=((1,), (0,))):
    return lax.dot_general(a.astype(BF16), b.astype(BF16), (dims, ((), ())), preferred_element_type=F32)


def _split3(x):
    hi = x.astype(BF16)
    r = x - hi.astype(F32)
    mid = r.astype(BF16)
    lo = (r - mid.astype(F32)).astype(BF16)
    return hi, mid, lo


def _mask_lhs(m, x, dims):
    return sum(lax.dot_general(m, p, (dims, ((), ())), preferred_element_type=F32) for p in _split3(x))


def _mask_rhs(x, m, dims):
    return sum(lax.dot_general(p, m, (dims, ((), ())), preferred_element_type=F32) for p in _split3(x))


@jax.custom_vjp
def mdot(m, x):
    return _mask_lhs(m, x, ((1,), (0,)))


def _mdot_fwd(m, x):
    return mdot(m, x), m


def _mdot_bwd(m, g):
    return jnp.zeros_like(m), _mask_lhs(m, g, TN)


mdot.defvjp(_mdot_fwd, _mdot_bwd)


@jax.custom_vjp
def xdot(x, m):
    return _mask_rhs(x, m, ((1,), (0,)))


def _xdot_fwd(x, m):
    return xdot(x, m), m


def _xdot_bwd(m, g):
    return _mask_rhs(g, m, NT), jnp.zeros_like(m)


xdot.defvjp(_xdot_fwd, _xdot_bwd)


@functools.partial(jax.custom_vjp, nondiff_argnums=(1,))
def _roll(x, s):
    return pltpu.roll(x, s, 0)


def _roll_fwd(x, s):
    return _roll(x, s), None


def _roll_bwd(s, _, g):
    return (_roll(g, x_rows(g) - s),)


def x_rows(x):
    return x.shape[0]


_roll.defvjp(_roll_fwd, _roll_bwd)


def sroll(x, s):
    s = s % x.shape[0]
    return x if s == 0 else _roll(x, s)


def _softplus(x):
    return jnp.maximum(x, 0.0) + jnp.log1p(jnp.exp(jnp.minimum(x, -x)))


def _sigmoid(x):
    return 0.5 * (jnp.tanh(0.5 * x) + 1.0)


def _silu(x):
    return x * _sigmoid(x)


def _rms(x):
    return x * lax.rsqrt(jnp.mean(x * x, axis=-1, keepdims=True) + EPS)


def _pick(n, cands):
    for c in cands:
        if n % c == 0:
            return c
    return n


def _cp(*sem):
    return pltpu.CompilerParams(dimension_semantics=sem, vmem_limit_bytes=VMEM_LIMIT)


def _tile(n, unit, cap):
    best = None
    for d in range(unit, min(n, cap) + 1, unit):
        if n % d == 0:
            best = d
    return best if best is not None else n


def _mm(a, b, ta, tb, out_dtype, name, b_pieces=False, out_pieces=False):
    M, K = (a.shape[1], a.shape[0]) if ta else a.shape
    n_div = k_div = None
    if b_pieces and tb:
        N, k_div = b.shape[1], b.shape[2]
    elif b_pieces:
        N, n_div = N_CHIPS * b.shape[2], b.shape[2]
    else:
        N = b.shape[0] if tb else b.shape[1]
    if out_pieces:
        n_div = N // N_CHIPS
    n_div, k_div = n_div or N, k_div or K
    tm = _tile(M, 128 if ta else 8, 1536 if ta else 1024)
    tn = _tile(n_div, 128, 1536)
    a_bytes, b_bytes, o_bytes = a.dtype.itemsize, b.dtype.itemsize, jnp.dtype(out_dtype).itemsize
    cands = {d for d in range(128, min(k_div, 2816) + 1, 128) if k_div % d == 0}
    if k_div <= 2816 or not cands:
        cands.add(k_div)
    for tk in sorted(cands, reverse=True):
        scratch = tm * tn * 4 if (K // tk > 1 and o_bytes != 4) else 0
        if 2 * (tm * tk * a_bytes + tk * tn * b_bytes + tm * tn * o_bytes) + scratch <= MM_VMEM_BUDGET:
            break
    nk = K // tk
    nq_n, nq_k = n_div // tn, k_div // tk
    dims = ((0 if ta else 1,), (1 if tb else 0,))

    def body(a_ref, b_ref, o_ref, *acc):
        part = _bdot(a_ref[...], b_ref[...], dims)
        if nk == 1:
            o_ref[...] = part.astype(o_ref.dtype)
            return
        acc_ref = acc[0] if acc else o_ref
        k = pl.program_id(2)

        @pl.when(k == 0)
        def _():
            acc_ref[...] = part

        @pl.when(k > 0)
        def _():
            acc_ref[...] += part

        if acc:
            @pl.when(k == nk - 1)
            def _():
                o_ref[...] = acc_ref[...].astype(o_ref.dtype)

    a_spec = pl.BlockSpec((tk, tm), lambda i, j, k: (k, i)) if ta else pl.BlockSpec((tm, tk), lambda i, j, k: (i, k))
    if b_pieces and tb:
        b_spec = pl.BlockSpec((None, tn, tk), lambda i, j, k: (k // nq_k, j, k % nq_k))
    elif b_pieces:
        b_spec = pl.BlockSpec((None, tk, tn), lambda i, j, k: (j // nq_n, k, j % nq_n))
    elif tb:
        b_spec = pl.BlockSpec((tn, tk), lambda i, j, k: (j, k))
    else:
        b_spec = pl.BlockSpec((tk, tn), lambda i, j, k: (k, j))
    if out_pieces:
        o_spec = pl.BlockSpec((None, tm, tn), lambda i, j, k: (j // nq_n, i, j % nq_n))
        o_shape = jax.ShapeDtypeStruct((N_CHIPS, M, N // N_CHIPS), out_dtype)
    else:
        o_spec = pl.BlockSpec((tm, tn), lambda i, j, k: (i, j))
        o_shape = jax.ShapeDtypeStruct((M, N), out_dtype)
    return pl.pallas_call(
        body, grid=(M // tm, N // tn, nk), in_specs=[a_spec, b_spec], out_specs=o_spec, out_shape=o_shape,
        scratch_shapes=[pltpu.VMEM((tm, tn), F32)] if (nk > 1 and o_bytes != 4) else [], name=name,
        compiler_params=_cp("parallel", "parallel", "arbitrary"))(a, b)


def linear(x, w16, wslot, name, out_dtype=F32, pieces=False):
    @jax.custom_vjp
    def f(x, w16, wslot):
        return _mm(x, w16, False, False, out_dtype, name + "_fwd", b_pieces=pieces)

    def fwd(x, w16, wslot):
        return f(x, w16, wslot), (x, w16)

    grad_dtype = wslot.dtype

    def bwd(res, dy):
        x, w16 = res
        dx = _mm(dy, w16, False, True, x.dtype, name + "_dx", b_pieces=pieces)
        dw = _mm(x, dy, True, False, grad_dtype, name + "_dw", out_pieces=pieces)
        return dx, jnp.zeros_like(w16), dw

    f.defvjp(fwd, bwd)
    return f(x, w16, wslot)


def rowwise(f, name, rows, mods, params, outs, n_ctx_blocks, tl=ROW_TILE, row_diff=None):
    rows, mods, params = tuple(rows), tuple(mods), tuple(params)
    nr, nm, npar, no = len(rows), len(mods), len(params), len(outs)
    B, L = rows[0].shape[:2]
    nblk = L // tl
    row_diff = tuple(row_diff) if row_diff is not None else (True,) * nr
    out_dtypes = [dt for _, dt in outs]

    def fcast(r, m, p):
        return tuple(o.astype(dt) for o, dt in zip(f(r, m, p), out_dtypes))

    def seg(i):
        return (i >= n_ctx_blocks).astype(jnp.int32) if n_ctx_blocks else 0

    def specs():
        row_specs = [pl.BlockSpec((1, tl, r.shape[2]), lambda b, i: (b, i, 0)) for r in rows]
        mod_specs = [pl.BlockSpec((1, 1, 1, m.shape[3]), lambda b, i: (b, seg(i), 0, 0)) for m in mods]
        par_specs = [pl.BlockSpec(p.shape, lambda b, i: (0, 0)) for p in params]
        out_specs = [pl.BlockSpec((1, tl, w), lambda b, i: (b, i, 0)) for w, _ in outs]
        return row_specs, mod_specs, par_specs, out_specs

    def load(ins):
        r = tuple(x[0] for x in ins[:nr])
        m = tuple(x[0, 0] for x in ins[nr:nr + nm])
        p = tuple(x[...] for x in ins[nr + nm:])
        return r, m, p

    def fwd_call(rows, mods, params):
        def body(*refs):
            r, m, p = load(refs[:nr + nm + npar])
            for o_ref, o in zip(refs[nr + nm + npar:], fcast(r, m, p)):
                o_ref[0] = o

        rs, ms, ps, os_ = specs()
        return pl.pallas_call(
            body, grid=(B, nblk), in_specs=rs + ms + ps, out_specs=os_,
            out_shape=[jax.ShapeDtypeStruct((B, L, w), dt) for w, dt in outs],
            name=name + "_fwd", compiler_params=_cp("parallel", "parallel"))(*rows, *mods, *params)

    def bwd_call(rows, mods, params, douts):
        didx = [j for j in range(nr) if row_diff[j]]

        def body(*refs):
            n_in = nr + nm + npar
            r, m, p = load(refs[:n_in])
            dins = refs[n_in:n_in + no]
            rest = refs[n_in + no:]
            dr_refs, dm_refs, dp_refs = rest[:len(didx)], rest[len(didx):len(didx) + nm], rest[len(didx) + nm:]
            _, vjp = jax.vjp(fcast, r, m, p)
            dr, dm, dp = vjp(tuple(d[0] for d in dins))
            for ref, j in zip(dr_refs, didx):
                ref[0] = dr[j].astype(ref.dtype)
            b, i = pl.program_id(0), pl.program_id(1)
            first_m = (i == 0) | (i == n_ctx_blocks) if n_ctx_blocks else (i == 0)
            first_p = (b == 0) & (i == 0)
            for ref, g in zip(dm_refs, dm):
                @pl.when(first_m)
                def _(ref=ref, g=g):
                    ref[0, 0] = g

                @pl.when(jnp.logical_not(first_m))
                def _(ref=ref, g=g):
                    ref[0, 0] += g
            for ref, g in zip(dp_refs, dp):
                @pl.when(first_p)
                def _(ref=ref, g=g):
                    ref[...] = g

                @pl.when(jnp.logical_not(first_p))
                def _(ref=ref, g=g):
                    ref[...] += g

        rs, ms, ps, os_ = specs()
        out_shape = ([jax.ShapeDtypeStruct(rows[j].shape, rows[j].dtype) for j in didx]
                     + [jax.ShapeDtypeStruct(m.shape, F32) for m in mods]
                     + [jax.ShapeDtypeStruct(p.shape, F32) for p in params])
        res = pl.pallas_call(
            body, grid=(B, nblk), in_specs=rs + ms + ps + os_,
            out_specs=[rs[j] for j in didx] + ms + ps, out_shape=out_shape,
            name=name + "_bwd", compiler_params=_cp("arbitrary", "arbitrary"))(*rows, *mods, *params, *douts)
        dr = [None] * nr
        for j, g in zip(didx, res[:len(didx)]):
            dr[j] = g
        dr = tuple(g if g is not None else jnp.zeros_like(rows[j]) for j, g in enumerate(dr))
        return dr, tuple(res[len(didx):len(didx) + nm]), tuple(res[len(didx) + nm:])

    @jax.custom_vjp
    def op(rows, mods, params):
        return tuple(fwd_call(rows, mods, params))

    def op_fwd(rows, mods, params):
        return op(rows, mods, params), (rows, mods, params)

    def op_bwd(res, douts):
        return bwd_call(*res, tuple(douts))

    op.defvjp(op_fwd, op_bwd)
    return op(rows, mods, params)


def chunk_scan(step, name, rev, seqs, seq_t, params, consts, state_shapes, out_widths, Q, n_ctx):
    seqs, params, consts = tuple(seqs), tuple(params), tuple(consts)
    ns, npar, nc, nst, no = len(seqs), len(params), len(consts), len(state_shapes), len(out_widths)
    B = seqs[0].shape[0]
    L = seqs[0].shape[2] if seq_t[0] else seqs[0].shape[1]
    nck, ncc = L // Q, n_ctx // Q
    nb = SCAN_SAMPLES if B % SCAN_SAMPLES == 0 else 1

    def chunk_of(k):
        if not rev:
            return k
        return jnp.where(k < ncc, ncc - 1 - k, nck + ncc - 1 - k)

    def specs(order):
        def seq_spec(s, t):
            if t:
                return pl.BlockSpec((nb, s.shape[1], Q), lambda b, k: (b, 0, chunk_of(order(k))))
            return pl.BlockSpec((nb, Q, s.shape[2]), lambda b, k: (b, chunk_of(order(k)), 0))

        seq_specs = [seq_spec(s, t) for s, t in zip(seqs, seq_t)]
        par_specs = [pl.BlockSpec(p.shape, lambda b, k: (0, 0)) for p in params]
        con_specs = [pl.BlockSpec(c.shape, lambda b, k, nd=c.ndim: (0,) * nd) for c in consts]
        out_specs = [pl.BlockSpec((nb, Q, w), lambda b, k: (b, chunk_of(order(k)), 0)) for w in out_widths]
        sav_specs = [pl.BlockSpec((nb, 1) + tuple(s), lambda b, k: (b, order(k), 0, 0)) for s in state_shapes]
        return seq_specs, par_specs, con_specs, out_specs, sav_specs

    def fwd_call(seqs, params):
        def body(*refs):
            seq_refs = refs[:ns]
            par_refs = refs[ns:ns + npar]
            con_refs = refs[ns + npar:ns + npar + nc]
            rest = refs[ns + npar + nc:]
            out_refs, sav_refs, st_refs = rest[:no], rest[no:no + nst], rest[no + nst:]

            @pl.when(pl.program_id(1) == 0)
            def _():
                for s in st_refs:
                    s[...] = jnp.zeros_like(s)

            pvals, cvals = tuple(p[...] for p in par_refs), tuple(c[...] for c in con_refs)
            for i in range(nb):
                s_in = tuple(s[i] for s in st_refs)
                for sv, s in zip(sav_refs, s_in):
                    sv[i, 0] = s
                s_new, outs = step(s_in, tuple(r[i].astype(F32) for r in seq_refs), pvals, cvals)
                for s_ref, s in zip(st_refs, s_new):
                    s_ref[i] = s
                for o_ref, o in zip(out_refs, outs):
                    o_ref[i] = o

        ss, ps, cs, os_, vs = specs(lambda k: k)
        res = pl.pallas_call(
            body, grid=(B // nb, nck), in_specs=ss + ps + cs, out_specs=os_ + vs,
            out_shape=[jax.ShapeDtypeStruct((B, L, w), F32) for w in out_widths]
            + [jax.ShapeDtypeStruct((B, nck) + tuple(s), F32) for s in state_shapes],
            scratch_shapes=[pltpu.VMEM((nb,) + tuple(s), F32) for s in state_shapes],
            name=name + "_fwd", compiler_params=_cp("parallel", "arbitrary"))(*seqs, *params, *consts)
        return tuple(res[:no]), tuple(res[no:])

    def bwd_call(seqs, params, saved, douts):
        def body(*refs):
            seq_refs = refs[:ns]
            par_refs = refs[ns:ns + npar]
            con_refs = refs[ns + npar:ns + npar + nc]
            rest = refs[ns + npar + nc:]
            sav_refs, dout_refs = rest[:nst], rest[nst:nst + no]
            rest = rest[nst + no:]
            dseq_refs, dpar_refs, dst_refs = rest[:ns], rest[ns:ns + npar], rest[ns + npar:]
            b, k = pl.program_id(0), pl.program_id(1)

            @pl.when(k == 0)
            def _():
                for s in dst_refs:
                    s[...] = jnp.zeros_like(s)

            cvals, pvals = tuple(c[...] for c in con_refs), tuple(p[...] for p in par_refs)
            dp = None
            for i in range(nb):
                _, vjp = jax.vjp(lambda s, x, p: step(s, x, p, cvals), tuple(s[i, 0] for s in sav_refs),
                                 tuple(r[i].astype(F32) for r in seq_refs), pvals)
                ds, dx, dp_i = vjp((tuple(s[i] for s in dst_refs), tuple(d[i] for d in dout_refs)))
                for s_ref, s in zip(dst_refs, ds):
                    s_ref[i] = s
                for x_ref, x in zip(dseq_refs, dx):
                    x_ref[i] = x.astype(x_ref.dtype)
                dp = dp_i if dp is None else tuple(a + g for a, g in zip(dp, dp_i))
            first = (b == 0) & (k == 0)
            for ref, g in zip(dpar_refs, dp):
                @pl.when(first)
                def _(ref=ref, g=g):
                    ref[...] = g

                @pl.when(jnp.logical_not(first))
                def _(ref=ref, g=g):
                    ref[...] += g

        ss, ps, cs, os_, vs = specs(lambda k: nck - 1 - k)
        res = pl.pallas_call(
            body, grid=(B // nb, nck), in_specs=ss + ps + cs + vs + os_, out_specs=ss + ps,
            out_shape=[jax.ShapeDtypeStruct(s.shape, s.dtype) for s in seqs]
            + [jax.ShapeDtypeStruct(p.shape, F32) for p in params],
            scratch_shapes=[pltpu.VMEM((nb,) + tuple(s), F32) for s in state_shapes],
            name=name + "_bwd", compiler_params=_cp("arbitrary", "arbitrary"))(*seqs, *params, *consts, *saved, *douts)
        return tuple(res[:ns]), tuple(res[ns:])

    @jax.custom_vjp
    def op(seqs, params):
        return fwd_call(seqs, params)[0]

    def op_fwd(seqs, params):
        outs, saved = fwd_call(seqs, params)
        return outs, (seqs, params, saved)

    def op_bwd(res, douts):
        seqs, params, saved = res
        return bwd_call(seqs, params, saved, tuple(douts))

    op.defvjp(op_fwd, op_bwd)
    return op(seqs, params)


def _positions(Q, rev):
    t = np.arange(Q)
    return (Q - 1 - t) if rev else t


def _ssd_consts(rev):
    Q = SSD_Q
    pos = _positions(Q, rev)
    tri = pos[:, None] >= pos[None, :]
    head_of = np.arange(A_INNER) // (A_INNER // A_HEADS)
    expand = np.arange(A_HEADS)[:, None] == head_of[None, :]
    group_of_row = np.arange(A_GROUPS * A_STATE) // A_STATE
    block = group_of_row[:, None] == (head_of // (A_HEADS // A_GROUPS))[None, :]
    as_bf = lambda m: jnp.asarray(m.astype(np.float32), BF16)
    return (as_bf(tri), as_bf(tri.T), jnp.asarray(tri.astype(np.float32)), as_bf(expand),
            jnp.asarray(block.astype(np.float32)), jnp.asarray(expand.astype(np.float32)[:, None, :]))


def _ssd_step(state, seqs, params, consts):
    (st,) = state
    xs, bm, cm, dtr, dtr_t = seqs
    bias, bias_t, alog, alog_t = params
    tri_b, tri_t_b, tri_f, expand, block, head_lanes = consts
    dt = _softplus(dtr + bias)
    dta = dt * (-jnp.exp(alog))
    dt_t = _softplus(dtr_t + bias_t)
    dta_t = dt_t * (-jnp.exp(alog_t))
    cum = mdot(tri_b, dta)
    cum_t = xdot(dta_t, tri_t_b)
    total = jnp.sum(dta, axis=0, keepdims=True)
    w_end = jnp.exp(total - cum) * dt
    ecum_x = xdot(jnp.exp(cum), expand)
    wend_x = xdot(w_end, expand)
    dec_x = jnp.exp(jnp.sum(xdot(dta, expand), axis=0, keepdims=True))
    y = _bdot(cm, st) * ecum_x
    st_new = st * dec_x + block * _bdot(bm, xs * wend_x, TN)
    lane_g = lax.broadcasted_iota(jnp.int32, (1, A_GROUPS * A_STATE), 1) // A_STATE
    sub_h = lax.broadcasted_iota(jnp.int32, (A_HEADS, 1), 0)
    lane8 = lax.broadcasted_iota(jnp.int32, (1, A_HEADS), 1)
    Q = xs.shape[0]
    scores = []
    for g in range(A_GROUPS):
        cb = _bdot(jnp.where(lane_g == g, cm, 0.0), bm, NT)
        for h in range(g * (A_HEADS // A_GROUPS), (g + 1) * (A_HEADS // A_GROUPS)):
            col = jnp.sum(jnp.where(lane8 == h, cum, 0.0), axis=1, keepdims=True)
            row = jnp.sum(jnp.where(sub_h == h, cum_t, 0.0), axis=0, keepdims=True)
            dt_row = jnp.sum(jnp.where(sub_h == h, dt_t, 0.0), axis=0, keepdims=True)
            scores.append(cb * (tri_f * jnp.exp(tri_f * (col - row))) * dt_row)
    y_all = _bdot(jnp.stack(scores).reshape(A_HEADS * Q, Q), xs).reshape(A_HEADS, Q, A_INNER)
    y = y + jnp.sum(y_all * head_lanes, axis=0)
    return (st_new,), (y,)


def ssd_scan(name, rev, xs, bm, cm, dtr, dt_bias, a_log, n_ctx):
    seqs = (xs, bm, cm, dtr, jnp.swapaxes(dtr, 1, 2))
    params = (dt_bias.reshape(1, -1), dt_bias.reshape(-1, 1), a_log.reshape(1, -1), a_log.reshape(-1, 1))
    (y,) = chunk_scan(_ssd_step, name, rev, seqs, (False, False, False, False, True), params, _ssd_consts(rev),
                      [(A_GROUPS * A_STATE, A_INNER)], [A_INNER], SSD_Q, n_ctx)
    return y


def _recurrence(v, a, order, n_ctx, name, x_fwd=None):
    B, L = v.shape[:2]
    T = REC_T
    nck, ncc = L // T, n_ctx // T
    nlat = nck - ncc
    chunk_of = {"F": lambda k: k,
                "R": lambda k: jnp.where(k < ncc, ncc - 1 - k, nck + ncc - 1 - k),
                "FT": lambda k: nck - 1 - k,
                "RT": lambda k: jnp.where(k < nlat, ncc + k, k - nlat)}[order]
    descending = order in ("R", "FT")
    with_da = x_fwd is not None
    H = 8

    def body(*refs):
        if with_da:
            v_ref, a_ref, xf_ref, x_ref, da_ref, st_ref = refs
        else:
            v_ref, a_ref, x_ref, st_ref = refs
        k = pl.program_id(0)

        @pl.when(k == 0)
        def _():
            st_ref[...] = jnp.zeros_like(st_ref)
            if with_da:
                da_ref[...] = jnp.zeros_like(da_ref)

        ar, ai = a_ref[pl.ds(0, H), :], a_ref[pl.ds(H, H), :]
        zero = jnp.zeros((H, 128), F32)

        def step(i, carry):
            chains, acc_r, acc_i = carry
            t = (T - 1 - i) if descending else i
            out = []
            for b, (xr, xi) in enumerate(chains):
                if with_da:
                    f = xf_ref[b, t].astype(F32)
                    fr, fi = f[:H], f[H:]
                    acc_r = acc_r + xr * fr + xi * fi
                    acc_i = acc_i + xi * fr - xr * fi
                vt = v_ref[b, t].astype(F32)
                nr = ar * xr - ai * xi + vt[:H]
                ni = ar * xi + ai * xr + vt[H:]
                x_ref[b, t] = jnp.concatenate([nr, ni], axis=0).astype(x_ref.dtype)
                out.append((nr, ni))
            return tuple(out), acc_r, acc_i

        init = tuple((st_ref[b, pl.ds(0, H), :], st_ref[b, pl.ds(H, H), :]) for b in range(B))
        chains, acc_r, acc_i = lax.fori_loop(0, T, step, (init, zero, zero), unroll=8)
        for b, (xr, xi) in enumerate(chains):
            st_ref[b, pl.ds(0, H), :] = xr
            st_ref[b, pl.ds(H, H), :] = xi
        if with_da:
            da_ref[pl.ds(0, H), :] += acc_r
            da_ref[pl.ds(H, H), :] += acc_i

    seq = pl.BlockSpec((B, T, 2 * H, 128), lambda k: (0, chunk_of(k), 0, 0))
    par = pl.BlockSpec((2 * H, 128), lambda k: (0, 0))
    x_shape = jax.ShapeDtypeStruct(v.shape, v.dtype)
    if with_da:
        return pl.pallas_call(
            body, grid=(nck,), in_specs=[seq, par, seq], out_specs=[seq, par],
            out_shape=[x_shape, jax.ShapeDtypeStruct((2 * H, 128), F32)],
            scratch_shapes=[pltpu.VMEM((B, 2 * H, 128), F32)], name=name, compiler_params=_cp("arbitrary"))(v, a, x_fwd)
    return pl.pallas_call(
        body, grid=(nck,), in_specs=[seq, par], out_specs=seq, out_shape=x_shape,
        scratch_shapes=[pltpu.VMEM((B, 2 * H, 128), F32)], name=name, compiler_params=_cp("arbitrary"))(v, a)


def lin_rec(v, a, rev, n_ctx, name):
    fwd_order, bwd_order = ("R", "RT") if rev else ("F", "FT")
    conj = jnp.concatenate([jnp.ones((8, 128), F32), -jnp.ones((8, 128), F32)], axis=0)

    @jax.custom_vjp
    def f(v, a):
        return _recurrence(v, a, fwd_order, n_ctx, name + "_fwd")

    def fwd(v, a):
        x = f(v, a)
        return x, (x, a)

    def bwd(res, dx):
        x, a = res
        g, da = _recurrence(dx, a * conj, bwd_order, n_ctx, name + "_bwd", x_fwd=x)
        return g, da

    f.defvjp(fwd, bwd)
    return f(v, a)


def _block_diag(t):
    G, a, b = t.shape
    eye = jnp.eye(G, dtype=t.dtype)
    return (t[:, :, None, :] * eye[:, None, :, None]).reshape(G * a, G * b)


def s5_scan(name, rev, u, lam_re, lam_im, log_step, b_re, b_im, c_re, c_im, n_ctx):
    step = jnp.exp(log_step)[:, None]
    mag = jnp.exp(lam_re * step)
    ar = mag * jnp.cos(lam_im * step)
    ai = mag * jnp.sin(lam_im * step)
    den = lam_re * lam_re + lam_im * lam_im
    nr = ar - 1.0
    kr = (nr * lam_re + ai * lam_im) / den
    ki = (ai * lam_re - nr * lam_im) / den
    br = kr[..., None] * b_re - ki[..., None] * b_im
    bi = kr[..., None] * b_im + ki[..., None] * b_re
    B, L, _ = u.shape
    width = B_NGROUPS * B_STATE
    w_b = jnp.concatenate([_block_diag(jnp.swapaxes(br, 1, 2)), _block_diag(jnp.swapaxes(bi, 1, 2))], axis=1)
    w_c = jnp.concatenate([_block_diag(jnp.swapaxes(c_re, 1, 2)), -_block_diag(jnp.swapaxes(c_im, 1, 2))], axis=0)
    a = jnp.concatenate([ar.reshape(8, 128), ai.reshape(8, 128)], axis=0)
    v = linear(u.reshape(B * L, B_WIDTH), w_b.astype(BF16), w_b, name + "_in", BF16)
    x = lin_rec(v.reshape(B, L, 16, 128), a, rev, n_ctx, name)
    y = linear(x.reshape(B * L, 2 * width), w_c.astype(BF16), w_c, name + "_out")
    return y.reshape(B, L, B_WIDTH)


def _hg_consts(rev):
    Q = HG_Q
    pos = _positions(Q, rev)
    pi, pj = pos[:, None], pos[None, :]
    mats = [pi >= pj, pj > pi]
    pairs = []
    s = HG_BAND
    while s < Q:
        second = (pos // s) % 2 == 1
        mid = (pos // (2 * s)) * 2 * s + s
        mats.append(second[:, None] & (pj >= mid[:, None]) & (pj <= pi))
        mats.append((~second)[:, None] & (pj > pi) & (pj < mid[:, None]))
        pairs.append(second[:, None] & (~second)[None, :] & ((pi // (2 * s)) == (pj // (2 * s))))
        s *= 2
    rows = [((pos // s_) % 2 == 1) for s_ in [HG_BAND * 2 ** n for n in range(len(pairs))]]
    band = [(pos % HG_BAND) >= d for d in range(HG_BAND)]
    mats_b = jnp.asarray(np.stack(mats).astype(np.float32), BF16)
    pairs_f = jnp.asarray(np.tile(np.stack(pairs).astype(np.float32), (1, C_HEADS, 1)))
    rows_f = jnp.asarray(np.stack(rows).astype(np.float32)[:, :, None])
    band_f = jnp.asarray(np.stack(band).astype(np.float32)[:, :, None])
    head_lanes = np.arange(C_HEADS)[:, None] == (np.arange(C_WIDTH) // C_KEY)[None, :]
    return mats_b, pairs_f, rows_f, band_f, jnp.asarray(head_lanes.astype(np.float32)[:, None, :])


def _hg_step_fn(rev, n_levels):
    def step(state, seqs, params, consts):
        (st,) = state
        qr, zf, v = seqs
        (lb,) = params
        mats, pairs, rows, band, head_lanes = consts
        W = C_WIDTH
        Q = qr.shape[0]
        same_head = (lax.broadcasted_iota(jnp.int32, (W, W), 0) // C_KEY
                     == lax.broadcasted_iota(jnp.int32, (W, W), 1) // C_KEY)
        q = _silu(qr)
        f = lb + (1.0 - lb) * jax.nn.sigmoid(zf)
        logf = jnp.log(f)
        kk = 1.0 - f
        qd = q * jnp.exp(mdot(mats[0], logf))
        w = kk * jnp.exp(mdot(mats[1], logf))
        total = jnp.sum(logf, axis=0, keepdims=True)
        o = _bdot(qd, st, NT)
        st_new = st * jnp.exp(total) + jnp.where(same_head, _bdot(v, w, TN), 0.0)
        scores = None
        for n in range(n_levels):
            a = q * jnp.exp(mdot(mats[2 + 2 * n], logf)) * rows[n]
            bk = kk * jnp.exp(mdot(mats[3 + 2 * n], logf)) * (1.0 - rows[n])
            sc = pairs[n] * _bdot((a[None] * head_lanes).reshape(C_HEADS * Q, W), bk, NT)
            scores = sc if scores is None else scores + sc
        o = o + jnp.sum(_bdot(scores, v).reshape(C_HEADS, Q, W) * head_lanes, axis=0)
        same_head_b = same_head.astype(BF16)
        e = jnp.zeros_like(logf)
        for d in range(HG_BAND):
            s = -d if rev else d
            if d > 0:
                e = e + sroll(logf, s + (1 if rev else -1))
            kd, vd = (kk, v) if d == 0 else (sroll(kk, s), sroll(v, s))
            p = q * kd * jnp.exp(e)
            o = o + band[d] * (_bdot(p, same_head_b) * vd)
        return (st_new,), (o,)

    return step


def hg_scan(name, rev, qr, zf, v, lower, n_ctx):
    consts = _hg_consts(rev)
    (o,) = chunk_scan(_hg_step_fn(rev, consts[1].shape[0]), name, rev, (qr, zf, v), (False,) * 3,
                      (lower.reshape(1, -1),), consts, [(C_WIDTH, C_WIDTH)], [C_WIDTH], HG_Q, n_ctx)
    return o


def conv_silu(x, w, b, n_ctx, name):
    B, L, C = x.shape
    TC = 128
    w8 = jnp.zeros((8, C), F32).at[:A_CONV].set(w)
    b2 = b.reshape(1, C)
    pad = A_CONV // 2

    def taps(v, sign):
        t = lax.broadcasted_iota(jnp.int32, v.shape, 0)
        out = []
        for k in range(A_CONV):
            s = sign * (k - pad)
            src = t + s
            ok = (src >= 0) & (src < L) & ((t >= n_ctx) == (src >= n_ctx))
            vs = v if s == 0 else pltpu.roll(v, (-s) % L, 0)
            out.append((k, jnp.where(ok, vs, 0.0)))
        return out

    def pre(x_ref, w_ref, b_ref):
        xv = x_ref[0].astype(F32)
        y = b_ref[...] + sum(xs * w_ref[pl.ds(k, 1), :] for k, xs in taps(xv, 1))
        return xv, y

    def fwd_body(x_ref, w_ref, b_ref, o_ref):
        _, y = pre(x_ref, w_ref, b_ref)
        o_ref[0] = _silu(y)

    def bwd_body(x_ref, w_ref, b_ref, g_ref, dx_ref, dw_ref, db_ref):
        xv, y = pre(x_ref, w_ref, b_ref)
        sg = _sigmoid(y)
        dy = g_ref[0] * (sg + y * sg * (1.0 - sg))
        dx_ref[0] = sum(ds * w_ref[pl.ds(k, 1), :] for k, ds in taps(dy, -1)).astype(dx_ref.dtype)
        first = pl.program_id(1) == 0

        @pl.when(first)
        def _():
            dw_ref[...] = jnp.zeros_like(dw_ref)
            db_ref[...] = jnp.zeros_like(db_ref)

        for k, xs in taps(xv, 1):
            dw_ref[pl.ds(k, 1), :] += jnp.sum(dy * xs, axis=0, keepdims=True)
        db_ref[...] += jnp.sum(dy, axis=0, keepdims=True)

    x_spec = pl.BlockSpec((1, L, TC), lambda j, bb: (bb, 0, j))
    w_spec = pl.BlockSpec((8, TC), lambda j, bb: (0, j))
    b_spec = pl.BlockSpec((1, TC), lambda j, bb: (0, j))

    @jax.custom_vjp
    def op(x, w8, b2):
        return pl.pallas_call(fwd_body, grid=(C // TC, B), in_specs=[x_spec, w_spec, b_spec], out_specs=x_spec,
                              out_shape=jax.ShapeDtypeStruct(x.shape, F32), name=name + "_fwd",
                              compiler_params=_cp("parallel", "parallel"))(x, w8, b2)

    def op_fwd(x, w8, b2):
        return op(x, w8, b2), (x, w8, b2)

    def op_bwd(res, g):
        x, w8, b2 = res
        return tuple(pl.pallas_call(
            bwd_body, grid=(C // TC, B), in_specs=[x_spec, w_spec, b_spec, x_spec],
            out_specs=[x_spec, w_spec, b_spec],
            out_shape=[jax.ShapeDtypeStruct(x.shape, x.dtype), jax.ShapeDtypeStruct(w8.shape, F32),
                       jax.ShapeDtypeStruct(b2.shape, F32)],
            name=name + "_bwd", compiler_params=_cp("parallel", "arbitrary"))(x, w8, b2, g))

    op.defvjp(op_fwd, op_bwd)
    return op(x, w8, b2)


def loss_head(h, target, norm_w, n_ctx):
    B, L, Dm = h.shape
    S = target.shape[1]
    tl = ROW_TILE
    skip = n_ctx // tl

    def body(h_ref, t_ref, w_ref, loss_ref, dh_ref, dw_ref):
        b, i = pl.program_id(0), pl.program_id(1)
        fn = lambda hv, wv: _rms(hv) * wv
        y, vjp = jax.vjp(fn, h_ref[0], w_ref[...])
        err = y - t_ref[0]
        dhv, dwv = vjp(err * (1.0 / Dm))
        dh_ref[0] = dhv
        part = 0.5 * jnp.sum(jnp.mean(err * err, axis=-1, keepdims=True), axis=0, keepdims=True)
        first = (b == 0) & (i == 0)

        @pl.when(first)
        def _():
            loss_ref[...] = jnp.zeros_like(loss_ref)
            dw_ref[...] = jnp.zeros_like(dw_ref)

        loss_ref[...] += jnp.broadcast_to(part, loss_ref.shape)
        dw_ref[...] += dwv

    loss, dh, dw = pl.pallas_call(
        body, grid=(B, S // tl),
        in_specs=[pl.BlockSpec((1, tl, Dm), lambda b, i: (b, i + skip, 0)),
                  pl.BlockSpec((1, tl, Dm), lambda b, i: (b, i, 0)),
                  pl.BlockSpec((1, Dm), lambda b, i: (0, 0))],
        out_specs=[pl.BlockSpec((8, 128), lambda b, i: (0, 0)),
                   pl.BlockSpec((1, tl, Dm), lambda b, i: (b, i, 0)),
                   pl.BlockSpec((1, Dm), lambda b, i: (0, 0))],
        out_shape=[jax.ShapeDtypeStruct((8, 128), F32), jax.ShapeDtypeStruct((B, S, Dm), F32),
                   jax.ShapeDtypeStruct((1, Dm), F32)],
        name="loss_head", compiler_params=_cp("arbitrary", "arbitrary"))(h, target, norm_w.reshape(1, Dm))
    dh_full = jnp.concatenate([jnp.zeros((B, n_ctx, Dm), F32), dh], axis=1)
    return loss[0, 0], dh_full, dw.reshape(Dm)


def _modulate(h, shift, scale):
    return _rms(h) * (1.0 + scale) + shift


def _f_mod(r, m, p):
    return (_modulate(r[0], m[0], m[1]),)


def _f_resid_mod(coef):
    def f(r, m, p):
        h2 = r[0] + coef * m[0] * r[1]
        return h2, _modulate(h2, m[1], m[2])
    return f


def _f_resid(coef):
    def f(r, m, p):
        return (r[0] + coef * m[0] * r[1],)
    return f


def _f_swiglu(r, m, p):
    pre = r[0].astype(F32)
    return (_silu(pre[:, :D_FF]) * pre[:, D_FF:],)


def _f_ssd_post(r, m, p):
    y_f, y_b, xs, z = (t.astype(F32) for t in r)
    d_skip, norm_w = p
    y = (y_f + y_b + d_skip * xs) * _silu(z)
    return (_rms(y) * norm_w,)


def _f_s5_post(r, m, p):
    y_f, y_b, u = (t.astype(F32) for t in r)
    d_skip, glu_w, glu_b = p
    y = jax.nn.gelu(y_f + y_b + d_skip * u)
    return (y * _sigmoid(_bdot(y, glu_w) + glu_b),)


def _f_hg_post(r, m, p):
    o_f, o_b, g = (t.astype(F32) for t in r)
    (norm_w,) = p
    W = C_WIDTH
    same_head = (lax.broadcasted_iota(jnp.int32, (W, W), 0) // C_KEY
                 == lax.broadcasted_iota(jnp.int32, (W, W), 1) // C_KEY).astype(BF16)
    o = o_f + o_b
    ms = xdot(o * o, same_head) * (1.0 / C_KEY)
    return (o * lax.rsqrt(ms + EPS) * norm_w * _silu(g),)


def _pad_w_in(w):
    dt0 = A_INNER + A_CONV_DIM
    zeros = jnp.zeros((w.shape[0], IN_COLS_PAD - IN_COLS), w.dtype)
    return jnp.concatenate([w[:, :dt0], w[:, A_COLS:], w[:, dt0:A_COLS], zeros], axis=1)


def _to_columns(t, n_ctx):
    B, L, W = t.shape
    rows = (L - n_ctx) // GRID_W
    lat = t[:, n_ctx:].reshape(B, rows, GRID_W, W).transpose(0, 2, 1, 3).reshape(B, L - n_ctx, W)
    return jnp.concatenate([t[:, :n_ctx], lat], axis=1)


def _to_raster(t, n_ctx):
    B, L, W = t.shape
    rows = (L - n_ctx) // GRID_W
    lat = t[:, n_ctx:].reshape(B, GRID_W, rows, W).transpose(0, 2, 1, 3).reshape(B, L - n_ctx, W)
    return jnp.concatenate([t[:, :n_ctx], lat], axis=1)


def _forward(h0, c, big, P, n_ctx):
    B, L, Dm = h0.shape
    depth = len(big["mod_w"])
    ncb = n_ctx // ROW_TILE

    def rows_of(t):
        return t.reshape(t.shape[0] * t.shape[1], t.shape[2])

    def w_in_of(t):
        return _pad_w_in(jnp.moveaxis(t, 0, 1).reshape(t.shape[1], IN_COLS))

    p_lb = jax.nn.softmax(P["hg_lb_logits"], axis=0)
    lower_bounds = jnp.cumsum(p_lb, axis=0) - p_lb[:1]
    cc = jnp.concatenate([c, P["c_ctx"][None], jnp.zeros((8 - B - 1, Dm), F32)], axis=0)
    cc = _silu(cc)

    def mods_of(l):
        m = linear(cc, big["mod_w"][l], P["mod_w"][l], f"mod{l}", pieces=True) + P["mod_b"][l]
        m = m.reshape(8, N_MOD, Dm)
        seg = jnp.stack([jnp.broadcast_to(m[B], (B, N_MOD, Dm)), m[:B]], axis=1)
        return [seg[:, :, j:j + 1, :] for j in range(N_MOD)]

    def ffn(u, l, j):
        pre = linear(u.reshape(B * L, Dm), big["ffn_w_in"][l][j], P["ffn_w_in"][l][j], f"ffn_in{l}{j}", BF16,
                     pieces=True)
        (act,) = rowwise(_f_swiglu, f"swiglu{l}{j}", [pre.reshape(B, L, 2 * D_FF)], [], [], [(D_FF, BF16)], ncb)
        out = linear(act.reshape(B * L, D_FF), rows_of(big["ffn_w_out"][l][j]), rows_of(P["ffn_w_out"][l][j]),
                     f"ffn_out{l}{j}")
        return out.reshape(B, L, Dm)

    h = h0
    mods = mods_of(0)
    (u,) = rowwise(_f_mod, "mod_first", [h], [mods[0], mods[1]], [], [(Dm, BF16)], ncb)
    for l in range(depth):
        col_major = l % 2 == 1
        o = ffn(u, l, 0)
        h, u = rowwise(_f_resid_mod(0.5), f"resid_a{l}", [h, o], [mods[2], mods[3], mods[4]], [],
                       [(Dm, F32), (Dm, BF16)], ncb)
        if col_major:
            u = _to_columns(u, n_ctx)
        pre = linear(u.reshape(B * L, Dm), w_in_of(big["w_in"][l]), w_in_of(P["w_in"][l]), f"w_in{l}", BF16)
        pre = pre.reshape(B, L, IN_COLS_PAD)
        o0 = 0
        z, o0 = pre[..., o0:o0 + A_INNER], o0 + A_INNER
        xbc, o0 = pre[..., o0:o0 + A_CONV_DIM], o0 + A_CONV_DIM
        pb, o0 = pre[..., o0:o0 + B_WIDTH], o0 + B_WIDTH
        pc, o0 = pre[..., o0:o0 + 5 * C_WIDTH], o0 + 5 * C_WIDTH
        dtr = pre[..., o0:o0 + 2 * A_HEADS]
        xbc = conv_silu(xbc, P["a_conv_w"][l], P["a_conv_b"][l], n_ctx, f"conv{l}")
        xs, bm, cm = xbc[..., :A_INNER], xbc[..., A_INNER:A_INNER + 128], xbc[..., A_INNER + 128:]
        ya_dir = [ssd_scan(f"ssd{l}{d}", bool(d), xs, bm, cm, dtr[..., d * A_HEADS:(d + 1) * A_HEADS],
                           P["a_dt_bias"][l, d], P["a_log"][l, d], n_ctx) for d in range(2)]
        d_skip = jnp.repeat(P["a_d"][l], A_INNER // A_HEADS).reshape(1, A_INNER)
        (ya,) = rowwise(_f_ssd_post, f"ssd_post{l}", [ya_dir[0], ya_dir[1], xs, z], [],
                        [d_skip, P["a_norm_w"][l].reshape(1, -1)], [(A_INNER, BF16)], ncb)
        yb_dir = [s5_scan(f"s5{l}{d}", bool(d), pb, P["s5_lam_re"][l, d], P["s5_lam_im"][l, d],
                          P["s5_log_step"][l, d], P["s5_b_re"][l, d], P["s5_b_im"][l, d], P["s5_c_re"][l, d],
                          P["s5_c_im"][l, d], n_ctx) for d in range(2)]
        (yb,) = rowwise(_f_s5_post, f"s5_post{l}", [yb_dir[0], yb_dir[1], pb], [],
                        [P["s5_d"][l].reshape(1, -1), P["s5_glu_w"][l], P["s5_glu_b"][l].reshape(1, -1)],
                        [(B_WIDTH, BF16)], ncb)
        qr, f_raw, vi, gate = (pc[..., :C_WIDTH], pc[..., C_WIDTH:3 * C_WIDTH], pc[..., 3 * C_WIDTH:4 * C_WIDTH],
                               pc[..., 4 * C_WIDTH:])
        yc_dir = [hg_scan(f"hg{l}{d}", bool(d), qr, f_raw[..., d * C_WIDTH:(d + 1) * C_WIDTH], vi,
                          lower_bounds[l, d], n_ctx) for d in range(2)]
        (yc,) = rowwise(_f_hg_post, f"hg_post{l}", [yc_dir[0], yc_dir[1], gate], [],
                        [P["hg_norm_w"][l].reshape(1, -1)], [(C_WIDTH, BF16)], ncb)
        mix = jnp.concatenate([ya, yb, yc], axis=-1)
        y = linear(mix.reshape(B * L, Dm), rows_of(big["w_out"][l]), rows_of(P["w_out"][l]), f"w_out{l}")
        y = y.reshape(B, L, Dm)
        if col_major:
            y = _to_raster(y, n_ctx)
        h, u = rowwise(_f_resid_mod(1.0), f"resid_b{l}", [h, y], [mods[5], mods[6], mods[7]], [],
                       [(Dm, F32), (Dm, BF16)], ncb)
        o = ffn(u, l, 1)
        gate8 = mods[8]
        if l + 1 < depth:
            mods = mods_of(l + 1)
            h, u = rowwise(_f_resid_mod(0.5), f"resid_c{l}", [h, o], [gate8, mods[0], mods[1]], [],
                           [(Dm, F32), (Dm, BF16)], ncb)
        else:
            (h,) = rowwise(_f_resid(0.5), f"resid_c{l}", [h, o], [gate8], [], [(Dm, F32)], ncb)
    return h


BIG = ("mod_w", "ffn_w_in", "ffn_w_out", "w_in", "w_out")


def local_step(x, c, ctx, target, big, small):
    n_ctx = ctx.shape[1]
    h0 = jnp.concatenate([ctx, x], axis=1)
    P = dict(small)
    for k in BIG:
        P[k] = jax.tree.map(lambda t: jnp.zeros(t.shape, BF16), big[k])
    h, vjp = jax.vjp(lambda h0, P: _forward(h0, c, big, P, n_ctx), h0, P)
    loss, dh, d_final = loss_head(h, target, small["final_norm_w"], n_ctx)
    dh0, grads = vjp(dh)
    grads = dict(grads)
    grads["final_norm_w"] = grads["final_norm_w"] + d_final
    return loss, dh0[:, n_ctx:], grads


MESH = pl.DeviceIdType.MESH
LANES = 1024
N_CHIPS = 4
ELEMENTWISE_BLOCK_BYTES = 2 * 1024 * 1024
ANY = pl.BlockSpec(memory_space=pl.ANY)


def _place():
    x, y, c = lax.axis_index("x"), lax.axis_index("y"), lax.axis_index("c")
    return x, y, c, 2 * x + y


def _other_chips(x, y):
    return [(x ^ kx, y ^ ky, 2 * (x ^ kx) + (y ^ ky)) for kx, ky in ((0, 1), (1, 0), (1, 1))]


def all_gather_shards(shard, name):
    def body(x_ref, out_ref, send_sems, recv_sems, local_sem):
        x, y, c, q = _place()
        sibling = (x, y, 1 - c)
        chips = _other_chips(x, y)

        def copy(k, src, dst, to):
            return pltpu.make_async_remote_copy(src_ref=src, dst_ref=dst, send_sem=send_sems.at[k],
                                                recv_sem=recv_sems.at[k], device_id=to, device_id_type=MESH)

        mine = pltpu.make_async_copy(x_ref, out_ref.at[q], local_sem)
        mine.start()
        first = [copy(k, x_ref.at[c], out_ref.at[q, c], (px, py, c)) for k, (px, py, _) in enumerate(chips)]
        for cp in first:
            cp.start()
        passed = [copy(3 + k, out_ref.at[pq, c], out_ref.at[pq, c], sibling) for k, (_, _, pq) in enumerate(chips)]
        for k, (_, _, pq) in enumerate(chips):
            copy(k, x_ref.at[c], out_ref.at[pq, c], sibling).wait_recv()
            passed[k].start()
        for k, (_, _, pq) in enumerate(chips):
            copy(3 + k, x_ref.at[c], out_ref.at[pq, 1 - c], sibling).wait_recv()
        for cp in first + passed:
            cp.wait_send()
        mine.wait()

    return pl.pallas_call(
        body, out_shape=jax.ShapeDtypeStruct((N_CHIPS,) + shard.shape, shard.dtype), in_specs=[ANY], out_specs=ANY,
        scratch_shapes=[pltpu.SemaphoreType.DMA((6,)), pltpu.SemaphoreType.DMA((6,)), pltpu.SemaphoreType.DMA],
        name=name)(shard)


def all_gather_devices(part, name):
    def body(x_ref, out_ref, send_sems, recv_sems, local_sem):
        x, y, c, _ = _place()
        me = 4 * x + 2 * y + c
        mine = pltpu.make_async_copy(x_ref, out_ref.at[me], local_sem)
        mine.start()
        copies = []
        for k in range(1, 8):
            px, py, pc = x ^ (k >> 2), y ^ ((k >> 1) & 1), c ^ (k & 1)
            copies.append(pltpu.make_async_remote_copy(
                src_ref=x_ref, dst_ref=out_ref.at[me], send_sem=send_sems.at[k - 1], recv_sem=recv_sems.at[k - 1],
                device_id=(px, py, pc), device_id_type=MESH))
            copies[-1].start()
        for k in range(1, 8):
            peer = 4 * (x ^ (k >> 2)) + 2 * (y ^ ((k >> 1) & 1)) + (c ^ (k & 1))
            pltpu.make_async_remote_copy(
                src_ref=x_ref, dst_ref=out_ref.at[peer], send_sem=send_sems.at[k - 1], recv_sem=recv_sems.at[k - 1],
                device_id=(x, y, c), device_id_type=MESH).wait_recv()
        for cp in copies:
            cp.wait_send()
        mine.wait()

    return pl.pallas_call(
        body, out_shape=jax.ShapeDtypeStruct((8,) + part.shape, part.dtype), in_specs=[ANY], out_specs=ANY,
        scratch_shapes=[pltpu.SemaphoreType.DMA((7,)), pltpu.SemaphoreType.DMA((7,)), pltpu.SemaphoreType.DMA],
        name=name)(part)


def _half(ref, c, axis):
    h = ref.shape[axis] // 2
    return ref.at[(slice(None),) * axis + (pl.ds(pl.multiple_of(c * h, 8), h),)]


def _remote(src, dst, send_sems, recv_sems, s, to):
    return pltpu.make_async_remote_copy(src_ref=src, dst_ref=dst, send_sem=send_sems.at[s], recv_sem=recv_sems.at[s],
                                        device_id=to, device_id_type=MESH)


def gather_pieces(arrays, name):
    n = len(arrays)

    def body(*refs):
        ins, outs, (send_sems, recv_sems) = refs[:n], refs[n:2 * n], refs[2 * n:]
        x, y, c, q = _place()
        sibling = (x, y, 1 - c)
        chips = _other_chips(x, y)
        started = []

        def go(src, dst, s, to):
            started.append(_remote(src, dst, send_sems, recv_sems, s, to))
            started[-1].start()

        for i, (a, o) in enumerate(zip(ins, outs)):
            go(a, o.at[q], 7 * i + 6, sibling)
            for k, (px, py, _) in enumerate(chips):
                go(_half(a, c, 0), _half(o.at[q], c, 0), 7 * i + k, (px, py, c))
        for i, o in enumerate(outs):
            for k, (_, _, pq) in enumerate(chips):
                land = _half(o.at[pq], c, 0)
                _remote(land, land, send_sems, recv_sems, 7 * i + k, sibling).wait_recv()
                go(land, land, 7 * i + 3 + k, sibling)
        for i, (a, o) in enumerate(zip(ins, outs)):
            for k, (_, _, pq) in enumerate(chips):
                land = _half(o.at[pq], 1 - c, 0)
                _remote(land, land, send_sems, recv_sems, 7 * i + 3 + k, sibling).wait_recv()
            _remote(a, o.at[q], send_sems, recv_sems, 7 * i + 6, sibling).wait_recv()
        for cp in started:
            cp.wait_send()

    return pl.pallas_call(
        body, out_shape=[jax.ShapeDtypeStruct((N_CHIPS,) + a.shape, a.dtype) for a in arrays], in_specs=[ANY] * n,
        out_specs=[ANY] * n, scratch_shapes=[pltpu.SemaphoreType.DMA((7 * n,)), pltpu.SemaphoreType.DMA((7 * n,))],
        name=name)(*arrays)


def swap_with_sibling(arrays, name, send_other_half_of_axis=None):
    n = len(arrays)
    ax = send_other_half_of_axis

    def body(*refs):
        ins, outs, (send_sems, recv_sems) = refs[:n], refs[n:2 * n], refs[2 * n:]
        x, y, c, _ = _place()
        copies = [_remote(a if ax is None else _half(a, 1 - c, ax), o, send_sems, recv_sems, i, (x, y, 1 - c))
                  for i, (a, o) in enumerate(zip(ins, outs))]
        for cp in copies:
            cp.start()
        for cp in copies:
            cp.wait()

    def out_of(a):
        shape = list(a.shape)
        if ax is not None:
            shape[ax] //= 2
        return jax.ShapeDtypeStruct(tuple(shape), a.dtype)

    return pl.pallas_call(
        body, out_shape=[out_of(a) for a in arrays], in_specs=[ANY] * n, out_specs=[ANY] * n,
        scratch_shapes=[pltpu.SemaphoreType.DMA((n,)), pltpu.SemaphoreType.DMA((n,))], name=name)(*arrays)


def pieces_to_chips(pairs, name):
    n = len(pairs)

    def body(*refs):
        ins, outs, (send_sems, recv_sems) = refs[:n], refs[n:2 * n], refs[2 * n:]
        x, y, c, q = _place()
        chips = _other_chips(x, y)
        copies = [_remote(a.at[pq], o.at[q], send_sems, recv_sems, 3 * i + k, (px, py, c))
                  for i, (a, o) in enumerate(zip(ins, outs)) for k, (px, py, pq) in enumerate(chips)]
        for cp in copies:
            cp.start()
        for i, (a, o) in enumerate(zip(ins, outs)):
            for k, (_, _, pq) in enumerate(chips):
                _remote(a.at[pq], o.at[pq], send_sems, recv_sems, 3 * i + k, (x, y, c)).wait_recv()
        for cp in copies:
            cp.wait_send()

    return pl.pallas_call(
        body, out_shape=[jax.ShapeDtypeStruct(a.shape, a.dtype) for a in pairs], in_specs=[ANY] * n,
        out_specs=[ANY] * n, scratch_shapes=[pltpu.SemaphoreType.DMA((3 * n,)), pltpu.SemaphoreType.DMA((3 * n,))],
        name=name)(*pairs)


def _rows_block(h, n_cols):
    return _tile(h, 16, max(16, ELEMENTWISE_BLOCK_BYTES // (n_cols * 4)))


def add_pair(core, g, got, name):
    _, K, N = g.shape
    h = K // 2
    th = _rows_block(h, N)
    nb = h // th

    def body(c_ref, g_ref, got_ref, o_ref):
        o_ref[...] = (g_ref[...].astype(F32) + got_ref[...].astype(F32)).astype(o_ref.dtype)

    grid_spec = pltpu.PrefetchScalarGridSpec(
        num_scalar_prefetch=1, grid=(N_CHIPS, nb),
        in_specs=[pl.BlockSpec((None, th, N), lambda p, i, c_ref: (p, c_ref[0] * nb + i, 0)),
                  pl.BlockSpec((None, th, N), lambda p, i, c_ref: (p, i, 0))],
        out_specs=pl.BlockSpec((None, th, N), lambda p, i, c_ref: (p, i, 0)))
    return pl.pallas_call(body, grid_spec=grid_spec, out_shape=jax.ShapeDtypeStruct(got.shape, g.dtype), name=name,
                          compiler_params=_cp("parallel", "parallel"))(core, g, got)


def sum_pieces(chip, pair, got, name):
    _, h, N = pair.shape
    th = _rows_block(h, N)

    def body(q_ref, pair_ref, *rest):
        got_refs, o_ref = rest[:N_CHIPS], rest[N_CHIPS]
        q = q_ref[0]
        for p in range(N_CHIPS):
            def put(val, p=p):
                if p == 0:
                    o_ref[...] = val
                else:
                    o_ref[...] += val

            @pl.when(q == p)
            def _():
                put(pair_ref[...].astype(F32))

            @pl.when(q != p)
            def _(p=p):
                put(got_refs[p][...].astype(F32))

    def got_spec(p):
        return pl.BlockSpec((None, th, N), lambda i, q_ref: (jnp.where(q_ref[0] == p, (p + 1) % N_CHIPS, p), i, 0))

    grid_spec = pltpu.PrefetchScalarGridSpec(
        num_scalar_prefetch=1, grid=(h // th,),
        in_specs=[pl.BlockSpec((None, th, N), lambda i, q_ref: (q_ref[0], i, 0))] + [got_spec(p) for p in range(N_CHIPS)],
        out_specs=pl.BlockSpec((th, N), lambda i, q_ref: (i, 0)))
    return pl.pallas_call(body, grid_spec=grid_spec, out_shape=jax.ShapeDtypeStruct((h, N), F32), name=name,
                          compiler_params=_cp("parallel"))(chip, pair, *([got] * N_CHIPS))


def adamw_halves(core, mine, other, w, m, v, name):
    n, K, N = w.shape
    h = K // 2
    th = _rows_block(h, N)
    nb = h // th

    def body(c_ref, *refs):
        mine_refs, other_refs = refs[:n], refs[n:2 * n]
        w_ref, m_ref, v_ref, g_ref, d_ref, m2_ref, v2_ref = refs[2 * n:]
        a, s, c = pl.program_id(0), pl.program_id(1), c_ref[0]
        for idx in range(n):
            @pl.when((a == idx) & (s == c))
            def _(idx=idx):
                g_ref[...] = mine_refs[idx][...]

            @pl.when((a == idx) & (s != c))
            def _(idx=idx):
                g_ref[...] = other_refs[idx][...]
        _adamw_math(g_ref[...], w_ref, m_ref, v_ref, d_ref, m2_ref, v2_ref)

    def half_spec(idx, is_mine):
        def index(a, s, i, c_ref):
            right_half = (s == c_ref[0]) if is_mine else (s != c_ref[0])
            return (jnp.where((a == idx) & right_half, i, 0), 0)
        return pl.BlockSpec((th, N), index)

    full = pl.BlockSpec((None, th, N), lambda a, s, i, c_ref: (a, s * nb + i, 0))
    grid_spec = pltpu.PrefetchScalarGridSpec(
        num_scalar_prefetch=1, grid=(n, 2, nb),
        in_specs=[half_spec(idx, True) for idx in range(n)] + [half_spec(idx, False) for idx in range(n)] + [full] * 3,
        out_specs=[full] * 4)
    return pl.pallas_call(body, grid_spec=grid_spec, out_shape=[jax.ShapeDtypeStruct((n, K, N), F32)] * 4, name=name,
                          compiler_params=_cp("arbitrary", "arbitrary", "arbitrary"))(core, *mine, *other, w, m, v)


def _adamw_math(gv, w_ref, m_ref, v_ref, d_ref, m2_ref, v2_ref):
    m2 = ADAM_B1 * m_ref[...] + (1.0 - ADAM_B1) * gv
    v2 = ADAM_B2 * v_ref[...] + (1.0 - ADAM_B2) * (gv * gv)
    m_hat = m2 / (1.0 - ADAM_B1 ** ADAM_STEP)
    v_hat = v2 / (1.0 - ADAM_B2 ** ADAM_STEP)
    d_ref[...] = -ADAM_LR * (m_hat / (jnp.sqrt(v_hat) + ADAM_EPS) + ADAM_WD * w_ref[...])
    m2_ref[...] = m2
    v2_ref[...] = v2


def _row_tile(R):
    return _pick(R, (512, 256, 128, 64, 32, 16, 8))


def sum_parts(parts, name):
    P, R, _ = parts.shape
    rt = _row_tile(R)

    def body(p_ref, o_ref):
        acc = p_ref[0]
        for p in range(1, P):
            acc = acc + p_ref[p]
        o_ref[...] = acc

    return pl.pallas_call(
        body, grid=(R // rt,), in_specs=[pl.BlockSpec((P, rt, LANES), lambda i: (0, i, 0))],
        out_specs=pl.BlockSpec((rt, LANES), lambda i: (i, 0)), out_shape=jax.ShapeDtypeStruct((R, LANES), F32),
        name=name, compiler_params=_cp("parallel"))(parts)


def adamw(g, w, m, v, name):
    R = g.shape[0]
    rt = _row_tile(R)

    def body(g_ref, w_ref, m_ref, v_ref, d_ref, m2_ref, v2_ref):
        _adamw_math(g_ref[...], w_ref, m_ref, v_ref, d_ref, m2_ref, v2_ref)

    spec = pl.BlockSpec((rt, LANES), lambda i: (i, 0))
    return pl.pallas_call(
        body, grid=(R // rt,), in_specs=[spec] * 4, out_specs=[spec] * 3,
        out_shape=[jax.ShapeDtypeStruct((R, LANES), F32)] * 3, name=name, compiler_params=_cp("parallel"))(g, w, m, v)


def _pack(arrays, rows_multiple, dtype):
    flat = jnp.concatenate([a.reshape(-1).astype(dtype) for a in arrays])
    n = flat.shape[0]
    per = rows_multiple * LANES
    total = -(-n // per) * per
    return jnp.pad(flat, (0, total - n)).reshape(total // LANES, LANES)


def _unpack(buf, shapes):
    flat = buf.reshape(-1)
    out, off = [], 0
    for s in shapes:
        n = math.prod(s)
        out.append(flat[off:off + n].reshape(s))
        off += n
    return out


SHARDED = (("mod_w", 2), ("ffn_w_in", 3), ("ffn_w_out", 2), ("w_in", 2), ("w_out", 1),
           ("a_conv_w", 2), ("s5_glu_w", 1), ("hg_lb_logits", 2))
WEIGHTS = ("c_ctx", "mod_w", "mod_b", "ffn_w_in", "ffn_w_out", "w_in", "w_out", "a_conv_w", "a_conv_b", "a_dt_bias",
           "a_log", "a_d", "a_norm_w", "s5_lam_re", "s5_lam_im", "s5_log_step", "s5_b_re", "s5_b_im", "s5_c_re",
           "s5_c_im", "s5_d", "s5_glu_w", "s5_glu_b", "hg_lb_logits", "hg_norm_w", "final_norm_w")


def _gather_full(local, names_axes, dtype, rows_multiple, name):
    buf = _pack([local[n] for n, _ in names_axes], 2 * rows_multiple, dtype)
    R = buf.shape[0] // 2
    full = all_gather_shards(buf.reshape(2, R, LANES), name).reshape(N_CHIPS, 2 * R, LANES)
    per_chip = [_unpack(full[q], [local[n].shape for n, _ in names_axes]) for q in range(N_CHIPS)]
    return {n: jnp.concatenate([per_chip[q][j] for q in range(N_CHIPS)], axis=ax)
            for j, (n, ax) in enumerate(names_axes)}


def kernel(x, c, ctx, c_ctx, mod_w, mod_b, ffn_w_in, ffn_w_out, w_in, w_out, a_conv_w, a_conv_b, a_dt_bias, a_log, a_d, a_norm_w, s5_lam_re, s5_lam_im, s5_log_step, s5_b_re, s5_b_im, s5_c_re, s5_c_im, s5_d, s5_glu_w, s5_glu_b, hg_lb_logits, hg_norm_w, final_norm_w, loss_target, m_c_ctx, m_mod_w, m_mod_b, m_ffn_w_in, m_ffn_w_out, m_w_in, m_w_out, m_a_conv_w, m_a_conv_b, m_a_dt_bias, m_a_log, m_a_d, m_a_norm_w, m_s5_lam_re, m_s5_lam_im, m_s5_log_step, m_s5_b_re, m_s5_b_im, m_s5_c_re, m_s5_c_im, m_s5_d, m_s5_glu_w, m_s5_glu_b, m_hg_lb_logits, m_hg_norm_w, m_final_norm_w, v_c_ctx, v_mod_w, v_mod_b, v_ffn_w_in, v_ffn_w_out, v_w_in, v_w_out, v_a_conv_w, v_a_conv_b, v_a_dt_bias, v_a_log, v_a_d, v_a_norm_w, v_s5_lam_re, v_s5_lam_im, v_s5_log_step, v_s5_b_re, v_s5_b_im, v_s5_c_re, v_s5_c_im, v_s5_d, v_s5_glu_w, v_s5_glu_b, v_hg_lb_logits, v_hg_norm_w, v_final_norm_w):
    given = dict(locals())
    w = {n: given[n] for n in WEIGHTS}
    m = {n: given["m_" + n] for n in WEIGHTS}
    v = {n: given["v_" + n] for n in WEIGHTS}
    small_sharded = SHARDED[len(BIG):]
    replicated = [n for n in WEIGHTS if n not in [s for s, _ in SHARDED]]
    core = lax.axis_index("c").astype(jnp.int32).reshape(1)
    chip = (2 * lax.axis_index("x") + lax.axis_index("y")).astype(jnp.int32)

    def leaves(t):
        return list(t.reshape((-1,) + t.shape[-2:]))

    def nest(kind, flat):
        flat = list(flat)
        return [flat[2 * l:2 * l + 2] for l in range(len(flat) // 2)] if w[kind].ndim == 4 else flat

    counts = [len(leaves(w[k])) for k in BIG]
    gathered = gather_pieces([a.astype(BF16) for k in BIG for a in leaves(w[k])], "gather_big")
    big, off = {}, 0
    for k, cnt in zip(BIG, counts):
        big[k], off = nest(k, gathered[off:off + cnt]), off + cnt
    small = {n: w[n] for n in replicated}
    small.update(_gather_full(w, small_sharded, F32, 8, "gather_small"))

    loss, grad_x, grads = local_step(x, c, ctx, loss_target, big, small)

    g_big = [g for k in BIG for g in jax.tree.leaves(grads[k])]
    got = swap_with_sibling(g_big, "rs_sibling", send_other_half_of_axis=1)
    pairs = [add_pair(core, g, t, f"rs_add_pair{i}") for i, (g, t) in enumerate(zip(g_big, got))]
    from_chips = pieces_to_chips(pairs, "rs_chips")
    mine = [sum_pieces(chip.reshape(1), p, t, f"rs_sum_chips{i}") for i, (p, t) in enumerate(zip(pairs, from_chips))]
    other = swap_with_sibling(mine, "rs_halves")
    out, off = {}, 0
    for k, cnt in zip(BIG, counts):
        stack = lambda t: t.reshape((cnt,) + t.shape[-2:])
        res = adamw_halves(core, mine[off:off + cnt], other[off:off + cnt], stack(w[k]), stack(m[k]), stack(v[k]),
                           "adamw_" + k)
        off += cnt
        for kind, a in zip(("grad", "delta", "new_m", "new_v"), res):
            out[kind, k] = a.reshape(w[k].shape)

    names = replicated + [n for n, _ in small_sharded]
    part = _pack([grads[n] for n in names] + [loss.reshape(1)], 8, F32)
    g_all = _unpack(sum_parts(all_gather_devices(part, "gather_parts"), "sum_parts"),
                    [grads[n].shape for n in names] + [(1,)])
    g_of = dict(zip(names + ["loss"], g_all))
    for n, ax in small_sharded:
        g_of[n] = lax.dynamic_slice_in_dim(g_of[n], chip * w[n].shape[ax], w[n].shape[ax], axis=ax)
    rest = lambda d: _pack([d[n] for n in names], 8, F32)
    g_rest = rest(g_of)
    d_rest, m_rest, v_rest = adamw(g_rest, rest(w), rest(m), rest(v), "adamw_rest")
    shapes = [w[n].shape for n in names]
    for kind, buf in (("grad", g_rest), ("delta", d_rest), ("new_m", m_rest), ("new_v", v_rest)):
        for n, a in zip(names, _unpack(buf, shapes)):
            out[kind, n] = a

    loss_total = g_of["loss"].reshape(())
    return (loss_total, grad_x, *[out["grad", n] for n in WEIGHTS], *[out["delta", n] for n in WEIGHTS],
            *[out["new_m", n] for n in WEIGHTS], *[out["new_v", n] for n in WEIGHTS])
```

```python
import functools
import math

import numpy as np
import jax
import jax.numpy as jnp
from jax import lax
from jax.experimental import pallas as pl
from jax.experimental.pallas import tpu as pltpu

F32, BF16 = jnp.float32, jnp.bfloat16
EPS = 1e-6
N_MOD = 9
D_FF = 2816
A_INNER, A_HEADS, A_GROUPS, A_STATE, A_CONV, A_CONV_DIM = 512, 8, 2, 64, 5, 768
A_COLS = A_INNER + A_CONV_DIM + 2 * A_HEADS
B_WIDTH, B_GROUP, B_NGROUPS, B_STATE = 256, 16, 16, 64
C_WIDTH, C_HEADS, C_KEY = 256, 4, 64
GRID_W = 64
IN_COLS = A_COLS + B_WIDTH + 5 * C_WIDTH
IN_COLS_PAD = 3072
ADAM_LR, ADAM_B1, ADAM_B2, ADAM_EPS, ADAM_WD, ADAM_STEP = 0.001, 0.9, 0.999, 1e-08, 0.01, 10

ROW_TILE = 256
SSD_Q, HG_Q = 128, 64
REC_T = 128
SCAN_SAMPLES = 4
ROW_SAMPLES = 2
ROW_SAMPLES_MAX_WIDTH = 1024
HG_BAND = 8
VMEM_LIMIT = 56 * 1024 * 1024
MM_VMEM_BUDGET = 36 * 1024 * 1024

NT = ((1,), (1,))
TN = ((0,), (0,))


def _bdot(a, b, dims=((1,), (0,))):
    return lax.dot_general(a.astype(BF16), b.astype(BF16), (dims, ((), ())), preferred_element_type=F32)


def _split3(x):
    hi = x.astype(BF16)
    r = x - hi.astype(F32)
    mid = r.astype(BF16)
    lo = (r - mid.astype(F32)).astype(BF16)
    return hi, mid, lo


def _mask_lhs(m, x, dims):
    return sum(lax.dot_general(m, p, (dims, ((), ())), preferred_element_type=F32) for p in _split3(x))


def _mask_rhs(x, m, dims):
    return sum(lax.dot_general(p, m, (dims, ((), ())), preferred_element_type=F32) for p in _split3(x))


@jax.custom_vjp
def mdot(m, x):
    return _mask_lhs(m, x, ((1,), (0,)))


def _mdot_fwd(m, x):
    return mdot(m, x), m


def _mdot_bwd(m, g):
    return jnp.zeros_like(m), _mask_lhs(m, g, TN)


mdot.defvjp(_mdot_fwd, _mdot_bwd)


@jax.custom_vjp
def xdot(x, m):
    return _mask_rhs(x, m, ((1,), (0,)))


def _xdot_fwd(x, m):
    return xdot(x, m), m


def _xdot_bwd(m, g):
    return _mask_rhs(g, m, NT), jnp.zeros_like(m)


xdot.defvjp(_xdot_fwd, _xdot_bwd)


@functools.partial(jax.custom_vjp, nondiff_argnums=(1,))
def _roll(x, s):
    return pltpu.roll(x, s, 0)


def _roll_fwd(x, s):
    return _roll(x, s), None


def _roll_bwd(s, _, g):
    return (_roll(g, x_rows(g) - s),)


def x_rows(x):
    return x.shape[0]


_roll.defvjp(_roll_fwd, _roll_bwd)


def sroll(x, s):
    s = s % x.shape[0]
    return x if s == 0 else _roll(x, s)


def _softplus(x):
    return jnp.maximum(x, 0.0) + jnp.log1p(jnp.exp(jnp.minimum(x, -x)))


def _sigmoid(x):
    return 0.5 * (jnp.tanh(0.5 * x) + 1.0)


def _silu(x):
    return x * _sigmoid(x)


def _rms(x):
    return x * lax.rsqrt(jnp.mean(x * x, axis=-1, keepdims=True) + EPS)


def _pick(n, cands):
    for c in cands:
        if n % c == 0:
            return c
    return n


def _cp(*sem):
    return pltpu.CompilerParams(dimension_semantics=sem, vmem_limit_bytes=VMEM_LIMIT)


def _tile(n, unit, cap):
    best = None
    for d in range(unit, min(n, cap) + 1, unit):
        if n % d == 0:
            best = d
    return best if best is not None else n


def _mm(a, b, ta, tb, out_dtype, name, b_pieces=False, out_pieces=False):
    M, K = (a.shape[1], a.shape[0]) if ta else a.shape
    n_div = k_div = None
    if b_pieces and tb:
        N, k_div = b.shape[1], b.shape[2]
    elif b_pieces:
        N, n_div = N_CHIPS * b.shape[2], b.shape[2]
    else:
        N = b.shape[0] if tb else b.shape[1]
    if out_pieces:
        n_div = N // N_CHIPS
    n_div, k_div = n_div or N, k_div or K
    tm = _tile(M, 128 if ta else 8, 1536 if ta else 1024)
    tn = _tile(n_div, 128, 1536)
    a_bytes, b_bytes, o_bytes = a.dtype.itemsize, b.dtype.itemsize, jnp.dtype(out_dtype).itemsize
    cands = {d for d in range(128, min(k_div, 2816) + 1, 128) if k_div % d == 0}
    if k_div <= 2816 or not cands:
        cands.add(k_div)
    for tk in sorted(cands, reverse=True):
        scratch = tm * tn * 4 if (K // tk > 1 and o_bytes != 4) else 0
        if 2 * (tm * tk * a_bytes + tk * tn * b_bytes + tm * tn * o_bytes) + scratch <= MM_VMEM_BUDGET:
            break
    nk = K // tk
    nq_n, nq_k = n_div // tn, k_div // tk
    dims = ((0 if ta else 1,), (1 if tb else 0,))

    def body(a_ref, b_ref, o_ref, *acc):
        part = _bdot(a_ref[...], b_ref[...], dims)
        if nk == 1:
            o_ref[...] = part.astype(o_ref.dtype)
            return
        acc_ref = acc[0] if acc else o_ref
        k = pl.program_id(2)

        @pl.when(k == 0)
        def _():
            acc_ref[...] = part

        @pl.when(k > 0)
        def _():
            acc_ref[...] += part

        if acc:
            @pl.when(k == nk - 1)
            def _():
                o_ref[...] = acc_ref[...].astype(o_ref.dtype)

    a_spec = pl.BlockSpec((tk, tm), lambda i, j, k: (k, i)) if ta else pl.BlockSpec((tm, tk), lambda i, j, k: (i, k))
    if b_pieces and tb:
        b_spec = pl.BlockSpec((None, tn, tk), lambda i, j, k: (k // nq_k, j, k % nq_k))
    elif b_pieces:
        b_spec = pl.BlockSpec((None, tk, tn), lambda i, j, k: (j // nq_n, k, j % nq_n))
    elif tb:
        b_spec = pl.BlockSpec((tn, tk), lambda i, j, k: (j, k))
    else:
        b_spec = pl.BlockSpec((tk, tn), lambda i, j, k: (k, j))
    if out_pieces:
        o_spec = pl.BlockSpec((None, tm, tn), lambda i, j, k: (j // nq_n, i, j % nq_n))
        o_shape = jax.ShapeDtypeStruct((N_CHIPS, M, N // N_CHIPS), out_dtype)
    else:
        o_spec = pl.BlockSpec((tm, tn), lambda i, j, k: (i, j))
        o_shape = jax.ShapeDtypeStruct((M, N), out_dtype)
    return pl.pallas_call(
        body, grid=(M // tm, N // tn, nk), in_specs=[a_spec, b_spec], out_specs=o_spec, out_shape=o_shape,
        scratch_shapes=[pltpu.VMEM((tm, tn), F32)] if (nk > 1 and o_bytes != 4) else [], name=name,
        compiler_params=_cp("parallel", "parallel", "arbitrary"))(a, b)


def linear(x, w16, wslot, name, out_dtype=F32, pieces=False):
    @jax.custom_vjp
    def f(x, w16, wslot):
        return _mm(x, w16, False, False, out_dtype, name + "_fwd", b_pieces=pieces)

    def fwd(x, w16, wslot):
        return f(x, w16, wslot), (x, w16)

    grad_dtype = wslot.dtype

    def bwd(res, dy):
        x, w16 = res
        dx = _mm(dy, w16, False, True, x.dtype, name + "_dx", b_pieces=pieces)
        dw = _mm(x, dy, True, False, grad_dtype, name + "_dw", out_pieces=pieces)
        return dx, jnp.zeros_like(w16), dw

    f.defvjp(fwd, bwd)
    return f(x, w16, wslot)


def rowwise(f, name, rows, mods, params, outs, n_ctx_blocks, tl=ROW_TILE, row_diff=None):
    rows, mods, params = tuple(rows), tuple(mods), tuple(params)
    nr, nm, npar, no = len(rows), len(mods), len(params), len(outs)
    B, L = rows[0].shape[:2]
    nblk = L // tl
    row_diff = tuple(row_diff) if row_diff is not None else (True,) * nr
    out_dtypes = [dt for _, dt in outs]
    widest = max([r.shape[2] for r in rows] + [w for w, _ in outs])
    ns = ROW_SAMPLES if (B % ROW_SAMPLES == 0 and widest <= ROW_SAMPLES_MAX_WIDTH) else 1

    def fcast(r, m, p):
        return tuple(o.astype(dt) for o, dt in zip(f(r, m, p), out_dtypes))

    def seg(i):
        return (i >= n_ctx_blocks).astype(jnp.int32) if n_ctx_blocks else 0

    def specs():
        row_specs = [pl.BlockSpec((ns, tl, r.shape[2]), lambda b, i: (b, i, 0)) for r in rows]
        mod_specs = [pl.BlockSpec((ns, 1, 1, m.shape[3]), lambda b, i: (b, seg(i), 0, 0)) for m in mods]
        par_specs = [pl.BlockSpec(p.shape, lambda b, i: (0, 0)) for p in params]
        out_specs = [pl.BlockSpec((ns, tl, w), lambda b, i: (b, i, 0)) for w, _ in outs]
        return row_specs, mod_specs, par_specs, out_specs

    def load(ins, s):
        r = tuple(x[s] for x in ins[:nr])
        m = tuple(x[s, 0] for x in ins[nr:nr + nm])
        p = tuple(x[...] for x in ins[nr + nm:])
        return r, m, p

    def fwd_call(rows, mods, params):
        def body(*refs):
            for s in range(ns):
                r, m, p = load(refs[:nr + nm + npar], s)
                for o_ref, o in zip(refs[nr + nm + npar:], fcast(r, m, p)):
                    o_ref[s] = o

        rs, ms, ps, os_ = specs()
        return pl.pallas_call(
            body, grid=(B // ns, nblk), in_specs=rs + ms + ps, out_specs=os_,
            out_shape=[jax.ShapeDtypeStruct((B, L, w), dt) for w, dt in outs],
            name=name + "_fwd", compiler_params=_cp("parallel", "parallel"))(*rows, *mods, *params)

    def bwd_call(rows, mods, params, douts):
        didx = [j for j in range(nr) if row_diff[j]]

        def body(*refs):
            n_in = nr + nm + npar
            dins = refs[n_in:n_in + no]
            rest = refs[n_in + no:]
            dr_refs, dm_refs, dp_refs = rest[:len(didx)], rest[len(didx):len(didx) + nm], rest[len(didx) + nm:]
            b, i = pl.program_id(0), pl.program_id(1)
            first_m = (i == 0) | (i == n_ctx_blocks) if n_ctx_blocks else (i == 0)
            first_p = (b == 0) & (i == 0)
            dp = None
            for s in range(ns):
                r, m, p = load(refs[:n_in], s)
                _, vjp = jax.vjp(fcast, r, m, p)
                dr, dm, dp_s = vjp(tuple(d[s] for d in dins))
                dp = dp_s if dp is None else tuple(a + g for a, g in zip(dp, dp_s))
                for ref, j in zip(dr_refs, didx):
                    ref[s] = dr[j].astype(ref.dtype)
                for ref, g in zip(dm_refs, dm):
                    @pl.when(first_m)
                    def _(ref=ref, g=g, s=s):
                        ref[s, 0] = g

                    @pl.when(jnp.logical_not(first_m))
                    def _(ref=ref, g=g, s=s):
                        ref[s, 0] += g
            for ref, g in zip(dp_refs, dp):
                @pl.when(first_p)
                def _(ref=ref, g=g):
                    ref[...] = g

                @pl.when(jnp.logical_not(first_p))
                def _(ref=ref, g=g):
                    ref[...] += g

        rs, ms, ps, os_ = specs()
        out_shape = ([jax.ShapeDtypeStruct(rows[j].shape, rows[j].dtype) for j in didx]
                     + [jax.ShapeDtypeStruct(m.shape, F32) for m in mods]
                     + [jax.ShapeDtypeStruct(p.shape, F32) for p in params])
        res = pl.pallas_call(
            body, grid=(B // ns, nblk), in_specs=rs + ms + ps + os_,
            out_specs=[rs[j] for j in didx] + ms + ps, out_shape=out_shape,
            name=name + "_bwd", compiler_params=_cp("arbitrary", "arbitrary"))(*rows, *mods, *params, *douts)
        dr = [None] * nr
        for j, g in zip(didx, res[:len(didx)]):
            dr[j] = g
        dr = tuple(g if g is not None else jnp.zeros_like(rows[j]) for j, g in enumerate(dr))
        return dr, tuple(res[len(didx):len(didx) + nm]), tuple(res[len(didx) + nm:])

    @jax.custom_vjp
    def op(rows, mods, params):
        return tuple(fwd_call(rows, mods, params))

    def op_fwd(rows, mods, params):
        return op(rows, mods, params), (rows, mods, params)

    def op_bwd(res, douts):
        return bwd_call(*res, tuple(douts))

    op.defvjp(op_fwd, op_bwd)
    return op(rows, mods, params)


def chunk_scan(step, name, rev, seqs, seq_t, params, consts, state_shapes, out_widths, Q, n_ctx):
    seqs, params, consts = tuple(seqs), tuple(params), tuple(consts)
    ns, npar, nc, nst, no = len(seqs), len(params), len(consts), len(state_shapes), len(out_widths)
    B = seqs[0].shape[0]
    L = seqs[0].shape[2] if seq_t[0] else seqs[0].shape[1]
    nck, ncc = L // Q, n_ctx // Q
    nb = SCAN_SAMPLES if B % SCAN_SAMPLES == 0 else 1

    def chunk_of(k):
        if not rev:
            return k
        return jnp.where(k < ncc, ncc - 1 - k, nck + ncc - 1 - k)

    def specs(order):
        def seq_spec(s, t):
            if t:
                return pl.BlockSpec((nb, s.shape[1], Q), lambda b, k: (b, 0, chunk_of(order(k))))
            return pl.BlockSpec((nb, Q, s.shape[2]), lambda b, k: (b, chunk_of(order(k)), 0))

        seq_specs = [seq_spec(s, t) for s, t in zip(seqs, seq_t)]
        par_specs = [pl.BlockSpec(p.shape, lambda b, k: (0, 0)) for p in params]
        con_specs = [pl.BlockSpec(c.shape, lambda b, k, nd=c.ndim: (0,) * nd) for c in consts]
        out_specs = [pl.BlockSpec((nb, Q, w), lambda b, k: (b, chunk_of(order(k)), 0)) for w in out_widths]
        sav_specs = [pl.BlockSpec((nb, 1) + tuple(s), lambda b, k: (b, order(k), 0, 0)) for s in state_shapes]
        return seq_specs, par_specs, con_specs, out_specs, sav_specs

    def fwd_call(seqs, params):
        def body(*refs):
            seq_refs = refs[:ns]
            par_refs = refs[ns:ns + npar]
            con_refs = refs[ns + npar:ns + npar + nc]
            rest = refs[ns + npar + nc:]
            out_refs, sav_refs, st_refs = rest[:no], rest[no:no + nst], rest[no + nst:]

            @pl.when(pl.program_id(1) == 0)
            def _():
                for s in st_refs:
                    s[...] = jnp.zeros_like(s)

            pvals, cvals = tuple(p[...] for p in par_refs), tuple(c[...] for c in con_refs)
            for i in range(nb):
                s_in = tuple(s[i] for s in st_refs)
                for sv, s in zip(sav_refs, s_in):
                    sv[i, 0] = s
                s_new, outs = step(s_in, tuple(r[i].astype(F32) for r in seq_refs), pvals, cvals)
                for s_ref, s in zip(st_refs, s_new):
                    s_ref[i] = s
                for o_ref, o in zip(out_refs, outs):
                    o_ref[i] = o

        ss, ps, cs, os_, vs = specs(lambda k: k)
        res = pl.pallas_call(
            body, grid=(B // nb, nck), in_specs=ss + ps + cs, out_specs=os_ + vs,
            out_shape=[jax.ShapeDtypeStruct((B, L, w), F32) for w in out_widths]
            + [jax.ShapeDtypeStruct((B, nck) + tuple(s), F32) for s in state_shapes],
            scratch_shapes=[pltpu.VMEM((nb,) + tuple(s), F32) for s in state_shapes],
            name=name + "_fwd", compiler_params=_cp("parallel", "arbitrary"))(*seqs, *params, *consts)
        return tuple(res[:no]), tuple(res[no:])

    def bwd_call(seqs, params, saved, douts):
        def body(*refs):
            seq_refs = refs[:ns]
            par_refs = refs[ns:ns + npar]
            con_refs = refs[ns + npar:ns + npar + nc]
            rest = refs[ns + npar + nc:]
            sav_refs, dout_refs = rest[:nst], rest[nst:nst + no]
            rest = rest[nst + no:]
            dseq_refs, dpar_refs, dst_refs = rest[:ns], rest[ns:ns + npar], rest[ns + npar:]
            b, k = pl.program_id(0), pl.program_id(1)

            @pl.when(k == 0)
            def _():
                for s in dst_refs:
                    s[...] = jnp.zeros_like(s)

            cvals, pvals = tuple(c[...] for c in con_refs), tuple(p[...] for p in par_refs)
            dp = None
            for i in range(nb):
                _, vjp = jax.vjp(lambda s, x, p: step(s, x, p, cvals), tuple(s[i, 0] for s in sav_refs),
                                 tuple(r[i].astype(F32) for r in seq_refs), pvals)
                ds, dx, dp_i = vjp((tuple(s[i] for s in dst_refs), tuple(d[i] for d in dout_refs)))
                for s_ref, s in zip(dst_refs, ds):
                    s_ref[i] = s
                for x_ref, x in zip(dseq_refs, dx):
                    x_ref[i] = x.astype(x_ref.dtype)
                dp = dp_i if dp is None else tuple(a + g for a, g in zip(dp, dp_i))
            first = (b == 0) & (k == 0)
            for ref, g in zip(dpar_refs, dp):
                @pl.when(first)
                def _(ref=ref, g=g):
                    ref[...] = g

                @pl.when(jnp.logical_not(first))
                def _(ref=ref, g=g):
                    ref[...] += g

        ss, ps, cs, os_, vs = specs(lambda k: nck - 1 - k)
        res = pl.pallas_call(
            body, grid=(B // nb, nck), in_specs=ss + ps + cs + vs + os_, out_specs=ss + ps,
            out_shape=[jax.ShapeDtypeStruct(s.shape, s.dtype) for s in seqs]
            + [jax.ShapeDtypeStruct(p.shape, F32) for p in params],
            scratch_shapes=[pltpu.VMEM((nb,) + tuple(s), F32) for s in state_shapes],
            name=name + "_bwd", compiler_params=_cp("arbitrary", "arbitrary"))(*seqs, *params, *consts, *saved, *douts)
        return tuple(res[:ns]), tuple(res[ns:])

    @jax.custom_vjp
    def op(seqs, params):
        return fwd_call(seqs, params)[0]

    def op_fwd(seqs, params):
        outs, saved = fwd_call(seqs, params)
        return outs, (seqs, params, saved)

    def op_bwd(res, douts):
        seqs, params, saved = res
        return bwd_call(seqs, params, saved, tuple(douts))

    op.defvjp(op_fwd, op_bwd)
    return op(seqs, params)


def _positions(Q, rev):
    t = np.arange(Q)
    return (Q - 1 - t) if rev else t


def _ssd_consts(rev):
    Q = SSD_Q
    pos = _positions(Q, rev)
    tri = pos[:, None] >= pos[None, :]
    head_of = np.arange(A_INNER) // (A_INNER // A_HEADS)
    expand = np.arange(A_HEADS)[:, None] == head_of[None, :]
    group_of_row = np.arange(A_GROUPS * A_STATE) // A_STATE
    block = group_of_row[:, None] == (head_of // (A_HEADS // A_GROUPS))[None, :]
    as_bf = lambda m: jnp.asarray(m.astype(np.float32), BF16)
    return (as_bf(tri), as_bf(tri.T), jnp.asarray(tri.astype(np.float32)), as_bf(expand),
            jnp.asarray(block.astype(np.float32)), jnp.asarray(expand.astype(np.float32)[:, None, :]))


def _ssd_step(state, seqs, params, consts):
    (st,) = state
    xs, bm, cm, dtr, dtr_t = seqs
    bias, bias_t, alog, alog_t = params
    tri_b, tri_t_b, tri_f, expand, block, head_lanes = consts
    dt = _softplus(dtr + bias)
    dta = dt * (-jnp.exp(alog))
    dt_t = _softplus(dtr_t + bias_t)
    dta_t = dt_t * (-jnp.exp(alog_t))
    cum = mdot(tri_b, dta)
    cum_t = xdot(dta_t, tri_t_b)
    total = jnp.sum(dta, axis=0, keepdims=True)
    w_end = jnp.exp(total - cum) * dt
    ecum_x = xdot(jnp.exp(cum), expand)
    wend_x = xdot(w_end, expand)
    dec_x = jnp.exp(jnp.sum(xdot(dta, expand), axis=0, keepdims=True))
    y = _bdot(cm, st) * ecum_x
    st_new = st * dec_x + block * _bdot(bm, xs * wend_x, TN)
    lane_g = lax.broadcasted_iota(jnp.int32, (1, A_GROUPS * A_STATE), 1) // A_STATE
    sub_h = lax.broadcasted_iota(jnp.int32, (A_HEADS, 1), 0)
    lane8 = lax.broadcasted_iota(jnp.int32, (1, A_HEADS), 1)
    Q = xs.shape[0]
    scores = []
    for g in range(A_GROUPS):
        cb = _bdot(jnp.where(lane_g == g, cm, 0.0), bm, NT)
        for h in range(g * (A_HEADS // A_GROUPS), (g + 1) * (A_HEADS // A_GROUPS)):
            col = jnp.sum(jnp.where(lane8 == h, cum, 0.0), axis=1, keepdims=True)
            row = jnp.sum(jnp.where(sub_h == h, cum_t, 0.0), axis=0, keepdims=True)
            dt_row = jnp.sum(jnp.where(sub_h == h, dt_t, 0.0), axis=0, keepdims=True)
            scores.append(cb * (tri_f * jnp.exp(tri_f * (col - row))) * dt_row)
    y_all = _bdot(jnp.stack(scores).reshape(A_HEADS * Q, Q), xs).reshape(A_HEADS, Q, A_INNER)
    y = y + jnp.sum(y_all * head_lanes, axis=0)
    return (st_new,), (y,)


def ssd_scan(name, rev, xs, bm, cm, dtr, dt_bias, a_log, n_ctx):
    seqs = (xs, bm, cm, dtr, jnp.swapaxes(dtr, 1, 2))
    params = (dt_bias.reshape(1, -1), dt_bias.reshape(-1, 1), a_log.reshape(1, -1), a_log.reshape(-1, 1))
    (y,) = chunk_scan(_ssd_step, name, rev, seqs, (False, False, False, False, True), params, _ssd_consts(rev),
                      [(A_GROUPS * A_STATE, A_INNER)], [A_INNER], SSD_Q, n_ctx)
    return y


def _recurrence(v, a, order, n_ctx, name, x_fwd=None):
    B, L = v.shape[:2]
    T = REC_T
    nck, ncc = L // T, n_ctx // T
    nlat = nck - ncc
    chunk_of = {"F": lambda k: k,
                "R": lambda k: jnp.where(k < ncc, ncc - 1 - k, nck + ncc - 1 - k),
                "FT": lambda k: nck - 1 - k,
                "RT": lambda k: jnp.where(k < nlat, ncc + k, k - nlat)}[order]
    descending = order in ("R", "FT")
    with_da = x_fwd is not None
    H = 8

    def body(*refs):
        if with_da:
            v_ref, a_ref, xf_ref, x_ref, da_ref, st_ref = refs
        else:
            v_ref, a_ref, x_ref, st_ref = refs
        k = pl.program_id(0)

        @pl.when(k == 0)
        def _():
            st_ref[...] = jnp.zeros_like(st_ref)
            if with_da:
                da_ref[...] = jnp.zeros_like(da_ref)

        ar, ai = a_ref[pl.ds(0, H), :], a_ref[pl.ds(H, H), :]
        zero = jnp.zeros((H, 128), F32)

        def step(i, carry):
            chains, acc_r, acc_i = carry
            t = (T - 1 - i) if descending else i
            out = []
            for b, (xr, xi) in enumerate(chains):
                if with_da:
                    f = xf_ref[b, t].astype(F32)
                    fr, fi = f[:H], f[H:]
                    acc_r = acc_r + xr * fr + xi * fi
                    acc_i = acc_i + xi * fr - xr * fi
                vt = v_ref[b, t].astype(F32)
                nr = ar * xr - ai * xi + vt[:H]
                ni = ar * xi + ai * xr + vt[H:]
                x_ref[b, t] = jnp.concatenate([nr, ni], axis=0).astype(x_ref.dtype)
                out.append((nr, ni))
            return tuple(out), acc_r, acc_i

        init = tuple((st_ref[b, pl.ds(0, H), :], st_ref[b, pl.ds(H, H), :]) for b in range(B))
        chains, acc_r, acc_i = lax.fori_loop(0, T, step, (init, zero, zero), unroll=8)
        for b, (xr, xi) in enumerate(chains):
            st_ref[b, pl.ds(0, H), :] = xr
            st_ref[b, pl.ds(H, H), :] = xi
        if with_da:
            da_ref[pl.ds(0, H), :] += acc_r
            da_ref[pl.ds(H, H), :] += acc_i

    seq = pl.BlockSpec((B, T, 2 * H, 128), lambda k: (0, chunk_of(k), 0, 0))
    par = pl.BlockSpec((2 * H, 128), lambda k: (0, 0))
    x_shape = jax.ShapeDtypeStruct(v.shape, v.dtype)
    if with_da:
        return pl.pallas_call(
            body, grid=(nck,), in_specs=[seq, par, seq], out_specs=[seq, par],
            out_shape=[x_shape, jax.ShapeDtypeStruct((2 * H, 128), F32)],
            scratch_shapes=[pltpu.VMEM((B, 2 * H, 128), F32)], name=name, compiler_params=_cp("arbitrary"))(v, a, x_fwd)
    return pl.pallas_call(
        body, grid=(nck,), in_specs=[seq, par], out_specs=seq, out_shape=x_shape,
        scratch_shapes=[pltpu.VMEM((B, 2 * H, 128), F32)], name=name, compiler_params=_cp("arbitrary"))(v, a)


def lin_rec(v, a, rev, n_ctx, name):
    fwd_order, bwd_order = ("R", "RT") if rev else ("F", "FT")
    conj = jnp.concatenate([jnp.ones((8, 128), F32), -jnp.ones((8, 128), F32)], axis=0)

    @jax.custom_vjp
    def f(v, a):
        return _recurrence(v, a, fwd_order, n_ctx, name + "_fwd")

    def fwd(v, a):
        x = f(v, a)
        return x, (x, a)

    def bwd(res, dx):
        x, a = res
        g, da = _recurrence(dx, a * conj, bwd_order, n_ctx, name + "_bwd", x_fwd=x)
        return g, da

    f.defvjp(fwd, bwd)
    return f(v, a)


def _block_diag(t):
    G, a, b = t.shape
    eye = jnp.eye(G, dtype=t.dtype)
    return (t[:, :, None, :] * eye[:, None, :, None]).reshape(G * a, G * b)


def s5_scan(name, rev, u, lam_re, lam_im, log_step, b_re, b_im, c_re, c_im, n_ctx):
    step = jnp.exp(log_step)[:, None]
    mag = jnp.exp(lam_re * step)
    ar = mag * jnp.cos(lam_im * step)
    ai = mag * jnp.sin(lam_im * step)
    den = lam_re * lam_re + lam_im * lam_im
    nr = ar - 1.0
    kr = (nr * lam_re + ai * lam_im) / den
    ki = (ai * lam_re - nr * lam_im) / den
    br = kr[..., None] * b_re - ki[..., None] * b_im
    bi = kr[..., None] * b_im + ki[..., None] * b_re
    B, L, _ = u.shape
    width = B_NGROUPS * B_STATE
    w_b = jnp.concatenate([_block_diag(jnp.swapaxes(br, 1, 2)), _block_diag(jnp.swapaxes(bi, 1, 2))], axis=1)
    w_c = jnp.concatenate([_block_diag(jnp.swapaxes(c_re, 1, 2)), -_block_diag(jnp.swapaxes(c_im, 1, 2))], axis=0)
    a = jnp.concatenate([ar.reshape(8, 128), ai.reshape(8, 128)], axis=0)
    v = linear(u.reshape(B * L, B_WIDTH), w_b.astype(BF16), w_b, name + "_in", BF16)
    x = lin_rec(v.reshape(B, L, 16, 128), a, rev, n_ctx, name)
    y = linear(x.reshape(B * L, 2 * width), w_c.astype(BF16), w_c, name + "_out")
    return y.reshape(B, L, B_WIDTH)


def _hg_consts(rev):
    Q = HG_Q
    pos = _positions(Q, rev)
    pi, pj = pos[:, None], pos[None, :]
    mats = [pi >= pj, pj > pi]
    pairs = []
    s = HG_BAND
    while s < Q:
        second = (pos // s) % 2 == 1
        mid = (pos // (2 * s)) * 2 * s + s
        mats.append(second[:, None] & (pj >= mid[:, None]) & (pj <= pi))
        mats.append((~second)[:, None] & (pj > pi) & (pj < mid[:, None]))
        pairs.append(second[:, None] & (~second)[None, :] & ((pi // (2 * s)) == (pj // (2 * s))))
        s *= 2
    rows = [((pos // s_) % 2 == 1) for s_ in [HG_BAND * 2 ** n for n in range(len(pairs))]]
    band = [(pos % HG_BAND) >= d for d in range(HG_BAND)]
    mats_b = jnp.asarray(np.stack(mats).astype(np.float32), BF16)
    pairs_f = jnp.asarray(np.tile(np.stack(pairs).astype(np.float32), (1, C_HEADS, 1)))
    rows_f = jnp.asarray(np.stack(rows).astype(np.float32)[:, :, None])
    band_f = jnp.asarray(np.stack(band).astype(np.float32)[:, :, None])
    head_lanes = np.arange(C_HEADS)[:, None] == (np.arange(C_WIDTH) // C_KEY)[None, :]
    return mats_b, pairs_f, rows_f, band_f, jnp.asarray(head_lanes.astype(np.float32)[:, None, :])


def _hg_step_fn(rev, n_levels):
    def step(state, seqs, params, consts):
        (st,) = state
        qr, zf, v = seqs
        (lb,) = params
        mats, pairs, rows, band, head_lanes = consts
        W = C_WIDTH
        Q = qr.shape[0]
        same_head = (lax.broadcasted_iota(jnp.int32, (W, W), 0) // C_KEY
                     == lax.broadcasted_iota(jnp.int32, (W, W), 1) // C_KEY)
        q = _silu(qr)
        f = lb + (1.0 - lb) * jax.nn.sigmoid(zf)
        logf = jnp.log(f)
        kk = 1.0 - f
        qd = q * jnp.exp(mdot(mats[0], logf))
        w = kk * jnp.exp(mdot(mats[1], logf))
        total = jnp.sum(logf, axis=0, keepdims=True)
        o = _bdot(qd, st, NT)
        st_new = st * jnp.exp(total) + jnp.where(same_head, _bdot(v, w, TN), 0.0)
        scores = None
        for n in range(n_levels):
            a = q * jnp.exp(mdot(mats[2 + 2 * n], logf)) * rows[n]
            bk = kk * jnp.exp(mdot(mats[3 + 2 * n], logf)) * (1.0 - rows[n])
            sc = pairs[n] * _bdot((a[None] * head_lanes).reshape(C_HEADS * Q, W), bk, NT)
            scores = sc if scores is None else scores + sc
        o = o + jnp.sum(_bdot(scores, v).reshape(C_HEADS, Q, W) * head_lanes, axis=0)
        same_head_b = same_head.astype(BF16)
        e = jnp.zeros_like(logf)
        for d in range(HG_BAND):
            s = -d if rev else d
            if d > 0:
                e = e + sroll(logf, s + (1 if rev else -1))
            kd, vd = (kk, v) if d == 0 else (sroll(kk, s), sroll(v, s))
            p = q * kd * jnp.exp(e)
            o = o + band[d] * (_bdot(p, same_head_b) * vd)
        return (st_new,), (o,)

    return step


def hg_scan(name, rev, qr, zf, v, lower, n_ctx):
    consts = _hg_consts(rev)
    (o,) = chunk_scan(_hg_step_fn(rev, consts[1].shape[0]), name, rev, (qr, zf, v), (False,) * 3,
                      (lower.reshape(1, -1),), consts, [(C_WIDTH, C_WIDTH)], [C_WIDTH], HG_Q, n_ctx)
    return o


def conv_silu(x, w, b, n_ctx, name):
    B, L, C = x.shape
    TC = 128
    w8 = jnp.zeros((8, C), F32).at[:A_CONV].set(w)
    b2 = b.reshape(1, C)
    pad = A_CONV // 2

    def taps(v, sign):
        t = lax.broadcasted_iota(jnp.int32, v.shape, 0)
        out = []
        for k in range(A_CONV):
            s = sign * (k - pad)
            src = t + s
            ok = (src >= 0) & (src < L) & ((t >= n_ctx) == (src >= n_ctx))
            vs = v if s == 0 else pltpu.roll(v, (-s) % L, 0)
            out.append((k, jnp.where(ok, vs, 0.0)))
        return out

    def pre(x_ref, w_ref, b_ref):
        xv = x_ref[0].astype(F32)
        y = b_ref[...] + sum(xs * w_ref[pl.ds(k, 1), :] for k, xs in taps(xv, 1))
        return xv, y

    def fwd_body(x_ref, w_ref, b_ref, o_ref):
        _, y = pre(x_ref, w_ref, b_ref)
        o_ref[0] = _silu(y)

    def bwd_body(x_ref, w_ref, b_ref, g_ref, dx_ref, dw_ref, db_ref):
        xv, y = pre(x_ref, w_ref, b_ref)
        sg = _sigmoid(y)
        dy = g_ref[0] * (sg + y * sg * (1.0 - sg))
        dx_ref[0] = sum(ds * w_ref[pl.ds(k, 1), :] for k, ds in taps(dy, -1)).astype(dx_ref.dtype)
        first = pl.program_id(1) == 0

        @pl.when(first)
        def _():
            dw_ref[...] = jnp.zeros_like(dw_ref)
            db_ref[...] = jnp.zeros_like(db_ref)

        for k, xs in taps(xv, 1):
            dw_ref[pl.ds(k, 1), :] += jnp.sum(dy * xs, axis=0, keepdims=True)
        db_ref[...] += jnp.sum(dy, axis=0, keepdims=True)

    x_spec = pl.BlockSpec((1, L, TC), lambda j, bb: (bb, 0, j))
    w_spec = pl.BlockSpec((8, TC), lambda j, bb: (0, j))
    b_spec = pl.BlockSpec((1, TC), lambda j, bb: (0, j))

    @jax.custom_vjp
    def op(x, w8, b2):
        return pl.pallas_call(fwd_body, grid=(C // TC, B), in_specs=[x_spec, w_spec, b_spec], out_specs=x_spec,
                              out_shape=jax.ShapeDtypeStruct(x.shape, F32), name=name + "_fwd",
                              compiler_params=_cp("parallel", "parallel"))(x, w8, b2)

    def op_fwd(x, w8, b2):
        return op(x, w8, b2), (x, w8, b2)

    def op_bwd(res, g):
        x, w8, b2 = res
        return tuple(pl.pallas_call(
            bwd_body, grid=(C // TC, B), in_specs=[x_spec, w_spec, b_spec, x_spec],
            out_specs=[x_spec, w_spec, b_spec],
            out_shape=[jax.ShapeDtypeStruct(x.shape, x.dtype), jax.ShapeDtypeStruct(w8.shape, F32),
                       jax.ShapeDtypeStruct(b2.shape, F32)],
            name=name + "_bwd", compiler_params=_cp("parallel", "arbitrary"))(x, w8, b2, g))

    op.defvjp(op_fwd, op_bwd)
    return op(x, w8, b2)


def loss_head(h, target, norm_w, n_ctx):
    B, L, Dm = h.shape
    S = target.shape[1]
    tl = ROW_TILE
    skip = n_ctx // tl

    def body(h_ref, t_ref, w_ref, loss_ref, dh_ref, dw_ref):
        b, i = pl.program_id(0), pl.program_id(1)
        fn = lambda hv, wv: _rms(hv) * wv
        y, vjp = jax.vjp(fn, h_ref[0], w_ref[...])
        err = y - t_ref[0]
        dhv, dwv = vjp(err * (1.0 / Dm))
        dh_ref[0] = dhv
        part = 0.5 * jnp.sum(jnp.mean(err * err, axis=-1, keepdims=True), axis=0, keepdims=True)
        first = (b == 0) & (i == 0)

        @pl.when(first)
        def _():
            loss_ref[...] = jnp.zeros_like(loss_ref)
            dw_ref[...] = jnp.zeros_like(dw_ref)

        loss_ref[...] += jnp.broadcast_to(part, loss_ref.shape)
        dw_ref[...] += dwv

    loss, dh, dw = pl.pallas_call(
        body, grid=(B, S // tl),
        in_specs=[pl.BlockSpec((1, tl, Dm), lambda b, i: (b, i + skip, 0)),
                  pl.BlockSpec((1, tl, Dm), lambda b, i: (b, i, 0)),
                  pl.BlockSpec((1, Dm), lambda b, i: (0, 0))],
        out_specs=[pl.BlockSpec((8, 128), lambda b, i: (0, 0)),
                   pl.BlockSpec((1, tl, Dm), lambda b, i: (b, i, 0)),
                   pl.BlockSpec((1, Dm), lambda b, i: (0, 0))],
        out_shape=[jax.ShapeDtypeStruct((8, 128), F32), jax.ShapeDtypeStruct((B, S, Dm), F32),
                   jax.ShapeDtypeStruct((1, Dm), F32)],
        name="loss_head", compiler_params=_cp("arbitrary", "arbitrary"))(h, target, norm_w.reshape(1, Dm))
    dh_full = jnp.concatenate([jnp.zeros((B, n_ctx, Dm), F32), dh], axis=1)
    return loss[0, 0], dh_full, dw.reshape(Dm)


def _modulate(h, shift, scale):
    return _rms(h) * (1.0 + scale) + shift


def _f_mod(r, m, p):
    return (_modulate(r[0], m[0], m[1]),)


def _f_resid_mod(coef):
    def f(r, m, p):
        h2 = r[0] + coef * m[0] * r[1]
        return h2, _modulate(h2, m[1], m[2])
    return f


def _f_resid(coef):
    def f(r, m, p):
        return (r[0] + coef * m[0] * r[1],)
    return f


def _f_swiglu(r, m, p):
    pre = r[0].astype(F32)
    return (_silu(pre[:, :D_FF]) * pre[:, D_FF:],)


def _f_ssd_post(r, m, p):
    y_f, y_b, xs, z = (t.astype(F32) for t in r)
    d_skip, norm_w = p
    y = (y_f + y_b + d_skip * xs) * _silu(z)
    return (_rms(y) * norm_w,)


def _f_s5_post(r, m, p):
    y_f, y_b, u = (t.astype(F32) for t in r)
    d_skip, glu_w, glu_b = p
    y = jax.nn.gelu(y_f + y_b + d_skip * u)
    return (y * _sigmoid(_bdot(y, glu_w) + glu_b),)


def _f_hg_post(r, m, p):
    o_f, o_b, g = (t.astype(F32) for t in r)
    (norm_w,) = p
    W = C_WIDTH
    same_head = (lax.broadcasted_iota(jnp.int32, (W, W), 0) // C_KEY
                 == lax.broadcasted_iota(jnp.int32, (W, W), 1) // C_KEY).astype(BF16)
    o = o_f + o_b
    ms = xdot(o * o, same_head) * (1.0 / C_KEY)
    return (o * lax.rsqrt(ms + EPS) * norm_w * _silu(g),)


def _pad_w_in(w):
    dt0 = A_INNER + A_CONV_DIM
    zeros = jnp.zeros((w.shape[0], IN_COLS_PAD - IN_COLS), w.dtype)
    return jnp.concatenate([w[:, :dt0], w[:, A_COLS:], w[:, dt0:A_COLS], zeros], axis=1)


def _to_columns(t, n_ctx):
    B, L, W = t.shape
    rows = (L - n_ctx) // GRID_W
    lat = t[:, n_ctx:].reshape(B, rows, GRID_W, W).transpose(0, 2, 1, 3).reshape(B, L - n_ctx, W)
    return jnp.concatenate([t[:, :n_ctx], lat], axis=1)


def _to_raster(t, n_ctx):
    B, L, W = t.shape
    rows = (L - n_ctx) // GRID_W
    lat = t[:, n_ctx:].reshape(B, GRID_W, rows, W).transpose(0, 2, 1, 3).reshape(B, L - n_ctx, W)
    return jnp.concatenate([t[:, :n_ctx], lat], axis=1)


def _forward(h0, c, big, P, n_ctx):
    B, L, Dm = h0.shape
    depth = len(big["mod_w"])
    ncb = n_ctx // ROW_TILE

    def rows_of(t):
        return t.reshape(t.shape[0] * t.shape[1], t.shape[2])

    def w_in_of(t):
        return _pad_w_in(jnp.moveaxis(t, 0, 1).reshape(t.shape[1], IN_COLS))

    p_lb = jax.nn.softmax(P["hg_lb_logits"], axis=0)
    lower_bounds = jnp.cumsum(p_lb, axis=0) - p_lb[:1]
    cc = jnp.concatenate([c, P["c_ctx"][None], jnp.zeros((8 - B - 1, Dm), F32)], axis=0)
    cc = _silu(cc)

    def mods_of(l):
        m = linear(cc, big["mod_w"][l], P["mod_w"][l], f"mod{l}", pieces=True) + P["mod_b"][l]
        m = m.reshape(8, N_MOD, Dm)
        seg = jnp.stack([jnp.broadcast_to(m[B], (B, N_MOD, Dm)), m[:B]], axis=1)
        return [seg[:, :, j:j + 1, :] for j in range(N_MOD)]

    def ffn(u, l, j):
        pre = linear(u.reshape(B * L, Dm), big["ffn_w_in"][l][j], P["ffn_w_in"][l][j], f"ffn_in{l}{j}", BF16,
                     pieces=True)
        (act,) = rowwise(_f_swiglu, f"swiglu{l}{j}", [pre.reshape(B, L, 2 * D_FF)], [], [], [(D_FF, BF16)], ncb)
        out = linear(act.reshape(B * L, D_FF), rows_of(big["ffn_w_out"][l][j]), rows_of(P["ffn_w_out"][l][j]),
                     f"ffn_out{l}{j}")
        return out.reshape(B, L, Dm)

    h = h0
    mods = mods_of(0)
    (u,) = rowwise(_f_mod, "mod_first", [h], [mods[0], mods[1]], [], [(Dm, BF16)], ncb)
    for l in range(depth):
        col_major = l % 2 == 1
        o = ffn(u, l, 0)
        h, u = rowwise(_f_resid_mod(0.5), f"resid_a{l}", [h, o], [mods[2], mods[3], mods[4]], [],
                       [(Dm, F32), (Dm, BF16)], ncb)
        if col_major:
            u = _to_columns(u, n_ctx)
        pre = linear(u.reshape(B * L, Dm), w_in_of(big["w_in"][l]), w_in_of(P["w_in"][l]), f"w_in{l}", BF16)
        pre = pre.reshape(B, L, IN_COLS_PAD)
        o0 = 0
        z, o0 = pre[..., o0:o0 + A_INNER], o0 + A_INNER
        xbc, o0 = pre[..., o0:o0 + A_CONV_DIM], o0 + A_CONV_DIM
        pb, o0 = pre[..., o0:o0 + B_WIDTH], o0 + B_WIDTH
        pc, o0 = pre[..., o0:o0 + 5 * C_WIDTH], o0 + 5 * C_WIDTH
        dtr = pre[..., o0:o0 + 2 * A_HEADS]
        xbc = conv_silu(xbc, P["a_conv_w"][l], P["a_conv_b"][l], n_ctx, f"conv{l}")
        xs, bm, cm = xbc[..., :A_INNER], xbc[..., A_INNER:A_INNER + 128], xbc[..., A_INNER + 128:]
        ya_dir = [ssd_scan(f"ssd{l}{d}", bool(d), xs, bm, cm, dtr[..., d * A_HEADS:(d + 1) * A_HEADS],
                           P["a_dt_bias"][l, d], P["a_log"][l, d], n_ctx) for d in range(2)]
        d_skip = jnp.repeat(P["a_d"][l], A_INNER // A_HEADS).reshape(1, A_INNER)
        (ya,) = rowwise(_f_ssd_post, f"ssd_post{l}", [ya_dir[0], ya_dir[1], xs, z], [],
                        [d_skip, P["a_norm_w"][l].reshape(1, -1)], [(A_INNER, BF16)], ncb)
        yb_dir = [s5_scan(f"s5{l}{d}", bool(d), pb, P["s5_lam_re"][l, d], P["s5_lam_im"][l, d],
                          P["s5_log_step"][l, d], P["s5_b_re"][l, d], P["s5_b_im"][l, d], P["s5_c_re"][l, d],
                          P["s5_c_im"][l, d], n_ctx) for d in range(2)]
        (yb,) = rowwise(_f_s5_post, f"s5_post{l}", [yb_dir[0], yb_dir[1], pb], [],
                        [P["s5_d"][l].reshape(1, -1), P["s5_glu_w"][l], P["s5_glu_b"][l].reshape(1, -1)],
                        [(B_WIDTH, BF16)], ncb)
        qr, f_raw, vi, gate = (pc[..., :C_WIDTH], pc[..., C_WIDTH:3 * C_WIDTH], pc[..., 3 * C_WIDTH:4 * C_WIDTH],
                               pc[..., 4 * C_WIDTH:])
        yc_dir = [hg_scan(f"hg{l}{d}", bool(d), qr, f_raw[..., d * C_WIDTH:(d + 1) * C_WIDTH], vi,
                          lower_bounds[l, d], n_ctx) for d in range(2)]
        (yc,) = rowwise(_f_hg_post, f"hg_post{l}", [yc_dir[0], yc_dir[1], gate], [],
                        [P["hg_norm_w"][l].reshape(1, -1)], [(C_WIDTH, BF16)], ncb)
        mix = jnp.concatenate([ya, yb, yc], axis=-1)
        y = linear(mix.reshape(B * L, Dm), rows_of(big["w_out"][l]), rows_of(P["w_out"][l]), f"w_out{l}")
        y = y.reshape(B, L, Dm)
        if col_major:
            y = _to_raster(y, n_ctx)
        h, u = rowwise(_f_resid_mod(1.0), f"resid_b{l}", [h, y], [mods[5], mods[6], mods[7]], [],
                       [(Dm, F32), (Dm, BF16)], ncb)
        o = ffn(u, l, 1)
        gate8 = mods[8]
        if l + 1 < depth:
            mods = mods_of(l + 1)
            h, u = rowwise(_f_resid_mod(0.5), f"resid_c{l}", [h, o], [gate8, mods[0], mods[1]], [],
                           [(Dm, F32), (Dm, BF16)], ncb)
        else:
            (h,) = rowwise(_f_resid(0.5), f"resid_c{l}", [h, o], [gate8], [], [(Dm, F32)], ncb)
    return h


BIG = ("mod_w", "ffn_w_in", "ffn_w_out", "w_in", "w_out")


def local_step(x, c, ctx, target, big, small):
    n_ctx = ctx.shape[1]
    h0 = jnp.concatenate([ctx, x], axis=1)
    P = dict(small)
    for k in BIG:
        P[k] = jax.tree.map(lambda t: jnp.zeros(t.shape, BF16), big[k])
    h, vjp = jax.vjp(lambda h0, P: _forward(h0, c, big, P, n_ctx), h0, P)
    loss, dh, d_final = loss_head(h, target, small["final_norm_w"], n_ctx)
    dh0, grads = vjp(dh)
    grads = dict(grads)
    grads["final_norm_w"] = grads["final_norm_w"] + d_final
    return loss, dh0[:, n_ctx:], grads


MESH = pl.DeviceIdType.MESH
LANES = 1024
N_CHIPS = 4
ELEMENTWISE_BLOCK_BYTES = 2 * 1024 * 1024
ANY = pl.BlockSpec(memory_space=pl.ANY)


def _place():
    x, y, c = lax.axis_index("x"), lax.axis_index("y"), lax.axis_index("c")
    return x, y, c, 2 * x + y


def _other_chips(x, y):
    return [(x ^ kx, y ^ ky, 2 * (x ^ kx) + (y ^ ky)) for kx, ky in ((0, 1), (1, 0), (1, 1))]


def all_gather_shards(shard, name):
    def body(x_ref, out_ref, send_sems, recv_sems, local_sem):
        x, y, c, q = _place()
        sibling = (x, y, 1 - c)
        chips = _other_chips(x, y)

        def copy(k, src, dst, to):
            return pltpu.make_async_remote_copy(src_ref=src, dst_ref=dst, send_sem=send_sems.at[k],
                                                recv_sem=recv_sems.at[k], device_id=to, device_id_type=MESH)

        mine = pltpu.make_async_copy(x_ref, out_ref.at[q], local_sem)
        mine.start()
        first = [copy(k, x_ref.at[c], out_ref.at[q, c], (px, py, c)) for k, (px, py, _) in enumerate(chips)]
        for cp in first:
            cp.start()
        passed = [copy(3 + k, out_ref.at[pq, c], out_ref.at[pq, c], sibling) for k, (_, _, pq) in enumerate(chips)]
        for k, (_, _, pq) in enumerate(chips):
            copy(k, x_ref.at[c], out_ref.at[pq, c], sibling).wait_recv()
            passed[k].start()
        for k, (_, _, pq) in enumerate(chips):
            copy(3 + k, x_ref.at[c], out_ref.at[pq, 1 - c], sibling).wait_recv()
        for cp in first + passed:
            cp.wait_send()
        mine.wait()

    return pl.pallas_call(
        body, out_shape=jax.ShapeDtypeStruct((N_CHIPS,) + shard.shape, shard.dtype), in_specs=[ANY], out_specs=ANY,
        scratch_shapes=[pltpu.SemaphoreType.DMA((6,)), pltpu.SemaphoreType.DMA((6,)), pltpu.SemaphoreType.DMA],
        name=name)(shard)


def all_gather_devices(part, name):
    def body(x_ref, out_ref, send_sems, recv_sems, local_sem):
        x, y, c, _ = _place()
        me = 4 * x + 2 * y + c
        mine = pltpu.make_async_copy(x_ref, out_ref.at[me], local_sem)
        mine.start()
        copies = []
        for k in range(1, 8):
            px, py, pc = x ^ (k >> 2), y ^ ((k >> 1) & 1), c ^ (k & 1)
            copies.append(pltpu.make_async_remote_copy(
                src_ref=x_ref, dst_ref=out_ref.at[me], send_sem=send_sems.at[k - 1], recv_sem=recv_sems.at[k - 1],
                device_id=(px, py, pc), device_id_type=MESH))
            copies[-1].start()
        for k in range(1, 8):
            peer = 4 * (x ^ (k >> 2)) + 2 * (y ^ ((k >> 1) & 1)) + (c ^ (k & 1))
            pltpu.make_async_remote_copy(
                src_ref=x_ref, dst_ref=out_ref.at[peer], send_sem=send_sems.at[k - 1], recv_sem=recv_sems.at[k - 1],
                device_id=(x, y, c), device_id_type=MESH).wait_recv()
        for cp in copies:
            cp.wait_send()
        mine.wait()

    return pl.pallas_call(
        body, out_shape=jax.ShapeDtypeStruct((8,) + part.shape, part.dtype), in_specs=[ANY], out_specs=ANY,
        scratch_shapes=[pltpu.SemaphoreType.DMA((7,)), pltpu.SemaphoreType.DMA((7,)), pltpu.SemaphoreType.DMA],
        name=name)(part)


def _half(ref, c, axis):
    h = ref.shape[axis] // 2
    return ref.at[(slice(None),) * axis + (pl.ds(pl.multiple_of(c * h, 8), h),)]


def _remote(src, dst, send_sems, recv_sems, s, to):
    return pltpu.make_async_remote_copy(src_ref=src, dst_ref=dst, send_sem=send_sems.at[s], recv_sem=recv_sems.at[s],
                                        device_id=to, device_id_type=MESH)


def gather_pieces(arrays, name):
    n = len(arrays)

    def body(*refs):
        ins, outs, (send_sems, recv_sems) = refs[:n], refs[n:2 * n], refs[2 * n:]
        x, y, c, q = _place()
        sibling = (x, y, 1 - c)
        chips = _other_chips(x, y)
        started = []

        def go(src, dst, s, to):
            started.append(_remote(src, dst, send_sems, recv_sems, s, to))
            started[-1].start()

        for i, (a, o) in enumerate(zip(ins, outs)):
            go(a, o.at[q], 7 * i + 6, sibling)
            for k, (px, py, _) in enumerate(chips):
                go(_half(a, c, 0), _half(o.at[q], c, 0), 7 * i + k, (px, py, c))
        for i, o in enumerate(outs):
            for k, (_, _, pq) in enumerate(chips):
                land = _half(o.at[pq], c, 0)
                _remote(land, land, send_sems, recv_sems, 7 * i + k, sibling).wait_recv()
                go(land, land, 7 * i + 3 + k, sibling)
        for i, (a, o) in enumerate(zip(ins, outs)):
            for k, (_, _, pq) in enumerate(chips):
                land = _half(o.at[pq], 1 - c, 0)
                _remote(land, land, send_sems, recv_sems, 7 * i + 3 + k, sibling).wait_recv()
            _remote(a, o.at[q], send_sems, recv_sems, 7 * i + 6, sibling).wait_recv()
        for cp in started:
            cp.wait_send()

    return pl.pallas_call(
        body, out_shape=[jax.ShapeDtypeStruct((N_CHIPS,) + a.shape, a.dtype) for a in arrays], in_specs=[ANY] * n,
        out_specs=[ANY] * n, scratch_shapes=[pltpu.SemaphoreType.DMA((7 * n,)), pltpu.SemaphoreType.DMA((7 * n,))],
        name=name)(*arrays)


def swap_with_sibling(arrays, name, send_other_half_of_axis=None):
    n = len(arrays)
    ax = send_other_half_of_axis

    def body(*refs):
        ins, outs, (send_sems, recv_sems) = refs[:n], refs[n:2 * n], refs[2 * n:]
        x, y, c, _ = _place()
        copies = [_remote(a if ax is None else _half(a, 1 - c, ax), o, send_sems, recv_sems, i, (x, y, 1 - c))
                  for i, (a, o) in enumerate(zip(ins, outs))]
        for cp in copies:
            cp.start()
        for cp in copies:
            cp.wait()

    def out_of(a):
        shape = list(a.shape)
        if ax is not None:
            shape[ax] //= 2
        return jax.ShapeDtypeStruct(tuple(shape), a.dtype)

    return pl.pallas_call(
        body, out_shape=[out_of(a) for a in arrays], in_specs=[ANY] * n, out_specs=[ANY] * n,
        scratch_shapes=[pltpu.SemaphoreType.DMA((n,)), pltpu.SemaphoreType.DMA((n,))], name=name)(*arrays)


def pieces_to_chips(pairs, name):
    n = len(pairs)

    def body(*refs):
        ins, outs, (send_sems, recv_sems) = refs[:n], refs[n:2 * n], refs[2 * n:]
        x, y, c, q = _place()
        chips = _other_chips(x, y)
        copies = [_remote(a.at[pq], o.at[q], send_sems, recv_sems, 3 * i + k, (px, py, c))
                  for i, (a, o) in enumerate(zip(ins, outs)) for k, (px, py, pq) in enumerate(chips)]
        for cp in copies:
            cp.start()
        for i, (a, o) in enumerate(zip(ins, outs)):
            for k, (_, _, pq) in enumerate(chips):
                _remote(a.at[pq], o.at[pq], send_sems, recv_sems, 3 * i + k, (x, y, c)).wait_recv()
        for cp in copies:
            cp.wait_send()

    return pl.pallas_call(
        body, out_shape=[jax.ShapeDtypeStruct(a.shape, a.dtype) for a in pairs], in_specs=[ANY] * n,
        out_specs=[ANY] * n, scratch_shapes=[pltpu.SemaphoreType.DMA((3 * n,)), pltpu.SemaphoreType.DMA((3 * n,))],
        name=name)(*pairs)


def _rows_block(h, n_cols):
    return _tile(h, 16, max(16, ELEMENTWISE_BLOCK_BYTES // (n_cols * 4)))


def add_pair(core, g, got, name):
    _, K, N = g.shape
    h = K // 2
    th = _rows_block(h, N)
    nb = h // th

    def body(c_ref, g_ref, got_ref, o_ref):
        o_ref[...] = (g_ref[...].astype(F32) + got_ref[...].astype(F32)).astype(o_ref.dtype)

    grid_spec = pltpu.PrefetchScalarGridSpec(
        num_scalar_prefetch=1, grid=(N_CHIPS, nb),
        in_specs=[pl.BlockSpec((None, th, N), lambda p, i, c_ref: (p, c_ref[0] * nb + i, 0)),
                  pl.BlockSpec((None, th, N), lambda p, i, c_ref: (p, i, 0))],
        out_specs=pl.BlockSpec((None, th, N), lambda p, i, c_ref: (p, i, 0)))
    return pl.pallas_call(body, grid_spec=grid_spec, out_shape=jax.ShapeDtypeStruct(got.shape, g.dtype), name=name,
                          compiler_params=_cp("parallel", "parallel"))(core, g, got)


def sum_pieces(chip, pair, got, name):
    _, h, N = pair.shape
    th = _rows_block(h, N)

    def body(q_ref, pair_ref, *rest):
        got_refs, o_ref = rest[:N_CHIPS], rest[N_CHIPS]
        q = q_ref[0]
        for p in range(N_CHIPS):
            def put(val, p=p):
                if p == 0:
                    o_ref[...] = val
                else:
                    o_ref[...] += val

            @pl.when(q == p)
            def _():
                put(pair_ref[...].astype(F32))

            @pl.when(q != p)
            def _(p=p):
                put(got_refs[p][...].astype(F32))

    def got_spec(p):
        return pl.BlockSpec((None, th, N), lambda i, q_ref: (jnp.where(q_ref[0] == p, (p + 1) % N_CHIPS, p), i, 0))

    grid_spec = pltpu.PrefetchScalarGridSpec(
        num_scalar_prefetch=1, grid=(h // th,),
        in_specs=[pl.BlockSpec((None, th, N), lambda i, q_ref: (q_ref[0], i, 0))] + [got_spec(p) for p in range(N_CHIPS)],
        out_specs=pl.BlockSpec((th, N), lambda i, q_ref: (i, 0)))
    return pl.pallas_call(body, grid_spec=grid_spec, out_shape=jax.ShapeDtypeStruct((h, N), F32), name=name,
                          compiler_params=_cp("parallel"))(chip, pair, *([got] * N_CHIPS))


def adamw_halves(core, mine, other, w, m, v, name):
    n, K, N = w.shape
    h = K // 2
    th = _rows_block(h, N)
    nb = h // th

    def body(c_ref, *refs):
        mine_refs, other_refs = refs[:n], refs[n:2 * n]
        w_ref, m_ref, v_ref, g_ref, d_ref, m2_ref, v2_ref = refs[2 * n:]
        a, s, c = pl.program_id(0), pl.program_id(1), c_ref[0]
        for idx in range(n):
            @pl.when((a == idx) & (s == c))
            def _(idx=idx):
                g_ref[...] = mine_refs[idx][...]

            @pl.when((a == idx) & (s != c))
            def _(idx=idx):
                g_ref[...] = other_refs[idx][...]
        _adamw_math(g_ref[...], w_ref, m_ref, v_ref, d_ref, m2_ref, v2_ref)

    def half_spec(idx, is_mine):
        def index(a, s, i, c_ref):
            right_half = (s == c_ref[0]) if is_mine else (s != c_ref[0])
            return (jnp.where((a == idx) & right_half, i, 0), 0)
        return pl.BlockSpec((th, N), index)

    full = pl.BlockSpec((None, th, N), lambda a, s, i, c_ref: (a, s * nb + i, 0))
    grid_spec = pltpu.PrefetchScalarGridSpec(
        num_scalar_prefetch=1, grid=(n, 2, nb),
        in_specs=[half_spec(idx, True) for idx in range(n)] + [half_spec(idx, False) for idx in range(n)] + [full] * 3,
        out_specs=[full] * 4)
    return pl.pallas_call(body, grid_spec=grid_spec, out_shape=[jax.ShapeDtypeStruct((n, K, N), F32)] * 4, name=name,
                          compiler_params=_cp("arbitrary", "arbitrary", "arbitrary"))(core, *mine, *other, w, m, v)


def _adamw_math(gv, w_ref, m_ref, v_ref, d_ref, m2_ref, v2_ref):
    m2 = ADAM_B1 * m_ref[...] + (1.0 - ADAM_B1) * gv
    v2 = ADAM_B2 * v_ref[...] + (1.0 - ADAM_B2) * (gv * gv)
    m_hat = m2 / (1.0 - ADAM_B1 ** ADAM_STEP)
    v_hat = v2 / (1.0 - ADAM_B2 ** ADAM_STEP)
    d_ref[...] = -ADAM_LR * (m_hat / (jnp.sqrt(v_hat) + ADAM_EPS) + ADAM_WD * w_ref[...])
    m2_ref[...] = m2
    v2_ref[...] = v2


def _row_tile(R):
    return _pick(R, (512, 256, 128, 64, 32, 16, 8))


def sum_parts(parts, name):
    P, R, _ = parts.shape
    rt = _row_tile(R)

    def body(p_ref, o_ref):
        acc = p_ref[0]
        for p in range(1, P):
            acc = acc + p_ref[p]
        o_ref[...] = acc

    return pl.pallas_call(
        body, grid=(R // rt,), in_specs=[pl.BlockSpec((P, rt, LANES), lambda i: (0, i, 0))],
        out_specs=pl.BlockSpec((rt, LANES), lambda i: (i, 0)), out_shape=jax.ShapeDtypeStruct((R, LANES), F32),
        name=name, compiler_params=_cp("parallel"))(parts)


def adamw(g, w, m, v, name):
    R = g.shape[0]
    rt = _row_tile(R)

    def body(g_ref, w_ref, m_ref, v_ref, d_ref, m2_ref, v2_ref):
        _adamw_math(g_ref[...], w_ref, m_ref, v_ref, d_ref, m2_ref, v2_ref)

    spec = pl.BlockSpec((rt, LANES), lambda i: (i, 0))
    return pl.pallas_call(
        body, grid=(R // rt,), in_specs=[spec] * 4, out_specs=[spec] * 3,
        out_shape=[jax.ShapeDtypeStruct((R, LANES), F32)] * 3, name=name, compiler_params=_cp("parallel"))(g, w, m, v)


def _pack(arrays, rows_multiple, dtype):
    flat = jnp.concatenate([a.reshape(-1).astype(dtype) for a in arrays])
    n = flat.shape[0]
    per = rows_multiple * LANES
    total = -(-n // per) * per
    return jnp.pad(flat, (0, total - n)).reshape(total // LANES, LANES)


def _unpack(buf, shapes):
    flat = buf.reshape(-1)
    out, off = [], 0
    for s in shapes:
        n = math.prod(s)
        out.append(flat[off:off + n].reshape(s))
        off += n
    return out


SHARDED = (("mod_w", 2), ("ffn_w_in", 3), ("ffn_w_out", 2), ("w_in", 2), ("w_out", 1),
           ("a_conv_w", 2), ("s5_glu_w", 1), ("hg_lb_logits", 2))
WEIGHTS = ("c_ctx", "mod_w", "mod_b", "ffn_w_in", "ffn_w_out", "w_in", "w_out", "a_conv_w", "a_conv_b", "a_dt_bias",
           "a_log", "a_d", "a_norm_w", "s5_lam_re", "s5_lam_im", "s5_log_step", "s5_b_re", "s5_b_im", "s5_c_re",
           "s5_c_im", "s5_d", "s5_glu_w", "s5_glu_b", "hg_lb_logits", "hg_norm_w", "final_norm_w")


def _gather_full(local, names_axes, dtype, rows_multiple, name):
    buf = _pack([local[n] for n, _ in names_axes], 2 * rows_multiple, dtype)
    R = buf.shape[0] // 2
    full = all_gather_shards(buf.reshape(2, R, LANES), name).reshape(N_CHIPS, 2 * R, LANES)
    per_chip = [_unpack(full[q], [local[n].shape for n, _ in names_axes]) for q in range(N_CHIPS)]
    return {n: jnp.concatenate([per_chip[q][j] for q in range(N_CHIPS)], axis=ax)
            for j, (n, ax) in enumerate(names_axes)}


def kernel(x, c, ctx, c_ctx, mod_w, mod_b, ffn_w_in, ffn_w_out, w_in, w_out, a_conv_w, a_conv_b, a_dt_bias, a_log, a_d, a_norm_w, s5_lam_re, s5_lam_im, s5_log_step, s5_b_re, s5_b_im, s5_c_re, s5_c_im, s5_d, s5_glu_w, s5_glu_b, hg_lb_logits, hg_norm_w, final_norm_w, loss_target, m_c_ctx, m_mod_w, m_mod_b, m_ffn_w_in, m_ffn_w_out, m_w_in, m_w_out, m_a_conv_w, m_a_conv_b, m_a_dt_bias, m_a_log, m_a_d, m_a_norm_w, m_s5_lam_re, m_s5_lam_im, m_s5_log_step, m_s5_b_re, m_s5_b_im, m_s5_c_re, m_s5_c_im, m_s5_d, m_s5_glu_w, m_s5_glu_b, m_hg_lb_logits, m_hg_norm_w, m_final_norm_w, v_c_ctx, v_mod_w, v_mod_b, v_ffn_w_in, v_ffn_w_out, v_w_in, v_w_out, v_a_conv_w, v_a_conv_b, v_a_dt_bias, v_a_log, v_a_d, v_a_norm_w, v_s5_lam_re, v_s5_lam_im, v_s5_log_step, v_s5_b_re, v_s5_b_im, v_s5_c_re, v_s5_c_im, v_s5_d, v_s5_glu_w, v_s5_glu_b, v_hg_lb_logits, v_hg_norm_w, v_final_norm_w):
    given = dict(locals())
    w = {n: given[n] for n in WEIGHTS}
    m = {n: given["m_" + n] for n in WEIGHTS}
    v = {n: given["v_" + n] for n in WEIGHTS}
    small_sharded = SHARDED[len(BIG):]
    replicated = [n for n in WEIGHTS if n not in [s for s, _ in SHARDED]]
    core = lax.axis_index("c").astype(jnp.int32).reshape(1)
    chip = (2 * lax.axis_index("x") + lax.axis_index("y")).astype(jnp.int32)

    def leaves(t):
        return list(t.reshape((-1,) + t.shape[-2:]))

    def nest(kind, flat):
        flat = list(flat)
        return [flat[2 * l:2 * l + 2] for l in range(len(flat) // 2)] if w[kind].ndim == 4 else flat

    counts = [len(leaves(w[k])) for k in BIG]
    gathered = gather_pieces([a.astype(BF16) for k in BIG for a in leaves(w[k])], "gather_big")
    big, off = {}, 0
    for k, cnt in zip(BIG, counts):
        big[k], off = nest(k, gathered[off:off + cnt]), off + cnt
    small = {n: w[n] for n in replicated}
    small.update(_gather_full(w, small_sharded, F32, 8, "gather_small"))

    loss, grad_x, grads = local_step(x, c, ctx, loss_target, big, small)

    g_big = [g for k in BIG for g in jax.tree.leaves(grads[k])]
    got = swap_with_sibling(g_big, "rs_sibling", send_other_half_of_axis=1)
    pairs = [add_pair(core, g, t, f"rs_add_pair{i}") for i, (g, t) in enumerate(zip(g_big, got))]
    from_chips = pieces_to_chips(pairs, "rs_chips")
    mine = [sum_pieces(chip.reshape(1), p, t, f"rs_sum_chips{i}") for i, (p, t) in enumerate(zip(pairs, from_chips))]
    other = swap_with_sibling(mine, "rs_halves")
    out, off = {}, 0
    for k, cnt in zip(BIG, counts):
        stack = lambda t: t.reshape((cnt,) + t.shape[-2:])
        res = adamw_halves(core, mine[off:off + cnt], other[off:off + cnt], stack(w[k]), stack(m[k]), stack(v[k]),
                           "adamw_" + k)
        off += cnt
        for kind, a in zip(("grad", "delta", "new_m", "new_v"), res):
            out[kind, k] = a.reshape(w[k].shape)

    names = replicated + [n for n, _ in small_sharded]
    part = _pack([grads[n] for n in names] + [loss.reshape(1)], 8, F32)
    g_all = _unpack(sum_parts(all_gather_devices(part, "gather_parts"), "sum_parts"),
                    [grads[n].shape for n in names] + [(1,)])
    g_of = dict(zip(names + ["loss"], g_all))
    for n, ax in small_sharded:
        g_of[n] = lax.dynamic_slice_in_dim(g_of[n], chip * w[n].shape[ax], w[n].shape[ax], axis=ax)
    rest = lambda d: _pack([d[n] for n in names], 8, F32)
    g_rest = rest(g_of)
    d_rest, m_rest, v_rest = adamw(g_rest, rest(w), rest(m), rest(v), "adamw_rest")
    shapes = [w[n].shape for n in names]
    for kind, buf in (("grad", g_rest), ("delta", d_rest), ("new_m", m_rest), ("new_v", v_rest)):
        for n, a in zip(names, _unpack(buf, shapes)):
            out[kind, n] = a

    loss_total = g_of["loss"].reshape(())
    return (loss_total, grad_x, *[out["grad", n] for n in WEIGHTS], *[out["delta", n] for n in WEIGHTS],
            *[out["new_m", n] for n in WEIGHTS], *[out["new_v", n] for n in WEIGHTS])
```

```python
import functools
import math

import numpy as np
import jax
import jax.numpy as jnp
from jax import lax
from jax.experimental import pallas as pl
from jax.experimental.pallas import tpu as pltpu

F32, BF16 = jnp.float32, jnp.bfloat16
EPS = 1e-6
N_MOD = 9
D_FF = 2816
A_INNER, A_HEADS, A_GROUPS, A_STATE, A_CONV, A_CONV_DIM = 512, 8, 2, 64, 5, 768
A_COLS = A_INNER + A_CONV_DIM + 2 * A_HEADS
B_WIDTH, B_GROUP, B_NGROUPS, B_STATE = 256, 16, 16, 64
C_WIDTH, C_HEADS, C_KEY = 256, 4, 64
GRID_W = 64
IN_COLS = A_COLS + B_WIDTH + 5 * C_WIDTH
IN_COLS_PAD = 3072
ADAM_LR, ADAM_B1, ADAM_B2, ADAM_EPS, ADAM_WD, ADAM_STEP = 0.001, 0.9, 0.999, 1e-08, 0.01, 10

ROW_TILE = 256
SSD_Q, HG_Q = 128, 64
REC_T = 128
SCAN_SAMPLES = 4
ROW_SAMPLES = 2
ROW_SAMPLES_MAX_WIDTH = 1024
HG_BAND = 8
VMEM_LIMIT = 56 * 1024 * 1024
MM_VMEM_BUDGET = 36 * 1024 * 1024

NT = ((1,), (1,))
TN = ((0,), (0,))


def _bdot(a, b, dims=((1,), (0,))):
    return lax.dot_general(a.astype(BF16), b.astype(BF16), (dims, ((), ())), preferred_element_type=F32)


def _split3(x):
    hi = x.astype(BF16)
    r = x - hi.astype(F32)
    mid = r.astype(BF16)
    lo = (r - mid.astype(F32)).astype(BF16)
    return hi, mid, lo


def _mask_lhs(m, x, dims):
    return sum(lax.dot_general(m, p, (dims, ((), ())), preferred_element_type=F32) for p in _split3(x))


def _mask_rhs(x, m, dims):
    return sum(lax.dot_general(p, m, (dims, ((), ())), preferred_element_type=F32) for p in _split3(x))


@jax.custom_vjp
def mdot(m, x):
    return _mask_lhs(m, x, ((1,), (0,)))


def _mdot_fwd(m, x):
    return mdot(m, x), m


def _mdot_bwd(m, g):
    return jnp.zeros_like(m), _mask_lhs(m, g, TN)


mdot.defvjp(_mdot_fwd, _mdot_bwd)


@jax.custom_vjp
def xdot(x, m):
    return _mask_rhs(x, m, ((1,), (0,)))


def _xdot_fwd(x, m):
    return xdot(x, m), m


def _xdot_bwd(m, g):
    return _mask_rhs(g, m, NT), jnp.zeros_like(m)


xdot.defvjp(_xdot_fwd, _xdot_bwd)


@functools.partial(jax.custom_vjp, nondiff_argnums=(1,))
def _roll(x, s):
    return pltpu.roll(x, s, 0)


def _roll_fwd(x, s):
    return _roll(x, s), None


def _roll_bwd(s, _, g):
    return (_roll(g, x_rows(g) - s),)


def x_rows(x):
    return x.shape[0]


_roll.defvjp(_roll_fwd, _roll_bwd)


def sroll(x, s):
    s = s % x.shape[0]
    return x if s == 0 else _roll(x, s)


def _softplus(x):
    return jnp.maximum(x, 0.0) + jnp.log1p(jnp.exp(jnp.minimum(x, -x)))


def _sigmoid(x):
    return 0.5 * (jnp.tanh(0.5 * x) + 1.0)


def _silu(x):
    return x * _sigmoid(x)


def _rms(x):
    return x * lax.rsqrt(jnp.mean(x * x, axis=-1, keepdims=True) + EPS)


def _pick(n, cands):
    for c in cands:
        if n % c == 0:
            return c
    return n


def _cp(*sem):
    return pltpu.CompilerParams(dimension_semantics=sem, vmem_limit_bytes=VMEM_LIMIT)


def _tile(n, unit, cap):
    best = None
    for d in range(unit, min(n, cap) + 1, unit):
        if n % d == 0:
            best = d
    return best if best is not None else n


def _mm(a, b, ta, tb, out_dtype, name, b_pieces=False, out_pieces=False):
    M, K = (a.shape[1], a.shape[0]) if ta else a.shape
    n_div = k_div = None
    if b_pieces and tb:
        N, k_div = b.shape[1], b.shape[2]
    elif b_pieces:
        N, n_div = N_CHIPS * b.shape[2], b.shape[2]
    else:
        N = b.shape[0] if tb else b.shape[1]
    if out_pieces:
        n_div = N // N_CHIPS
    n_div, k_div = n_div or N, k_div or K
    tm = _tile(M, 128 if ta else 8, 1536 if ta else 1024)
    tn = _tile(n_div, 128, 1536)
    a_bytes, b_bytes, o_bytes = a.dtype.itemsize, b.dtype.itemsize, jnp.dtype(out_dtype).itemsize
    cands = {d for d in range(128, min(k_div, 2816) + 1, 128) if k_div % d == 0}
    if k_div <= 2816 or not cands:
        cands.add(k_div)
    for tk in sorted(cands, reverse=True):
        scratch = tm * tn * 4 if (K // tk > 1 and o_bytes != 4) else 0
        if 2 * (tm * tk * a_bytes + tk * tn * b_bytes + tm * tn * o_bytes) + scratch <= MM_VMEM_BUDGET:
            break
    nk = K // tk
    nq_n, nq_k = n_div // tn, k_div // tk
    dims = ((0 if ta else 1,), (1 if tb else 0,))

    def body(a_ref, b_ref, o_ref, *acc):
        part = _bdot(a_ref[...], b_ref[...], dims)
        if nk == 1:
            o_ref[...] = part.astype(o_ref.dtype)
            return
        acc_ref = acc[0] if acc else o_ref
        k = pl.program_id(2)

        @pl.when(k == 0)
        def _():
            acc_ref[...] = part

        @pl.when(k > 0)
        def _():
            acc_ref[...] += part

        if acc:
            @pl.when(k == nk - 1)
            def _():
                o_ref[...] = acc_ref[...].astype(o_ref.dtype)

    a_spec = pl.BlockSpec((tk, tm), lambda i, j, k: (k, i)) if ta else pl.BlockSpec((tm, tk), lambda i, j, k: (i, k))
    if b_pieces and tb:
        b_spec = pl.BlockSpec((None, tn, tk), lambda i, j, k: (k // nq_k, j, k % nq_k))
    elif b_pieces:
        b_spec = pl.BlockSpec((None, tk, tn), lambda i, j, k: (j // nq_n, k, j % nq_n))
    elif tb:
        b_spec = pl.BlockSpec((tn, tk), lambda i, j, k: (j, k))
    else:
        b_spec = pl.BlockSpec((tk, tn), lambda i, j, k: (k, j))
    if out_pieces:
        o_spec = pl.BlockSpec((None, tm, tn), lambda i, j, k: (j // nq_n, i, j % nq_n))
        o_shape = jax.ShapeDtypeStruct((N_CHIPS, M, N // N_CHIPS), out_dtype)
    else:
        o_spec = pl.BlockSpec((tm, tn), lambda i, j, k: (i, j))
        o_shape = jax.ShapeDtypeStruct((M, N), out_dtype)
    return pl.pallas_call(
        body, grid=(M // tm, N // tn, nk), in_specs=[a_spec, b_spec], out_specs=o_spec, out_shape=o_shape,
        scratch_shapes=[pltpu.VMEM((tm, tn), F32)] if (nk > 1 and o_bytes != 4) else [], name=name,
        compiler_params=_cp("parallel", "parallel", "arbitrary"))(a, b)


def linear(x, w16, wslot, name, out_dtype=F32, pieces=False):
    @jax.custom_vjp
    def f(x, w16, wslot):
        return _mm(x, w16, False, False, out_dtype, name + "_fwd", b_pieces=pieces)

    def fwd(x, w16, wslot):
        return f(x, w16, wslot), (x, w16)

    grad_dtype = wslot.dtype

    def bwd(res, dy):
        x, w16 = res
        dx = _mm(dy, w16, False, True, x.dtype, name + "_dx", b_pieces=pieces)
        dw = _mm(x, dy, True, False, grad_dtype, name + "_dw", out_pieces=pieces)
        return dx, jnp.zeros_like(w16), dw

    f.defvjp(fwd, bwd)
    return f(x, w16, wslot)


def rowwise(f, name, rows, mods, params, outs, n_ctx_blocks, tl=ROW_TILE, row_diff=None):
    rows, mods, params = tuple(rows), tuple(mods), tuple(params)
    nr, nm, npar, no = len(rows), len(mods), len(params), len(outs)
    B, L = rows[0].shape[:2]
    nblk = L // tl
    row_diff = tuple(row_diff) if row_diff is not None else (True,) * nr
    out_dtypes = [dt for _, dt in outs]
    widest = max([r.shape[2] for r in rows] + [w for w, _ in outs])
    ns = ROW_SAMPLES if (B % ROW_SAMPLES == 0 and widest <= ROW_SAMPLES_MAX_WIDTH) else 1

    def fcast(r, m, p):
        return tuple(o.astype(dt) for o, dt in zip(f(r, m, p), out_dtypes))

    def seg(i):
        return (i >= n_ctx_blocks).astype(jnp.int32) if n_ctx_blocks else 0

    def specs():
        row_specs = [pl.BlockSpec((ns, tl, r.shape[2]), lambda b, i: (b, i, 0)) for r in rows]
        mod_specs = [pl.BlockSpec((ns, 1, 1, m.shape[3]), lambda b, i: (b, seg(i), 0, 0)) for m in mods]
        par_specs = [pl.BlockSpec(p.shape, lambda b, i: (0, 0)) for p in params]
        out_specs = [pl.BlockSpec((ns, tl, w), lambda b, i: (b, i, 0)) for w, _ in outs]
        return row_specs, mod_specs, par_specs, out_specs

    def load(ins, s):
        r = tuple(x[s] for x in ins[:nr])
        m = tuple(x[s, 0] for x in ins[nr:nr + nm])
        p = tuple(x[...] for x in ins[nr + nm:])
        return r, m, p

    def fwd_call(rows, mods, params):
        def body(*refs):
            for s in range(ns):
                r, m, p = load(refs[:nr + nm + npar], s)
                for o_ref, o in zip(refs[nr + nm + npar:], fcast(r, m, p)):
                    o_ref[s] = o

        rs, ms, ps, os_ = specs()
        return pl.pallas_call(
            body, grid=(B // ns, nblk), in_specs=rs + ms + ps, out_specs=os_,
            out_shape=[jax.ShapeDtypeStruct((B, L, w), dt) for w, dt in outs],
            name=name + "_fwd", compiler_params=_cp("parallel", "parallel"))(*rows, *mods, *params)

    def bwd_call(rows, mods, params, douts):
        didx = [j for j in range(nr) if row_diff[j]]

        def body(*refs):
            n_in = nr + nm + npar
            dins = refs[n_in:n_in + no]
            rest = refs[n_in + no:]
            dr_refs, dm_refs, dp_refs = rest[:len(didx)], rest[len(didx):len(didx) + nm], rest[len(didx) + nm:]
            b, i = pl.program_id(0), pl.program_id(1)
            first_m = (i == 0) | (i == n_ctx_blocks) if n_ctx_blocks else (i == 0)
            first_p = (b == 0) & (i == 0)
            dp = None
            for s in range(ns):
                r, m, p = load(refs[:n_in], s)
                _, vjp = jax.vjp(fcast, r, m, p)
                dr, dm, dp_s = vjp(tuple(d[s] for d in dins))
                dp = dp_s if dp is None else tuple(a + g for a, g in zip(dp, dp_s))
                for ref, j in zip(dr_refs, didx):
                    ref[s] = dr[j].astype(ref.dtype)
                for ref, g in zip(dm_refs, dm):
                    @pl.when(first_m)
                    def _(ref=ref, g=g, s=s):
                        ref[s, 0] = g

                    @pl.when(jnp.logical_not(first_m))
                    def _(ref=ref, g=g, s=s):
                        ref[s, 0] += g
            for ref, g in zip(dp_refs, dp):
                @pl.when(first_p)
                def _(ref=ref, g=g):
                    ref[...] = g

                @pl.when(jnp.logical_not(first_p))
                def _(ref=ref, g=g):
                    ref[...] += g

        rs, ms, ps, os_ = specs()
        out_shape = ([jax.ShapeDtypeStruct(rows[j].shape, rows[j].dtype) for j in didx]
                     + [jax.ShapeDtypeStruct(m.shape, F32) for m in mods]
                     + [jax.ShapeDtypeStruct(p.shape, F32) for p in params])
        res = pl.pallas_call(
            body, grid=(B // ns, nblk), in_specs=rs + ms + ps + os_,
            out_specs=[rs[j] for j in didx] + ms + ps, out_shape=out_shape,
            name=name + "_bwd", compiler_params=_cp("arbitrary", "arbitrary"))(*rows, *mods, *params, *douts)
        dr = [None] * nr
        for j, g in zip(didx, res[:len(didx)]):
            dr[j] = g
        dr = tuple(g if g is not None else jnp.zeros_like(rows[j]) for j, g in enumerate(dr))
        return dr, tuple(res[len(didx):len(didx) + nm]), tuple(res[len(didx) + nm:])

    @jax.custom_vjp
    def op(rows, mods, params):
        return tuple(fwd_call(rows, mods, params))

    def op_fwd(rows, mods, params):
        return op(rows, mods, params), (rows, mods, params)

    def op_bwd(res, douts):
        return bwd_call(*res, tuple(douts))

    op.defvjp(op_fwd, op_bwd)
    return op(rows, mods, params)


def chunk_scan(step, name, rev, seqs, seq_t, params, consts, state_shapes, out_widths, Q, n_ctx):
    seqs, params, consts = tuple(seqs), tuple(params), tuple(consts)
    ns, npar, nc, nst, no = len(seqs), len(params), len(consts), len(state_shapes), len(out_widths)
    B = seqs[0].shape[0]
    L = seqs[0].shape[2] if seq_t[0] else seqs[0].shape[1]
    nck, ncc = L // Q, n_ctx // Q
    nb = SCAN_SAMPLES if B % SCAN_SAMPLES == 0 else 1

    def chunk_of(k):
        if not rev:
            return k
        return jnp.where(k < ncc, ncc - 1 - k, nck + ncc - 1 - k)

    def specs(order):
        def seq_spec(s, t):
            if t:
                return pl.BlockSpec((nb, s.shape[1], Q), lambda b, k: (b, 0, chunk_of(order(k))))
            return pl.BlockSpec((nb, Q, s.shape[2]), lambda b, k: (b, chunk_of(order(k)), 0))

        seq_specs = [seq_spec(s, t) for s, t in zip(seqs, seq_t)]
        par_specs = [pl.BlockSpec(p.shape, lambda b, k: (0, 0)) for p in params]
        con_specs = [pl.BlockSpec(c.shape, lambda b, k, nd=c.ndim: (0,) * nd) for c in consts]
        out_specs = [pl.BlockSpec((nb, Q, w), lambda b, k: (b, chunk_of(order(k)), 0)) for w in out_widths]
        sav_specs = [pl.BlockSpec((nb, 1) + tuple(s), lambda b, k: (b, order(k), 0, 0)) for s in state_shapes]
        return seq_specs, par_specs, con_specs, out_specs, sav_specs

    def fwd_call(seqs, params):
        def body(*refs):
            seq_refs = refs[:ns]
            par_refs = refs[ns:ns + npar]
            con_refs = refs[ns + npar:ns + npar + nc]
            rest = refs[ns + npar + nc:]
            out_refs, sav_refs, st_refs = rest[:no], rest[no:no + nst], rest[no + nst:]

            @pl.when(pl.program_id(1) == 0)
            def _():
                for s in st_refs:
                    s[...] = jnp.zeros_like(s)

            pvals, cvals = tuple(p[...] for p in par_refs), tuple(c[...] for c in con_refs)
            for i in range(nb):
                s_in = tuple(s[i] for s in st_refs)
                for sv, s in zip(sav_refs, s_in):
                    sv[i, 0] = s
                s_new, outs = step(s_in, tuple(r[i].astype(F32) for r in seq_refs), pvals, cvals)
                for s_ref, s in zip(st_refs, s_new):
                    s_ref[i] = s
                for o_ref, o in zip(out_refs, outs):
                    o_ref[i] = o

        ss, ps, cs, os_, vs = specs(lambda k: k)
        res = pl.pallas_call(
            body, grid=(B // nb, nck), in_specs=ss + ps + cs, out_specs=os_ + vs,
            out_shape=[jax.ShapeDtypeStruct((B, L, w), F32) for w in out_widths]
            + [jax.ShapeDtypeStruct((B, nck) + tuple(s), F32) for s in state_shapes],
            scratch_shapes=[pltpu.VMEM((nb,) + tuple(s), F32) for s in state_shapes],
            name=name + "_fwd", compiler_params=_cp("parallel", "arbitrary"))(*seqs, *params, *consts)
        return tuple(res[:no]), tuple(res[no:])

    def bwd_call(seqs, params, saved, douts):
        def body(*refs):
            seq_refs = refs[:ns]
            par_refs = refs[ns:ns + npar]
            con_refs = refs[ns + npar:ns + npar + nc]
            rest = refs[ns + npar + nc:]
            sav_refs, dout_refs = rest[:nst], rest[nst:nst + no]
            rest = rest[nst + no:]
            dseq_refs, dpar_refs, dst_refs = rest[:ns], rest[ns:ns + npar], rest[ns + npar:]
            b, k = pl.program_id(0), pl.program_id(1)

            @pl.when(k == 0)
            def _():
                for s in dst_refs:
                    s[...] = jnp.zeros_like(s)

            cvals, pvals = tuple(c[...] for c in con_refs), tuple(p[...] for p in par_refs)
            dp = None
            for i in range(nb):
                _, vjp = jax.vjp(lambda s, x, p: step(s, x, p, cvals), tuple(s[i, 0] for s in sav_refs),
                                 tuple(r[i].astype(F32) for r in seq_refs), pvals)
                ds, dx, dp_i = vjp((tuple(s[i] for s in dst_refs), tuple(d[i] for d in dout_refs)))
                for s_ref, s in zip(dst_refs, ds):
                    s_ref[i] = s
                for x_ref, x in zip(dseq_refs, dx):
                    x_ref[i] = x.astype(x_ref.dtype)
                dp = dp_i if dp is None else tuple(a + g for a, g in zip(dp, dp_i))
            first = (b == 0) & (k == 0)
            for ref, g in zip(dpar_refs, dp):
                @pl.when(first)
                def _(ref=ref, g=g):
                    ref[...] = g

                @pl.when(jnp.logical_not(first))
                def _(ref=ref, g=g):
                    ref[...] += g

        ss, ps, cs, os_, vs = specs(lambda k: nck - 1 - k)
        res = pl.pallas_call(
            body, grid=(B // nb, nck), in_specs=ss + ps + cs + vs + os_, out_specs=ss + ps,
            out_shape=[jax.ShapeDtypeStruct(s.shape, s.dtype) for s in seqs]
            + [jax.ShapeDtypeStruct(p.shape, F32) for p in params],
            scratch_shapes=[pltpu.VMEM((nb,) + tuple(s), F32) for s in state_shapes],
            name=name + "_bwd", compiler_params=_cp("arbitrary", "arbitrary"))(*seqs, *params, *consts, *saved, *douts)
        return tuple(res[:ns]), tuple(res[ns:])

    @jax.custom_vjp
    def op(seqs, params):
        return fwd_call(seqs, params)[0]

    def op_fwd(seqs, params):
        outs, saved = fwd_call(seqs, params)
        return outs, (seqs, params, saved)

    def op_bwd(res, douts):
        seqs, params, saved = res
        return bwd_call(seqs, params, saved, tuple(douts))

    op.defvjp(op_fwd, op_bwd)
    return op(seqs, params)


def _positions(Q, rev):
    t = np.arange(Q)
    return (Q - 1 - t) if rev else t


def _ssd_consts(rev):
    Q = SSD_Q
    pos = _positions(Q, rev)
    tri = pos[:, None] >= pos[None, :]
    head_of = np.arange(A_INNER) // (A_INNER // A_HEADS)
    expand = np.arange(A_HEADS)[:, None] == head_of[None, :]
    group_of_row = np.arange(A_GROUPS * A_STATE) // A_STATE
    block = group_of_row[:, None] == (head_of // (A_HEADS // A_GROUPS))[None, :]
    as_bf = lambda m: jnp.asarray(m.astype(np.float32), BF16)
    return (as_bf(tri), as_bf(tri.T), jnp.asarray(tri.astype(np.float32)), as_bf(expand),
            jnp.asarray(block.astype(np.float32)), jnp.asarray(expand.astype(np.float32)[:, None, :]))


def _ssd_step(state, seqs, params, consts):
    (st,) = state
    xs, bm, cm, dtr, dtr_t = seqs
    bias, bias_t, alog, alog_t = params
    tri_b, tri_t_b, tri_f, expand, block, head_lanes = consts
    dt = _softplus(dtr + bias)
    dta = dt * (-jnp.exp(alog))
    dt_t = _softplus(dtr_t + bias_t)
    dta_t = dt_t * (-jnp.exp(alog_t))
    cum = mdot(tri_b, dta)
    cum_t = xdot(dta_t, tri_t_b)
    total = jnp.sum(dta, axis=0, keepdims=True)
    w_end = jnp.exp(total - cum) * dt
    ecum_x = xdot(jnp.exp(cum), expand)
    wend_x = xdot(w_end, expand)
    dec_x = jnp.exp(jnp.sum(xdot(dta, expand), axis=0, keepdims=True))
    y = _bdot(cm, st) * ecum_x
    st_new = st * dec_x + block * _bdot(bm, xs * wend_x, TN)
    lane_g = lax.broadcasted_iota(jnp.int32, (1, A_GROUPS * A_STATE), 1) // A_STATE
    sub_h = lax.broadcasted_iota(jnp.int32, (A_HEADS, 1), 0)
    lane8 = lax.broadcasted_iota(jnp.int32, (1, A_HEADS), 1)
    Q = xs.shape[0]
    scores = []
    for g in range(A_GROUPS):
        cb = _bdot(jnp.where(lane_g == g, cm, 0.0), bm, NT)
        for h in range(g * (A_HEADS // A_GROUPS), (g + 1) * (A_HEADS // A_GROUPS)):
            col = jnp.sum(jnp.where(lane8 == h, cum, 0.0), axis=1, keepdims=True)
            row = jnp.sum(jnp.where(sub_h == h, cum_t, 0.0), axis=0, keepdims=True)
            dt_row = jnp.sum(jnp.where(sub_h == h, dt_t, 0.0), axis=0, keepdims=True)
            scores.append(cb * (tri_f * jnp.exp(tri_f * (col - row))) * dt_row)
    y_all = _bdot(jnp.stack(scores).reshape(A_HEADS * Q, Q), xs).reshape(A_HEADS, Q, A_INNER)
    y = y + jnp.sum(y_all * head_lanes, axis=0)
    return (st_new,), (y,)


def ssd_scan(name, rev, xs, bm, cm, dtr, dt_bias, a_log, n_ctx):
    seqs = (xs, bm, cm, dtr, jnp.swapaxes(dtr, 1, 2))
    params = (dt_bias.reshape(1, -1), dt_bias.reshape(-1, 1), a_log.reshape(1, -1), a_log.reshape(-1, 1))
    (y,) = chunk_scan(_ssd_step, name, rev, seqs, (False, False, False, False, True), params, _ssd_consts(rev),
                      [(A_GROUPS * A_STATE, A_INNER)], [A_INNER], SSD_Q, n_ctx)
    return y


def _recurrence(v, a, order, n_ctx, name, x_fwd=None):
    B, L = v.shape[:2]
    T = REC_T
    nck, ncc = L // T, n_ctx // T
    nlat = nck - ncc
    chunk_of = {"F": lambda k: k,
                "R": lambda k: jnp.where(k < ncc, ncc - 1 - k, nck + ncc - 1 - k),
                "FT": lambda k: nck - 1 - k,
                "RT": lambda k: jnp.where(k < nlat, ncc + k, k - nlat)}[order]
    descending = order in ("R", "FT")
    with_da = x_fwd is not None
    H = 8

    def body(*refs):
        if with_da:
            v_ref, a_ref, xf_ref, x_ref, da_ref, st_ref = refs
        else:
            v_ref, a_ref, x_ref, st_ref = refs
        k = pl.program_id(0)

        @pl.when(k == 0)
        def _():
            st_ref[...] = jnp.zeros_like(st_ref)
            if with_da:
                da_ref[...] = jnp.zeros_like(da_ref)

        ar, ai = a_ref[pl.ds(0, H), :], a_ref[pl.ds(H, H), :]
        zero = jnp.zeros((H, 128), F32)

        def step(i, carry):
            chains, acc_r, acc_i = carry
            t = (T - 1 - i) if descending else i
            out = []
            for b, (xr, xi) in enumerate(chains):
                if with_da:
                    f = xf_ref[b, t].astype(F32)
                    fr, fi = f[:H], f[H:]
                    acc_r = acc_r + xr * fr + xi * fi
                    acc_i = acc_i + xi * fr - xr * fi
                vt = v_ref[b, t].astype(F32)
                nr = ar * xr - ai * xi + vt[:H]
                ni = ar * xi + ai * xr + vt[H:]
                x_ref[b, t] = jnp.concatenate([nr, ni], axis=0).astype(x_ref.dtype)
                out.append((nr, ni))
            return tuple(out), acc_r, acc_i

        init = tuple((st_ref[b, pl.ds(0, H), :], st_ref[b, pl.ds(H, H), :]) for b in range(B))
        chains, acc_r, acc_i = lax.fori_loop(0, T, step, (init, zero, zero), unroll=8)
        for b, (xr, xi) in enumerate(chains):
            st_ref[b, pl.ds(0, H), :] = xr
            st_ref[b, pl.ds(H, H), :] = xi
        if with_da:
            da_ref[pl.ds(0, H), :] += acc_r
            da_ref[pl.ds(H, H), :] += acc_i

    seq = pl.BlockSpec((B, T, 2 * H, 128), lambda k: (0, chunk_of(k), 0, 0))
    par = pl.BlockSpec((2 * H, 128), lambda k: (0, 0))
    x_shape = jax.ShapeDtypeStruct(v.shape, v.dtype)
    if with_da:
        return pl.pallas_call(
            body, grid=(nck,), in_specs=[seq, par, seq], out_specs=[seq, par],
            out_shape=[x_shape, jax.ShapeDtypeStruct((2 * H, 128), F32)],
            scratch_shapes=[pltpu.VMEM((B, 2 * H, 128), F32)], name=name, compiler_params=_cp("arbitrary"))(v, a, x_fwd)
    return pl.pallas_call(
        body, grid=(nck,), in_specs=[seq, par], out_specs=seq, out_shape=x_shape,
        scratch_shapes=[pltpu.VMEM((B, 2 * H, 128), F32)], name=name, compiler_params=_cp("arbitrary"))(v, a)


def lin_rec(v, a, rev, n_ctx, name):
    fwd_order, bwd_order = ("R", "RT") if rev else ("F", "FT")
    conj = jnp.concatenate([jnp.ones((8, 128), F32), -jnp.ones((8, 128), F32)], axis=0)

    @jax.custom_vjp
    def f(v, a):
        return _recurrence(v, a, fwd_order, n_ctx, name + "_fwd")

    def fwd(v, a):
        x = f(v, a)
        return x, (x, a)

    def bwd(res, dx):
        x, a = res
        g, da = _recurrence(dx, a * conj, bwd_order, n_ctx, name + "_bwd", x_fwd=x)
        return g, da

    f.defvjp(fwd, bwd)
    return f(v, a)


def _block_diag(t):
    G, a, b = t.shape
    eye = jnp.eye(G, dtype=t.dtype)
    return (t[:, :, None, :] * eye[:, None, :, None]).reshape(G * a, G * b)


def s5_scan(name, rev, u, lam_re, lam_im, log_step, b_re, b_im, c_re, c_im, n_ctx):
    step = jnp.exp(log_step)[:, None]
    mag = jnp.exp(lam_re * step)
    ar = mag * jnp.cos(lam_im * step)
    ai = mag * jnp.sin(lam_im * step)
    den = lam_re * lam_re + lam_im * lam_im
    nr = ar - 1.0
    kr = (nr * lam_re + ai * lam_im) / den
    ki = (ai * lam_re - nr * lam_im) / den
    br = kr[..., None] * b_re - ki[..., None] * b_im
    bi = kr[..., None] * b_im + ki[..., None] * b_re
    B, L, _ = u.shape
    width = B_NGROUPS * B_STATE
    w_b = jnp.concatenate([_block_diag(jnp.swapaxes(br, 1, 2)), _block_diag(jnp.swapaxes(bi, 1, 2))], axis=1)
    w_c = jnp.concatenate([_block_diag(jnp.swapaxes(c_re, 1, 2)), -_block_diag(jnp.swapaxes(c_im, 1, 2))], axis=0)
    a = jnp.concatenate([ar.reshape(8, 128), ai.reshape(8, 128)], axis=0)
    v = linear(u.reshape(B * L, B_WIDTH), w_b.astype(BF16), w_b, name + "_in", BF16)
    x = lin_rec(v.reshape(B, L, 16, 128), a, rev, n_ctx, name)
    y = linear(x.reshape(B * L, 2 * width), w_c.astype(BF16), w_c, name + "_out")
    return y.reshape(B, L, B_WIDTH)


def _hg_consts(rev):
    Q = HG_Q
    pos = _positions(Q, rev)
    pi, pj = pos[:, None], pos[None, :]
    mats = [pi >= pj, pj > pi]
    pairs = []
    s = HG_BAND
    while s < Q:
        second = (pos // s) % 2 == 1
        mid = (pos // (2 * s)) * 2 * s + s
        mats.append(second[:, None] & (pj >= mid[:, None]) & (pj <= pi))
        mats.append((~second)[:, None] & (pj > pi) & (pj < mid[:, None]))
        pairs.append(second[:, None] & (~second)[None, :] & ((pi // (2 * s)) == (pj // (2 * s))))
        s *= 2
    rows = [((pos // s_) % 2 == 1) for s_ in [HG_BAND * 2 ** n for n in range(len(pairs))]]
    band = [(pos % HG_BAND) >= d for d in range(HG_BAND)]
    mats_b = jnp.asarray(np.stack(mats).astype(np.float32), BF16)
    pairs_f = jnp.asarray(np.tile(np.stack(pairs).astype(np.float32), (1, C_HEADS, 1)))
    rows_f = jnp.asarray(np.stack(rows).astype(np.float32)[:, :, None])
    band_f = jnp.asarray(np.stack(band).astype(np.float32)[:, :, None])
    head_lanes = np.arange(C_HEADS)[:, None] == (np.arange(C_WIDTH) // C_KEY)[None, :]
    return mats_b, pairs_f, rows_f, band_f, jnp.asarray(head_lanes.astype(np.float32)[:, None, :])


def _hg_step_fn(rev, n_levels):
    def step(state, seqs, params, consts):
        (st,) = state
        qr, zf, v = seqs
        (lb,) = params
        mats, pairs, rows, band, head_lanes = consts
        W = C_WIDTH
        Q = qr.shape[0]
        same_head = (lax.broadcasted_iota(jnp.int32, (W, W), 0) // C_KEY
                     == lax.broadcasted_iota(jnp.int32, (W, W), 1) // C_KEY)
        q = _silu(qr)
        f = lb + (1.0 - lb) * jax.nn.sigmoid(zf)
        logf = jnp.log(f)
        kk = 1.0 - f
        qd = q * jnp.exp(mdot(mats[0], logf))
        w = kk * jnp.exp(mdot(mats[1], logf))
        total = jnp.sum(logf, axis=0, keepdims=True)
        o = _bdot(qd, st, NT)
        st_new = st * jnp.exp(total) + jnp.where(same_head, _bdot(v, w, TN), 0.0)
        scores = None
        for n in range(n_levels):
            a = q * jnp.exp(mdot(mats[2 + 2 * n], logf)) * rows[n]
            bk = kk * jnp.exp(mdot(mats[3 + 2 * n], logf)) * (1.0 - rows[n])
            sc = pairs[n] * _bdot((a[None] * head_lanes).reshape(C_HEADS * Q, W), bk, NT)
            scores = sc if scores is None else scores + sc
        o = o + jnp.sum(_bdot(scores, v).reshape(C_HEADS, Q, W) * head_lanes, axis=0)
        same_head_b = same_head.astype(BF16)
        e = jnp.zeros_like(logf)
        for d in range(HG_BAND):
            s = -d if rev else d
            if d > 0:
                e = e + sroll(logf, s + (1 if rev else -1))
            kd, vd = (kk, v) if d == 0 else (sroll(kk, s), sroll(v, s))
            p = q * kd * jnp.exp(e)
            o = o + band[d] * (_bdot(p, same_head_b) * vd)
        return (st_new,), (o,)

    return step


def hg_scan(name, rev, qr, zf, v, lower, n_ctx):
    consts = _hg_consts(rev)
    (o,) = chunk_scan(_hg_step_fn(rev, consts[1].shape[0]), name, rev, (qr, zf, v), (False,) * 3,
                      (lower.reshape(1, -1),), consts, [(C_WIDTH, C_WIDTH)], [C_WIDTH], HG_Q, n_ctx)
    return o


def conv_silu(x, w, b, n_ctx, name):
    B, L, C = x.shape
    TC = 128
    w8 = jnp.zeros((8, C), F32).at[:A_CONV].set(w)
    b2 = b.reshape(1, C)
    pad = A_CONV // 2

    def taps(v, sign):
        t = lax.broadcasted_iota(jnp.int32, v.shape, 0)
        out = []
        for k in range(A_CONV):
            s = sign * (k - pad)
            src = t + s
            ok = (src >= 0) & (src < L) & ((t >= n_ctx) == (src >= n_ctx))
            vs = v if s == 0 else pltpu.roll(v, (-s) % L, 0)
            out.append((k, jnp.where(ok, vs, 0.0)))
        return out

    def pre(x_ref, w_ref, b_ref):
        xv = x_ref[0].astype(F32)
        y = b_ref[...] + sum(xs * w_ref[pl.ds(k, 1), :] for k, xs in taps(xv, 1))
        return xv, y

    def fwd_body(x_ref, w_ref, b_ref, o_ref):
        _, y = pre(x_ref, w_ref, b_ref)
        o_ref[0] = _silu(y)

    def bwd_body(x_ref, w_ref, b_ref, g_ref, dx_ref, dw_ref, db_ref):
        xv, y = pre(x_ref, w_ref, b_ref)
        sg = _sigmoid(y)
        dy = g_ref[0] * (sg + y * sg * (1.0 - sg))
        dx_ref[0] = sum(ds * w_ref[pl.ds(k, 1), :] for k, ds in taps(dy, -1)).astype(dx_ref.dtype)
        first = pl.program_id(1) == 0

        @pl.when(first)
        def _():
            dw_ref[...] = jnp.zeros_like(dw_ref)
            db_ref[...] = jnp.zeros_like(db_ref)

        for k, xs in taps(xv, 1):
            dw_ref[pl.ds(k, 1), :] += jnp.sum(dy * xs, axis=0, keepdims=True)
        db_ref[...] += jnp.sum(dy, axis=0, keepdims=True)

    x_spec = pl.BlockSpec((1, L, TC), lambda j, bb: (bb, 0, j))
    w_spec = pl.BlockSpec((8, TC), lambda j, bb: (0, j))
    b_spec = pl.BlockSpec((1, TC), lambda j, bb: (0, j))

    @jax.custom_vjp
    def op(x, w8, b2):
        return pl.pallas_call(fwd_body, grid=(C // TC, B), in_specs=[x_spec, w_spec, b_spec], out_specs=x_spec,
                              out_shape=jax.ShapeDtypeStruct(x.shape, F32), name=name + "_fwd",
                              compiler_params=_cp("parallel", "parallel"))(x, w8, b2)

    def op_fwd(x, w8, b2):
        return op(x, w8, b2), (x, w8, b2)

    def op_bwd(res, g):
        x, w8, b2 = res
        return tuple(pl.pallas_call(
            bwd_body, grid=(C // TC, B), in_specs=[x_spec, w_spec, b_spec, x_spec],
            out_specs=[x_spec, w_spec, b_spec],
            out_shape=[jax.ShapeDtypeStruct(x.shape, x.dtype), jax.ShapeDtypeStruct(w8.shape, F32),
                       jax.ShapeDtypeStruct(b2.shape, F32)],
            name=name + "_bwd", compiler_params=_cp("parallel", "arbitrary"))(x, w8, b2, g))

    op.defvjp(op_fwd, op_bwd)
    return op(x, w8, b2)


def loss_head(h, target, norm_w, n_ctx):
    B, L, Dm = h.shape
    S = target.shape[1]
    tl = ROW_TILE
    skip = n_ctx // tl

    def body(h_ref, t_ref, w_ref, loss_ref, dh_ref, dw_ref):
        b, i = pl.program_id(0), pl.program_id(1)
        fn = lambda hv, wv: _rms(hv) * wv
        y, vjp = jax.vjp(fn, h_ref[0], w_ref[...])
        err = y - t_ref[0]
        dhv, dwv = vjp(err * (1.0 / Dm))
        dh_ref[0] = dhv
        part = 0.5 * jnp.sum(jnp.mean(err * err, axis=-1, keepdims=True), axis=0, keepdims=True)
        first = (b == 0) & (i == 0)

        @pl.when(first)
        def _():
            loss_ref[...] = jnp.zeros_like(loss_ref)
            dw_ref[...] = jnp.zeros_like(dw_ref)

        loss_ref[...] += jnp.broadcast_to(part, loss_ref.shape)
        dw_ref[...] += dwv

    loss, dh, dw = pl.pallas_call(
        body, grid=(B, S // tl),
        in_specs=[pl.BlockSpec((1, tl, Dm), lambda b, i: (b, i + skip, 0)),
                  pl.BlockSpec((1, tl, Dm), lambda b, i: (b, i, 0)),
                  pl.BlockSpec((1, Dm), lambda b, i: (0, 0))],
        out_specs=[pl.BlockSpec((8, 128), lambda b, i: (0, 0)),
                   pl.BlockSpec((1, tl, Dm), lambda b, i: (b, i, 0)),
                   pl.BlockSpec((1, Dm), lambda b, i: (0, 0))],
        out_shape=[jax.ShapeDtypeStruct((8, 128), F32), jax.ShapeDtypeStruct((B, S, Dm), F32),
                   jax.ShapeDtypeStruct((1, Dm), F32)],
        name="loss_head", compiler_params=_cp("arbitrary", "arbitrary"))(h, target, norm_w.reshape(1, Dm))
    dh_full = jnp.concatenate([jnp.zeros((B, n_ctx, Dm), F32), dh], axis=1)
    return loss[0, 0], dh_full, dw.reshape(Dm)


def _modulate(h, shift, scale):
    return _rms(h) * (1.0 + scale) + shift


def _f_mod(r, m, p):
    return (_modulate(r[0], m[0], m[1]),)


def _f_resid_mod(coef):
    def f(r, m, p):
        h2 = r[0] + coef * m[0] * r[1].astype(F32)
        return h2, _modulate(h2, m[1], m[2])
    return f


def _f_resid(coef):
    def f(r, m, p):
        return (r[0] + coef * m[0] * r[1].astype(F32),)
    return f


def _f_swiglu(r, m, p):
    pre = r[0].astype(F32)
    return (_silu(pre[:, :D_FF]) * pre[:, D_FF:],)


def _f_ssd_post(r, m, p):
    y_f, y_b, xs, z = (t.astype(F32) for t in r)
    d_skip, norm_w = p
    y = (y_f + y_b + d_skip * xs) * _silu(z)
    return (_rms(y) * norm_w,)


def _f_s5_post(r, m, p):
    y_f, y_b, u = (t.astype(F32) for t in r)
    d_skip, glu_w, glu_b = p
    y = jax.nn.gelu(y_f + y_b + d_skip * u)
    return (y * _sigmoid(_bdot(y, glu_w) + glu_b),)


def _f_hg_post(r, m, p):
    o_f, o_b, g = (t.astype(F32) for t in r)
    (norm_w,) = p
    W = C_WIDTH
    same_head = (lax.broadcasted_iota(jnp.int32, (W, W), 0) // C_KEY
                 == lax.broadcasted_iota(jnp.int32, (W, W), 1) // C_KEY).astype(BF16)
    o = o_f + o_b
    ms = xdot(o * o, same_head) * (1.0 / C_KEY)
    return (o * lax.rsqrt(ms + EPS) * norm_w * _silu(g),)


def _pad_w_in(w):
    dt0 = A_INNER + A_CONV_DIM
    zeros = jnp.zeros((w.shape[0], IN_COLS_PAD - IN_COLS), w.dtype)
    return jnp.concatenate([w[:, :dt0], w[:, A_COLS:], w[:, dt0:A_COLS], zeros], axis=1)


def _to_columns(t, n_ctx):
    B, L, W = t.shape
    rows = (L - n_ctx) // GRID_W
    lat = t[:, n_ctx:].reshape(B, rows, GRID_W, W).transpose(0, 2, 1, 3).reshape(B, L - n_ctx, W)
    return jnp.concatenate([t[:, :n_ctx], lat], axis=1)


def _to_raster(t, n_ctx):
    B, L, W = t.shape
    rows = (L - n_ctx) // GRID_W
    lat = t[:, n_ctx:].reshape(B, GRID_W, rows, W).transpose(0, 2, 1, 3).reshape(B, L - n_ctx, W)
    return jnp.concatenate([t[:, :n_ctx], lat], axis=1)


def _forward(h0, c, big, P, n_ctx):
    B, L, Dm = h0.shape
    depth = len(big["mod_w"])
    ncb = n_ctx // ROW_TILE

    def rows_of(t):
        return t.reshape(t.shape[0] * t.shape[1], t.shape[2])

    def w_in_of(t):
        return _pad_w_in(jnp.moveaxis(t, 0, 1).reshape(t.shape[1], IN_COLS))

    p_lb = jax.nn.softmax(P["hg_lb_logits"], axis=0)
    lower_bounds = jnp.cumsum(p_lb, axis=0) - p_lb[:1]
    cc = jnp.concatenate([c, P["c_ctx"][None], jnp.zeros((8 - B - 1, Dm), F32)], axis=0)
    cc = _silu(cc)

    def mods_of(l):
        m = linear(cc, big["mod_w"][l], P["mod_w"][l], f"mod{l}", pieces=True) + P["mod_b"][l]
        m = m.reshape(8, N_MOD, Dm)
        seg = jnp.stack([jnp.broadcast_to(m[B], (B, N_MOD, Dm)), m[:B]], axis=1)
        return [seg[:, :, j:j + 1, :] for j in range(N_MOD)]

    def ffn(u, l, j):
        pre = linear(u.reshape(B * L, Dm), big["ffn_w_in"][l][j], P["ffn_w_in"][l][j], f"ffn_in{l}{j}", BF16,
                     pieces=True)
        (act,) = rowwise(_f_swiglu, f"swiglu{l}{j}", [pre.reshape(B, L, 2 * D_FF)], [], [], [(D_FF, BF16)], ncb)
        out = linear(act.reshape(B * L, D_FF), rows_of(big["ffn_w_out"][l][j]), rows_of(P["ffn_w_out"][l][j]),
                     f"ffn_out{l}{j}", BF16)
        return out.reshape(B, L, Dm)

    h = h0
    mods = mods_of(0)
    (u,) = rowwise(_f_mod, "mod_first", [h], [mods[0], mods[1]], [], [(Dm, BF16)], ncb)
    for l in range(depth):
        col_major = l % 2 == 1
        o = ffn(u, l, 0)
        h, u = rowwise(_f_resid_mod(0.5), f"resid_a{l}", [h, o], [mods[2], mods[3], mods[4]], [],
                       [(Dm, F32), (Dm, BF16)], ncb)
        if col_major:
            u = _to_columns(u, n_ctx)
        pre = linear(u.reshape(B * L, Dm), w_in_of(big["w_in"][l]), w_in_of(P["w_in"][l]), f"w_in{l}", BF16)
        pre = pre.reshape(B, L, IN_COLS_PAD)
        o0 = 0
        z, o0 = pre[..., o0:o0 + A_INNER], o0 + A_INNER
        xbc, o0 = pre[..., o0:o0 + A_CONV_DIM], o0 + A_CONV_DIM
        pb, o0 = pre[..., o0:o0 + B_WIDTH], o0 + B_WIDTH
        pc, o0 = pre[..., o0:o0 + 5 * C_WIDTH], o0 + 5 * C_WIDTH
        dtr = pre[..., o0:o0 + 2 * A_HEADS]
        xbc = conv_silu(xbc, P["a_conv_w"][l], P["a_conv_b"][l], n_ctx, f"conv{l}")
        xs, bm, cm = xbc[..., :A_INNER], xbc[..., A_INNER:A_INNER + 128], xbc[..., A_INNER + 128:]
        ya_dir = [ssd_scan(f"ssd{l}{d}", bool(d), xs, bm, cm, dtr[..., d * A_HEADS:(d + 1) * A_HEADS],
                           P["a_dt_bias"][l, d], P["a_log"][l, d], n_ctx) for d in range(2)]
        d_skip = jnp.repeat(P["a_d"][l], A_INNER // A_HEADS).reshape(1, A_INNER)
        (ya,) = rowwise(_f_ssd_post, f"ssd_post{l}", [ya_dir[0], ya_dir[1], xs, z], [],
                        [d_skip, P["a_norm_w"][l].reshape(1, -1)], [(A_INNER, BF16)], ncb)
        yb_dir = [s5_scan(f"s5{l}{d}", bool(d), pb, P["s5_lam_re"][l, d], P["s5_lam_im"][l, d],
                          P["s5_log_step"][l, d], P["s5_b_re"][l, d], P["s5_b_im"][l, d], P["s5_c_re"][l, d],
                          P["s5_c_im"][l, d], n_ctx) for d in range(2)]
        (yb,) = rowwise(_f_s5_post, f"s5_post{l}", [yb_dir[0], yb_dir[1], pb], [],
                        [P["s5_d"][l].reshape(1, -1), P["s5_glu_w"][l], P["s5_glu_b"][l].reshape(1, -1)],
                        [(B_WIDTH, BF16)], ncb)
        qr, f_raw, vi, gate = (pc[..., :C_WIDTH], pc[..., C_WIDTH:3 * C_WIDTH], pc[..., 3 * C_WIDTH:4 * C_WIDTH],
                               pc[..., 4 * C_WIDTH:])
        yc_dir = [hg_scan(f"hg{l}{d}", bool(d), qr, f_raw[..., d * C_WIDTH:(d + 1) * C_WIDTH], vi,
                          lower_bounds[l, d], n_ctx) for d in range(2)]
        (yc,) = rowwise(_f_hg_post, f"hg_post{l}", [yc_dir[0], yc_dir[1], gate], [],
                        [P["hg_norm_w"][l].reshape(1, -1)], [(C_WIDTH, BF16)], ncb)
        mix = jnp.concatenate([ya, yb, yc], axis=-1)
        y = linear(mix.reshape(B * L, Dm), rows_of(big["w_out"][l]), rows_of(P["w_out"][l]), f"w_out{l}", BF16)
        y = y.reshape(B, L, Dm)
        if col_major:
            y = _to_raster(y, n_ctx)
        h, u = rowwise(_f_resid_mod(1.0), f"resid_b{l}", [h, y], [mods[5], mods[6], mods[7]], [],
                       [(Dm, F32), (Dm, BF16)], ncb)
        o = ffn(u, l, 1)
        gate8 = mods[8]
        if l + 1 < depth:
            mods = mods_of(l + 1)
            h, u = rowwise(_f_resid_mod(0.5), f"resid_c{l}", [h, o], [gate8, mods[0], mods[1]], [],
                           [(Dm, F32), (Dm, BF16)], ncb)
        else:
            (h,) = rowwise(_f_resid(0.5), f"resid_c{l}", [h, o], [gate8], [], [(Dm, F32)], ncb)
    return h


BIG = ("mod_w", "ffn_w_in", "ffn_w_out", "w_in", "w_out")


def local_step(x, c, ctx, target, big, small):
    n_ctx = ctx.shape[1]
    h0 = jnp.concatenate([ctx, x], axis=1)
    P = dict(small)
    for k in BIG:
        P[k] = jax.tree.map(lambda t: jnp.zeros(t.shape, BF16), big[k])
    h, vjp = jax.vjp(lambda h0, P: _forward(h0, c, big, P, n_ctx), h0, P)
    loss, dh, d_final = loss_head(h, target, small["final_norm_w"], n_ctx)
    dh0, grads = vjp(dh)
    grads = dict(grads)
    grads["final_norm_w"] = grads["final_norm_w"] + d_final
    return loss, dh0[:, n_ctx:], grads


MESH = pl.DeviceIdType.MESH
LANES = 1024
N_CHIPS = 4
ELEMENTWISE_BLOCK_BYTES = 2 * 1024 * 1024
ANY = pl.BlockSpec(memory_space=pl.ANY)


def _place():
    x, y, c = lax.axis_index("x"), lax.axis_index("y"), lax.axis_index("c")
    return x, y, c, 2 * x + y


def _other_chips(x, y):
    return [(x ^ kx, y ^ ky, 2 * (x ^ kx) + (y ^ ky)) for kx, ky in ((0, 1), (1, 0), (1, 1))]


def all_gather_shards(shard, name):
    def body(x_ref, out_ref, send_sems, recv_sems, local_sem):
        x, y, c, q = _place()
        sibling = (x, y, 1 - c)
        chips = _other_chips(x, y)

        def copy(k, src, dst, to):
            return pltpu.make_async_remote_copy(src_ref=src, dst_ref=dst, send_sem=send_sems.at[k],
                                                recv_sem=recv_sems.at[k], device_id=to, device_id_type=MESH)

        mine = pltpu.make_async_copy(x_ref, out_ref.at[q], local_sem)
        mine.start()
        first = [copy(k, x_ref.at[c], out_ref.at[q, c], (px, py, c)) for k, (px, py, _) in enumerate(chips)]
        for cp in first:
            cp.start()
        passed = [copy(3 + k, out_ref.at[pq, c], out_ref.at[pq, c], sibling) for k, (_, _, pq) in enumerate(chips)]
        for k, (_, _, pq) in enumerate(chips):
            copy(k, x_ref.at[c], out_ref.at[pq, c], sibling).wait_recv()
            passed[k].start()
        for k, (_, _, pq) in enumerate(chips):
            copy(3 + k, x_ref.at[c], out_ref.at[pq, 1 - c], sibling).wait_recv()
        for cp in first + passed:
            cp.wait_send()
        mine.wait()

    return pl.pallas_call(
        body, out_shape=jax.ShapeDtypeStruct((N_CHIPS,) + shard.shape, shard.dtype), in_specs=[ANY], out_specs=ANY,
        scratch_shapes=[pltpu.SemaphoreType.DMA((6,)), pltpu.SemaphoreType.DMA((6,)), pltpu.SemaphoreType.DMA],
        name=name)(shard)


def all_gather_devices(part, name):
    def body(x_ref, out_ref, send_sems, recv_sems, local_sem):
        x, y, c, _ = _place()
        me = 4 * x + 2 * y + c
        mine = pltpu.make_async_copy(x_ref, out_ref.at[me], local_sem)
        mine.start()
        copies = []
        for k in range(1, 8):
            px, py, pc = x ^ (k >> 2), y ^ ((k >> 1) & 1), c ^ (k & 1)
            copies.append(pltpu.make_async_remote_copy(
                src_ref=x_ref, dst_ref=out_ref.at[me], send_sem=send_sems.at[k - 1], recv_sem=recv_sems.at[k - 1],
                device_id=(px, py, pc), device_id_type=MESH))
            copies[-1].start()
        for k in range(1, 8):
            peer = 4 * (x ^ (k >> 2)) + 2 * (y ^ ((k >> 1) & 1)) + (c ^ (k & 1))
            pltpu.make_async_remote_copy(
                src_ref=x_ref, dst_ref=out_ref.at[peer], send_sem=send_sems.at[k - 1], recv_sem=recv_sems.at[k - 1],
                device_id=(x, y, c), device_id_type=MESH).wait_recv()
        for cp in copies:
            cp.wait_send()
        mine.wait()

    return pl.pallas_call(
        body, out_shape=jax.ShapeDtypeStruct((8,) + part.shape, part.dtype), in_specs=[ANY], out_specs=ANY,
        scratch_shapes=[pltpu.SemaphoreType.DMA((7,)), pltpu.SemaphoreType.DMA((7,)), pltpu.SemaphoreType.DMA],
        name=name)(part)


def _half(ref, c, axis):
    h = ref.shape[axis] // 2
    return ref.at[(slice(None),) * axis + (pl.ds(pl.multiple_of(c * h, 8), h),)]


def _remote(src, dst, send_sems, recv_sems, s, to):
    return pltpu.make_async_remote_copy(src_ref=src, dst_ref=dst, send_sem=send_sems.at[s], recv_sem=recv_sems.at[s],
                                        device_id=to, device_id_type=MESH)


def gather_pieces(arrays, name):
    n = len(arrays)

    def body(*refs):
        ins, outs, (send_sems, recv_sems) = refs[:n], refs[n:2 * n], refs[2 * n:]
        x, y, c, q = _place()
        sibling = (x, y, 1 - c)
        chips = _other_chips(x, y)
        started = []

        def go(src, dst, s, to):
            started.append(_remote(src, dst, send_sems, recv_sems, s, to))
            started[-1].start()

        for i, (a, o) in enumerate(zip(ins, outs)):
            go(a, o.at[q], 7 * i + 6, sibling)
            for k, (px, py, _) in enumerate(chips):
                go(_half(a, c, 0), _half(o.at[q], c, 0), 7 * i + k, (px, py, c))
        for i, o in enumerate(outs):
            for k, (_, _, pq) in enumerate(chips):
                land = _half(o.at[pq], c, 0)
                _remote(land, land, send_sems, recv_sems, 7 * i + k, sibling).wait_recv()
                go(land, land, 7 * i + 3 + k, sibling)
        for i, (a, o) in enumerate(zip(ins, outs)):
            for k, (_, _, pq) in enumerate(chips):
                land = _half(o.at[pq], 1 - c, 0)
                _remote(land, land, send_sems, recv_sems, 7 * i + 3 + k, sibling).wait_recv()
            _remote(a, o.at[q], send_sems, recv_sems, 7 * i + 6, sibling).wait_recv()
        for cp in started:
            cp.wait_send()

    return pl.pallas_call(
        body, out_shape=[jax.ShapeDtypeStruct((N_CHIPS,) + a.shape, a.dtype) for a in arrays], in_specs=[ANY] * n,
        out_specs=[ANY] * n, scratch_shapes=[pltpu.SemaphoreType.DMA((7 * n,)), pltpu.SemaphoreType.DMA((7 * n,))],
        name=name)(*arrays)


def swap_with_sibling(arrays, name, send_other_half_of_axis=None):
    n = len(arrays)
    ax = send_other_half_of_axis

    def body(*refs):
        ins, outs, (send_sems, recv_sems) = refs[:n], refs[n:2 * n], refs[2 * n:]
        x, y, c, _ = _place()
        copies = [_remote(a if ax is None else _half(a, 1 - c, ax), o, send_sems, recv_sems, i, (x, y, 1 - c))
                  for i, (a, o) in enumerate(zip(ins, outs))]
        for cp in copies:
            cp.start()
        for cp in copies:
            cp.wait()

    def out_of(a):
        shape = list(a.shape)
        if ax is not None:
            shape[ax] //= 2
        return jax.ShapeDtypeStruct(tuple(shape), a.dtype)

    return pl.pallas_call(
        body, out_shape=[out_of(a) for a in arrays], in_specs=[ANY] * n, out_specs=[ANY] * n,
        scratch_shapes=[pltpu.SemaphoreType.DMA((n,)), pltpu.SemaphoreType.DMA((n,))], name=name)(*arrays)


def pieces_to_chips(pairs, name):
    n = len(pairs)

    def body(*refs):
        ins, outs, (send_sems, recv_sems) = refs[:n], refs[n:2 * n], refs[2 * n:]
        x, y, c, q = _place()
        chips = _other_chips(x, y)
        copies = [_remote(a.at[pq], o.at[q], send_sems, recv_sems, 3 * i + k, (px, py, c))
                  for i, (a, o) in enumerate(zip(ins, outs)) for k, (px, py, pq) in enumerate(chips)]
        for cp in copies:
            cp.start()
        for i, (a, o) in enumerate(zip(ins, outs)):
            for k, (_, _, pq) in enumerate(chips):
                _remote(a.at[pq], o.at[pq], send_sems, recv_sems, 3 * i + k, (x, y, c)).wait_recv()
        for cp in copies:
            cp.wait_send()

    return pl.pallas_call(
        body, out_shape=[jax.ShapeDtypeStruct(a.shape, a.dtype) for a in pairs], in_specs=[ANY] * n,
        out_specs=[ANY] * n, scratch_shapes=[pltpu.SemaphoreType.DMA((3 * n,)), pltpu.SemaphoreType.DMA((3 * n,))],
        name=name)(*pairs)


def _rows_block(h, n_cols):
    return _tile(h, 16, max(16, ELEMENTWISE_BLOCK_BYTES // (n_cols * 4)))


def add_pair(core, g, got, name):
    _, K, N = g.shape
    h = K // 2
    th = _rows_block(h, N)
    nb = h // th

    def body(c_ref, g_ref, got_ref, o_ref):
        o_ref[...] = (g_ref[...].astype(F32) + got_ref[...].astype(F32)).astype(o_ref.dtype)

    grid_spec = pltpu.PrefetchScalarGridSpec(
        num_scalar_prefetch=1, grid=(N_CHIPS, nb),
        in_specs=[pl.BlockSpec((None, th, N), lambda p, i, c_ref: (p, c_ref[0] * nb + i, 0)),
                  pl.BlockSpec((None, th, N), lambda p, i, c_ref: (p, i, 0))],
        out_specs=pl.BlockSpec((None, th, N), lambda p, i, c_ref: (p, i, 0)))
    return pl.pallas_call(body, grid_spec=grid_spec, out_shape=jax.ShapeDtypeStruct(got.shape, g.dtype), name=name,
                          compiler_params=_cp("parallel", "parallel"))(core, g, got)


def sum_pieces(chip, pair, got, name):
    _, h, N = pair.shape
    th = _rows_block(h, N)

    def body(q_ref, pair_ref, *rest):
        got_refs, o_ref = rest[:N_CHIPS], rest[N_CHIPS]
        q = q_ref[0]
        for p in range(N_CHIPS):
            def put(val, p=p):
                if p == 0:
                    o_ref[...] = val
                else:
                    o_ref[...] += val

            @pl.when(q == p)
            def _():
                put(pair_ref[...].astype(F32))

            @pl.when(q != p)
            def _(p=p):
                put(got_refs[p][...].astype(F32))

    def got_spec(p):
        return pl.BlockSpec((None, th, N), lambda i, q_ref: (jnp.where(q_ref[0] == p, (p + 1) % N_CHIPS, p), i, 0))

    grid_spec = pltpu.PrefetchScalarGridSpec(
        num_scalar_prefetch=1, grid=(h // th,),
        in_specs=[pl.BlockSpec((None, th, N), lambda i, q_ref: (q_ref[0], i, 0))] + [got_spec(p) for p in range(N_CHIPS)],
        out_specs=pl.BlockSpec((th, N), lambda i, q_ref: (i, 0)))
    return pl.pallas_call(body, grid_spec=grid_spec, out_shape=jax.ShapeDtypeStruct((h, N), F32), name=name,
                          compiler_params=_cp("parallel"))(chip, pair, *([got] * N_CHIPS))


def adamw_halves(core, mine, other, w, m, v, name):
    n, K, N = w.shape
    h = K // 2
    th = _rows_block(h, N)
    nb = h // th

    def body(c_ref, *refs):
        mine_refs, other_refs = refs[:n], refs[n:2 * n]
        w_ref, m_ref, v_ref, g_ref, d_ref, m2_ref, v2_ref = refs[2 * n:]
        a, s, c = pl.program_id(0), pl.program_id(1), c_ref[0]
        for idx in range(n):
            @pl.when((a == idx) & (s == c))
            def _(idx=idx):
                g_ref[...] = mine_refs[idx][...]

            @pl.when((a == idx) & (s != c))
            def _(idx=idx):
                g_ref[...] = other_refs[idx][...]
        _adamw_math(g_ref[...], w_ref, m_ref, v_ref, d_ref, m2_ref, v2_ref)

    def half_spec(idx, is_mine):
        def index(a, s, i, c_ref):
            right_half = (s == c_ref[0]) if is_mine else (s != c_ref[0])
            return (jnp.where((a == idx) & right_half, i, 0), 0)
        return pl.BlockSpec((th, N), index)

    full = pl.BlockSpec((None, th, N), lambda a, s, i, c_ref: (a, s * nb + i, 0))
    grid_spec = pltpu.PrefetchScalarGridSpec(
        num_scalar_prefetch=1, grid=(n, 2, nb),
        in_specs=[half_spec(idx, True) for idx in range(n)] + [half_spec(idx, False) for idx in range(n)] + [full] * 3,
        out_specs=[full] * 4)
    return pl.pallas_call(body, grid_spec=grid_spec, out_shape=[jax.ShapeDtypeStruct((n, K, N), F32)] * 4, name=name,
                          compiler_params=_cp("arbitrary", "arbitrary", "arbitrary"))(core, *mine, *other, w, m, v)


def _adamw_math(gv, w_ref, m_ref, v_ref, d_ref, m2_ref, v2_ref):
    m2 = ADAM_B1 * m_ref[...] + (1.0 - ADAM_B1) * gv
    v2 = ADAM_B2 * v_ref[...] + (1.0 - ADAM_B2) * (gv * gv)
    m_hat = m2 / (1.0 - ADAM_B1 ** ADAM_STEP)
    v_hat = v2 / (1.0 - ADAM_B2 ** ADAM_STEP)
    d_ref[...] = -ADAM_LR * (m_hat / (jnp.sqrt(v_hat) + ADAM_EPS) + ADAM_WD * w_ref[...])
    m2_ref[...] = m2
    v2_ref[...] = v2


def _row_tile(R):
    return _pick(R, (512, 256, 128, 64, 32, 16, 8))


def sum_parts(parts, name):
    P, R, _ = parts.shape
    rt = _row_tile(R)

    def body(p_ref, o_ref):
        acc = p_ref[0]
        for p in range(1, P):
            acc = acc + p_ref[p]
        o_ref[...] = acc

    return pl.pallas_call(
        body, grid=(R // rt,), in_specs=[pl.BlockSpec((P, rt, LANES), lambda i: (0, i, 0))],
        out_specs=pl.BlockSpec((rt, LANES), lambda i: (i, 0)), out_shape=jax.ShapeDtypeStruct((R, LANES), F32),
        name=name, compiler_params=_cp("parallel"))(parts)


def adamw(g, w, m, v, name):
    R = g.shape[0]
    rt = _row_tile(R)

    def body(g_ref, w_ref, m_ref, v_ref, d_ref, m2_ref, v2_ref):
        _adamw_math(g_ref[...], w_ref, m_ref, v_ref, d_ref, m2_ref, v2_ref)

    spec = pl.BlockSpec((rt, LANES), lambda i: (i, 0))
    return pl.pallas_call(
        body, grid=(R // rt,), in_specs=[spec] * 4, out_specs=[spec] * 3,
        out_shape=[jax.ShapeDtypeStruct((R, LANES), F32)] * 3, name=name, compiler_params=_cp("parallel"))(g, w, m, v)


def _pack(arrays, rows_multiple, dtype):
    flat = jnp.concatenate([a.reshape(-1).astype(dtype) for a in arrays])
    n = flat.shape[0]
    per = rows_multiple * LANES
    total = -(-n // per) * per
    return jnp.pad(flat, (0, total - n)).reshape(total // LANES, LANES)


def _unpack(buf, shapes):
    flat = buf.reshape(-1)
    out, off = [], 0
    for s in shapes:
        n = math.prod(s)
        out.append(flat[off:off + n].reshape(s))
        off += n
    return out


SHARDED = (("mod_w", 2), ("ffn_w_in", 3), ("ffn_w_out", 2), ("w_in", 2), ("w_out", 1),
           ("a_conv_w", 2), ("s5_glu_w", 1), ("hg_lb_logits", 2))
WEIGHTS = ("c_ctx", "mod_w", "mod_b", "ffn_w_in", "ffn_w_out", "w_in", "w_out", "a_conv_w", "a_conv_b", "a_dt_bias",
           "a_log", "a_d", "a_norm_w", "s5_lam_re", "s5_lam_im", "s5_log_step", "s5_b_re", "s5_b_im", "s5_c_re",
           "s5_c_im", "s5_d", "s5_glu_w", "s5_glu_b", "hg_lb_logits", "hg_norm_w", "final_norm_w")


def _gather_full(local, names_axes, dtype, rows_multiple, name):
    buf = _pack([local[n] for n, _ in names_axes], 2 * rows_multiple, dtype)
    R = buf.shape[0] // 2
    full = all_gather_shards(buf.reshape(2, R, LANES), name).reshape(N_CHIPS, 2 * R, LANES)
    per_chip = [_unpack(full[q], [local[n].shape for n, _ in names_axes]) for q in range(N_CHIPS)]
    return {n: jnp.concatenate([per_chip[q][j] for q in range(N_CHIPS)], axis=ax)
            for j, (n, ax) in enumerate(names_axes)}


def kernel(x, c, ctx, c_ctx, mod_w, mod_b, ffn_w_in, ffn_w_out, w_in, w_out, a_conv_w, a_conv_b, a_dt_bias, a_log, a_d, a_norm_w, s5_lam_re, s5_lam_im, s5_log_step, s5_b_re, s5_b_im, s5_c_re, s5_c_im, s5_d, s5_glu_w, s5_glu_b, hg_lb_logits, hg_norm_w, final_norm_w, loss_target, m_c_ctx, m_mod_w, m_mod_b, m_ffn_w_in, m_ffn_w_out, m_w_in, m_w_out, m_a_conv_w, m_a_conv_b, m_a_dt_bias, m_a_log, m_a_d, m_a_norm_w, m_s5_lam_re, m_s5_lam_im, m_s5_log_step, m_s5_b_re, m_s5_b_im, m_s5_c_re, m_s5_c_im, m_s5_d, m_s5_glu_w, m_s5_glu_b, m_hg_lb_logits, m_hg_norm_w, m_final_norm_w, v_c_ctx, v_mod_w, v_mod_b, v_ffn_w_in, v_ffn_w_out, v_w_in, v_w_out, v_a_conv_w, v_a_conv_b, v_a_dt_bias, v_a_log, v_a_d, v_a_norm_w, v_s5_lam_re, v_s5_lam_im, v_s5_log_step, v_s5_b_re, v_s5_b_im, v_s5_c_re, v_s5_c_im, v_s5_d, v_s5_glu_w, v_s5_glu_b, v_hg_lb_logits, v_hg_norm_w, v_final_norm_w):
    given = dict(locals())
    w = {n: given[n] for n in WEIGHTS}
    m = {n: given["m_" + n] for n in WEIGHTS}
    v = {n: given["v_" + n] for n in WEIGHTS}
    small_sharded = SHARDED[len(BIG):]
    replicated = [n for n in WEIGHTS if n not in [s for s, _ in SHARDED]]
    core = lax.axis_index("c").astype(jnp.int32).reshape(1)
    chip = (2 * lax.axis_index("x") + lax.axis_index("y")).astype(jnp.int32)

    def leaves(t):
        return list(t.reshape((-1,) + t.shape[-2:]))

    def nest(kind, flat):
        flat = list(flat)
        return [flat[2 * l:2 * l + 2] for l in range(len(flat) // 2)] if w[kind].ndim == 4 else flat

    counts = [len(leaves(w[k])) for k in BIG]
    gathered = gather_pieces([a.astype(BF16) for k in BIG for a in leaves(w[k])], "gather_big")
    big, off = {}, 0
    for k, cnt in zip(BIG, counts):
        big[k], off = nest(k, gathered[off:off + cnt]), off + cnt
    small = {n: w[n] for n in replicated}
    small.update(_gather_full(w, small_sharded, F32, 8, "gather_small"))

    loss, grad_x, grads = local_step(x, c, ctx, loss_target, big, small)

    g_big = [g for k in BIG for g in jax.tree.leaves(grads[k])]
    got = swap_with_sibling(g_big, "rs_sibling", send_other_half_of_axis=1)
    pairs = [add_pair(core, g, t, f"rs_add_pair{i}") for i, (g, t) in enumerate(zip(g_big, got))]
    from_chips = pieces_to_chips(pairs, "rs_chips")
    mine = [sum_pieces(chip.reshape(1), p, t, f"rs_sum_chips{i}") for i, (p, t) in enumerate(zip(pairs, from_chips))]
    other = swap_with_sibling(mine, "rs_halves")
    out, off = {}, 0
    for k, cnt in zip(BIG, counts):
        stack = lambda t: t.reshape((cnt,) + t.shape[-2:])
        res = adamw_halves(core, mine[off:off + cnt], other[off:off + cnt], stack(w[k]), stack(m[k]), stack(v[k]),
                           "adamw_" + k)
        off += cnt
        for kind, a in zip(("grad", "delta", "new_m", "new_v"), res):
            out[kind, k] = a.reshape(w[k].shape)

    names = replicated + [n for n, _ in small_sharded]
    part = _pack([grads[n] for n in names] + [loss.reshape(1)], 8, F32)
    g_all = _unpack(sum_parts(all_gather_devices(part, "gather_parts"), "sum_parts"),
                    [grads[n].shape for n in names] + [(1,)])
    g_of = dict(zip(names + ["loss"], g_all))
    for n, ax in small_sharded:
        g_of[n] = lax.dynamic_slice_in_dim(g_of[n], chip * w[n].shape[ax], w[n].shape[ax], axis=ax)
    rest = lambda d: _pack([d[n] for n in names], 8, F32)
    g_rest = rest(g_of)
    d_rest, m_rest, v_rest = adamw(g_rest, rest(w), rest(m), rest(v), "adamw_rest")
    shapes = [w[n].shape for n in names]
    for kind, buf in (("grad", g_rest), ("delta", d_rest), ("new_m", m_rest), ("new_v", v_rest)):
        for n, a in zip(names, _unpack(buf, shapes)):
            out[kind, n] = a

    loss_total = g_of["loss"].reshape(())
    return (loss_total, grad_x, *[out["grad", n] for n in WEIGHTS], *[out["delta", n] for n in WEIGHTS],
            *[out["new_m", n] for n in WEIGHTS], *[out["new_v", n] for n in WEIGHTS])
```

```python
import functools
import math

import numpy as np
import jax
import jax.numpy as jnp
from jax import lax
from jax.experimental import pallas as pl
from jax.experimental.pallas import tpu as pltpu

F32, BF16 = jnp.float32, jnp.bfloat16
EPS = 1e-6
N_MOD = 9
D_FF = 2816
A_INNER, A_HEADS, A_GROUPS, A_STATE, A_CONV, A_CONV_DIM = 512, 8, 2, 64, 5, 768
A_COLS = A_INNER + A_CONV_DIM + 2 * A_HEADS
B_WIDTH, B_GROUP, B_NGROUPS, B_STATE = 256, 16, 16, 64
C_WIDTH, C_HEADS, C_KEY = 256, 4, 64
GRID_W = 64
IN_COLS = A_COLS + B_WIDTH + 5 * C_WIDTH
IN_COLS_PAD = 3072
ADAM_LR, ADAM_B1, ADAM_B2, ADAM_EPS, ADAM_WD, ADAM_STEP = 0.001, 0.9, 0.999, 1e-08, 0.01, 10

ROW_TILE = 256
SSD_Q, HG_Q = 128, 64
REC_T = 128
SCAN_SAMPLES = 4
ROW_SAMPLES = 2
ROW_SAMPLES_MAX_WIDTH = 1024
HG_BAND = 8
VMEM_LIMIT = 56 * 1024 * 1024
MM_VMEM_BUDGET = 36 * 1024 * 1024

NT = ((1,), (1,))
TN = ((0,), (0,))


def _bdot(a, b, dims=((1,), (0,))):
    return lax.dot_general(a.astype(BF16), b.astype(BF16), (dims, ((), ())), preferred_element_type=F32)


def _split3(x):
    hi = x.astype(BF16)
    r = x - hi.astype(F32)
    mid = r.astype(BF16)
    lo = (r - mid.astype(F32)).astype(BF16)
    return hi, mid, lo


def _mask_lhs(m, x, dims):
    return sum(lax.dot_general(m, p, (dims, ((), ())), preferred_element_type=F32) for p in _split3(x))


def _mask_rhs(x, m, dims):
    return sum(lax.dot_general(p, m, (dims, ((), ())), preferred_element_type=F32) for p in _split3(x))


@jax.custom_vjp
def mdot(m, x):
    return _mask_lhs(m, x, ((1,), (0,)))


def _mdot_fwd(m, x):
    return mdot(m, x), m


def _mdot_bwd(m, g):
    return jnp.zeros_like(m), _mask_lhs(m, g, TN)


mdot.defvjp(_mdot_fwd, _mdot_bwd)


@jax.custom_vjp
def xdot(x, m):
    return _mask_rhs(x, m, ((1,), (0,)))


def _xdot_fwd(x, m):
    return xdot(x, m), m


def _xdot_bwd(m, g):
    return _mask_rhs(g, m, NT), jnp.zeros_like(m)


xdot.defvjp(_xdot_fwd, _xdot_bwd)


@functools.partial(jax.custom_vjp, nondiff_argnums=(1,))
def _roll(x, s):
    return pltpu.roll(x, s, 0)


def _roll_fwd(x, s):
    return _roll(x, s), None


def _roll_bwd(s, _, g):
    return (_roll(g, x_rows(g) - s),)


def x_rows(x):
    return x.shape[0]


_roll.defvjp(_roll_fwd, _roll_bwd)


def sroll(x, s):
    s = s % x.shape[0]
    return x if s == 0 else _roll(x, s)


def _softplus(x):
    return jnp.maximum(x, 0.0) + jnp.log1p(jnp.exp(jnp.minimum(x, -x)))


def _sigmoid(x):
    return 0.5 * (jnp.tanh(0.5 * x) + 1.0)


def _silu(x):
    return x * _sigmoid(x)


def _rms(x):
    return x * lax.rsqrt(jnp.mean(x * x, axis=-1, keepdims=True) + EPS)


def _pick(n, cands):
    for c in cands:
        if n % c == 0:
            return c
    return n


def _cp(*sem):
    return pltpu.CompilerParams(dimension_semantics=sem, vmem_limit_bytes=VMEM_LIMIT)


def _tile(n, unit, cap):
    best = None
    for d in range(unit, min(n, cap) + 1, unit):
        if n % d == 0:
            best = d
    return best if best is not None else n


def _mm(a, b, ta, tb, out_dtype, name, b_pieces=False, out_pieces=False):
    M, K = (a.shape[1], a.shape[0]) if ta else a.shape
    n_div = k_div = None
    if b_pieces and tb:
        N, k_div = b.shape[1], b.shape[2]
    elif b_pieces:
        N, n_div = N_CHIPS * b.shape[2], b.shape[2]
    else:
        N = b.shape[0] if tb else b.shape[1]
    if out_pieces:
        n_div = N // N_CHIPS
    n_div, k_div = n_div or N, k_div or K
    tm = _tile(M, 128 if ta else 8, 1536 if ta else 1024)
    tn = _tile(n_div, 128, 1536)
    a_bytes, b_bytes, o_bytes = a.dtype.itemsize, b.dtype.itemsize, jnp.dtype(out_dtype).itemsize
    cands = {d for d in range(128, min(k_div, 2816) + 1, 128) if k_div % d == 0}
    if k_div <= 2816 or not cands:
        cands.add(k_div)
    for tk in sorted(cands, reverse=True):
        scratch = tm * tn * 4 if (K // tk > 1 and o_bytes != 4) else 0
        if 2 * (tm * tk * a_bytes + tk * tn * b_bytes + tm * tn * o_bytes) + scratch <= MM_VMEM_BUDGET:
            break
    nk = K // tk
    nq_n, nq_k = n_div // tn, k_div // tk
    dims = ((0 if ta else 1,), (1 if tb else 0,))

    def body(a_ref, b_ref, o_ref, *acc):
        part = _bdot(a_ref[...], b_ref[...], dims)
        if nk == 1:
            o_ref[...] = part.astype(o_ref.dtype)
            return
        acc_ref = acc[0] if acc else o_ref
        k = pl.program_id(2)

        @pl.when(k == 0)
        def _():
            acc_ref[...] = part

        @pl.when(k > 0)
        def _():
            acc_ref[...] += part

        if acc:
            @pl.when(k == nk - 1)
            def _():
                o_ref[...] = acc_ref[...].astype(o_ref.dtype)

    a_spec = pl.BlockSpec((tk, tm), lambda i, j, k: (k, i)) if ta else pl.BlockSpec((tm, tk), lambda i, j, k: (i, k))
    if b_pieces and tb:
        b_spec = pl.BlockSpec((None, tn, tk), lambda i, j, k: (k // nq_k, j, k % nq_k))
    elif b_pieces:
        b_spec = pl.BlockSpec((None, tk, tn), lambda i, j, k: (j // nq_n, k, j % nq_n))
    elif tb:
        b_spec = pl.BlockSpec((tn, tk), lambda i, j, k: (j, k))
    else:
        b_spec = pl.BlockSpec((tk, tn), lambda i, j, k: (k, j))
    if out_pieces:
        o_spec = pl.BlockSpec((None, tm, tn), lambda i, j, k: (j // nq_n, i, j % nq_n))
        o_shape = jax.ShapeDtypeStruct((N_CHIPS, M, N // N_CHIPS), out_dtype)
    else:
        o_spec = pl.BlockSpec((tm, tn), lambda i, j, k: (i, j))
        o_shape = jax.ShapeDtypeStruct((M, N), out_dtype)
    return pl.pallas_call(
        body, grid=(M // tm, N // tn, nk), in_specs=[a_spec, b_spec], out_specs=o_spec, out_shape=o_shape,
        scratch_shapes=[pltpu.VMEM((tm, tn), F32)] if (nk > 1 and o_bytes != 4) else [], name=name,
        compiler_params=_cp("parallel", "parallel", "arbitrary"))(a, b)


def linear(x, w16, wslot, name, out_dtype=F32, pieces=False):
    @jax.custom_vjp
    def f(x, w16, wslot):
        return _mm(x, w16, False, False, out_dtype, name + "_fwd", b_pieces=pieces)

    def fwd(x, w16, wslot):
        return f(x, w16, wslot), (x, w16)

    grad_dtype = wslot.dtype

    def bwd(res, dy):
        x, w16 = res
        dx = _mm(dy, w16, False, True, x.dtype, name + "_dx", b_pieces=pieces)
        dw = _mm(x, dy, True, False, grad_dtype, name + "_dw", out_pieces=pieces)
        return dx, jnp.zeros_like(w16), dw

    f.defvjp(fwd, bwd)
    return f(x, w16, wslot)


def rowwise(f, name, rows, mods, params, outs, n_ctx_blocks, tl=ROW_TILE, row_diff=None):
    rows, mods, params = tuple(rows), tuple(mods), tuple(params)
    nr, nm, npar, no = len(rows), len(mods), len(params), len(outs)
    B, L = rows[0].shape[:2]
    nblk = L // tl
    row_diff = tuple(row_diff) if row_diff is not None else (True,) * nr
    out_dtypes = [dt for _, dt in outs]
    widest = max([r.shape[2] for r in rows] + [w for w, _ in outs])
    ns = ROW_SAMPLES if (B % ROW_SAMPLES == 0 and widest <= ROW_SAMPLES_MAX_WIDTH) else 1

    def fcast(r, m, p):
        return tuple(o.astype(dt) for o, dt in zip(f(r, m, p), out_dtypes))

    def seg(i):
        return (i >= n_ctx_blocks).astype(jnp.int32) if n_ctx_blocks else 0

    def specs():
        row_specs = [pl.BlockSpec((ns, tl, r.shape[2]), lambda b, i: (b, i, 0)) for r in rows]
        mod_specs = [pl.BlockSpec((ns, 1, 1, m.shape[3]), lambda b, i: (b, seg(i), 0, 0)) for m in mods]
        par_specs = [pl.BlockSpec(p.shape, lambda b, i: (0, 0)) for p in params]
        out_specs = [pl.BlockSpec((ns, tl, w), lambda b, i: (b, i, 0)) for w, _ in outs]
        return row_specs, mod_specs, par_specs, out_specs

    def load(ins, s):
        r = tuple(x[s] for x in ins[:nr])
        m = tuple(x[s, 0] for x in ins[nr:nr + nm])
        p = tuple(x[...] for x in ins[nr + nm:])
        return r, m, p

    def fwd_call(rows, mods, params):
        def body(*refs):
            for s in range(ns):
                r, m, p = load(refs[:nr + nm + npar], s)
                for o_ref, o in zip(refs[nr + nm + npar:], fcast(r, m, p)):
                    o_ref[s] = o

        rs, ms, ps, os_ = specs()
        return pl.pallas_call(
            body, grid=(B // ns, nblk), in_specs=rs + ms + ps, out_specs=os_,
            out_shape=[jax.ShapeDtypeStruct((B, L, w), dt) for w, dt in outs],
            name=name + "_fwd", compiler_params=_cp("parallel", "parallel"))(*rows, *mods, *params)

    def bwd_call(rows, mods, params, douts):
        didx = [j for j in range(nr) if row_diff[j]]

        def body(*refs):
            n_in = nr + nm + npar
            dins = refs[n_in:n_in + no]
            rest = refs[n_in + no:]
            dr_refs, dm_refs, dp_refs = rest[:len(didx)], rest[len(didx):len(didx) + nm], rest[len(didx) + nm:]
            b, i = pl.program_id(0), pl.program_id(1)
            first_m = (i == 0) | (i == n_ctx_blocks) if n_ctx_blocks else (i == 0)
            first_p = (b == 0) & (i == 0)
            dp = None
            for s in range(ns):
                r, m, p = load(refs[:n_in], s)
                _, vjp = jax.vjp(fcast, r, m, p)
                dr, dm, dp_s = vjp(tuple(d[s] for d in dins))
                dp = dp_s if dp is None else tuple(a + g for a, g in zip(dp, dp_s))
                for ref, j in zip(dr_refs, didx):
                    ref[s] = dr[j].astype(ref.dtype)
                for ref, g in zip(dm_refs, dm):
                    @pl.when(first_m)
                    def _(ref=ref, g=g, s=s):
                        ref[s, 0] = g

                    @pl.when(jnp.logical_not(first_m))
                    def _(ref=ref, g=g, s=s):
                        ref[s, 0] += g
            for ref, g in zip(dp_refs, dp):
                @pl.when(first_p)
                def _(ref=ref, g=g):
                    ref[...] = g

                @pl.when(jnp.logical_not(first_p))
                def _(ref=ref, g=g):
                    ref[...] += g

        rs, ms, ps, os_ = specs()
        out_shape = ([jax.ShapeDtypeStruct(rows[j].shape, rows[j].dtype) for j in didx]
                     + [jax.ShapeDtypeStruct(m.shape, F32) for m in mods]
                     + [jax.ShapeDtypeStruct(p.shape, F32) for p in params])
        res = pl.pallas_call(
            body, grid=(B // ns, nblk), in_specs=rs + ms + ps + os_,
            out_specs=[rs[j] for j in didx] + ms + ps, out_shape=out_shape,
            name=name + "_bwd", compiler_params=_cp("arbitrary", "arbitrary"))(*rows, *mods, *params, *douts)
        dr = [None] * nr
        for j, g in zip(didx, res[:len(didx)]):
            dr[j] = g
        dr = tuple(g if g is not None else jnp.zeros_like(rows[j]) for j, g in enumerate(dr))
        return dr, tuple(res[len(didx):len(didx) + nm]), tuple(res[len(didx) + nm:])

    @jax.custom_vjp
    def op(rows, mods, params):
        return tuple(fwd_call(rows, mods, params))

    def op_fwd(rows, mods, params):
        return op(rows, mods, params), (rows, mods, params)

    def op_bwd(res, douts):
        return bwd_call(*res, tuple(douts))

    op.defvjp(op_fwd, op_bwd)
    return op(rows, mods, params)


def chunk_scan(step, name, rev, seqs, seq_t, params, consts, state_shapes, out_widths, Q, n_ctx):
    seqs, params, consts = tuple(seqs), tuple(params), tuple(consts)
    ns, npar, nc, nst, no = len(seqs), len(params), len(consts), len(state_shapes), len(out_widths)
    B = seqs[0].shape[0]
    L = seqs[0].shape[2] if seq_t[0] else seqs[0].shape[1]
    nck, ncc = L // Q, n_ctx // Q
    nb = SCAN_SAMPLES if B % SCAN_SAMPLES == 0 else 1

    def chunk_of(k):
        if not rev:
            return k
        return jnp.where(k < ncc, ncc - 1 - k, nck + ncc - 1 - k)

    def specs(order):
        def seq_spec(s, t):
            if t:
                return pl.BlockSpec((nb, s.shape[1], Q), lambda b, k: (b, 0, chunk_of(order(k))))
            return pl.BlockSpec((nb, Q, s.shape[2]), lambda b, k: (b, chunk_of(order(k)), 0))

        seq_specs = [seq_spec(s, t) for s, t in zip(seqs, seq_t)]
        par_specs = [pl.BlockSpec(p.shape, lambda b, k: (0, 0)) for p in params]
        con_specs = [pl.BlockSpec(c.shape, lambda b, k, nd=c.ndim: (0,) * nd) for c in consts]
        out_specs = [pl.BlockSpec((nb, Q, w), lambda b, k: (b, chunk_of(order(k)), 0)) for w in out_widths]
        sav_specs = [pl.BlockSpec((nb, 1) + tuple(s), lambda b, k: (b, order(k), 0, 0)) for s in state_shapes]
        return seq_specs, par_specs, con_specs, out_specs, sav_specs

    def fwd_call(seqs, params):
        def body(*refs):
            seq_refs = refs[:ns]
            par_refs = refs[ns:ns + npar]
            con_refs = refs[ns + npar:ns + npar + nc]
            rest = refs[ns + npar + nc:]
            out_refs, sav_refs, st_refs = rest[:no], rest[no:no + nst], rest[no + nst:]

            @pl.when(pl.program_id(1) == 0)
            def _():
                for s in st_refs:
                    s[...] = jnp.zeros_like(s)

            pvals, cvals = tuple(p[...] for p in par_refs), tuple(c[...] for c in con_refs)
            for i in range(nb):
                s_in = tuple(s[i] for s in st_refs)
                for sv, s in zip(sav_refs, s_in):
                    sv[i, 0] = s
                s_new, outs = step(s_in, tuple(r[i].astype(F32) for r in seq_refs), pvals, cvals)
                for s_ref, s in zip(st_refs, s_new):
                    s_ref[i] = s
                for o_ref, o in zip(out_refs, outs):
                    o_ref[i] = o

        ss, ps, cs, os_, vs = specs(lambda k: k)
        res = pl.pallas_call(
            body, grid=(B // nb, nck), in_specs=ss + ps + cs, out_specs=os_ + vs,
            out_shape=[jax.ShapeDtypeStruct((B, L, w), F32) for w in out_widths]
            + [jax.ShapeDtypeStruct((B, nck) + tuple(s), F32) for s in state_shapes],
            scratch_shapes=[pltpu.VMEM((nb,) + tuple(s), F32) for s in state_shapes],
            name=name + "_fwd", compiler_params=_cp("parallel", "arbitrary"))(*seqs, *params, *consts)
        return tuple(res[:no]), tuple(res[no:])

    def bwd_call(seqs, params, saved, douts):
        def body(*refs):
            seq_refs = refs[:ns]
            par_refs = refs[ns:ns + npar]
            con_refs = refs[ns + npar:ns + npar + nc]
            rest = refs[ns + npar + nc:]
            sav_refs, dout_refs = rest[:nst], rest[nst:nst + no]
            rest = rest[nst + no:]
            dseq_refs, dpar_refs, dst_refs = rest[:ns], rest[ns:ns + npar], rest[ns + npar:]
            b, k = pl.program_id(0), pl.program_id(1)

            @pl.when(k == 0)
            def _():
                for s in dst_refs:
                    s[...] = jnp.zeros_like(s)

            cvals, pvals = tuple(c[...] for c in con_refs), tuple(p[...] for p in par_refs)
            dp = None
            for i in range(nb):
                _, vjp = jax.vjp(lambda s, x, p: step(s, x, p, cvals), tuple(s[i, 0] for s in sav_refs),
                                 tuple(r[i].astype(F32) for r in seq_refs), pvals)
                ds, dx, dp_i = vjp((tuple(s[i] for s in dst_refs), tuple(d[i] for d in dout_refs)))
                for s_ref, s in zip(dst_refs, ds):
                    s_ref[i] = s
                for x_ref, x in zip(dseq_refs, dx):
                    x_ref[i] = x.astype(x_ref.dtype)
                dp = dp_i if dp is None else tuple(a + g for a, g in zip(dp, dp_i))
            first = (b == 0) & (k == 0)
            for ref, g in zip(dpar_refs, dp):
                @pl.when(first)
                def _(ref=ref, g=g):
                    ref[...] = g

                @pl.when(jnp.logical_not(first))
                def _(ref=ref, g=g):
                    ref[...] += g

        ss, ps, cs, os_, vs = specs(lambda k: nck - 1 - k)
        res = pl.pallas_call(
            body, grid=(B // nb, nck), in_specs=ss + ps + cs + vs + os_, out_specs=ss + ps,
            out_shape=[jax.ShapeDtypeStruct(s.shape, s.dtype) for s in seqs]
            + [jax.ShapeDtypeStruct(p.shape, F32) for p in params],
            scratch_shapes=[pltpu.VMEM((nb,) + tuple(s), F32) for s in state_shapes],
            name=name + "_bwd", compiler_params=_cp("arbitrary", "arbitrary"))(*seqs, *params, *consts, *saved, *douts)
        return tuple(res[:ns]), tuple(res[ns:])

    @jax.custom_vjp
    def op(seqs, params):
        return fwd_call(seqs, params)[0]

    def op_fwd(seqs, params):
        outs, saved = fwd_call(seqs, params)
        return outs, (seqs, params, saved)

    def op_bwd(res, douts):
        seqs, params, saved = res
        return bwd_call(seqs, params, saved, tuple(douts))

    op.defvjp(op_fwd, op_bwd)
    return op(seqs, params)


def _positions(Q, rev):
    t = np.arange(Q)
    return (Q - 1 - t) if rev else t


def _ssd_consts(rev):
    Q = SSD_Q
    pos = _positions(Q, rev)
    tri = pos[:, None] >= pos[None, :]
    head_of = np.arange(A_INNER) // (A_INNER // A_HEADS)
    expand = np.arange(A_HEADS)[:, None] == head_of[None, :]
    group_of_row = np.arange(A_GROUPS * A_STATE) // A_STATE
    block = group_of_row[:, None] == (head_of // (A_HEADS // A_GROUPS))[None, :]
    as_bf = lambda m: jnp.asarray(m.astype(np.float32), BF16)
    return (as_bf(tri), as_bf(tri.T), jnp.asarray(tri.astype(np.float32)), as_bf(expand),
            jnp.asarray(block.astype(np.float32)), jnp.asarray(expand.astype(np.float32)[:, None, :]))


def _ssd_step(state, seqs, params, consts):
    (st,) = state
    xs, bm, cm, dtr, dtr_t = seqs
    bias, bias_t, alog, alog_t = params
    tri_b, tri_t_b, tri_f, expand, block, head_lanes = consts
    dt = _softplus(dtr + bias)
    dta = dt * (-jnp.exp(alog))
    dt_t = _softplus(dtr_t + bias_t)
    dta_t = dt_t * (-jnp.exp(alog_t))
    cum = mdot(tri_b, dta)
    cum_t = xdot(dta_t, tri_t_b)
    total = jnp.sum(dta, axis=0, keepdims=True)
    w_end = jnp.exp(total - cum) * dt
    ecum_x = xdot(jnp.exp(cum), expand)
    wend_x = xdot(w_end, expand)
    dec_x = jnp.exp(jnp.sum(xdot(dta, expand), axis=0, keepdims=True))
    y = _bdot(cm, st) * ecum_x
    st_new = st * dec_x + block * _bdot(bm, xs * wend_x, TN)
    lane_g = lax.broadcasted_iota(jnp.int32, (1, A_GROUPS * A_STATE), 1) // A_STATE
    sub_h = lax.broadcasted_iota(jnp.int32, (A_HEADS, 1), 0)
    lane8 = lax.broadcasted_iota(jnp.int32, (1, A_HEADS), 1)
    Q = xs.shape[0]
    scores = []
    for g in range(A_GROUPS):
        cb = _bdot(jnp.where(lane_g == g, cm, 0.0), bm, NT)
        for h in range(g * (A_HEADS // A_GROUPS), (g + 1) * (A_HEADS // A_GROUPS)):
            col = jnp.sum(jnp.where(lane8 == h, cum, 0.0), axis=1, keepdims=True)
            row = jnp.sum(jnp.where(sub_h == h, cum_t, 0.0), axis=0, keepdims=True)
            dt_row = jnp.sum(jnp.where(sub_h == h, dt_t, 0.0), axis=0, keepdims=True)
            scores.append(cb * (tri_f * jnp.exp(tri_f * (col - row))) * dt_row)
    y_all = _bdot(jnp.stack(scores).reshape(A_HEADS * Q, Q), xs).reshape(A_HEADS, Q, A_INNER)
    y = y + jnp.sum(y_all * head_lanes, axis=0)
    return (st_new,), (y,)


def ssd_scan(name, rev, xs, bm, cm, dtr, dt_bias, a_log, n_ctx):
    seqs = (xs, bm, cm, dtr, jnp.swapaxes(dtr, 1, 2))
    params = (dt_bias.reshape(1, -1), dt_bias.reshape(-1, 1), a_log.reshape(1, -1), a_log.reshape(-1, 1))
    (y,) = chunk_scan(_ssd_step, name, rev, seqs, (False, False, False, False, True), params, _ssd_consts(rev),
                      [(A_GROUPS * A_STATE, A_INNER)], [A_INNER], SSD_Q, n_ctx)
    return y


def _recurrence(v, a, order, n_ctx, name, x_fwd=None):
    B, L = v.shape[:2]
    T = REC_T
    nck, ncc = L // T, n_ctx // T
    nlat = nck - ncc
    chunk_of = {"F": lambda k: k,
                "R": lambda k: jnp.where(k < ncc, ncc - 1 - k, nck + ncc - 1 - k),
                "FT": lambda k: nck - 1 - k,
                "RT": lambda k: jnp.where(k < nlat, ncc + k, k - nlat)}[order]
    descending = order in ("R", "FT")
    with_da = x_fwd is not None
    H = 8

    def body(*refs):
        if with_da:
            v_ref, a_ref, xf_ref, x_ref, da_ref, st_ref = refs
        else:
            v_ref, a_ref, x_ref, st_ref = refs
        k = pl.program_id(0)

        @pl.when(k == 0)
        def _():
            st_ref[...] = jnp.zeros_like(st_ref)
            if with_da:
                da_ref[...] = jnp.zeros_like(da_ref)

        ar, ai = a_ref[pl.ds(0, H), :], a_ref[pl.ds(H, H), :]
        zero = jnp.zeros((H, 128), F32)

        def step(i, carry):
            chains, acc_r, acc_i = carry
            t = (T - 1 - i) if descending else i
            out = []
            for b, (xr, xi) in enumerate(chains):
                if with_da:
                    f = xf_ref[b, t].astype(F32)
                    fr, fi = f[:H], f[H:]
                    acc_r = acc_r + xr * fr + xi * fi
                    acc_i = acc_i + xi * fr - xr * fi
                vt = v_ref[b, t].astype(F32)
                nr = ar * xr - ai * xi + vt[:H]
                ni = ar * xi + ai * xr + vt[H:]
                x_ref[b, t] = jnp.concatenate([nr, ni], axis=0).astype(x_ref.dtype)
                out.append((nr, ni))
            return tuple(out), acc_r, acc_i

        init = tuple((st_ref[b, pl.ds(0, H), :], st_ref[b, pl.ds(H, H), :]) for b in range(B))
        chains, acc_r, acc_i = lax.fori_loop(0, T, step, (init, zero, zero), unroll=8)
        for b, (xr, xi) in enumerate(chains):
            st_ref[b, pl.ds(0, H), :] = xr
            st_ref[b, pl.ds(H, H), :] = xi
        if with_da:
            da_ref[pl.ds(0, H), :] += acc_r
            da_ref[pl.ds(H, H), :] += acc_i

    seq = pl.BlockSpec((B, T, 2 * H, 128), lambda k: (0, chunk_of(k), 0, 0))
    par = pl.BlockSpec((2 * H, 128), lambda k: (0, 0))
    x_shape = jax.ShapeDtypeStruct(v.shape, v.dtype)
    if with_da:
        return pl.pallas_call(
            body, grid=(nck,), in_specs=[seq, par, seq], out_specs=[seq, par],
            out_shape=[x_shape, jax.ShapeDtypeStruct((2 * H, 128), F32)],
            scratch_shapes=[pltpu.VMEM((B, 2 * H, 128), F32)], name=name, compiler_params=_cp("arbitrary"))(v, a, x_fwd)
    return pl.pallas_call(
        body, grid=(nck,), in_specs=[seq, par], out_specs=seq, out_shape=x_shape,
        scratch_shapes=[pltpu.VMEM((B, 2 * H, 128), F32)], name=name, compiler_params=_cp("arbitrary"))(v, a)


def lin_rec(v, a, rev, n_ctx, name):
    fwd_order, bwd_order = ("R", "RT") if rev else ("F", "FT")
    conj = jnp.concatenate([jnp.ones((8, 128), F32), -jnp.ones((8, 128), F32)], axis=0)

    @jax.custom_vjp
    def f(v, a):
        return _recurrence(v, a, fwd_order, n_ctx, name + "_fwd")

    def fwd(v, a):
        x = f(v, a)
        return x, (x, a)

    def bwd(res, dx):
        x, a = res
        g, da = _recurrence(dx, a * conj, bwd_order, n_ctx, name + "_bwd", x_fwd=x)
        return g, da

    f.defvjp(fwd, bwd)
    return f(v, a)


def _block_diag(t):
    G, a, b = t.shape
    eye = jnp.eye(G, dtype=t.dtype)
    return (t[:, :, None, :] * eye[:, None, :, None]).reshape(G * a, G * b)


def s5_scan(name, rev, u, lam_re, lam_im, log_step, b_re, b_im, c_re, c_im, n_ctx):
    step = jnp.exp(log_step)[:, None]
    mag = jnp.exp(lam_re * step)
    ar = mag * jnp.cos(lam_im * step)
    ai = mag * jnp.sin(lam_im * step)
    den = lam_re * lam_re + lam_im * lam_im
    nr = ar - 1.0
    kr = (nr * lam_re + ai * lam_im) / den
    ki = (ai * lam_re - nr * lam_im) / den
    br = kr[..., None] * b_re - ki[..., None] * b_im
    bi = kr[..., None] * b_im + ki[..., None] * b_re
    B, L, _ = u.shape
    width = B_NGROUPS * B_STATE
    w_b = jnp.concatenate([_block_diag(jnp.swapaxes(br, 1, 2)), _block_diag(jnp.swapaxes(bi, 1, 2))], axis=1)
    w_c = jnp.concatenate([_block_diag(jnp.swapaxes(c_re, 1, 2)), -_block_diag(jnp.swapaxes(c_im, 1, 2))], axis=0)
    a = jnp.concatenate([ar.reshape(8, 128), ai.reshape(8, 128)], axis=0)
    v = linear(u.reshape(B * L, B_WIDTH), w_b.astype(BF16), w_b, name + "_in", BF16)
    x = lin_rec(v.reshape(B, L, 16, 128), a, rev, n_ctx, name)
    y = linear(x.reshape(B * L, 2 * width), w_c.astype(BF16), w_c, name + "_out")
    return y.reshape(B, L, B_WIDTH)


def _hg_consts(rev):
    Q = HG_Q
    pos = _positions(Q, rev)
    pi, pj = pos[:, None], pos[None, :]
    mats = [pi >= pj, pj > pi]
    pairs = []
    s = HG_BAND
    while s < Q:
        second = (pos // s) % 2 == 1
        mid = (pos // (2 * s)) * 2 * s + s
        mats.append(second[:, None] & (pj >= mid[:, None]) & (pj <= pi))
        mats.append((~second)[:, None] & (pj > pi) & (pj < mid[:, None]))
        pairs.append(second[:, None] & (~second)[None, :] & ((pi // (2 * s)) == (pj // (2 * s))))
        s *= 2
    rows = [((pos // s_) % 2 == 1) for s_ in [HG_BAND * 2 ** n for n in range(len(pairs))]]
    band = [(pos % HG_BAND) >= d for d in range(HG_BAND)]
    mats_b = jnp.asarray(np.stack(mats).astype(np.float32), BF16)
    pairs_f = jnp.asarray(np.tile(np.stack(pairs).astype(np.float32), (1, C_HEADS, 1)))
    rows_f = jnp.asarray(np.stack(rows).astype(np.float32)[:, :, None])
    band_f = jnp.asarray(np.stack(band).astype(np.float32)[:, :, None])
    head_lanes = np.arange(C_HEADS)[:, None] == (np.arange(C_WIDTH) // C_KEY)[None, :]
    return mats_b, pairs_f, rows_f, band_f, jnp.asarray(head_lanes.astype(np.float32)[:, None, :])


def _hg_step_fn(rev, n_levels):
    def step(state, seqs, params, consts):
        (st,) = state
        qr, zf, v = seqs
        (lb,) = params
        mats, pairs, rows, band, head_lanes = consts
        W = C_WIDTH
        Q = qr.shape[0]
        same_head = (lax.broadcasted_iota(jnp.int32, (W, W), 0) // C_KEY
                     == lax.broadcasted_iota(jnp.int32, (W, W), 1) // C_KEY)
        q = _silu(qr)
        f = lb + (1.0 - lb) * jax.nn.sigmoid(zf)
        logf = jnp.log(f)
        kk = 1.0 - f
        qd = q * jnp.exp(mdot(mats[0], logf))
        w = kk * jnp.exp(mdot(mats[1], logf))
        total = jnp.sum(logf, axis=0, keepdims=True)
        o = _bdot(qd, st, NT)
        st_new = st * jnp.exp(total) + jnp.where(same_head, _bdot(v, w, TN), 0.0)
        scores = None
        for n in range(n_levels):
            a = q * jnp.exp(mdot(mats[2 + 2 * n], logf)) * rows[n]
            bk = kk * jnp.exp(mdot(mats[3 + 2 * n], logf)) * (1.0 - rows[n])
            sc = pairs[n] * _bdot((a[None] * head_lanes).reshape(C_HEADS * Q, W), bk, NT)
            scores = sc if scores is None else scores + sc
        o = o + jnp.sum(_bdot(scores, v).reshape(C_HEADS, Q, W) * head_lanes, axis=0)
        same_head_b = same_head.astype(BF16)
        e = jnp.zeros_like(logf)
        for d in range(HG_BAND):
            s = -d if rev else d
            if d > 0:
                e = e + sroll(logf, s + (1 if rev else -1))
            kd, vd = (kk, v) if d == 0 else (sroll(kk, s), sroll(v, s))
            p = q * kd * jnp.exp(e)
            o = o + band[d] * (_bdot(p, same_head_b) * vd)
        return (st_new,), (o,)

    return step


def hg_scan(name, rev, qr, zf, v, lower, n_ctx):
    consts = _hg_consts(rev)
    (o,) = chunk_scan(_hg_step_fn(rev, consts[1].shape[0]), name, rev, (qr, zf, v), (False,) * 3,
                      (lower.reshape(1, -1),), consts, [(C_WIDTH, C_WIDTH)], [C_WIDTH], HG_Q, n_ctx)
    return o


def conv_silu(x, w, b, n_ctx, name):
    B, L, C = x.shape
    TC = 128
    w8 = jnp.zeros((8, C), F32).at[:A_CONV].set(w)
    b2 = b.reshape(1, C)
    pad = A_CONV // 2

    def taps(v, sign):
        t = lax.broadcasted_iota(jnp.int32, v.shape, 0)
        out = []
        for k in range(A_CONV):
            s = sign * (k - pad)
            src = t + s
            ok = (src >= 0) & (src < L) & ((t >= n_ctx) == (src >= n_ctx))
            vs = v if s == 0 else pltpu.roll(v, (-s) % L, 0)
            out.append((k, jnp.where(ok, vs, 0.0)))
        return out

    def pre(x_ref, w_ref, b_ref):
        xv = x_ref[0].astype(F32)
        y = b_ref[...] + sum(xs * w_ref[pl.ds(k, 1), :] for k, xs in taps(xv, 1))
        return xv, y

    def fwd_body(x_ref, w_ref, b_ref, o_ref):
        _, y = pre(x_ref, w_ref, b_ref)
        o_ref[0] = _silu(y)

    def bwd_body(x_ref, w_ref, b_ref, g_ref, dx_ref, dw_ref, db_ref):
        xv, y = pre(x_ref, w_ref, b_ref)
        sg = _sigmoid(y)
        dy = g_ref[0] * (sg + y * sg * (1.0 - sg))
        dx_ref[0] = sum(ds * w_ref[pl.ds(k, 1), :] for k, ds in taps(dy, -1)).astype(dx_ref.dtype)
        first = pl.program_id(1) == 0

        @pl.when(first)
        def _():
            dw_ref[...] = jnp.zeros_like(dw_ref)
            db_ref[...] = jnp.zeros_like(db_ref)

        for k, xs in taps(xv, 1):
            dw_ref[pl.ds(k, 1), :] += jnp.sum(dy * xs, axis=0, keepdims=True)
        db_ref[...] += jnp.sum(dy, axis=0, keepdims=True)

    x_spec = pl.BlockSpec((1, L, TC), lambda j, bb: (bb, 0, j))
    w_spec = pl.BlockSpec((8, TC), lambda j, bb: (0, j))
    b_spec = pl.BlockSpec((1, TC), lambda j, bb: (0, j))

    @jax.custom_vjp
    def op(x, w8, b2):
        return pl.pallas_call(fwd_body, grid=(C // TC, B), in_specs=[x_spec, w_spec, b_spec], out_specs=x_spec,
                              out_shape=jax.ShapeDtypeStruct(x.shape, F32), name=name + "_fwd",
                              compiler_params=_cp("parallel", "parallel"))(x, w8, b2)

    def op_fwd(x, w8, b2):
        return op(x, w8, b2), (x, w8, b2)

    def op_bwd(res, g):
        x, w8, b2 = res
        return tuple(pl.pallas_call(
            bwd_body, grid=(C // TC, B), in_specs=[x_spec, w_spec, b_spec, x_spec],
            out_specs=[x_spec, w_spec, b_spec],
            out_shape=[jax.ShapeDtypeStruct(x.shape, x.dtype), jax.ShapeDtypeStruct(w8.shape, F32),
                       jax.ShapeDtypeStruct(b2.shape, F32)],
            name=name + "_bwd", compiler_params=_cp("parallel", "arbitrary"))(x, w8, b2, g))

    op.defvjp(op_fwd, op_bwd)
    return op(x, w8, b2)


def loss_head(h, target, norm_w, n_ctx):
    B, L, Dm = h.shape
    S = target.shape[1]
    tl = ROW_TILE
    skip = n_ctx // tl

    def body(h_ref, t_ref, w_ref, loss_ref, dh_ref, dw_ref):
        b, i = pl.program_id(0), pl.program_id(1)
        fn = lambda hv, wv: _rms(hv) * wv
        y, vjp = jax.vjp(fn, h_ref[0], w_ref[...])
        err = y - t_ref[0]
        dhv, dwv = vjp(err * (1.0 / Dm))
        dh_ref[0] = dhv
        part = 0.5 * jnp.sum(jnp.mean(err * err, axis=-1, keepdims=True), axis=0, keepdims=True)
        first = (b == 0) & (i == 0)

        @pl.when(first)
        def _():
            loss_ref[...] = jnp.zeros_like(loss_ref)
            dw_ref[...] = jnp.zeros_like(dw_ref)

        loss_ref[...] += jnp.broadcast_to(part, loss_ref.shape)
        dw_ref[...] += dwv

    loss, dh, dw = pl.pallas_call(
        body, grid=(B, S // tl),
        in_specs=[pl.BlockSpec((1, tl, Dm), lambda b, i: (b, i + skip, 0)),
                  pl.BlockSpec((1, tl, Dm), lambda b, i: (b, i, 0)),
                  pl.BlockSpec((1, Dm), lambda b, i: (0, 0))],
        out_specs=[pl.BlockSpec((8, 128), lambda b, i: (0, 0)),
                   pl.BlockSpec((1, tl, Dm), lambda b, i: (b, i, 0)),
                   pl.BlockSpec((1, Dm), lambda b, i: (0, 0))],
        out_shape=[jax.ShapeDtypeStruct((8, 128), F32), jax.ShapeDtypeStruct((B, S, Dm), F32),
                   jax.ShapeDtypeStruct((1, Dm), F32)],
        name="loss_head", compiler_params=_cp("arbitrary", "arbitrary"))(h, target, norm_w.reshape(1, Dm))
    dh_full = jnp.concatenate([jnp.zeros((B, n_ctx, Dm), F32), dh], axis=1)
    return loss[0, 0], dh_full, dw.reshape(Dm)


def _modulate(h, shift, scale):
    return _rms(h) * (1.0 + scale) + shift


def _f_mod(r, m, p):
    return (_modulate(r[0], m[0], m[1]),)


def _f_resid_mod(coef):
    def f(r, m, p):
        h2 = r[0] + coef * m[0] * r[1].astype(F32)
        return h2, _modulate(h2, m[1], m[2])
    return f


def _f_resid(coef):
    def f(r, m, p):
        return (r[0] + coef * m[0] * r[1].astype(F32),)
    return f


def _f_swiglu(r, m, p):
    pre = r[0].astype(F32)
    return (_silu(pre[:, :D_FF]) * pre[:, D_FF:],)


def _f_ssd_post(r, m, p):
    y_f, y_b, xs, z = (t.astype(F32) for t in r)
    d_skip, norm_w = p
    y = (y_f + y_b + d_skip * xs) * _silu(z)
    return (_rms(y) * norm_w,)


def _f_s5_post(r, m, p):
    y_f, y_b, u = (t.astype(F32) for t in r)
    d_skip, glu_w, glu_b = p
    y = jax.nn.gelu(y_f + y_b + d_skip * u)
    return (y * _sigmoid(_bdot(y, glu_w) + glu_b),)


def _f_hg_post(r, m, p):
    o_f, o_b, g = (t.astype(F32) for t in r)
    (norm_w,) = p
    W = C_WIDTH
    same_head = (lax.broadcasted_iota(jnp.int32, (W, W), 0) // C_KEY
                 == lax.broadcasted_iota(jnp.int32, (W, W), 1) // C_KEY).astype(BF16)
    o = o_f + o_b
    ms = xdot(o * o, same_head) * (1.0 / C_KEY)
    return (o * lax.rsqrt(ms + EPS) * norm_w * _silu(g),)


def swiglu_out(pre, w16, wslot, name):
    B, L, _ = pre.shape
    Dm = w16.shape[1]
    grad_dtype = wslot.dtype
    tl = ROW_TILE

    def act_of(pre):
        return rowwise(_f_swiglu, name + "_act", [pre], [], [], [(D_FF, BF16)], 0)[0]

    def out_of(act):
        return _mm(act.reshape(B * L, D_FF), w16, False, False, BF16, name + "_fwd").reshape(B, L, Dm)

    def dpre_body(pre_ref, dy_ref, w_ref, o_ref):
        dact = _bdot(dy_ref[0], w_ref[...], NT)
        p = pre_ref[0].astype(F32)
        g, u = p[:, :D_FF], p[:, D_FF:]
        sg = _sigmoid(g)
        s = g * sg
        o_ref[0] = jnp.concatenate([dact * u * (sg + s * (1.0 - sg)), dact * s], axis=1).astype(o_ref.dtype)

    @jax.custom_vjp
    def f(pre, w16, wslot):
        return out_of(act_of(pre))

    def fwd(pre, w16, wslot):
        act = act_of(pre)
        return out_of(act), (pre, act, w16)

    def bwd(res, dy):
        pre, act, w16 = res
        dpre = pl.pallas_call(
            dpre_body, grid=(B, L // tl),
            in_specs=[pl.BlockSpec((1, tl, 2 * D_FF), lambda b, i: (b, i, 0)),
                      pl.BlockSpec((1, tl, Dm), lambda b, i: (b, i, 0)),
                      pl.BlockSpec(w16.shape, lambda b, i: (0, 0))],
            out_specs=pl.BlockSpec((1, tl, 2 * D_FF), lambda b, i: (b, i, 0)),
            out_shape=jax.ShapeDtypeStruct(pre.shape, pre.dtype), name=name + "_dpre",
            compiler_params=_cp("parallel", "parallel"))(pre, dy, w16)
        dw = _mm(act.reshape(B * L, D_FF), dy.reshape(B * L, Dm), True, False, grad_dtype, name + "_dw")
        return dpre, jnp.zeros_like(w16), dw

    f.defvjp(fwd, bwd)
    return f(pre, w16, wslot)


def _pad_w_in(w):
    dt0 = A_INNER + A_CONV_DIM
    zeros = jnp.zeros((w.shape[0], IN_COLS_PAD - IN_COLS), w.dtype)
    return jnp.concatenate([w[:, :dt0], w[:, A_COLS:], w[:, dt0:A_COLS], zeros], axis=1)


def _to_columns(t, n_ctx):
    B, L, W = t.shape
    rows = (L - n_ctx) // GRID_W
    lat = t[:, n_ctx:].reshape(B, rows, GRID_W, W).transpose(0, 2, 1, 3).reshape(B, L - n_ctx, W)
    return jnp.concatenate([t[:, :n_ctx], lat], axis=1)


def _to_raster(t, n_ctx):
    B, L, W = t.shape
    rows = (L - n_ctx) // GRID_W
    lat = t[:, n_ctx:].reshape(B, GRID_W, rows, W).transpose(0, 2, 1, 3).reshape(B, L - n_ctx, W)
    return jnp.concatenate([t[:, :n_ctx], lat], axis=1)


def _forward(h0, c, big, P, n_ctx):
    B, L, Dm = h0.shape
    depth = len(big["mod_w"])
    ncb = n_ctx // ROW_TILE

    def rows_of(t):
        return t.reshape(t.shape[0] * t.shape[1], t.shape[2])

    def w_in_of(t):
        return _pad_w_in(jnp.moveaxis(t, 0, 1).reshape(t.shape[1], IN_COLS))

    p_lb = jax.nn.softmax(P["hg_lb_logits"], axis=0)
    lower_bounds = jnp.cumsum(p_lb, axis=0) - p_lb[:1]
    cc = jnp.concatenate([c, P["c_ctx"][None], jnp.zeros((8 - B - 1, Dm), F32)], axis=0)
    cc = _silu(cc)

    def mods_of(l):
        m = linear(cc, big["mod_w"][l], P["mod_w"][l], f"mod{l}", pieces=True) + P["mod_b"][l]
        m = m.reshape(8, N_MOD, Dm)
        seg = jnp.stack([jnp.broadcast_to(m[B], (B, N_MOD, Dm)), m[:B]], axis=1)
        return [seg[:, :, j:j + 1, :] for j in range(N_MOD)]

    def ffn(u, l, j):
        pre = linear(u.reshape(B * L, Dm), big["ffn_w_in"][l][j], P["ffn_w_in"][l][j], f"ffn_in{l}{j}", BF16,
                     pieces=True)
        return swiglu_out(pre.reshape(B, L, 2 * D_FF), rows_of(big["ffn_w_out"][l][j]),
                          rows_of(P["ffn_w_out"][l][j]), f"ffn_out{l}{j}")

    h = h0
    mods = mods_of(0)
    (u,) = rowwise(_f_mod, "mod_first", [h], [mods[0], mods[1]], [], [(Dm, BF16)], ncb)
    for l in range(depth):
        col_major = l % 2 == 1
        o = ffn(u, l, 0)
        h, u = rowwise(_f_resid_mod(0.5), f"resid_a{l}", [h, o], [mods[2], mods[3], mods[4]], [],
                       [(Dm, F32), (Dm, BF16)], ncb)
        if col_major:
            u = _to_columns(u, n_ctx)
        pre = linear(u.reshape(B * L, Dm), w_in_of(big["w_in"][l]), w_in_of(P["w_in"][l]), f"w_in{l}", BF16)
        pre = pre.reshape(B, L, IN_COLS_PAD)
        o0 = 0
        z, o0 = pre[..., o0:o0 + A_INNER], o0 + A_INNER
        xbc, o0 = pre[..., o0:o0 + A_CONV_DIM], o0 + A_CONV_DIM
        pb, o0 = pre[..., o0:o0 + B_WIDTH], o0 + B_WIDTH
        pc, o0 = pre[..., o0:o0 + 5 * C_WIDTH], o0 + 5 * C_WIDTH
        dtr = pre[..., o0:o0 + 2 * A_HEADS]
        xbc = conv_silu(xbc, P["a_conv_w"][l], P["a_conv_b"][l], n_ctx, f"conv{l}")
        xs, bm, cm = xbc[..., :A_INNER], xbc[..., A_INNER:A_INNER + 128], xbc[..., A_INNER + 128:]
        ya_dir = [ssd_scan(f"ssd{l}{d}", bool(d), xs, bm, cm, dtr[..., d * A_HEADS:(d + 1) * A_HEADS],
                           P["a_dt_bias"][l, d], P["a_log"][l, d], n_ctx) for d in range(2)]
        d_skip = jnp.repeat(P["a_d"][l], A_INNER // A_HEADS).reshape(1, A_INNER)
        (ya,) = rowwise(_f_ssd_post, f"ssd_post{l}", [ya_dir[0], ya_dir[1], xs, z], [],
                        [d_skip, P["a_norm_w"][l].reshape(1, -1)], [(A_INNER, BF16)], ncb)
        yb_dir = [s5_scan(f"s5{l}{d}", bool(d), pb, P["s5_lam_re"][l, d], P["s5_lam_im"][l, d],
                          P["s5_log_step"][l, d], P["s5_b_re"][l, d], P["s5_b_im"][l, d], P["s5_c_re"][l, d],
                          P["s5_c_im"][l, d], n_ctx) for d in range(2)]
        (yb,) = rowwise(_f_s5_post, f"s5_post{l}", [yb_dir[0], yb_dir[1], pb], [],
                        [P["s5_d"][l].reshape(1, -1), P["s5_glu_w"][l], P["s5_glu_b"][l].reshape(1, -1)],
                        [(B_WIDTH, BF16)], ncb)
        qr, f_raw, vi, gate = (pc[..., :C_WIDTH], pc[..., C_WIDTH:3 * C_WIDTH], pc[..., 3 * C_WIDTH:4 * C_WIDTH],
                               pc[..., 4 * C_WIDTH:])
        yc_dir = [hg_scan(f"hg{l}{d}", bool(d), qr, f_raw[..., d * C_WIDTH:(d + 1) * C_WIDTH], vi,
                          lower_bounds[l, d], n_ctx) for d in range(2)]
        (yc,) = rowwise(_f_hg_post, f"hg_post{l}", [yc_dir[0], yc_dir[1], gate], [],
                        [P["hg_norm_w"][l].reshape(1, -1)], [(C_WIDTH, BF16)], ncb)
        mix = jnp.concatenate([ya, yb, yc], axis=-1)
        y = linear(mix.reshape(B * L, Dm), rows_of(big["w_out"][l]), rows_of(P["w_out"][l]), f"w_out{l}", BF16)
        y = y.reshape(B, L, Dm)
        if col_major:
            y = _to_raster(y, n_ctx)
        h, u = rowwise(_f_resid_mod(1.0), f"resid_b{l}", [h, y], [mods[5], mods[6], mods[7]], [],
                       [(Dm, F32), (Dm, BF16)], ncb)
        o = ffn(u, l, 1)
        gate8 = mods[8]
        if l + 1 < depth:
            mods = mods_of(l + 1)
            h, u = rowwise(_f_resid_mod(0.5), f"resid_c{l}", [h, o], [gate8, mods[0], mods[1]], [],
                           [(Dm, F32), (Dm, BF16)], ncb)
        else:
            (h,) = rowwise(_f_resid(0.5), f"resid_c{l}", [h, o], [gate8], [], [(Dm, F32)], ncb)
    return h


BIG = ("mod_w", "ffn_w_in", "ffn_w_out", "w_in", "w_out")


def local_step(x, c, ctx, target, big, small):
    n_ctx = ctx.shape[1]
    h0 = jnp.concatenate([ctx, x], axis=1)
    P = dict(small)
    for k in BIG:
        P[k] = jax.tree.map(lambda t: jnp.zeros(t.shape, BF16), big[k])
    h, vjp = jax.vjp(lambda h0, P: _forward(h0, c, big, P, n_ctx), h0, P)
    loss, dh, d_final = loss_head(h, target, small["final_norm_w"], n_ctx)
    dh0, grads = vjp(dh)
    grads = dict(grads)
    grads["final_norm_w"] = grads["final_norm_w"] + d_final
    return loss, dh0[:, n_ctx:], grads


MESH = pl.DeviceIdType.MESH
LANES = 1024
N_CHIPS = 4
ELEMENTWISE_BLOCK_BYTES = 2 * 1024 * 1024
ANY = pl.BlockSpec(memory_space=pl.ANY)


def _place():
    x, y, c = lax.axis_index("x"), lax.axis_index("y"), lax.axis_index("c")
    return x, y, c, 2 * x + y


def _other_chips(x, y):
    return [(x ^ kx, y ^ ky, 2 * (x ^ kx) + (y ^ ky)) for kx, ky in ((0, 1), (1, 0), (1, 1))]


def all_gather_shards(shard, name):
    def body(x_ref, out_ref, send_sems, recv_sems, local_sem):
        x, y, c, q = _place()
        sibling = (x, y, 1 - c)
        chips = _other_chips(x, y)

        def copy(k, src, dst, to):
            return pltpu.make_async_remote_copy(src_ref=src, dst_ref=dst, send_sem=send_sems.at[k],
                                                recv_sem=recv_sems.at[k], device_id=to, device_id_type=MESH)

        mine = pltpu.make_async_copy(x_ref, out_ref.at[q], local_sem)
        mine.start()
        first = [copy(k, x_ref.at[c], out_ref.at[q, c], (px, py, c)) for k, (px, py, _) in enumerate(chips)]
        for cp in first:
            cp.start()
        passed = [copy(3 + k, out_ref.at[pq, c], out_ref.at[pq, c], sibling) for k, (_, _, pq) in enumerate(chips)]
        for k, (_, _, pq) in enumerate(chips):
            copy(k, x_ref.at[c], out_ref.at[pq, c], sibling).wait_recv()
            passed[k].start()
        for k, (_, _, pq) in enumerate(chips):
            copy(3 + k, x_ref.at[c], out_ref.at[pq, 1 - c], sibling).wait_recv()
        for cp in first + passed:
            cp.wait_send()
        mine.wait()

    return pl.pallas_call(
        body, out_shape=jax.ShapeDtypeStruct((N_CHIPS,) + shard.shape, shard.dtype), in_specs=[ANY], out_specs=ANY,
        scratch_shapes=[pltpu.SemaphoreType.DMA((6,)), pltpu.SemaphoreType.DMA((6,)), pltpu.SemaphoreType.DMA],
        name=name)(shard)


def all_gather_devices(part, name):
    def body(x_ref, out_ref, send_sems, recv_sems, local_sem):
        x, y, c, _ = _place()
        me = 4 * x + 2 * y + c
        mine = pltpu.make_async_copy(x_ref, out_ref.at[me], local_sem)
        mine.start()
        copies = []
        for k in range(1, 8):
            px, py, pc = x ^ (k >> 2), y ^ ((k >> 1) & 1), c ^ (k & 1)
            copies.append(pltpu.make_async_remote_copy(
                src_ref=x_ref, dst_ref=out_ref.at[me], send_sem=send_sems.at[k - 1], recv_sem=recv_sems.at[k - 1],
                device_id=(px, py, pc), device_id_type=MESH))
            copies[-1].start()
        for k in range(1, 8):
            peer = 4 * (x ^ (k >> 2)) + 2 * (y ^ ((k >> 1) & 1)) + (c ^ (k & 1))
            pltpu.make_async_remote_copy(
                src_ref=x_ref, dst_ref=out_ref.at[peer], send_sem=send_sems.at[k - 1], recv_sem=recv_sems.at[k - 1],
                device_id=(x, y, c), device_id_type=MESH).wait_recv()
        for cp in copies:
            cp.wait_send()
        mine.wait()

    return pl.pallas_call(
        body, out_shape=jax.ShapeDtypeStruct((8,) + part.shape, part.dtype), in_specs=[ANY], out_specs=ANY,
        scratch_shapes=[pltpu.SemaphoreType.DMA((7,)), pltpu.SemaphoreType.DMA((7,)), pltpu.SemaphoreType.DMA],
        name=name)(part)


def _half(ref, c, axis):
    h = ref.shape[axis] // 2
    return ref.at[(slice(None),) * axis + (pl.ds(pl.multiple_of(c * h, 8), h),)]


def _remote(src, dst, send_sems, recv_sems, s, to):
    return pltpu.make_async_remote_copy(src_ref=src, dst_ref=dst, send_sem=send_sems.at[s], recv_sem=recv_sems.at[s],
                                        device_id=to, device_id_type=MESH)


def gather_pieces(arrays, name):
    n = len(arrays)

    def body(*refs):
        ins, outs, (send_sems, recv_sems) = refs[:n], refs[n:2 * n], refs[2 * n:]
        x, y, c, q = _place()
        sibling = (x, y, 1 - c)
        chips = _other_chips(x, y)
        started = []

        def go(src, dst, s, to):
            started.append(_remote(src, dst, send_sems, recv_sems, s, to))
            started[-1].start()

        for i, (a, o) in enumerate(zip(ins, outs)):
            go(a, o.at[q], 7 * i + 6, sibling)
            for k, (px, py, _) in enumerate(chips):
                go(_half(a, c, 0), _half(o.at[q], c, 0), 7 * i + k, (px, py, c))
        for i, o in enumerate(outs):
            for k, (_, _, pq) in enumerate(chips):
                land = _half(o.at[pq], c, 0)
                _remote(land, land, send_sems, recv_sems, 7 * i + k, sibling).wait_recv()
                go(land, land, 7 * i + 3 + k, sibling)
        for i, (a, o) in enumerate(zip(ins, outs)):
            for k, (_, _, pq) in enumerate(chips):
                land = _half(o.at[pq], 1 - c, 0)
                _remote(land, land, send_sems, recv_sems, 7 * i + 3 + k, sibling).wait_recv()
            _remote(a, o.at[q], send_sems, recv_sems, 7 * i + 6, sibling).wait_recv()
        for cp in started:
            cp.wait_send()

    return pl.pallas_call(
        body, out_shape=[jax.ShapeDtypeStruct((N_CHIPS,) + a.shape, a.dtype) for a in arrays], in_specs=[ANY] * n,
        out_specs=[ANY] * n, scratch_shapes=[pltpu.SemaphoreType.DMA((7 * n,)), pltpu.SemaphoreType.DMA((7 * n,))],
        name=name)(*arrays)


def swap_with_sibling(arrays, name, send_other_half_of_axis=None):
    n = len(arrays)
    ax = send_other_half_of_axis

    def body(*refs):
        ins, outs, (send_sems, recv_sems) = refs[:n], refs[n:2 * n], refs[2 * n:]
        x, y, c, _ = _place()
        copies = [_remote(a if ax is None else _half(a, 1 - c, ax), o, send_sems, recv_sems, i, (x, y, 1 - c))
                  for i, (a, o) in enumerate(zip(ins, outs))]
        for cp in copies:
            cp.start()
        for cp in copies:
            cp.wait()

    def out_of(a):
        shape = list(a.shape)
        if ax is not None:
            shape[ax] //= 2
        return jax.ShapeDtypeStruct(tuple(shape), a.dtype)

    return pl.pallas_call(
        body, out_shape=[out_of(a) for a in arrays], in_specs=[ANY] * n, out_specs=[ANY] * n,
        scratch_shapes=[pltpu.SemaphoreType.DMA((n,)), pltpu.SemaphoreType.DMA((n,))], name=name)(*arrays)


def pieces_to_chips(pairs, name):
    n = len(pairs)

    def body(*refs):
        ins, outs, (send_sems, recv_sems) = refs[:n], refs[n:2 * n], refs[2 * n:]
        x, y, c, q = _place()
        chips = _other_chips(x, y)
        copies = [_remote(a.at[pq], o.at[q], send_sems, recv_sems, 3 * i + k, (px, py, c))
                  for i, (a, o) in enumerate(zip(ins, outs)) for k, (px, py, pq) in enumerate(chips)]
        for cp in copies:
            cp.start()
        for i, (a, o) in enumerate(zip(ins, outs)):
            for k, (_, _, pq) in enumerate(chips):
                _remote(a.at[pq], o.at[pq], send_sems, recv_sems, 3 * i + k, (x, y, c)).wait_recv()
        for cp in copies:
            cp.wait_send()

    return pl.pallas_call(
        body, out_shape=[jax.ShapeDtypeStruct(a.shape, a.dtype) for a in pairs], in_specs=[ANY] * n,
        out_specs=[ANY] * n, scratch_shapes=[pltpu.SemaphoreType.DMA((3 * n,)), pltpu.SemaphoreType.DMA((3 * n,))],
        name=name)(*pairs)


def _rows_block(h, n_cols):
    return _tile(h, 16, max(16, ELEMENTWISE_BLOCK_BYTES // (n_cols * 4)))


def add_pair(core, g, got, name):
    _, K, N = g.shape
    h = K // 2
    th = _rows_block(h, N)
    nb = h // th

    def body(c_ref, g_ref, got_ref, o_ref):
        o_ref[...] = (g_ref[...].astype(F32) + got_ref[...].astype(F32)).astype(o_ref.dtype)

    grid_spec = pltpu.PrefetchScalarGridSpec(
        num_scalar_prefetch=1, grid=(N_CHIPS, nb),
        in_specs=[pl.BlockSpec((None, th, N), lambda p, i, c_ref: (p, c_ref[0] * nb + i, 0)),
                  pl.BlockSpec((None, th, N), lambda p, i, c_ref: (p, i, 0))],
        out_specs=pl.BlockSpec((None, th, N), lambda p, i, c_ref: (p, i, 0)))
    return pl.pallas_call(body, grid_spec=grid_spec, out_shape=jax.ShapeDtypeStruct(got.shape, g.dtype), name=name,
                          compiler_params=_cp("parallel", "parallel"))(core, g, got)


def sum_pieces(chip, pair, got, name):
    _, h, N = pair.shape
    th = _rows_block(h, N)

    def body(q_ref, pair_ref, *rest):
        got_refs, o_ref = rest[:N_CHIPS], rest[N_CHIPS]
        q = q_ref[0]
        for p in range(N_CHIPS):
            def put(val, p=p):
                if p == 0:
                    o_ref[...] = val
                else:
                    o_ref[...] += val

            @pl.when(q == p)
            def _():
                put(pair_ref[...].astype(F32))

            @pl.when(q != p)
            def _(p=p):
                put(got_refs[p][...].astype(F32))

    def got_spec(p):
        return pl.BlockSpec((None, th, N), lambda i, q_ref: (jnp.where(q_ref[0] == p, (p + 1) % N_CHIPS, p), i, 0))

    grid_spec = pltpu.PrefetchScalarGridSpec(
        num_scalar_prefetch=1, grid=(h // th,),
        in_specs=[pl.BlockSpec((None, th, N), lambda i, q_ref: (q_ref[0], i, 0))] + [got_spec(p) for p in range(N_CHIPS)],
        out_specs=pl.BlockSpec((th, N), lambda i, q_ref: (i, 0)))
    return pl.pallas_call(body, grid_spec=grid_spec, out_shape=jax.ShapeDtypeStruct((h, N), F32), name=name,
                          compiler_params=_cp("parallel"))(chip, pair, *([got] * N_CHIPS))


def adamw_halves(core, mine, other, w, m, v, name):
    n, K, N = w.shape
    h = K // 2
    th = _rows_block(h, N)
    nb = h // th

    def body(c_ref, *refs):
        mine_refs, other_refs = refs[:n], refs[n:2 * n]
        w_ref, m_ref, v_ref, g_ref, d_ref, m2_ref, v2_ref = refs[2 * n:]
        a, s, c = pl.program_id(0), pl.program_id(1), c_ref[0]
        for idx in range(n):
            @pl.when((a == idx) & (s == c))
            def _(idx=idx):
                g_ref[...] = mine_refs[idx][...]

            @pl.when((a == idx) & (s != c))
            def _(idx=idx):
                g_ref[...] = other_refs[idx][...]
        _adamw_math(g_ref[...], w_ref, m_ref, v_ref, d_ref, m2_ref, v2_ref)

    def half_spec(idx, is_mine):
        def index(a, s, i, c_ref):
            right_half = (s == c_ref[0]) if is_mine else (s != c_ref[0])
            return (jnp.where((a == idx) & right_half, i, 0), 0)
        return pl.BlockSpec((th, N), index)

    full = pl.BlockSpec((None, th, N), lambda a, s, i, c_ref: (a, s * nb + i, 0))
    grid_spec = pltpu.PrefetchScalarGridSpec(
        num_scalar_prefetch=1, grid=(n, 2, nb),
        in_specs=[half_spec(idx, True) for idx in range(n)] + [half_spec(idx, False) for idx in range(n)] + [full] * 3,
        out_specs=[full] * 4)
    return pl.pallas_call(body, grid_spec=grid_spec, out_shape=[jax.ShapeDtypeStruct((n, K, N), F32)] * 4, name=name,
                          compiler_params=_cp("arbitrary", "arbitrary", "arbitrary"))(core, *mine, *other, w, m, v)


def _adamw_math(gv, w_ref, m_ref, v_ref, d_ref, m2_ref, v2_ref):
    m2 = ADAM_B1 * m_ref[...] + (1.0 - ADAM_B1) * gv
    v2 = ADAM_B2 * v_ref[...] + (1.0 - ADAM_B2) * (gv * gv)
    m_hat = m2 / (1.0 - ADAM_B1 ** ADAM_STEP)
    v_hat = v2 / (1.0 - ADAM_B2 ** ADAM_STEP)
    d_ref[...] = -ADAM_LR * (m_hat / (jnp.sqrt(v_hat) + ADAM_EPS) + ADAM_WD * w_ref[...])
    m2_ref[...] = m2
    v2_ref[...] = v2


def _row_tile(R):
    return _pick(R, (512, 256, 128, 64, 32, 16, 8))


def sum_parts(parts, name):
    P, R, _ = parts.shape
    rt = _row_tile(R)

    def body(p_ref, o_ref):
        acc = p_ref[0]
        for p in range(1, P):
            acc = acc + p_ref[p]
        o_ref[...] = acc

    return pl.pallas_call(
        body, grid=(R // rt,), in_specs=[pl.BlockSpec((P, rt, LANES), lambda i: (0, i, 0))],
        out_specs=pl.BlockSpec((rt, LANES), lambda i: (i, 0)), out_shape=jax.ShapeDtypeStruct((R, LANES), F32),
        name=name, compiler_params=_cp("parallel"))(parts)


def adamw(g, w, m, v, name):
    R = g.shape[0]
    rt = _row_tile(R)

    def body(g_ref, w_ref, m_ref, v_ref, d_ref, m2_ref, v2_ref):
        _adamw_math(g_ref[...], w_ref, m_ref, v_ref, d_ref, m2_ref, v2_ref)

    spec = pl.BlockSpec((rt, LANES), lambda i: (i, 0))
    return pl.pallas_call(
        body, grid=(R // rt,), in_specs=[spec] * 4, out_specs=[spec] * 3,
        out_shape=[jax.ShapeDtypeStruct((R, LANES), F32)] * 3, name=name, compiler_params=_cp("parallel"))(g, w, m, v)


def _pack(arrays, rows_multiple, dtype):
    flat = jnp.concatenate([a.reshape(-1).astype(dtype) for a in arrays])
    n = flat.shape[0]
    per = rows_multiple * LANES
    total = -(-n // per) * per
    return jnp.pad(flat, (0, total - n)).reshape(total // LANES, LANES)


def _unpack(buf, shapes):
    flat = buf.reshape(-1)
    out, off = [], 0
    for s in shapes:
        n = math.prod(s)
        out.append(flat[off:off + n].reshape(s))
        off += n
    return out


SHARDED = (("mod_w", 2), ("ffn_w_in", 3), ("ffn_w_out", 2), ("w_in", 2), ("w_out", 1),
           ("a_conv_w", 2), ("s5_glu_w", 1), ("hg_lb_logits", 2))
WEIGHTS = ("c_ctx", "mod_w", "mod_b", "ffn_w_in", "ffn_w_out", "w_in", "w_out", "a_conv_w", "a_conv_b", "a_dt_bias",
           "a_log", "a_d", "a_norm_w", "s5_lam_re", "s5_lam_im", "s5_log_step", "s5_b_re", "s5_b_im", "s5_c_re",
           "s5_c_im", "s5_d", "s5_glu_w", "s5_glu_b", "hg_lb_logits", "hg_norm_w", "final_norm_w")


def _gather_full(local, names_axes, dtype, rows_multiple, name):
    buf = _pack([local[n] for n, _ in names_axes], 2 * rows_multiple, dtype)
    R = buf.shape[0] // 2
    full = all_gather_shards(buf.reshape(2, R, LANES), name).reshape(N_CHIPS, 2 * R, LANES)
    per_chip = [_unpack(full[q], [local[n].shape for n, _ in names_axes]) for q in range(N_CHIPS)]
    return {n: jnp.concatenate([per_chip[q][j] for q in range(N_CHIPS)], axis=ax)
            for j, (n, ax) in enumerate(names_axes)}


def kernel(x, c, ctx, c_ctx, mod_w, mod_b, ffn_w_in, ffn_w_out, w_in, w_out, a_conv_w, a_conv_b, a_dt_bias, a_log, a_d, a_norm_w, s5_lam_re, s5_lam_im, s5_log_step, s5_b_re, s5_b_im, s5_c_re, s5_c_im, s5_d, s5_glu_w, s5_glu_b, hg_lb_logits, hg_norm_w, final_norm_w, loss_target, m_c_ctx, m_mod_w, m_mod_b, m_ffn_w_in, m_ffn_w_out, m_w_in, m_w_out, m_a_conv_w, m_a_conv_b, m_a_dt_bias, m_a_log, m_a_d, m_a_norm_w, m_s5_lam_re, m_s5_lam_im, m_s5_log_step, m_s5_b_re, m_s5_b_im, m_s5_c_re, m_s5_c_im, m_s5_d, m_s5_glu_w, m_s5_glu_b, m_hg_lb_logits, m_hg_norm_w, m_final_norm_w, v_c_ctx, v_mod_w, v_mod_b, v_ffn_w_in, v_ffn_w_out, v_w_in, v_w_out, v_a_conv_w, v_a_conv_b, v_a_dt_bias, v_a_log, v_a_d, v_a_norm_w, v_s5_lam_re, v_s5_lam_im, v_s5_log_step, v_s5_b_re, v_s5_b_im, v_s5_c_re, v_s5_c_im, v_s5_d, v_s5_glu_w, v_s5_glu_b, v_hg_lb_logits, v_hg_norm_w, v_final_norm_w):
    given = dict(locals())
    w = {n: given[n] for n in WEIGHTS}
    m = {n: given["m_" + n] for n in WEIGHTS}
    v = {n: given["v_" + n] for n in WEIGHTS}
    small_sharded = SHARDED[len(BIG):]
    replicated = [n for n in WEIGHTS if n not in [s for s, _ in SHARDED]]
    core = lax.axis_index("c").astype(jnp.int32).reshape(1)
    chip = (2 * lax.axis_index("x") + lax.axis_index("y")).astype(jnp.int32)

    def leaves(t):
        return list(t.reshape((-1,) + t.shape[-2:]))

    def nest(kind, flat):
        flat = list(flat)
        return [flat[2 * l:2 * l + 2] for l in range(len(flat) // 2)] if w[kind].ndim == 4 else flat

    counts = [len(leaves(w[k])) for k in BIG]
    gathered = gather_pieces([a.astype(BF16) for k in BIG for a in leaves(w[k])], "gather_big")
    big, off = {}, 0
    for k, cnt in zip(BIG, counts):
        big[k], off = nest(k, gathered[off:off + cnt]), off + cnt
    small = {n: w[n] for n in replicated}
    small.update(_gather_full(w, small_sharded, F32, 8, "gather_small"))

    loss, grad_x, grads = local_step(x, c, ctx, loss_target, big, small)

    g_big = [g for k in BIG for g in jax.tree.leaves(grads[k])]
    got = swap_with_sibling(g_big, "rs_sibling", send_other_half_of_axis=1)
    pairs = [add_pair(core, g, t, f"rs_add_pair{i}") for i, (g, t) in enumerate(zip(g_big, got))]
    from_chips = pieces_to_chips(pairs, "rs_chips")
    mine = [sum_pieces(chip.reshape(1), p, t, f"rs_sum_chips{i}") for i, (p, t) in enumerate(zip(pairs, from_chips))]
    other = swap_with_sibling(mine, "rs_halves")
    out, off = {}, 0
    for k, cnt in zip(BIG, counts):
        stack = lambda t: t.reshape((cnt,) + t.shape[-2:])
        res = adamw_halves(core, mine[off:off + cnt], other[off:off + cnt], stack(w[k]), stack(m[k]), stack(v[k]),
                           "adamw_" + k)
        off += cnt
        for kind, a in zip(("grad", "delta", "new_m", "new_v"), res):
            out[kind, k] = a.reshape(w[k].shape)

    names = replicated + [n for n, _ in small_sharded]
    part = _pack([grads[n] for n in names] + [loss.reshape(1)], 8, F32)
    g_all = _unpack(sum_parts(all_gather_devices(part, "gather_parts"), "sum_parts"),
                    [grads[n].shape for n in names] + [(1,)])
    g_of = dict(zip(names + ["loss"], g_all))
    for n, ax in small_sharded:
        g_of[n] = lax.dynamic_slice_in_dim(g_of[n], chip * w[n].shape[ax], w[n].shape[ax], axis=ax)
    rest = lambda d: _pack([d[n] for n in names], 8, F32)
    g_rest = rest(g_of)
    d_rest, m_rest, v_rest = adamw(g_rest, rest(w), rest(m), rest(v), "adamw_rest")
    shapes = [w[n].shape for n in names]
    for kind, buf in (("grad", g_rest), ("delta", d_rest), ("new_m", m_rest), ("new_v", v_rest)):
        for n, a in zip(names, _unpack(buf, shapes)):
            out[kind, n] = a

    loss_total = g_of["loss"].reshape(())
    return (loss_total, grad_x, *[out["grad", n] for n in WEIGHTS], *[out["delta", n] for n in WEIGHTS],
            *[out["new_m", n] for n in WEIGHTS], *[out["new_v", n] for n in WEIGHTS])
```

```python
import functools
import math

import numpy as np
import jax
import jax.numpy as jnp
from jax import lax
from jax.experimental import pallas as pl
from jax.experimental.pallas import tpu as pltpu

F32, BF16 = jnp.float32, jnp.bfloat16
EPS = 1e-6
N_MOD = 9
D_FF = 2816
A_INNER, A_HEADS, A_GROUPS, A_STATE, A_CONV, A_CONV_DIM = 512, 8, 2, 64, 5, 768
A_COLS = A_INNER + A_CONV_DIM + 2 * A_HEADS
B_WIDTH, B_GROUP, B_NGROUPS, B_STATE = 256, 16, 16, 64
C_WIDTH, C_HEADS, C_KEY = 256, 4, 64
GRID_W = 64
IN_COLS = A_COLS + B_WIDTH + 5 * C_WIDTH
IN_COLS_PAD = 3072
ADAM_LR, ADAM_B1, ADAM_B2, ADAM_EPS, ADAM_WD, ADAM_STEP = 0.001, 0.9, 0.999, 1e-08, 0.01, 10

ROW_TILE = 256
SSD_Q, HG_Q = 128, 64
REC_T = 128
SCAN_SAMPLES = 4
ROW_SAMPLES = 2
ROW_SAMPLES_MAX_WIDTH = 1024
HG_BAND = 8
VMEM_LIMIT = 56 * 1024 * 1024
MM_VMEM_BUDGET = 36 * 1024 * 1024

NT = ((1,), (1,))
TN = ((0,), (0,))


def _bdot(a, b, dims=((1,), (0,))):
    return lax.dot_general(a.astype(BF16), b.astype(BF16), (dims, ((), ())), preferred_element_type=F32)


def _split3(x):
    hi = x.astype(BF16)
    r = x - hi.astype(F32)
    mid = r.astype(BF16)
    lo = (r - mid.astype(F32)).astype(BF16)
    return hi, mid, lo


def _mask_lhs(m, x, dims):
    return sum(lax.dot_general(m, p, (dims, ((), ())), preferred_element_type=F32) for p in _split3(x))


def _mask_rhs(x, m, dims):
    return sum(lax.dot_general(p, m, (dims, ((), ())), preferred_element_type=F32) for p in _split3(x))


@jax.custom_vjp
def mdot(m, x):
    return _mask_lhs(m, x, ((1,), (0,)))


def _mdot_fwd(m, x):
    return mdot(m, x), m


def _mdot_bwd(m, g):
    return jnp.zeros_like(m), _mask_lhs(m, g, TN)


mdot.defvjp(_mdot_fwd, _mdot_bwd)


@jax.custom_vjp
def xdot(x, m):
    return _mask_rhs(x, m, ((1,), (0,)))


def _xdot_fwd(x, m):
    return xdot(x, m), m


def _xdot_bwd(m, g):
    return _mask_rhs(g, m, NT), jnp.zeros_like(m)


xdot.defvjp(_xdot_fwd, _xdot_bwd)


@functools.partial(jax.custom_vjp, nondiff_argnums=(1,))
def _roll(x, s):
    return pltpu.roll(x, s, 0)


def _roll_fwd(x, s):
    return _roll(x, s), None


def _roll_bwd(s, _, g):
    return (_roll(g, x_rows(g) - s),)


def x_rows(x):
    return x.shape[0]


_roll.defvjp(_roll_fwd, _roll_bwd)


def sroll(x, s):
    s = s % x.shape[0]
    return x if s == 0 else _roll(x, s)


def _softplus(x):
    return jnp.maximum(x, 0.0) + jnp.log1p(jnp.exp(jnp.minimum(x, -x)))


def _sigmoid(x):
    return 0.5 * (jnp.tanh(0.5 * x) + 1.0)


def _silu(x):
    return x * _sigmoid(x)


def _rms(x):
    return x * lax.rsqrt(jnp.mean(x * x, axis=-1, keepdims=True) + EPS)


def _pick(n, cands):
    for c in cands:
        if n % c == 0:
            return c
    return n


def _cp(*sem):
    return pltpu.CompilerParams(dimension_semantics=sem, vmem_limit_bytes=VMEM_LIMIT)


def _tile(n, unit, cap):
    best = None
    for d in range(unit, min(n, cap) + 1, unit):
        if n % d == 0:
            best = d
    return best if best is not None else n


def _mm(a, b, ta, tb, out_dtype, name, b_pieces=False, out_pieces=False):
    M, K = (a.shape[1], a.shape[0]) if ta else a.shape
    n_div = k_div = None
    if b_pieces and tb:
        N, k_div = b.shape[1], b.shape[2]
    elif b_pieces:
        N, n_div = N_CHIPS * b.shape[2], b.shape[2]
    else:
        N = b.shape[0] if tb else b.shape[1]
    if out_pieces:
        n_div = N // N_CHIPS
    n_div, k_div = n_div or N, k_div or K
    tm = _tile(M, 128 if ta else 8, 1536 if ta else 1024)
    tn = _tile(n_div, 128, 1536)
    a_bytes, b_bytes, o_bytes = a.dtype.itemsize, b.dtype.itemsize, jnp.dtype(out_dtype).itemsize
    cands = {d for d in range(128, min(k_div, 2816) + 1, 128) if k_div % d == 0}
    if k_div <= 2816 or not cands:
        cands.add(k_div)
    for tk in sorted(cands, reverse=True):
        scratch = tm * tn * 4 if (K // tk > 1 and o_bytes != 4) else 0
        if 2 * (tm * tk * a_bytes + tk * tn * b_bytes + tm * tn * o_bytes) + scratch <= MM_VMEM_BUDGET:
            break
    nk = K // tk
    nq_n, nq_k = n_div // tn, k_div // tk
    dims = ((0 if ta else 1,), (1 if tb else 0,))

    def body(a_ref, b_ref, o_ref, *acc):
        part = _bdot(a_ref[...], b_ref[...], dims)
        if nk == 1:
            o_ref[...] = part.astype(o_ref.dtype)
            return
        acc_ref = acc[0] if acc else o_ref
        k = pl.program_id(2)

        @pl.when(k == 0)
        def _():
            acc_ref[...] = part

        @pl.when(k > 0)
        def _():
            acc_ref[...] += part

        if acc:
            @pl.when(k == nk - 1)
            def _():
                o_ref[...] = acc_ref[...].astype(o_ref.dtype)

    a_spec = pl.BlockSpec((tk, tm), lambda i, j, k: (k, i)) if ta else pl.BlockSpec((tm, tk), lambda i, j, k: (i, k))
    if b_pieces and tb:
        b_spec = pl.BlockSpec((None, tn, tk), lambda i, j, k: (k // nq_k, j, k % nq_k))
    elif b_pieces:
        b_spec = pl.BlockSpec((None, tk, tn), lambda i, j, k: (j // nq_n, k, j % nq_n))
    elif tb:
        b_spec = pl.BlockSpec((tn, tk), lambda i, j, k: (j, k))
    else:
        b_spec = pl.BlockSpec((tk, tn), lambda i, j, k: (k, j))
    if out_pieces:
        o_spec = pl.BlockSpec((None, tm, tn), lambda i, j, k: (j // nq_n, i, j % nq_n))
        o_shape = jax.ShapeDtypeStruct((N_CHIPS, M, N // N_CHIPS), out_dtype)
    else:
        o_spec = pl.BlockSpec((tm, tn), lambda i, j, k: (i, j))
        o_shape = jax.ShapeDtypeStruct((M, N), out_dtype)
    return pl.pallas_call(
        body, grid=(M // tm, N // tn, nk), in_specs=[a_spec, b_spec], out_specs=o_spec, out_shape=o_shape,
        scratch_shapes=[pltpu.VMEM((tm, tn), F32)] if (nk > 1 and o_bytes != 4) else [], name=name,
        compiler_params=_cp("parallel", "parallel", "arbitrary"))(a, b)


def linear(x, w16, wslot, name, out_dtype=F32, pieces=False):
    @jax.custom_vjp
    def f(x, w16, wslot):
        return _mm(x, w16, False, False, out_dtype, name + "_fwd", b_pieces=pieces)

    def fwd(x, w16, wslot):
        return f(x, w16, wslot), (x, w16)

    grad_dtype = wslot.dtype

    def bwd(res, dy):
        x, w16 = res
        dx = _mm(dy, w16, False, True, x.dtype, name + "_dx", b_pieces=pieces)
        dw = _mm(x, dy, True, False, grad_dtype, name + "_dw", out_pieces=pieces)
        return dx, jnp.zeros_like(w16), dw

    f.defvjp(fwd, bwd)
    return f(x, w16, wslot)


def rowwise(f, name, rows, mods, params, outs, n_ctx_blocks, tl=ROW_TILE, row_diff=None):
    rows, mods, params = tuple(rows), tuple(mods), tuple(params)
    nr, nm, npar, no = len(rows), len(mods), len(params), len(outs)
    B, L = rows[0].shape[:2]
    nblk = L // tl
    row_diff = tuple(row_diff) if row_diff is not None else (True,) * nr
    out_dtypes = [dt for _, dt in outs]
    widest = max([r.shape[2] for r in rows] + [w for w, _ in outs])
    ns = ROW_SAMPLES if (B % ROW_SAMPLES == 0 and widest <= ROW_SAMPLES_MAX_WIDTH) else 1

    def fcast(r, m, p):
        return tuple(o.astype(dt) for o, dt in zip(f(r, m, p), out_dtypes))

    def seg(i):
        return (i >= n_ctx_blocks).astype(jnp.int32) if n_ctx_blocks else 0

    def specs():
        row_specs = [pl.BlockSpec((ns, tl, r.shape[2]), lambda b, i: (b, i, 0)) for r in rows]
        mod_specs = [pl.BlockSpec((ns, 1, 1, m.shape[3]), lambda b, i: (b, seg(i), 0, 0)) for m in mods]
        par_specs = [pl.BlockSpec(p.shape, lambda b, i: (0, 0)) for p in params]
        out_specs = [pl.BlockSpec((ns, tl, w), lambda b, i: (b, i, 0)) for w, _ in outs]
        return row_specs, mod_specs, par_specs, out_specs

    def load(ins, s):
        r = tuple(x[s] for x in ins[:nr])
        m = tuple(x[s, 0] for x in ins[nr:nr + nm])
        p = tuple(x[...] for x in ins[nr + nm:])
        return r, m, p

    def fwd_call(rows, mods, params):
        def body(*refs):
            for s in range(ns):
                r, m, p = load(refs[:nr + nm + npar], s)
                for o_ref, o in zip(refs[nr + nm + npar:], fcast(r, m, p)):
                    o_ref[s] = o

        rs, ms, ps, os_ = specs()
        return pl.pallas_call(
            body, grid=(B // ns, nblk), in_specs=rs + ms + ps, out_specs=os_,
            out_shape=[jax.ShapeDtypeStruct((B, L, w), dt) for w, dt in outs],
            name=name + "_fwd", compiler_params=_cp("parallel", "parallel"))(*rows, *mods, *params)

    def bwd_call(rows, mods, params, douts):
        didx = [j for j in range(nr) if row_diff[j]]

        def body(*refs):
            n_in = nr + nm + npar
            dins = refs[n_in:n_in + no]
            rest = refs[n_in + no:]
            dr_refs, dm_refs, dp_refs = rest[:len(didx)], rest[len(didx):len(didx) + nm], rest[len(didx) + nm:]
            b, i = pl.program_id(0), pl.program_id(1)
            first_m = (i == 0) | (i == n_ctx_blocks) if n_ctx_blocks else (i == 0)
            first_p = (b == 0) & (i == 0)
            dp = None
            for s in range(ns):
                r, m, p = load(refs[:n_in], s)
                _, vjp = jax.vjp(fcast, r, m, p)
                dr, dm, dp_s = vjp(tuple(d[s] for d in dins))
                dp = dp_s if dp is None else tuple(a + g for a, g in zip(dp, dp_s))
                for ref, j in zip(dr_refs, didx):
                    ref[s] = dr[j].astype(ref.dtype)
                for ref, g in zip(dm_refs, dm):
                    @pl.when(first_m)
                    def _(ref=ref, g=g, s=s):
                        ref[s, 0] = g

                    @pl.when(jnp.logical_not(first_m))
                    def _(ref=ref, g=g, s=s):
                        ref[s, 0] += g
            for ref, g in zip(dp_refs, dp):
                @pl.when(first_p)
                def _(ref=ref, g=g):
                    ref[...] = g

                @pl.when(jnp.logical_not(first_p))
                def _(ref=ref, g=g):
                    ref[...] += g

        rs, ms, ps, os_ = specs()
        out_shape = ([jax.ShapeDtypeStruct(rows[j].shape, rows[j].dtype) for j in didx]
                     + [jax.ShapeDtypeStruct(m.shape, F32) for m in mods]
                     + [jax.ShapeDtypeStruct(p.shape, F32) for p in params])
        res = pl.pallas_call(
            body, grid=(B // ns, nblk), in_specs=rs + ms + ps + os_,
            out_specs=[rs[j] for j in didx] + ms + ps, out_shape=out_shape,
            name=name + "_bwd", compiler_params=_cp("arbitrary", "arbitrary"))(*rows, *mods, *params, *douts)
        dr = [None] * nr
        for j, g in zip(didx, res[:len(didx)]):
            dr[j] = g
        dr = tuple(g if g is not None else jnp.zeros_like(rows[j]) for j, g in enumerate(dr))
        return dr, tuple(res[len(didx):len(didx) + nm]), tuple(res[len(didx) + nm:])

    @jax.custom_vjp
    def op(rows, mods, params):
        return tuple(fwd_call(rows, mods, params))

    def op_fwd(rows, mods, params):
        return op(rows, mods, params), (rows, mods, params)

    def op_bwd(res, douts):
        return bwd_call(*res, tuple(douts))

    op.defvjp(op_fwd, op_bwd)
    return op(rows, mods, params)


def chunk_scan(step, name, rev, seqs, seq_t, params, consts, state_shapes, out_widths, Q, n_ctx):
    seqs, params, consts = tuple(seqs), tuple(params), tuple(consts)
    ns, npar, nc, nst, no = len(seqs), len(params), len(consts), len(state_shapes), len(out_widths)
    B = seqs[0].shape[0]
    L = seqs[0].shape[2] if seq_t[0] else seqs[0].shape[1]
    nck, ncc = L // Q, n_ctx // Q
    nb = SCAN_SAMPLES if B % SCAN_SAMPLES == 0 else 1

    def chunk_of(k):
        if not rev:
            return k
        return jnp.where(k < ncc, ncc - 1 - k, nck + ncc - 1 - k)

    def specs(order):
        def seq_spec(s, t):
            if t:
                return pl.BlockSpec((nb, s.shape[1], Q), lambda b, k: (b, 0, chunk_of(order(k))))
            return pl.BlockSpec((nb, Q, s.shape[2]), lambda b, k: (b, chunk_of(order(k)), 0))

        seq_specs = [seq_spec(s, t) for s, t in zip(seqs, seq_t)]
        par_specs = [pl.BlockSpec(p.shape, lambda b, k: (0, 0)) for p in params]
        con_specs = [pl.BlockSpec(c.shape, lambda b, k, nd=c.ndim: (0,) * nd) for c in consts]
        out_specs = [pl.BlockSpec((nb, Q, w), lambda b, k: (b, chunk_of(order(k)), 0)) for w in out_widths]
        sav_specs = [pl.BlockSpec((nb, 1) + tuple(s), lambda b, k: (b, order(k), 0, 0)) for s in state_shapes]
        return seq_specs, par_specs, con_specs, out_specs, sav_specs

    def fwd_call(seqs, params):
        def body(*refs):
            seq_refs = refs[:ns]
            par_refs = refs[ns:ns + npar]
            con_refs = refs[ns + npar:ns + npar + nc]
            rest = refs[ns + npar + nc:]
            out_refs, sav_refs, st_refs = rest[:no], rest[no:no + nst], rest[no + nst:]

            @pl.when(pl.program_id(1) == 0)
            def _():
                for s in st_refs:
                    s[...] = jnp.zeros_like(s)

            pvals, cvals = tuple(p[...] for p in par_refs), tuple(c[...] for c in con_refs)
            for i in range(nb):
                s_in = tuple(s[i] for s in st_refs)
                for sv, s in zip(sav_refs, s_in):
                    sv[i, 0] = s
                s_new, outs = step(s_in, tuple(r[i].astype(F32) for r in seq_refs), pvals, cvals)
                for s_ref, s in zip(st_refs, s_new):
                    s_ref[i] = s
                for o_ref, o in zip(out_refs, outs):
                    o_ref[i] = o

        ss, ps, cs, os_, vs = specs(lambda k: k)
        res = pl.pallas_call(
            body, grid=(B // nb, nck), in_specs=ss + ps + cs, out_specs=os_ + vs,
            out_shape=[jax.ShapeDtypeStruct((B, L, w), F32) for w in out_widths]
            + [jax.ShapeDtypeStruct((B, nck) + tuple(s), F32) for s in state_shapes],
            scratch_shapes=[pltpu.VMEM((nb,) + tuple(s), F32) for s in state_shapes],
            name=name + "_fwd", compiler_params=_cp("parallel", "arbitrary"))(*seqs, *params, *consts)
        return tuple(res[:no]), tuple(res[no:])

    def bwd_call(seqs, params, saved, douts):
        def body(*refs):
            seq_refs = refs[:ns]
            par_refs = refs[ns:ns + npar]
            con_refs = refs[ns + npar:ns + npar + nc]
            rest = refs[ns + npar + nc:]
            sav_refs, dout_refs = rest[:nst], rest[nst:nst + no]
            rest = rest[nst + no:]
            dseq_refs, dpar_refs, dst_refs = rest[:ns], rest[ns:ns + npar], rest[ns + npar:]
            b, k = pl.program_id(0), pl.program_id(1)

            @pl.when(k == 0)
            def _():
                for s in dst_refs:
                    s[...] = jnp.zeros_like(s)

            cvals, pvals = tuple(c[...] for c in con_refs), tuple(p[...] for p in par_refs)
            dp = None
            for i in range(nb):
                _, vjp = jax.vjp(lambda s, x, p: step(s, x, p, cvals), tuple(s[i, 0] for s in sav_refs),
                                 tuple(r[i].astype(F32) for r in seq_refs), pvals)
                ds, dx, dp_i = vjp((tuple(s[i] for s in dst_refs), tuple(d[i] for d in dout_refs)))
                for s_ref, s in zip(dst_refs, ds):
                    s_ref[i] = s
                for x_ref, x in zip(dseq_refs, dx):
                    x_ref[i] = x.astype(x_ref.dtype)
                dp = dp_i if dp is None else tuple(a + g for a, g in zip(dp, dp_i))
            first = (b == 0) & (k == 0)
            for ref, g in zip(dpar_refs, dp):
                @pl.when(first)
                def _(ref=ref, g=g):
                    ref[...] = g

                @pl.when(jnp.logical_not(first))
                def _(ref=ref, g=g):
                    ref[...] += g

        ss, ps, cs, os_, vs = specs(lambda k: nck - 1 - k)
        res = pl.pallas_call(
            body, grid=(B // nb, nck), in_specs=ss + ps + cs + vs + os_, out_specs=ss + ps,
            out_shape=[jax.ShapeDtypeStruct(s.shape, s.dtype) for s in seqs]
            + [jax.ShapeDtypeStruct(p.shape, F32) for p in params],
            scratch_shapes=[pltpu.VMEM((nb,) + tuple(s), F32) for s in state_shapes],
            name=name + "_bwd", compiler_params=_cp("arbitrary", "arbitrary"))(*seqs, *params, *consts, *saved, *douts)
        return tuple(res[:ns]), tuple(res[ns:])

    @jax.custom_vjp
    def op(seqs, params):
        return fwd_call(seqs, params)[0]

    def op_fwd(seqs, params):
        outs, saved = fwd_call(seqs, params)
        return outs, (seqs, params, saved)

    def op_bwd(res, douts):
        seqs, params, saved = res
        return bwd_call(seqs, params, saved, tuple(douts))

    op.defvjp(op_fwd, op_bwd)
    return op(seqs, params)


def _positions(Q, rev):
    t = np.arange(Q)
    return (Q - 1 - t) if rev else t


def _ssd_consts(rev):
    Q = SSD_Q
    pos = _positions(Q, rev)
    tri = pos[:, None] >= pos[None, :]
    head_of = np.arange(A_INNER) // (A_INNER // A_HEADS)
    expand = np.arange(A_HEADS)[:, None] == head_of[None, :]
    group_of_row = np.arange(A_GROUPS * A_STATE) // A_STATE
    block = group_of_row[:, None] == (head_of // (A_HEADS // A_GROUPS))[None, :]
    as_bf = lambda m: jnp.asarray(m.astype(np.float32), BF16)
    return (as_bf(tri), as_bf(tri.T), jnp.asarray(tri.astype(np.float32)), as_bf(expand),
            jnp.asarray(block.astype(np.float32)), jnp.asarray(expand.astype(np.float32)[:, None, :]))


def _ssd_step(state, seqs, params, consts):
    (st,) = state
    xs, bm, cm, dtr, dtr_t = seqs
    bias, bias_t, alog, alog_t = params
    tri_b, tri_t_b, tri_f, expand, block, head_lanes = consts
    dt = _softplus(dtr + bias)
    dta = dt * (-jnp.exp(alog))
    dt_t = _softplus(dtr_t + bias_t)
    dta_t = dt_t * (-jnp.exp(alog_t))
    cum = mdot(tri_b, dta)
    cum_t = xdot(dta_t, tri_t_b)
    total = jnp.sum(dta, axis=0, keepdims=True)
    w_end = jnp.exp(total - cum) * dt
    ecum_x = xdot(jnp.exp(cum), expand)
    wend_x = xdot(w_end, expand)
    dec_x = jnp.exp(jnp.sum(xdot(dta, expand), axis=0, keepdims=True))
    y = _bdot(cm, st) * ecum_x
    st_new = st * dec_x + block * _bdot(bm, xs * wend_x, TN)
    lane_g = lax.broadcasted_iota(jnp.int32, (1, A_GROUPS * A_STATE), 1) // A_STATE
    sub_h = lax.broadcasted_iota(jnp.int32, (A_HEADS, 1), 0)
    lane8 = lax.broadcasted_iota(jnp.int32, (1, A_HEADS), 1)
    Q = xs.shape[0]
    scores = []
    for g in range(A_GROUPS):
        cb = _bdot(jnp.where(lane_g == g, cm, 0.0), bm, NT)
        for h in range(g * (A_HEADS // A_GROUPS), (g + 1) * (A_HEADS // A_GROUPS)):
            col = jnp.sum(jnp.where(lane8 == h, cum, 0.0), axis=1, keepdims=True)
            row = jnp.sum(jnp.where(sub_h == h, cum_t, 0.0), axis=0, keepdims=True)
            dt_row = jnp.sum(jnp.where(sub_h == h, dt_t, 0.0), axis=0, keepdims=True)
            scores.append(cb * (tri_f * jnp.exp(tri_f * (col - row))) * dt_row)
    y_all = _bdot(jnp.stack(scores).reshape(A_HEADS * Q, Q), xs).reshape(A_HEADS, Q, A_INNER)
    y = y + jnp.sum(y_all * head_lanes, axis=0)
    return (st_new,), (y,)


def ssd_scan(name, rev, xs, bm, cm, dtr, dt_bias, a_log, n_ctx):
    seqs = (xs, bm, cm, dtr, jnp.swapaxes(dtr, 1, 2))
    params = (dt_bias.reshape(1, -1), dt_bias.reshape(-1, 1), a_log.reshape(1, -1), a_log.reshape(-1, 1))
    (y,) = chunk_scan(_ssd_step, name, rev, seqs, (False, False, False, False, True), params, _ssd_consts(rev),
                      [(A_GROUPS * A_STATE, A_INNER)], [A_INNER], SSD_Q, n_ctx)
    return y


def _recurrence(v, a, order, n_ctx, name, x_fwd=None):
    B, L = v.shape[:2]
    T = REC_T
    nck, ncc = L // T, n_ctx // T
    nlat = nck - ncc
    chunk_of = {"F": lambda k: k,
                "R": lambda k: jnp.where(k < ncc, ncc - 1 - k, nck + ncc - 1 - k),
                "FT": lambda k: nck - 1 - k,
                "RT": lambda k: jnp.where(k < nlat, ncc + k, k - nlat)}[order]
    descending = order in ("R", "FT")
    with_da = x_fwd is not None
    H = 8

    def body(*refs):
        if with_da:
            v_ref, a_ref, xf_ref, x_ref, da_ref, st_ref = refs
        else:
            v_ref, a_ref, x_ref, st_ref = refs
        k = pl.program_id(0)

        @pl.when(k == 0)
        def _():
            st_ref[...] = jnp.zeros_like(st_ref)
            if with_da:
                da_ref[...] = jnp.zeros_like(da_ref)

        ar, ai = a_ref[pl.ds(0, H), :], a_ref[pl.ds(H, H), :]
        zero = jnp.zeros((H, 128), F32)

        def step(i, carry):
            chains, acc_r, acc_i = carry
            t = (T - 1 - i) if descending else i
            out = []
            for b, (xr, xi) in enumerate(chains):
                if with_da:
                    f = xf_ref[b, t].astype(F32)
                    fr, fi = f[:H], f[H:]
                    acc_r = acc_r + xr * fr + xi * fi
                    acc_i = acc_i + xi * fr - xr * fi
                vt = v_ref[b, t].astype(F32)
                nr = ar * xr - ai * xi + vt[:H]
                ni = ar * xi + ai * xr + vt[H:]
                x_ref[b, t] = jnp.concatenate([nr, ni], axis=0).astype(x_ref.dtype)
                out.append((nr, ni))
            return tuple(out), acc_r, acc_i

        init = tuple((st_ref[b, pl.ds(0, H), :], st_ref[b, pl.ds(H, H), :]) for b in range(B))
        chains, acc_r, acc_i = lax.fori_loop(0, T, step, (init, zero, zero), unroll=8)
        for b, (xr, xi) in enumerate(chains):
            st_ref[b, pl.ds(0, H), :] = xr
            st_ref[b, pl.ds(H, H), :] = xi
        if with_da:
            da_ref[pl.ds(0, H), :] += acc_r
            da_ref[pl.ds(H, H), :] += acc_i

    seq = pl.BlockSpec((B, T, 2 * H, 128), lambda k: (0, chunk_of(k), 0, 0))
    par = pl.BlockSpec((2 * H, 128), lambda k: (0, 0))
    x_shape = jax.ShapeDtypeStruct(v.shape, v.dtype)
    if with_da:
        return pl.pallas_call(
            body, grid=(nck,), in_specs=[seq, par, seq], out_specs=[seq, par],
            out_shape=[x_shape, jax.ShapeDtypeStruct((2 * H, 128), F32)],
            scratch_shapes=[pltpu.VMEM((B, 2 * H, 128), F32)], name=name, compiler_params=_cp("arbitrary"))(v, a, x_fwd)
    return pl.pallas_call(
        body, grid=(nck,), in_specs=[seq, par], out_specs=seq, out_shape=x_shape,
        scratch_shapes=[pltpu.VMEM((B, 2 * H, 128), F32)], name=name, compiler_params=_cp("arbitrary"))(v, a)


def lin_rec(v, a, rev, n_ctx, name):
    fwd_order, bwd_order = ("R", "RT") if rev else ("F", "FT")
    conj = jnp.concatenate([jnp.ones((8, 128), F32), -jnp.ones((8, 128), F32)], axis=0)

    @jax.custom_vjp
    def f(v, a):
        return _recurrence(v, a, fwd_order, n_ctx, name + "_fwd")

    def fwd(v, a):
        x = f(v, a)
        return x, (x, a)

    def bwd(res, dx):
        x, a = res
        g, da = _recurrence(dx, a * conj, bwd_order, n_ctx, name + "_bwd", x_fwd=x)
        return g, da

    f.defvjp(fwd, bwd)
    return f(v, a)


def _block_diag(t):
    G, a, b = t.shape
    eye = jnp.eye(G, dtype=t.dtype)
    return (t[:, :, None, :] * eye[:, None, :, None]).reshape(G * a, G * b)


def s5_scan(name, rev, u, lam_re, lam_im, log_step, b_re, b_im, c_re, c_im, n_ctx):
    step = jnp.exp(log_step)[:, None]
    mag = jnp.exp(lam_re * step)
    ar = mag * jnp.cos(lam_im * step)
    ai = mag * jnp.sin(lam_im * step)
    den = lam_re * lam_re + lam_im * lam_im
    nr = ar - 1.0
    kr = (nr * lam_re + ai * lam_im) / den
    ki = (ai * lam_re - nr * lam_im) / den
    br = kr[..., None] * b_re - ki[..., None] * b_im
    bi = kr[..., None] * b_im + ki[..., None] * b_re
    B, L, _ = u.shape
    width = B_NGROUPS * B_STATE
    w_b = jnp.concatenate([_block_diag(jnp.swapaxes(br, 1, 2)), _block_diag(jnp.swapaxes(bi, 1, 2))], axis=1)
    w_c = jnp.concatenate([_block_diag(jnp.swapaxes(c_re, 1, 2)), -_block_diag(jnp.swapaxes(c_im, 1, 2))], axis=0)
    a = jnp.concatenate([ar.reshape(8, 128), ai.reshape(8, 128)], axis=0)
    v = linear(u.reshape(B * L, B_WIDTH), w_b.astype(BF16), w_b, name + "_in", BF16)
    x = lin_rec(v.reshape(B, L, 16, 128), a, rev, n_ctx, name)
    y = linear(x.reshape(B * L, 2 * width), w_c.astype(BF16), w_c, name + "_out")
    return y.reshape(B, L, B_WIDTH)


def _hg_consts(rev):
    Q = HG_Q
    pos = _positions(Q, rev)
    pi, pj = pos[:, None], pos[None, :]
    mats = [pi >= pj, pj > pi]
    pairs = []
    s = HG_BAND
    while s < Q:
        second = (pos // s) % 2 == 1
        mid = (pos // (2 * s)) * 2 * s + s
        mats.append(second[:, None] & (pj >= mid[:, None]) & (pj <= pi))
        mats.append((~second)[:, None] & (pj > pi) & (pj < mid[:, None]))
        pairs.append(second[:, None] & (~second)[None, :] & ((pi // (2 * s)) == (pj // (2 * s))))
        s *= 2
    rows = [((pos // s_) % 2 == 1) for s_ in [HG_BAND * 2 ** n for n in range(len(pairs))]]
    band = [(pos % HG_BAND) >= d for d in range(HG_BAND)]
    mats_b = jnp.asarray(np.stack(mats).astype(np.float32), BF16)
    pairs_f = jnp.asarray(np.tile(np.stack(pairs).astype(np.float32), (1, C_HEADS, 1)))
    rows_f = jnp.asarray(np.stack(rows).astype(np.float32)[:, :, None])
    band_f = jnp.asarray(np.stack(band).astype(np.float32)[:, :, None])
    head_lanes = np.arange(C_HEADS)[:, None] == (np.arange(C_WIDTH) // C_KEY)[None, :]
    return mats_b, pairs_f, rows_f, band_f, jnp.asarray(head_lanes.astype(np.float32)[:, None, :])


def _hg_step_fn(rev, n_levels):
    def step(state, seqs, params, consts):
        (st,) = state
        qr, zf, v = seqs
        (lb,) = params
        mats, pairs, rows, band, head_lanes = consts
        W = C_WIDTH
        Q = qr.shape[0]
        same_head = (lax.broadcasted_iota(jnp.int32, (W, W), 0) // C_KEY
                     == lax.broadcasted_iota(jnp.int32, (W, W), 1) // C_KEY)
        q = _silu(qr)
        f = lb + (1.0 - lb) * jax.nn.sigmoid(zf)
        logf = jnp.log(f)
        kk = 1.0 - f
        qd = q * jnp.exp(mdot(mats[0], logf))
        w = kk * jnp.exp(mdot(mats[1], logf))
        total = jnp.sum(logf, axis=0, keepdims=True)
        o = _bdot(qd, st, NT)
        st_new = st * jnp.exp(total) + jnp.where(same_head, _bdot(v, w, TN), 0.0)
        scores = None
        for n in range(n_levels):
            a = q * jnp.exp(mdot(mats[2 + 2 * n], logf)) * rows[n]
            bk = kk * jnp.exp(mdot(mats[3 + 2 * n], logf)) * (1.0 - rows[n])
            sc = pairs[n] * _bdot((a[None] * head_lanes).reshape(C_HEADS * Q, W), bk, NT)
            scores = sc if scores is None else scores + sc
        o = o + jnp.sum(_bdot(scores, v).reshape(C_HEADS, Q, W) * head_lanes, axis=0)
        same_head_b = same_head.astype(BF16)
        e = jnp.zeros_like(logf)
        for d in range(HG_BAND):
            s = -d if rev else d
            if d > 0:
                e = e + sroll(logf, s + (1 if rev else -1))
            kd, vd = (kk, v) if d == 0 else (sroll(kk, s), sroll(v, s))
            p = q * kd * jnp.exp(e)
            o = o + band[d] * (_bdot(p, same_head_b) * vd)
        return (st_new,), (o,)

    return step


def hg_scan(name, rev, qr, zf, v, lower, n_ctx):
    consts = _hg_consts(rev)
    (o,) = chunk_scan(_hg_step_fn(rev, consts[1].shape[0]), name, rev, (qr, zf, v), (False,) * 3,
                      (lower.reshape(1, -1),), consts, [(C_WIDTH, C_WIDTH)], [C_WIDTH], HG_Q, n_ctx)
    return o


def conv_silu(x, w, b, n_ctx, name):
    B, L, C = x.shape
    TC = 128
    w8 = jnp.zeros((8, C), F32).at[:A_CONV].set(w)
    b2 = b.reshape(1, C)
    pad = A_CONV // 2

    def taps(v, sign):
        t = lax.broadcasted_iota(jnp.int32, v.shape, 0)
        out = []
        for k in range(A_CONV):
            s = sign * (k - pad)
            src = t + s
            ok = (src >= 0) & (src < L) & ((t >= n_ctx) == (src >= n_ctx))
            vs = v if s == 0 else pltpu.roll(v, (-s) % L, 0)
            out.append((k, jnp.where(ok, vs, 0.0)))
        return out

    def pre(x_ref, w_ref, b_ref):
        xv = x_ref[0].astype(F32)
        y = b_ref[...] + sum(xs * w_ref[pl.ds(k, 1), :] for k, xs in taps(xv, 1))
        return xv, y

    def fwd_body(x_ref, w_ref, b_ref, o_ref):
        _, y = pre(x_ref, w_ref, b_ref)
        o_ref[0] = _silu(y)

    def bwd_body(x_ref, w_ref, b_ref, g_ref, dx_ref, dw_ref, db_ref):
        xv, y = pre(x_ref, w_ref, b_ref)
        sg = _sigmoid(y)
        dy = g_ref[0] * (sg + y * sg * (1.0 - sg))
        dx_ref[0] = sum(ds * w_ref[pl.ds(k, 1), :] for k, ds in taps(dy, -1)).astype(dx_ref.dtype)
        first = pl.program_id(1) == 0

        @pl.when(first)
        def _():
            dw_ref[...] = jnp.zeros_like(dw_ref)
            db_ref[...] = jnp.zeros_like(db_ref)

        for k, xs in taps(xv, 1):
            dw_ref[pl.ds(k, 1), :] += jnp.sum(dy * xs, axis=0, keepdims=True)
        db_ref[...] += jnp.sum(dy, axis=0, keepdims=True)

    x_spec = pl.BlockSpec((1, L, TC), lambda j, bb: (bb, 0, j))
    w_spec = pl.BlockSpec((8, TC), lambda j, bb: (0, j))
    b_spec = pl.BlockSpec((1, TC), lambda j, bb: (0, j))

    @jax.custom_vjp
    def op(x, w8, b2):
        return pl.pallas_call(fwd_body, grid=(C // TC, B), in_specs=[x_spec, w_spec, b_spec], out_specs=x_spec,
                              out_shape=jax.ShapeDtypeStruct(x.shape, F32), name=name + "_fwd",
                              compiler_params=_cp("parallel", "parallel"))(x, w8, b2)

    def op_fwd(x, w8, b2):
        return op(x, w8, b2), (x, w8, b2)

    def op_bwd(res, g):
        x, w8, b2 = res
        return tuple(pl.pallas_call(
            bwd_body, grid=(C // TC, B), in_specs=[x_spec, w_spec, b_spec, x_spec],
            out_specs=[x_spec, w_spec, b_spec],
            out_shape=[jax.ShapeDtypeStruct(x.shape, x.dtype), jax.ShapeDtypeStruct(w8.shape, F32),
                       jax.ShapeDtypeStruct(b2.shape, F32)],
            name=name + "_bwd", compiler_params=_cp("parallel", "arbitrary"))(x, w8, b2, g))

    op.defvjp(op_fwd, op_bwd)
    return op(x, w8, b2)


def loss_head(h, target, norm_w, n_ctx):
    B, L, Dm = h.shape
    S = target.shape[1]
    tl = ROW_TILE
    skip = n_ctx // tl

    def body(h_ref, t_ref, w_ref, loss_ref, dh_ref, dw_ref):
        b, i = pl.program_id(0), pl.program_id(1)
        fn = lambda hv, wv: _rms(hv) * wv
        y, vjp = jax.vjp(fn, h_ref[0], w_ref[...])
        err = y - t_ref[0]
        dhv, dwv = vjp(err * (1.0 / Dm))
        dh_ref[0] = dhv
        part = 0.5 * jnp.sum(jnp.mean(err * err, axis=-1, keepdims=True), axis=0, keepdims=True)
        first = (b == 0) & (i == 0)

        @pl.when(first)
        def _():
            loss_ref[...] = jnp.zeros_like(loss_ref)
            dw_ref[...] = jnp.zeros_like(dw_ref)

        loss_ref[...] += jnp.broadcast_to(part, loss_ref.shape)
        dw_ref[...] += dwv

    loss, dh, dw = pl.pallas_call(
        body, grid=(B, S // tl),
        in_specs=[pl.BlockSpec((1, tl, Dm), lambda b, i: (b, i + skip, 0)),
                  pl.BlockSpec((1, tl, Dm), lambda b, i: (b, i, 0)),
                  pl.BlockSpec((1, Dm), lambda b, i: (0, 0))],
        out_specs=[pl.BlockSpec((8, 128), lambda b, i: (0, 0)),
                   pl.BlockSpec((1, tl, Dm), lambda b, i: (b, i, 0)),
                   pl.BlockSpec((1, Dm), lambda b, i: (0, 0))],
        out_shape=[jax.ShapeDtypeStruct((8, 128), F32), jax.ShapeDtypeStruct((B, S, Dm), F32),
                   jax.ShapeDtypeStruct((1, Dm), F32)],
        name="loss_head", compiler_params=_cp("arbitrary", "arbitrary"))(h, target, norm_w.reshape(1, Dm))
    dh_full = jnp.concatenate([jnp.zeros((B, n_ctx, Dm), F32), dh], axis=1)
    return loss[0, 0], dh_full, dw.reshape(Dm)


def _modulate(h, shift, scale):
    return _rms(h) * (1.0 + scale) + shift


def _f_mod(r, m, p):
    return (_modulate(r[0], m[0], m[1]),)


def _f_resid_mod(coef):
    def f(r, m, p):
        h2 = r[0] + coef * m[0] * r[1].astype(F32)
        return h2, _modulate(h2, m[1], m[2])
    return f


def _f_resid(coef):
    def f(r, m, p):
        return (r[0] + coef * m[0] * r[1].astype(F32),)
    return f


def _f_swiglu(r, m, p):
    pre = r[0].astype(F32)
    return (_silu(pre[:, :D_FF]) * pre[:, D_FF:],)


def _f_ssd_post(r, m, p):
    y_f, y_b, xs, z = (t.astype(F32) for t in r)
    d_skip, norm_w = p
    y = (y_f + y_b + d_skip * xs) * _silu(z)
    return (_rms(y) * norm_w,)


def _f_s5_post(r, m, p):
    y_f, y_b, u = (t.astype(F32) for t in r)
    d_skip, glu_w, glu_b = p
    y = jax.nn.gelu(y_f + y_b + d_skip * u)
    return (y * _sigmoid(_bdot(y, glu_w) + glu_b),)


def _f_hg_post(r, m, p):
    o_f, o_b, g = (t.astype(F32) for t in r)
    (norm_w,) = p
    W = C_WIDTH
    same_head = (lax.broadcasted_iota(jnp.int32, (W, W), 0) // C_KEY
                 == lax.broadcasted_iota(jnp.int32, (W, W), 1) // C_KEY).astype(BF16)
    o = o_f + o_b
    ms = xdot(o * o, same_head) * (1.0 / C_KEY)
    return (o * lax.rsqrt(ms + EPS) * norm_w * _silu(g),)


def swiglu_out(pre, w16, wslot, name):
    B, L, _ = pre.shape
    Dm = w16.shape[1]
    grad_dtype = wslot.dtype
    tl = ROW_TILE

    row = lambda width: pl.BlockSpec((1, tl, width), lambda b, i: (b, i, 0))
    whole = pl.BlockSpec(w16.shape, lambda b, i: (0, 0))

    def fwd_body(pre_ref, w_ref, o_ref, act_ref):
        p = pre_ref[0].astype(F32)
        act = (_silu(p[:, :D_FF]) * p[:, D_FF:]).astype(act_ref.dtype)
        act_ref[0] = act
        o_ref[0] = _bdot(act, w_ref[...]).astype(o_ref.dtype)

    def forward(pre, w16):
        return pl.pallas_call(
            fwd_body, grid=(B, L // tl), in_specs=[row(2 * D_FF), whole], out_specs=[row(Dm), row(D_FF)],
            out_shape=[jax.ShapeDtypeStruct((B, L, Dm), BF16), jax.ShapeDtypeStruct((B, L, D_FF), BF16)],
            name=name + "_fwd", compiler_params=_cp("parallel", "parallel"))(pre, w16)

    def dpre_body(pre_ref, dy_ref, w_ref, o_ref):
        dact = _bdot(dy_ref[0], w_ref[...], NT)
        p = pre_ref[0].astype(F32)
        g, u = p[:, :D_FF], p[:, D_FF:]
        sg = _sigmoid(g)
        s = g * sg
        o_ref[0] = jnp.concatenate([dact * u * (sg + s * (1.0 - sg)), dact * s], axis=1).astype(o_ref.dtype)

    @jax.custom_vjp
    def f(pre, w16, wslot):
        return forward(pre, w16)[0]

    def fwd(pre, w16, wslot):
        out, act = forward(pre, w16)
        return out, (pre, act, w16)

    def bwd(res, dy):
        pre, act, w16 = res
        dpre = pl.pallas_call(
            dpre_body, grid=(B, L // tl),
            in_specs=[pl.BlockSpec((1, tl, 2 * D_FF), lambda b, i: (b, i, 0)),
                      pl.BlockSpec((1, tl, Dm), lambda b, i: (b, i, 0)),
                      pl.BlockSpec(w16.shape, lambda b, i: (0, 0))],
            out_specs=pl.BlockSpec((1, tl, 2 * D_FF), lambda b, i: (b, i, 0)),
            out_shape=jax.ShapeDtypeStruct(pre.shape, pre.dtype), name=name + "_dpre",
            compiler_params=_cp("parallel", "parallel"))(pre, dy, w16)
        dw = _mm(act.reshape(B * L, D_FF), dy.reshape(B * L, Dm), True, False, grad_dtype, name + "_dw")
        return dpre, jnp.zeros_like(w16), dw

    f.defvjp(fwd, bwd)
    return f(pre, w16, wslot)


def _pad_w_in(w):
    dt0 = A_INNER + A_CONV_DIM
    zeros = jnp.zeros((w.shape[0], IN_COLS_PAD - IN_COLS), w.dtype)
    return jnp.concatenate([w[:, :dt0], w[:, A_COLS:], w[:, dt0:A_COLS], zeros], axis=1)


def _to_columns(t, n_ctx):
    B, L, W = t.shape
    rows = (L - n_ctx) // GRID_W
    lat = t[:, n_ctx:].reshape(B, rows, GRID_W, W).transpose(0, 2, 1, 3).reshape(B, L - n_ctx, W)
    return jnp.concatenate([t[:, :n_ctx], lat], axis=1)


def _to_raster(t, n_ctx):
    B, L, W = t.shape
    rows = (L - n_ctx) // GRID_W
    lat = t[:, n_ctx:].reshape(B, GRID_W, rows, W).transpose(0, 2, 1, 3).reshape(B, L - n_ctx, W)
    return jnp.concatenate([t[:, :n_ctx], lat], axis=1)


def _forward(h0, c, big, P, n_ctx):
    B, L, Dm = h0.shape
    depth = len(big["mod_w"])
    ncb = n_ctx // ROW_TILE

    def rows_of(t):
        return t.reshape(t.shape[0] * t.shape[1], t.shape[2])

    def w_in_of(t):
        return _pad_w_in(jnp.moveaxis(t, 0, 1).reshape(t.shape[1], IN_COLS))

    p_lb = jax.nn.softmax(P["hg_lb_logits"], axis=0)
    lower_bounds = jnp.cumsum(p_lb, axis=0) - p_lb[:1]
    cc = jnp.concatenate([c, P["c_ctx"][None], jnp.zeros((8 - B - 1, Dm), F32)], axis=0)
    cc = _silu(cc)

    def mods_of(l):
        m = linear(cc, big["mod_w"][l], P["mod_w"][l], f"mod{l}", pieces=True) + P["mod_b"][l]
        m = m.reshape(8, N_MOD, Dm)
        seg = jnp.stack([jnp.broadcast_to(m[B], (B, N_MOD, Dm)), m[:B]], axis=1)
        return [seg[:, :, j:j + 1, :] for j in range(N_MOD)]

    def ffn(u, l, j):
        pre = linear(u.reshape(B * L, Dm), big["ffn_w_in"][l][j], P["ffn_w_in"][l][j], f"ffn_in{l}{j}", BF16,
                     pieces=True)
        return swiglu_out(pre.reshape(B, L, 2 * D_FF), rows_of(big["ffn_w_out"][l][j]),
                          rows_of(P["ffn_w_out"][l][j]), f"ffn_out{l}{j}")

    h = h0
    mods = mods_of(0)
    (u,) = rowwise(_f_mod, "mod_first", [h], [mods[0], mods[1]], [], [(Dm, BF16)], ncb)
    for l in range(depth):
        col_major = l % 2 == 1
        o = ffn(u, l, 0)
        h, u = rowwise(_f_resid_mod(0.5), f"resid_a{l}", [h, o], [mods[2], mods[3], mods[4]], [],
                       [(Dm, F32), (Dm, BF16)], ncb)
        if col_major:
            u = _to_columns(u, n_ctx)
        pre = linear(u.reshape(B * L, Dm), w_in_of(big["w_in"][l]), w_in_of(P["w_in"][l]), f"w_in{l}", BF16)
        pre = pre.reshape(B, L, IN_COLS_PAD)
        o0 = 0
        z, o0 = pre[..., o0:o0 + A_INNER], o0 + A_INNER
        xbc, o0 = pre[..., o0:o0 + A_CONV_DIM], o0 + A_CONV_DIM
        pb, o0 = pre[..., o0:o0 + B_WIDTH], o0 + B_WIDTH
        pc, o0 = pre[..., o0:o0 + 5 * C_WIDTH], o0 + 5 * C_WIDTH
        dtr = pre[..., o0:o0 + 2 * A_HEADS]
        xbc = conv_silu(xbc, P["a_conv_w"][l], P["a_conv_b"][l], n_ctx, f"conv{l}")
        xs, bm, cm = xbc[..., :A_INNER], xbc[..., A_INNER:A_INNER + 128], xbc[..., A_INNER + 128:]
        ya_dir = [ssd_scan(f"ssd{l}{d}", bool(d), xs, bm, cm, dtr[..., d * A_HEADS:(d + 1) * A_HEADS],
                           P["a_dt_bias"][l, d], P["a_log"][l, d], n_ctx) for d in range(2)]
        d_skip = jnp.repeat(P["a_d"][l], A_INNER // A_HEADS).reshape(1, A_INNER)
        (ya,) = rowwise(_f_ssd_post, f"ssd_post{l}", [ya_dir[0], ya_dir[1], xs, z], [],
                        [d_skip, P["a_norm_w"][l].reshape(1, -1)], [(A_INNER, BF16)], ncb)
        yb_dir = [s5_scan(f"s5{l}{d}", bool(d), pb, P["s5_lam_re"][l, d], P["s5_lam_im"][l, d],
                          P["s5_log_step"][l, d], P["s5_b_re"][l, d], P["s5_b_im"][l, d], P["s5_c_re"][l, d],
                          P["s5_c_im"][l, d], n_ctx) for d in range(2)]
        (yb,) = rowwise(_f_s5_post, f"s5_post{l}", [yb_dir[0], yb_dir[1], pb], [],
                        [P["s5_d"][l].reshape(1, -1), P["s5_glu_w"][l], P["s5_glu_b"][l].reshape(1, -1)],
                        [(B_WIDTH, BF16)], ncb)
        qr, f_raw, vi, gate = (pc[..., :C_WIDTH], pc[..., C_WIDTH:3 * C_WIDTH], pc[..., 3 * C_WIDTH:4 * C_WIDTH],
                               pc[..., 4 * C_WIDTH:])
        yc_dir = [hg_scan(f"hg{l}{d}", bool(d), qr, f_raw[..., d * C_WIDTH:(d + 1) * C_WIDTH], vi,
                          lower_bounds[l, d], n_ctx) for d in range(2)]
        (yc,) = rowwise(_f_hg_post, f"hg_post{l}", [yc_dir[0], yc_dir[1], gate], [],
                        [P["hg_norm_w"][l].reshape(1, -1)], [(C_WIDTH, BF16)], ncb)
        mix = jnp.concatenate([ya, yb, yc], axis=-1)
        y = linear(mix.reshape(B * L, Dm), rows_of(big["w_out"][l]), rows_of(P["w_out"][l]), f"w_out{l}", BF16)
        y = y.reshape(B, L, Dm)
        if col_major:
            y = _to_raster(y, n_ctx)
        h, u = rowwise(_f_resid_mod(1.0), f"resid_b{l}", [h, y], [mods[5], mods[6], mods[7]], [],
                       [(Dm, F32), (Dm, BF16)], ncb)
        o = ffn(u, l, 1)
        gate8 = mods[8]
        if l + 1 < depth:
            mods = mods_of(l + 1)
            h, u = rowwise(_f_resid_mod(0.5), f"resid_c{l}", [h, o], [gate8, mods[0], mods[1]], [],
                           [(Dm, F32), (Dm, BF16)], ncb)
        else:
            (h,) = rowwise(_f_resid(0.5), f"resid_c{l}", [h, o], [gate8], [], [(Dm, F32)], ncb)
    return h


BIG = ("mod_w", "ffn_w_in", "ffn_w_out", "w_in", "w_out")


def local_step(x, c, ctx, target, big, small):
    n_ctx = ctx.shape[1]
    h0 = jnp.concatenate([ctx, x], axis=1)
    P = dict(small)
    for k in BIG:
        P[k] = jax.tree.map(lambda t: jnp.zeros(t.shape, BF16), big[k])
    h, vjp = jax.vjp(lambda h0, P: _forward(h0, c, big, P, n_ctx), h0, P)
    loss, dh, d_final = loss_head(h, target, small["final_norm_w"], n_ctx)
    dh0, grads = vjp(dh)
    grads = dict(grads)
    grads["final_norm_w"] = grads["final_norm_w"] + d_final
    return loss, dh0[:, n_ctx:], grads


MESH = pl.DeviceIdType.MESH
LANES = 1024
N_CHIPS = 4
ELEMENTWISE_BLOCK_BYTES = 2 * 1024 * 1024
ANY = pl.BlockSpec(memory_space=pl.ANY)


def _place():
    x, y, c = lax.axis_index("x"), lax.axis_index("y"), lax.axis_index("c")
    return x, y, c, 2 * x + y


def _other_chips(x, y):
    return [(x ^ kx, y ^ ky, 2 * (x ^ kx) + (y ^ ky)) for kx, ky in ((0, 1), (1, 0), (1, 1))]


def all_gather_shards(shard, name):
    def body(x_ref, out_ref, send_sems, recv_sems, local_sem):
        x, y, c, q = _place()
        sibling = (x, y, 1 - c)
        chips = _other_chips(x, y)

        def copy(k, src, dst, to):
            return pltpu.make_async_remote_copy(src_ref=src, dst_ref=dst, send_sem=send_sems.at[k],
                                                recv_sem=recv_sems.at[k], device_id=to, device_id_type=MESH)

        mine = pltpu.make_async_copy(x_ref, out_ref.at[q], local_sem)
        mine.start()
        first = [copy(k, x_ref.at[c], out_ref.at[q, c], (px, py, c)) for k, (px, py, _) in enumerate(chips)]
        for cp in first:
            cp.start()
        passed = [copy(3 + k, out_ref.at[pq, c], out_ref.at[pq, c], sibling) for k, (_, _, pq) in enumerate(chips)]
        for k, (_, _, pq) in enumerate(chips):
            copy(k, x_ref.at[c], out_ref.at[pq, c], sibling).wait_recv()
            passed[k].start()
        for k, (_, _, pq) in enumerate(chips):
            copy(3 + k, x_ref.at[c], out_ref.at[pq, 1 - c], sibling).wait_recv()
        for cp in first + passed:
            cp.wait_send()
        mine.wait()

    return pl.pallas_call(
        body, out_shape=jax.ShapeDtypeStruct((N_CHIPS,) + shard.shape, shard.dtype), in_specs=[ANY], out_specs=ANY,
        scratch_shapes=[pltpu.SemaphoreType.DMA((6,)), pltpu.SemaphoreType.DMA((6,)), pltpu.SemaphoreType.DMA],
        name=name)(shard)


def all_gather_devices(part, name):
    def body(x_ref, out_ref, send_sems, recv_sems, local_sem):
        x, y, c, _ = _place()
        me = 4 * x + 2 * y + c
        mine = pltpu.make_async_copy(x_ref, out_ref.at[me], local_sem)
        mine.start()
        copies = []
        for k in range(1, 8):
            px, py, pc = x ^ (k >> 2), y ^ ((k >> 1) & 1), c ^ (k & 1)
            copies.append(pltpu.make_async_remote_copy(
                src_ref=x_ref, dst_ref=out_ref.at[me], send_sem=send_sems.at[k - 1], recv_sem=recv_sems.at[k - 1],
                device_id=(px, py, pc), device_id_type=MESH))
            copies[-1].start()
        for k in range(1, 8):
            peer = 4 * (x ^ (k >> 2)) + 2 * (y ^ ((k >> 1) & 1)) + (c ^ (k & 1))
            pltpu.make_async_remote_copy(
                src_ref=x_ref, dst_ref=out_ref.at[peer], send_sem=send_sems.at[k - 1], recv_sem=recv_sems.at[k - 1],
                device_id=(x, y, c), device_id_type=MESH).wait_recv()
        for cp in copies:
            cp.wait_send()
        mine.wait()

    return pl.pallas_call(
        body, out_shape=jax.ShapeDtypeStruct((8,) + part.shape, part.dtype), in_specs=[ANY], out_specs=ANY,
        scratch_shapes=[pltpu.SemaphoreType.DMA((7,)), pltpu.SemaphoreType.DMA((7,)), pltpu.SemaphoreType.DMA],
        name=name)(part)


def _half(ref, c, axis):
    h = ref.shape[axis] // 2
    return ref.at[(slice(None),) * axis + (pl.ds(pl.multiple_of(c * h, 8), h),)]


def _remote(src, dst, send_sems, recv_sems, s, to):
    return pltpu.make_async_remote_copy(src_ref=src, dst_ref=dst, send_sem=send_sems.at[s], recv_sem=recv_sems.at[s],
                                        device_id=to, device_id_type=MESH)


def gather_pieces(arrays, name):
    n = len(arrays)

    def body(*refs):
        ins, outs, (send_sems, recv_sems) = refs[:n], refs[n:2 * n], refs[2 * n:]
        x, y, c, q = _place()
        sibling = (x, y, 1 - c)
        chips = _other_chips(x, y)
        started = []

        def go(src, dst, s, to):
            started.append(_remote(src, dst, send_sems, recv_sems, s, to))
            started[-1].start()

        for i, (a, o) in enumerate(zip(ins, outs)):
            go(a, o.at[q], 7 * i + 6, sibling)
            for k, (px, py, _) in enumerate(chips):
                go(_half(a, c, 0), _half(o.at[q], c, 0), 7 * i + k, (px, py, c))
        for i, o in enumerate(outs):
            for k, (_, _, pq) in enumerate(chips):
                land = _half(o.at[pq], c, 0)
                _remote(land, land, send_sems, recv_sems, 7 * i + k, sibling).wait_recv()
                go(land, land, 7 * i + 3 + k, sibling)
        for i, (a, o) in enumerate(zip(ins, outs)):
            for k, (_, _, pq) in enumerate(chips):
                land = _half(o.at[pq], 1 - c, 0)
                _remote(land, land, send_sems, recv_sems, 7 * i + 3 + k, sibling).wait_recv()
            _remote(a, o.at[q], send_sems, recv_sems, 7 * i + 6, sibling).wait_recv()
        for cp in started:
            cp.wait_send()

    return pl.pallas_call(
        body, out_shape=[jax.ShapeDtypeStruct((N_CHIPS,) + a.shape, a.dtype) for a in arrays], in_specs=[ANY] * n,
        out_specs=[ANY] * n, scratch_shapes=[pltpu.SemaphoreType.DMA((7 * n,)), pltpu.SemaphoreType.DMA((7 * n,))],
        name=name)(*arrays)


def swap_with_sibling(arrays, name, send_other_half_of_axis=None):
    n = len(arrays)
    ax = send_other_half_of_axis

    def body(*refs):
        ins, outs, (send_sems, recv_sems) = refs[:n], refs[n:2 * n], refs[2 * n:]
        x, y, c, _ = _place()
        copies = [_remote(a if ax is None else _half(a, 1 - c, ax), o, send_sems, recv_sems, i, (x, y, 1 - c))
                  for i, (a, o) in enumerate(zip(ins, outs))]
        for cp in copies:
            cp.start()
        for cp in copies:
            cp.wait()

    def out_of(a):
        shape = list(a.shape)
        if ax is not None:
            shape[ax] //= 2
        return jax.ShapeDtypeStruct(tuple(shape), a.dtype)

    return pl.pallas_call(
        body, out_shape=[out_of(a) for a in arrays], in_specs=[ANY] * n, out_specs=[ANY] * n,
        scratch_shapes=[pltpu.SemaphoreType.DMA((n,)), pltpu.SemaphoreType.DMA((n,))], name=name)(*arrays)


def pieces_to_chips(pairs, name):
    n = len(pairs)

    def body(*refs):
        ins, outs, (send_sems, recv_sems) = refs[:n], refs[n:2 * n], refs[2 * n:]
        x, y, c, q = _place()
        chips = _other_chips(x, y)
        copies = [_remote(a.at[pq], o.at[q], send_sems, recv_sems, 3 * i + k, (px, py, c))
                  for i, (a, o) in enumerate(zip(ins, outs)) for k, (px, py, pq) in enumerate(chips)]
        for cp in copies:
            cp.start()
        for i, (a, o) in enumerate(zip(ins, outs)):
            for k, (_, _, pq) in enumerate(chips):
                _remote(a.at[pq], o.at[pq], send_sems, recv_sems, 3 * i + k, (x, y, c)).wait_recv()
        for cp in copies:
            cp.wait_send()

    return pl.pallas_call(
        body, out_shape=[jax.ShapeDtypeStruct(a.shape, a.dtype) for a in pairs], in_specs=[ANY] * n,
        out_specs=[ANY] * n, scratch_shapes=[pltpu.SemaphoreType.DMA((3 * n,)), pltpu.SemaphoreType.DMA((3 * n,))],
        name=name)(*pairs)


def _rows_block(h, n_cols):
    return _tile(h, 16, max(16, ELEMENTWISE_BLOCK_BYTES // (n_cols * 4)))


def add_pair(core, g, got, name):
    _, K, N = g.shape
    h = K // 2
    th = _rows_block(h, N)
    nb = h // th

    def body(c_ref, g_ref, got_ref, o_ref):
        o_ref[...] = (g_ref[...].astype(F32) + got_ref[...].astype(F32)).astype(o_ref.dtype)

    grid_spec = pltpu.PrefetchScalarGridSpec(
        num_scalar_prefetch=1, grid=(N_CHIPS, nb),
        in_specs=[pl.BlockSpec((None, th, N), lambda p, i, c_ref: (p, c_ref[0] * nb + i, 0)),
                  pl.BlockSpec((None, th, N), lambda p, i, c_ref: (p, i, 0))],
        out_specs=pl.BlockSpec((None, th, N), lambda p, i, c_ref: (p, i, 0)))
    return pl.pallas_call(body, grid_spec=grid_spec, out_shape=jax.ShapeDtypeStruct(got.shape, g.dtype), name=name,
                          compiler_params=_cp("parallel", "parallel"))(core, g, got)


def sum_pieces(chip, pair, got, name):
    _, h, N = pair.shape
    th = _rows_block(h, N)

    def body(q_ref, pair_ref, *rest):
        got_refs, o_ref = rest[:N_CHIPS], rest[N_CHIPS]
        q = q_ref[0]
        for p in range(N_CHIPS):
            def put(val, p=p):
                if p == 0:
                    o_ref[...] = val
                else:
                    o_ref[...] += val

            @pl.when(q == p)
            def _():
                put(pair_ref[...].astype(F32))

            @pl.when(q != p)
            def _(p=p):
                put(got_refs[p][...].astype(F32))

    def got_spec(p):
        return pl.BlockSpec((None, th, N), lambda i, q_ref: (jnp.where(q_ref[0] == p, (p + 1) % N_CHIPS, p), i, 0))

    grid_spec = pltpu.PrefetchScalarGridSpec(
        num_scalar_prefetch=1, grid=(h // th,),
        in_specs=[pl.BlockSpec((None, th, N), lambda i, q_ref: (q_ref[0], i, 0))] + [got_spec(p) for p in range(N_CHIPS)],
        out_specs=pl.BlockSpec((th, N), lambda i, q_ref: (i, 0)))
    return pl.pallas_call(body, grid_spec=grid_spec, out_shape=jax.ShapeDtypeStruct((h, N), F32), name=name,
                          compiler_params=_cp("parallel"))(chip, pair, *([got] * N_CHIPS))


def adamw_halves(core, mine, other, w, m, v, name):
    n, K, N = w.shape
    h = K // 2
    th = _rows_block(h, N)
    nb = h // th

    def body(c_ref, *refs):
        mine_refs, other_refs = refs[:n], refs[n:2 * n]
        w_ref, m_ref, v_ref, g_ref, d_ref, m2_ref, v2_ref = refs[2 * n:]
        a, s, c = pl.program_id(0), pl.program_id(1), c_ref[0]
        for idx in range(n):
            @pl.when((a == idx) & (s == c))
            def _(idx=idx):
                g_ref[...] = mine_refs[idx][...]

            @pl.when((a == idx) & (s != c))
            def _(idx=idx):
                g_ref[...] = other_refs[idx][...]
        _adamw_math(g_ref[...], w_ref, m_ref, v_ref, d_ref, m2_ref, v2_ref)

    def half_spec(idx, is_mine):
        def index(a, s, i, c_ref):
            right_half = (s == c_ref[0]) if is_mine else (s != c_ref[0])
            return (jnp.where((a == idx) & right_half, i, 0), 0)
        return pl.BlockSpec((th, N), index)

    full = pl.BlockSpec((None, th, N), lambda a, s, i, c_ref: (a, s * nb + i, 0))
    grid_spec = pltpu.PrefetchScalarGridSpec(
        num_scalar_prefetch=1, grid=(n, 2, nb),
        in_specs=[half_spec(idx, True) for idx in range(n)] + [half_spec(idx, False) for idx in range(n)] + [full] * 3,
        out_specs=[full] * 4)
    return pl.pallas_call(body, grid_spec=grid_spec, out_shape=[jax.ShapeDtypeStruct((n, K, N), F32)] * 4, name=name,
                          compiler_params=_cp("arbitrary", "arbitrary", "arbitrary"))(core, *mine, *other, w, m, v)


def _adamw_math(gv, w_ref, m_ref, v_ref, d_ref, m2_ref, v2_ref):
    m2 = ADAM_B1 * m_ref[...] + (1.0 - ADAM_B1) * gv
    v2 = ADAM_B2 * v_ref[...] + (1.0 - ADAM_B2) * (gv * gv)
    m_hat = m2 / (1.0 - ADAM_B1 ** ADAM_STEP)
    v_hat = v2 / (1.0 - ADAM_B2 ** ADAM_STEP)
    d_ref[...] = -ADAM_LR * (m_hat / (jnp.sqrt(v_hat) + ADAM_EPS) + ADAM_WD * w_ref[...])
    m2_ref[...] = m2
    v2_ref[...] = v2


def _row_tile(R):
    return _pick(R, (512, 256, 128, 64, 32, 16, 8))


def sum_parts(parts, name):
    P, R, _ = parts.shape
    rt = _row_tile(R)

    def body(p_ref, o_ref):
        acc = p_ref[0]
        for p in range(1, P):
            acc = acc + p_ref[p]
        o_ref[...] = acc

    return pl.pallas_call(
        body, grid=(R // rt,), in_specs=[pl.BlockSpec((P, rt, LANES), lambda i: (0, i, 0))],
        out_specs=pl.BlockSpec((rt, LANES), lambda i: (i, 0)), out_shape=jax.ShapeDtypeStruct((R, LANES), F32),
        name=name, compiler_params=_cp("parallel"))(parts)


def adamw(g, w, m, v, name):
    R = g.shape[0]
    rt = _row_tile(R)

    def body(g_ref, w_ref, m_ref, v_ref, d_ref, m2_ref, v2_ref):
        _adamw_math(g_ref[...], w_ref, m_ref, v_ref, d_ref, m2_ref, v2_ref)

    spec = pl.BlockSpec((rt, LANES), lambda i: (i, 0))
    return pl.pallas_call(
        body, grid=(R // rt,), in_specs=[spec] * 4, out_specs=[spec] * 3,
        out_shape=[jax.ShapeDtypeStruct((R, LANES), F32)] * 3, name=name, compiler_params=_cp("parallel"))(g, w, m, v)


def _pack(arrays, rows_multiple, dtype):
    flat = jnp.concatenate([a.reshape(-1).astype(dtype) for a in arrays])
    n = flat.shape[0]
    per = rows_multiple * LANES
    total = -(-n // per) * per
    return jnp.pad(flat, (0, total - n)).reshape(total // LANES, LANES)


def _unpack(buf, shapes):
    flat = buf.reshape(-1)
    out, off = [], 0
    for s in shapes:
        n = math.prod(s)
        out.append(flat[off:off + n].reshape(s))
        off += n
    return out


SHARDED = (("mod_w", 2), ("ffn_w_in", 3), ("ffn_w_out", 2), ("w_in", 2), ("w_out", 1),
           ("a_conv_w", 2), ("s5_glu_w", 1), ("hg_lb_logits", 2))
WEIGHTS = ("c_ctx", "mod_w", "mod_b", "ffn_w_in", "ffn_w_out", "w_in", "w_out", "a_conv_w", "a_conv_b", "a_dt_bias",
           "a_log", "a_d", "a_norm_w", "s5_lam_re", "s5_lam_im", "s5_log_step", "s5_b_re", "s5_b_im", "s5_c_re",
           "s5_c_im", "s5_d", "s5_glu_w", "s5_glu_b", "hg_lb_logits", "hg_norm_w", "final_norm_w")


def _gather_full(local, names_axes, dtype, rows_multiple, name):
    buf = _pack([local[n] for n, _ in names_axes], 2 * rows_multiple, dtype)
    R = buf.shape[0] // 2
    full = all_gather_shards(buf.reshape(2, R, LANES), name).reshape(N_CHIPS, 2 * R, LANES)
    per_chip = [_unpack(full[q], [local[n].shape for n, _ in names_axes]) for q in range(N_CHIPS)]
    return {n: jnp.concatenate([per_chip[q][j] for q in range(N_CHIPS)], axis=ax)
            for j, (n, ax) in enumerate(names_axes)}


def kernel(x, c, ctx, c_ctx, mod_w, mod_b, ffn_w_in, ffn_w_out, w_in, w_out, a_conv_w, a_conv_b, a_dt_bias, a_log, a_d, a_norm_w, s5_lam_re, s5_lam_im, s5_log_step, s5_b_re, s5_b_im, s5_c_re, s5_c_im, s5_d, s5_glu_w, s5_glu_b, hg_lb_logits, hg_norm_w, final_norm_w, loss_target, m_c_ctx, m_mod_w, m_mod_b, m_ffn_w_in, m_ffn_w_out, m_w_in, m_w_out, m_a_conv_w, m_a_conv_b, m_a_dt_bias, m_a_log, m_a_d, m_a_norm_w, m_s5_lam_re, m_s5_lam_im, m_s5_log_step, m_s5_b_re, m_s5_b_im, m_s5_c_re, m_s5_c_im, m_s5_d, m_s5_glu_w, m_s5_glu_b, m_hg_lb_logits, m_hg_norm_w, m_final_norm_w, v_c_ctx, v_mod_w, v_mod_b, v_ffn_w_in, v_ffn_w_out, v_w_in, v_w_out, v_a_conv_w, v_a_conv_b, v_a_dt_bias, v_a_log, v_a_d, v_a_norm_w, v_s5_lam_re, v_s5_lam_im, v_s5_log_step, v_s5_b_re, v_s5_b_im, v_s5_c_re, v_s5_c_im, v_s5_d, v_s5_glu_w, v_s5_glu_b, v_hg_lb_logits, v_hg_norm_w, v_final_norm_w):
    given = dict(locals())
    w = {n: given[n] for n in WEIGHTS}
    m = {n: given["m_" + n] for n in WEIGHTS}
    v = {n: given["v_" + n] for n in WEIGHTS}
    small_sharded = SHARDED[len(BIG):]
    replicated = [n for n in WEIGHTS if n not in [s for s, _ in SHARDED]]
    core = lax.axis_index("c").astype(jnp.int32).reshape(1)
    chip = (2 * lax.axis_index("x") + lax.axis_index("y")).astype(jnp.int32)

    def leaves(t):
        return list(t.reshape((-1,) + t.shape[-2:]))

    def nest(kind, flat):
        flat = list(flat)
        return [flat[2 * l:2 * l + 2] for l in range(len(flat) // 2)] if w[kind].ndim == 4 else flat

    counts = [len(leaves(w[k])) for k in BIG]
    gathered = gather_pieces([a.astype(BF16) for k in BIG for a in leaves(w[k])], "gather_big")
    big, off = {}, 0
    for k, cnt in zip(BIG, counts):
        big[k], off = nest(k, gathered[off:off + cnt]), off + cnt
    small = {n: w[n] for n in replicated}
    small.update(_gather_full(w, small_sharded, F32, 8, "gather_small"))

    loss, grad_x, grads = local_step(x, c, ctx, loss_target, big, small)

    g_big = [g for k in BIG for g in jax.tree.leaves(grads[k])]
    got = swap_with_sibling(g_big, "rs_sibling", send_other_half_of_axis=1)
    pairs = [add_pair(core, g, t, f"rs_add_pair{i}") for i, (g, t) in enumerate(zip(g_big, got))]
    from_chips = pieces_to_chips(pairs, "rs_chips")
    mine = [sum_pieces(chip.reshape(1), p, t, f"rs_sum_chips{i}") for i, (p, t) in enumerate(zip(pairs, from_chips))]
    other = swap_with_sibling(mine, "rs_halves")
    out, off = {}, 0
    for k, cnt in zip(BIG, counts):
        stack = lambda t: t.reshape((cnt,) + t.shape[-2:])
        res = adamw_halves(core, mine[off:off + cnt], other[off:off + cnt], stack(w[k]), stack(m[k]), stack(v[k]),
                           "adamw_" + k)
        off += cnt
        for kind, a in zip(("grad", "delta", "new_m", "new_v"), res):
            out[kind, k] = a.reshape(w[k].shape)

    names = replicated + [n for n, _ in small_sharded]
    part = _pack([grads[n] for n in names] + [loss.reshape(1)], 8, F32)
    g_all = _unpack(sum_parts(all_gather_devices(part, "gather_parts"), "sum_parts"),
                    [grads[n].shape for n in names] + [(1,)])
    g_of = dict(zip(names + ["loss"], g_all))
    for n, ax in small_sharded:
        g_of[n] = lax.dynamic_slice_in_dim(g_of[n], chip * w[n].shape[ax], w[n].shape[ax], axis=ax)
    rest = lambda d: _pack([d[n] for n in names], 8, F32)
    g_rest = rest(g_of)
    d_rest, m_rest, v_rest = adamw(g_rest, rest(w), rest(m), rest(v), "adamw_rest")
    shapes = [w[n].shape for n in names]
    for kind, buf in (("grad", g_rest), ("delta", d_rest), ("new_m", m_rest), ("new_v", v_rest)):
        for n, a in zip(names, _unpack(buf, shapes)):
            out[kind, n] = a

    loss_total = g_of["loss"].reshape(())
    return (loss_total, grad_x, *[out["grad", n] for n in WEIGHTS], *[out["delta", n] for n in WEIGHTS],
            *[out["new_m", n] for n in WEIGHTS], *[out["new_v", n] for n in WEIGHTS])
```
